```python
import numpy as np
import jax, jax.numpy as jnp
from jax import lax

D_MODEL = 1024
BATCH = 8
SEQ = 4096
DEPTH = 1
DEC_BATCH = 128
DEC_SEQ = 8
PAST_LEN = 16384
PAGE_SIZE = 128

HEAD_DIM = 64
N_HEADS = 8
N_KV = 2
GROUP = N_HEADS // N_KV
Q_W = N_HEADS * HEAD_DIM
KV_W = N_KV * HEAD_DIM
WINDOW = 128
ROT_DIM = HEAD_DIM // 4
ROPE_THETA = 500000.0
ATTN_SCALE = HEAD_DIM ** -0.5
CONV_CH = D_MODEL // 2
CONV_K = 3
N_EXPERTS = 32
TOP_K = 4
D_FF = D_MODEL
SWIGLU_ALPHA = 1.702
SWIGLU_LIMIT = 7.0
MOE_BLOCK = 128
NORM_EPS = 1e-5
QK_EPS = 1e-6
NEG_INF = -1e30
SPLIT_SIZES = (Q_W, KV_W, KV_W, CONV_CH, CONV_CH, CONV_CH, D_MODEL, D_MODEL)
IN_COLS = Q_W + 2 * KV_W + 3 * CONV_CH + 2 * D_MODEL

kernel_name = 'hybrid_swa_sink_shortconv_moe_adaln_step'


def rms_norm(x, g, eps):
    xf = x.astype(jnp.float32)
    y = xf * lax.rsqrt(jnp.mean(xf * xf, axis=-1, keepdims=True) + eps)
    return (y * g.astype(jnp.float32)).astype(x.dtype)


def rope_partial(x, pos):
    half = ROT_DIM // 2
    inv_freq = ROPE_THETA ** (-jnp.arange(0, ROT_DIM, 2, dtype=jnp.float32) / ROT_DIM)
    ang = pos.astype(jnp.float32)[:, None] * inv_freq[None, :]
    shape = (ang.shape[0],) + (1,) * (x.ndim - 3) + (half,)
    cos = jnp.cos(ang).reshape(shape).astype(x.dtype)
    sin = jnp.sin(ang).reshape(shape).astype(x.dtype)
    x1 = x[..., :half]
    x2 = x[..., half:ROT_DIM]
    return jnp.concatenate([x1 * cos - x2 * sin, x2 * cos + x1 * sin, x[..., ROT_DIM:]], axis=-1)


def adaln(c, w_ada, b_ada):
    mod = jax.nn.silu(c) @ w_ada + b_ada
    return jnp.split(mod[:, None, :], 6, axis=-1)


def modulate(x, g, shift, scale):
    return rms_norm(x, g, NORM_EPS) * (1 + scale) + shift


def project_in(h, w_in, q_norm_g, k_norm_g, pos):
    N, L = h.shape[:2]
    cuts = np.cumsum(SPLIT_SIZES)[:-1].tolist()
    q, k, v, gate_b, gate_c, x_conv, br_attn, br_conv = jnp.split(h @ w_in, cuts, axis=-1)
    q = q.reshape(N, L, N_KV, GROUP, HEAD_DIM)
    k = k.reshape(N, L, N_KV, HEAD_DIM)
    v = v.reshape(N, L, N_KV, HEAD_DIM)
    q = rope_partial(rms_norm(q, q_norm_g, QK_EPS), pos)
    k = rope_partial(rms_norm(k, k_norm_g, QK_EPS), pos)
    return q, k, v, gate_b, gate_c * x_conv, br_attn, br_conv


def sink_softmax(s, sinks):
    sk = jnp.broadcast_to(sinks.astype(jnp.float32).reshape(N_KV, GROUP, 1, 1), s.shape[:-1] + (1,))
    return jax.nn.softmax(jnp.concatenate([s, sk], axis=-1), axis=-1)[..., :-1]


def window_attn_prompt(q, k, v, sinks):
    N, L = q.shape[:2]
    nb = L // WINDOW
    qb = q.reshape(N, nb, WINDOW, N_KV, GROUP, HEAD_DIM)

    def with_prev(t):
        tb = t.reshape(N, nb, WINDOW, N_KV, HEAD_DIM)
        prev = jnp.concatenate([jnp.zeros_like(tb[:, :1]), tb[:, :-1]], axis=1)
        return jnp.concatenate([prev, tb], axis=2)

    kk, vv = with_prev(k), with_prev(v)
    qpos = jnp.arange(nb)[:, None] * WINDOW + jnp.arange(WINDOW)[None, :]
    kpos = jnp.arange(nb)[:, None] * WINDOW - WINDOW + jnp.arange(2 * WINDOW)[None, :]
    rel = qpos[:, :, None] - kpos[:, None, :]
    mask = (rel >= 0) & (rel < WINDOW) & (kpos[:, None, :] >= 0)
    s = jnp.einsum('bnqkgd,bnskd->bnkgqs', qb, kk).astype(jnp.float32) * ATTN_SCALE
    s = jnp.where(mask[None, :, None, None], s, NEG_INF)
    p = sink_softmax(s, sinks).astype(vv.dtype)
    o = jnp.einsum('bnkgqs,bnskd->bnqkgd', p, vv)
    return o.reshape(N, L, Q_W)


def window_attn_sample(q, k_all, v_all, sinks):
    N, DS = q.shape[:2]
    n_keys = k_all.shape[1]
    w_buf = n_keys - DS
    rel = (w_buf + jnp.arange(DS))[:, None] - jnp.arange(n_keys)[None, :]
    mask = (rel >= 0) & (rel < WINDOW)
    s = jnp.einsum('bqkgd,bskd->bkgqs', q, k_all).astype(jnp.float32) * ATTN_SCALE
    p = sink_softmax(jnp.where(mask, s, NEG_INF), sinks).astype(v_all.dtype)
    return jnp.einsum('bkgqs,bskd->bqkgd', p, v_all).reshape(N, DS, Q_W)


def short_conv(z_ext, conv_w, conv_b):
    L = z_ext.shape[1] - (CONV_K - 1)
    out = conv_b
    for j in range(CONV_K):
        out = out + conv_w[j] * z_ext[:, j:j + L]
    return out


def merge_branches(o_attn, y_conv, br_attn, br_conv, w_attn_out, w_conv_out, w_o):
    m = jax.nn.sigmoid(br_attn) * (o_attn @ w_attn_out) + jax.nn.sigmoid(br_conv) * (y_conv @ w_conv_out)
    return m @ w_o


def moe(h, w_router, b_router, w_gate, b_gate, w_up, b_up, w_down, b_down):
    T = h.shape[0]
    logits = (h @ w_router + b_router).astype(jnp.float32)
    top_val, top_idx = lax.top_k(logits, TOP_K)
    gates = jax.nn.softmax(top_val, axis=-1)
    e_flat = top_idx.reshape(-1).astype(jnp.int32)
    tok_flat = jnp.repeat(jnp.arange(T, dtype=jnp.int32), TOP_K)
    g_flat = gates.reshape(-1)
    order = jnp.argsort(e_flat, stable=True)
    e_sorted = e_flat[order]
    counts = jnp.bincount(e_flat, length=N_EXPERTS).astype(jnp.int32)
    padded = ((counts + MOE_BLOCK - 1) // MOE_BLOCK) * MOE_BLOCK
    pad_end = jnp.cumsum(padded).astype(jnp.int32)
    pad_start = pad_end - padded
    sorted_start = jnp.cumsum(counts).astype(jnp.int32) - counts
    dest = pad_start[e_sorted] + jnp.arange(T * TOP_K, dtype=jnp.int32) - sorted_start[e_sorted]
    n_blocks = -(-(T * TOP_K) // MOE_BLOCK) + N_EXPERTS
    slots = n_blocks * MOE_BLOCK
    slot_tok = jnp.zeros((slots,), jnp.int32).at[dest].set(tok_flat[order])
    slot_gate = jnp.zeros((slots,), jnp.float32).at[dest].set(g_flat[order])
    blk_start = jnp.arange(n_blocks, dtype=jnp.int32) * MOE_BLOCK
    blk_exp = jnp.minimum(jnp.searchsorted(pad_end, blk_start, side='right'), N_EXPERTS - 1)

    def expert_block(args):
        tok, e = args
        xb = h[tok]
        g = jnp.minimum(xb @ w_gate[e] + b_gate[e], SWIGLU_LIMIT)
        u = jnp.clip(xb @ w_up[e] + b_up[e], -SWIGLU_LIMIT, SWIGLU_LIMIT)
        a = g * jax.nn.sigmoid(SWIGLU_ALPHA * g) * (u + 1)
        return a @ w_down[e] + b_down[e]

    out = lax.map(expert_block, (slot_tok.reshape(n_blocks, MOE_BLOCK), blk_exp))
    out = out.reshape(slots, D_MODEL) * slot_gate[:, None].astype(out.dtype)
    return jnp.zeros_like(h).at[slot_tok].add(out)


def setup_inputs(seed: int = 0) -> dict:
    key = jax.random.key(seed)
    ks = jax.random.split(key, 28)
    f32 = jnp.float32

    def nrm(k, shape, s):
        return jax.random.normal(k, shape, f32) * s

    w_buf = min(WINDOW, PAST_LEN)
    return {
        'x_prompt': nrm(ks[0], (BATCH, SEQ, D_MODEL), 1.0),
        'x_sample': nrm(ks[1], (DEC_BATCH, DEC_SEQ, D_MODEL), 1.0),
        'cache_k_win': nrm(ks[2], (DEPTH, DEC_BATCH, w_buf, N_KV, HEAD_DIM), 1.0),
        'cache_v_win': nrm(ks[3], (DEPTH, DEC_BATCH, w_buf, N_KV, HEAD_DIM), 1.0),
        'state_conv': nrm(ks[4], (DEPTH, DEC_BATCH, CONV_K - 1, CONV_CH), 1.0),
        'c_prompt': nrm(ks[5], (BATCH, D_MODEL), 1.0),
        'c_sample': nrm(ks[6], (DEC_BATCH, D_MODEL), 1.0),
        'w_ada': nrm(ks[7], (DEPTH, D_MODEL, 6 * D_MODEL), 0.2 * D_MODEL ** -0.5),
        'b_ada': nrm(ks[8], (DEPTH, 6 * D_MODEL), 0.02),
        'norm1_g': 1.0 + nrm(ks[9], (DEPTH, D_MODEL), 0.05),
        'norm2_g': 1.0 + nrm(ks[10], (DEPTH, D_MODEL), 0.05),
        'w_in': nrm(ks[11], (DEPTH, D_MODEL, IN_COLS), D_MODEL ** -0.5),
        'q_norm_g': 1.0 + nrm(ks[12], (DEPTH, HEAD_DIM), 0.05),
        'k_norm_g': 1.0 + nrm(ks[13], (DEPTH, HEAD_DIM), 0.05),
        'attn_sinks': nrm(ks[14], (DEPTH, N_HEADS), 0.5),
        'w_attn_out': nrm(ks[15], (DEPTH, Q_W, D_MODEL), Q_W ** -0.5),
        'conv_w': nrm(ks[16], (DEPTH, CONV_K, CONV_CH), CONV_K ** -0.5),
        'conv_b': nrm(ks[17], (DEPTH, CONV_CH), 0.01),
        'w_conv_out': nrm(ks[18], (DEPTH, CONV_CH, D_MODEL), CONV_CH ** -0.5),
        'w_o': nrm(ks[19], (DEPTH, D_MODEL, D_MODEL), D_MODEL ** -0.5),
        'w_router': nrm(ks[20], (DEPTH, D_MODEL, N_EXPERTS), D_MODEL ** -0.5),
        'b_router': nrm(ks[21], (DEPTH, N_EXPERTS), 0.01),
        'w_gate': nrm(ks[22], (DEPTH, N_EXPERTS, D_MODEL, D_FF), D_MODEL ** -0.5),
        'b_gate': nrm(ks[23], (DEPTH, N_EXPERTS, D_FF), 0.01),
        'w_up': nrm(ks[24], (DEPTH, N_EXPERTS, D_MODEL, D_FF), D_MODEL ** -0.5),
        'b_up': nrm(ks[25], (DEPTH, N_EXPERTS, D_FF), 0.01),
        'w_down': nrm(ks[26], (DEPTH, N_EXPERTS, D_FF, D_MODEL), D_FF ** -0.5),
        'b_down': nrm(ks[27], (DEPTH, N_EXPERTS, D_MODEL), 0.01),
    }


def reference(x_prompt, x_sample, cache_k_win, cache_v_win, state_conv, c_prompt, c_sample,
              w_ada, b_ada, norm1_g, norm2_g, w_in, q_norm_g, k_norm_g, attn_sinks,
              w_attn_out, conv_w, conv_b, w_conv_out, w_o, w_router, b_router,
              w_gate, b_gate, w_up, b_up, w_down, b_down):
    S = x_prompt.shape[1]
    DS = x_sample.shape[1]
    pos_p = jnp.arange(S)
    pos_s = PAST_LEN + jnp.arange(DS)
    xp, xs = x_prompt, x_sample
    kp_l, vp_l, cp_l, ks_l, vs_l, cs_l = [], [], [], [], [], []
    for l in range(DEPTH):
        moe_w = (w_router[l], b_router[l], w_gate[l], b_gate[l], w_up[l], b_up[l], w_down[l], b_down[l])
        sh1, sc1, g1, sh2, sc2, g2 = adaln(c_prompt, w_ada[l], b_ada[l])
        h = modulate(xp, norm1_g[l], sh1, sc1)
        q, k, v, gate_b, z, br_a, br_c = project_in(h, w_in[l], q_norm_g[l], k_norm_g[l], pos_p)
        o = window_attn_prompt(q, k, v, attn_sinks[l])
        z_ext = jnp.pad(z, ((0, 0), (CONV_K - 1, 0), (0, 0)))
        y_conv = gate_b * short_conv(z_ext, conv_w[l], conv_b[l])
        xp = xp + g1 * merge_branches(o, y_conv, br_a, br_c, w_attn_out[l], w_conv_out[l], w_o[l])
        h = modulate(xp, norm2_g[l], sh2, sc2)
        xp = xp + g2 * moe(h.reshape(-1, D_MODEL), *moe_w).reshape(h.shape)
        w_p = min(WINDOW, S)
        kp_l.append(k[:, S - w_p:])
        vp_l.append(v[:, S - w_p:])
        cp_l.append(z[:, S - (CONV_K - 1):])
        sh1, sc1, g1, sh2, sc2, g2 = adaln(c_sample, w_ada[l], b_ada[l])
        h = modulate(xs, norm1_g[l], sh1, sc1)
        q, k, v, gate_b, z, br_a, br_c = project_in(h, w_in[l], q_norm_g[l], k_norm_g[l], pos_s)
        w_buf = cache_k_win.shape[2]
        k_all = jnp.concatenate([cache_k_win[l].astype(k.dtype), k], axis=1)
        v_all = jnp.concatenate([cache_v_win[l].astype(v.dtype), v], axis=1)
        o = window_attn_sample(q, k_all, v_all, attn_sinks[l])
        z_ext = jnp.concatenate([state_conv[l].astype(z.dtype), z], axis=1)
        y_conv = gate_b * short_conv(z_ext, conv_w[l], conv_b[l])
        xs = xs + g1 * merge_branches(o, y_conv, br_a, br_c, w_attn_out[l], w_conv_out[l], w_o[l])
        h = modulate(xs, norm2_g[l], sh2, sc2)
        xs = xs + g2 * moe(h.reshape(-1, D_MODEL), *moe_w).reshape(h.shape)
        n_all = k_all.shape[1]
        ks_l.append(k_all[:, n_all - w_buf:])
        vs_l.append(v_all[:, n_all - w_buf:])
        cs_l.append(z_ext[:, z_ext.shape[1] - (CONV_K - 1):])
    return (xp, xs, jnp.stack(kp_l), jnp.stack(vp_l), jnp.stack(cp_l), jnp.stack(ks_l), jnp.stack(vs_l), jnp.stack(cs_l))
```

```python
import functools

import numpy as np
import jax
import jax.numpy as jnp
from jax import lax
from jax.experimental import pallas as pl
from jax.experimental.pallas import tpu as pltpu

D_MODEL = 1024
HEAD_DIM = 64
N_HEADS = 8
N_KV = 2
GROUP = N_HEADS // N_KV
Q_W = N_HEADS * HEAD_DIM
KV_W = N_KV * HEAD_DIM
WINDOW = 128
ROT_DIM = HEAD_DIM // 4
ROPE_THETA = 500000.0
ATTN_SCALE = HEAD_DIM ** -0.5
CONV_CH = D_MODEL // 2
CONV_K = 3
N_EXPERTS = 32
TOP_K = 4
SWIGLU_ALPHA = 1.702
SWIGLU_LIMIT = 7.0
MOE_BLOCK = 128
NORM_EPS = 1e-5
QK_EPS = 1e-6
NEG_INF = -1e30
PAST_LEN = 16384

LANES = 128
SUBLANES = 8
VMEM_LIMIT = 56 * 1024 * 1024

_OFF_Q = 0
_OFF_K = _OFF_Q + Q_W
_OFF_V = _OFF_K + KV_W
_OFF_GB = _OFF_V + KV_W
_OFF_GC = _OFF_GB + CONV_CH
_OFF_XC = _OFF_GC + CONV_CH
_OFF_BA = _OFF_XC + CONV_CH
_OFF_BC = _OFF_BA + D_MODEL
IN_COLS = _OFF_BC + D_MODEL

BF16 = jnp.bfloat16
F32 = jnp.float32
_NT = (((1,), (1,)), ((), ()))


def _dot(a, b):
    return jnp.dot(a, b, preferred_element_type=F32)


def _dot_nt(a, b):
    return lax.dot_general(a, b, _NT, preferred_element_type=F32)


def _sigmoid(x):
    return 1.0 / (1.0 + jnp.exp(-x))


def _split_bf16(x):
    hi = x.astype(BF16)
    lo = (x - hi.astype(F32)).astype(BF16)
    return hi, lo


def _const_spec(shape):
    nd = len(shape)
    return pl.BlockSpec(shape, lambda *_: (0,) * nd, pipeline_mode=pl.Buffered(1))


def _adaln_body(c_ref, w_ref, b_ref, o_ref):
    c = c_ref[...]
    s = c * _sigmoid(c)
    o_ref[...] = _dot(s.astype(BF16), w_ref[...].astype(BF16)) + b_ref[...]


def _adaln(c, w_ada, b_ada):
    n = c.shape[0]
    cols = w_ada.shape[1]
    bn = 1536
    return pl.pallas_call(
        _adaln_body,
        grid=(cols // bn,),
        in_specs=[pl.BlockSpec((n, D_MODEL), lambda i: (0, 0)),
                  pl.BlockSpec((D_MODEL, bn), lambda i: (0, i)),
                  pl.BlockSpec((1, bn), lambda i: (0, i))],
        out_specs=pl.BlockSpec((n, bn), lambda i: (0, i)),
        out_shape=jax.ShapeDtypeStruct((n, cols), F32),
        compiler_params=pltpu.CompilerParams(vmem_limit_bytes=VMEM_LIMIT),
        name="adaln",
    )(c, w_ada, b_ada.reshape(1, cols))


def _modulated_norm(x, g, shift, scale):
    y = x * lax.rsqrt(jnp.mean(x * x, axis=-1, keepdims=True) + NORM_EPS)
    return (y * g) * (1.0 + scale) + shift


def _head_norm_rope(t, bd, g, rope):
    sq_hi, sq_lo = _split_bf16(t * t)
    ms = (_dot(sq_hi, bd) + _dot(sq_lo, bd)) * (1.0 / HEAD_DIM)
    y = t * lax.rsqrt(ms + QK_EPS) * g
    c = rope[:, 0:LANES]
    s_next = rope[:, LANES:2 * LANES]
    s_prev = rope[:, 2 * LANES:3 * LANES]
    half = ROT_DIM // 2
    return y * c + pltpu.roll(y, LANES - half, 1) * s_next + pltpu.roll(y, half, 1) * s_prev


def _pair_expand(t):
    lane = lax.broadcasted_iota(jnp.int32, t.shape, 1)
    lo = lane < HEAD_DIM
    r = pltpu.roll(t, HEAD_DIM, 1)
    zero = jnp.zeros_like(t)
    a0 = jnp.where(lo, t, zero).astype(BF16)
    b0 = jnp.where(lo, zero, r).astype(BF16)
    a1 = jnp.where(lo, r, zero).astype(BF16)
    b1 = jnp.where(lo, zero, t).astype(BF16)
    return ((a0, b0), (a1, b1))


def _softmax_pv(s_list, v_list, sink):
    m = jnp.full((s_list[0].shape[0], 1), sink, F32)
    for s in s_list:
        m = jnp.maximum(m, jnp.max(s, axis=-1, keepdims=True))
    den = jnp.exp(sink - m)
    acc = None
    for s, v in zip(s_list, v_list):
        e = jnp.exp(s - m)
        den = den + jnp.sum(e, axis=-1, keepdims=True)
        pv = _dot(e.astype(BF16), v)
        acc = pv if acc is None else acc + pv
    return acc * (1.0 / den)


def _route(h2, wr_hi, wr_lo, br):
    hi, lo = _split_bf16(h2)
    logits = _dot_nt(wr_hi, hi) + _dot_nt(wr_hi, lo) + _dot_nt(wr_lo, hi) + br
    eid = lax.broadcasted_iota(jnp.int32, logits.shape, 0).astype(F32)
    vals, idxs = [], []
    for _ in range(TOP_K):
        m = jnp.max(logits, axis=0, keepdims=True)
        idx = jnp.min(jnp.where(logits == m, eid, float(N_EXPERTS)), axis=0, keepdims=True)
        vals.append(m)
        idxs.append(idx)
        logits = jnp.where(eid == idx, -jnp.inf, logits)
    ex = [jnp.exp(v - vals[0]) for v in vals]
    tot = ex[0] + ex[1] + ex[2] + ex[3]
    inv = 1.0 / tot
    gates = [e * inv for e in ex]
    rowid = lax.broadcasted_iota(jnp.int32, (SUBLANES, h2.shape[0]), 0)
    topi = jnp.zeros((SUBLANES, h2.shape[0]), F32)
    topg = jnp.zeros((SUBLANES, h2.shape[0]), F32)
    for k in range(TOP_K):
        topi = jnp.where(rowid == k, idxs[k], topi)
        topg = jnp.where(rowid == k, gates[k], topg)
    return topi.astype(jnp.int32), topg


def _project(h_bf, w_in_ref, off, width):
    return _dot(h_bf, w_in_ref[:, off:off + width])


def _merge_and_route(x, o_bf, y_conv, h_bf, mod, n2g_ref, w_in_ref, w_ao_ref, w_co_ref, w_o_ref,
                     wr_hi_ref, wr_lo_ref, br_ref):
    sh1, sc1, g1, sh2, sc2, g2 = mod
    a = _dot(o_bf, w_ao_ref[...])
    c = _dot(y_conv.astype(BF16), w_co_ref[...])
    br_a = _project(h_bf, w_in_ref, _OFF_BA, D_MODEL)
    br_c = _project(h_bf, w_in_ref, _OFF_BC, D_MODEL)
    m = _sigmoid(br_a) * a + _sigmoid(br_c) * c
    x1 = x + g1 * _dot(m.astype(BF16), w_o_ref[...])
    h2 = _modulated_norm(x1, n2g_ref[...], sh2, sc2)
    topi, topg = _route(h2, wr_hi_ref[...], wr_lo_ref[...], br_ref[...])
    return x1, h2, topi, topg


def _prompt_body(sinks_ref, x_ref, mod_ref, rope_ref, n1g_ref, n2g_ref, w_in_ref, bd_ref, qg_ref, kg_ref,
                 w_ao_ref, cw_ref, cb_ref, w_co_ref, w_o_ref, wr_hi_ref, wr_lo_ref, br_ref,
                 x1_ref, h2_ref, topi_ref, topg_ref, kwin_ref, vwin_ref, zwin_ref,
                 kbuf, vbuf, zbuf, *, tb):
    j = pl.program_id(1)
    nsub = tb // WINDOW

    @pl.when(j == 0)
    def _():
        kbuf[0:WINDOW, :] = jnp.zeros((WINDOW, KV_W), F32)
        vbuf[0:WINDOW, :] = jnp.zeros((WINDOW, KV_W), F32)
        zbuf[0:SUBLANES, :] = jnp.zeros((SUBLANES, CONV_CH), F32)

    x = x_ref[0]
    mod = tuple(mod_ref[0, i:i + 1, :] for i in range(6))
    h_bf = _modulated_norm(x, n1g_ref[...], mod[0], mod[1]).astype(BF16)
    rope = rope_ref[...]
    bd = bd_ref[...]

    k = _head_norm_rope(_project(h_bf, w_in_ref, _OFF_K, KV_W), bd, kg_ref[...], rope)
    v = _project(h_bf, w_in_ref, _OFF_V, KV_W)
    kbuf[WINDOW:WINDOW + tb, :] = k
    vbuf[WINDOW:WINDOW + tb, :] = v
    kwin_ref[0] = k[tb - WINDOW:tb]
    vwin_ref[0] = v[tb - WINDOW:tb]
    k_exp = _pair_expand(kbuf[...])
    v_exp = _pair_expand(vbuf[...])

    q_tiles = []
    for t in range(Q_W // LANES):
        qt = _head_norm_rope(_project(h_bf, w_in_ref, _OFF_Q + t * LANES, LANES), bd, qg_ref[...], rope)
        q_tiles.append((qt * ATTN_SCALE).astype(BF16))

    row = lax.broadcasted_iota(jnp.int32, (WINDOW, 2 * WINDOW), 0)
    col = lax.broadcasted_iota(jnp.int32, (WINDOW, 2 * WINDOW), 1)
    band = (col > row) & (col <= row + WINDOW)
    first_key = jnp.where(j > 0, 0, WINDOW)
    band_first = band & (col >= first_key)
    o_rows = []
    for i in range(nsub):
        mask = band_first if i == 0 else band
        keys = slice(i * WINDOW, (i + 2) * WINDOW)
        o_tiles = []
        for t in range(Q_W // LANES):
            c = t // 2
            qt = q_tiles[t][i * WINDOW:(i + 1) * WINDOW]
            o_t = None
            for half in range(2):
                s = jnp.where(mask, _dot_nt(qt, k_exp[c][half][keys]), NEG_INF)
                sink = sinks_ref[c * GROUP + (t % 2) * 2 + half]
                part = _softmax_pv([s], [v_exp[c][half][keys]], sink)
                o_t = part if o_t is None else o_t + part
            o_tiles.append(o_t.astype(BF16))
        o_rows.append(jnp.concatenate(o_tiles, axis=1))
    o_bf = jnp.concatenate(o_rows, axis=0) if nsub > 1 else o_rows[0]
    kbuf[0:WINDOW, :] = kbuf[tb:tb + WINDOW, :]
    vbuf[0:WINDOW, :] = vbuf[tb:tb + WINDOW, :]

    z = _project(h_bf, w_in_ref, _OFF_GC, CONV_CH) * _project(h_bf, w_in_ref, _OFF_XC, CONV_CH)
    zbuf[SUBLANES:SUBLANES + tb, :] = z
    zwin_ref[0] = z[tb - SUBLANES:tb]
    z1 = zbuf[SUBLANES - 1:SUBLANES - 1 + tb, :]
    z2 = zbuf[SUBLANES - 2:SUBLANES - 2 + tb, :]
    conv = cb_ref[...] + cw_ref[0:1, :] * z2 + cw_ref[1:2, :] * z1 + cw_ref[2:3, :] * z
    y_conv = _project(h_bf, w_in_ref, _OFF_GB, CONV_CH) * conv
    zbuf[0:SUBLANES, :] = zbuf[tb:tb + SUBLANES, :]

    x1, h2, topi, topg = _merge_and_route(x, o_bf, y_conv, h_bf, mod, n2g_ref, w_in_ref, w_ao_ref, w_co_ref,
                                          w_o_ref, wr_hi_ref, wr_lo_ref, br_ref)
    x1_ref[0] = x1
    h2_ref[0] = h2
    topi_ref[0] = topi
    topg_ref[0] = topg


def _prompt_mixer(x, mod, rope, sinks, weights, tb):
    nb, seq, _ = x.shape
    (n1g, n2g, w_in, bd, qg, kg, w_ao, cw, cb, w_co, w_o, wr_hi, wr_lo, br) = weights
    consts = (n1g, n2g, w_in, bd, qg, kg, w_ao, cw, cb, w_co, w_o, wr_hi, wr_lo, br)
    in_specs = [
        pl.BlockSpec((1, tb, D_MODEL), lambda b, j, s: (b, j, 0)),
        pl.BlockSpec((1, 6, D_MODEL), lambda b, j, s: (b, 0, 0)),
        pl.BlockSpec((tb, 3 * LANES), lambda b, j, s: (j, 0)),
    ] + [_const_spec(w.shape) for w in consts]
    out_specs = [
        pl.BlockSpec((1, tb, D_MODEL), lambda b, j, s: (b, j, 0)),
        pl.BlockSpec((1, tb, D_MODEL), lambda b, j, s: (b, j, 0)),
        pl.BlockSpec((1, SUBLANES, tb), lambda b, j, s: (b, 0, j)),
        pl.BlockSpec((1, SUBLANES, tb), lambda b, j, s: (b, 0, j)),
        pl.BlockSpec((1, WINDOW, KV_W), lambda b, j, s: (b, 0, 0)),
        pl.BlockSpec((1, WINDOW, KV_W), lambda b, j, s: (b, 0, 0)),
        pl.BlockSpec((1, SUBLANES, CONV_CH), lambda b, j, s: (b, 0, 0)),
    ]
    out_shape = [
        jax.ShapeDtypeStruct((nb, seq, D_MODEL), F32),
        jax.ShapeDtypeStruct((nb, seq, D_MODEL), F32),
        jax.ShapeDtypeStruct((nb, SUBLANES, seq), jnp.int32),
        jax.ShapeDtypeStruct((nb, SUBLANES, seq), F32),
        jax.ShapeDtypeStruct((nb, WINDOW, KV_W), F32),
        jax.ShapeDtypeStruct((nb, WINDOW, KV_W), F32),
        jax.ShapeDtypeStruct((nb, SUBLANES, CONV_CH), F32),
    ]
    return pl.pallas_call(
        functools.partial(_prompt_body, tb=tb),
        grid_spec=pltpu.PrefetchScalarGridSpec(
            num_scalar_prefetch=1, grid=(nb, seq // tb),
            in_specs=in_specs, out_specs=out_specs,
            scratch_shapes=[pltpu.VMEM((tb + WINDOW, KV_W), F32),
                            pltpu.VMEM((tb + WINDOW, KV_W), F32),
                            pltpu.VMEM((tb + SUBLANES, CONV_CH), F32)]),
        out_shape=out_shape,
        compiler_params=pltpu.CompilerParams(
            dimension_semantics=("arbitrary", "arbitrary"), vmem_limit_bytes=VMEM_LIMIT),
        name="prompt_mixer",
    )(sinks, x, mod, rope, *consts)


def _sample_body(sinks_ref, x_ref, mod_ref, rope_ref, ck_ref, cv_ref, st_ref,
                 n1g_ref, n2g_ref, w_in_ref, bd_ref, qg_ref, kg_ref,
                 w_ao_ref, cw_ref, cb_ref, w_co_ref, w_o_ref, wr_hi_ref, wr_lo_ref, br_ref,
                 x1_ref, h2_ref, topi_ref, topg_ref, knew_ref, vnew_ref, z_ref, *, nseq, dseq):
    rows = nseq * dseq
    wbuf = ck_ref.shape[1]
    x = x_ref[...]
    mod = tuple(mod_ref[i] for i in range(6))
    h_bf = _modulated_norm(x, n1g_ref[...], mod[0], mod[1]).astype(BF16)
    rope = rope_ref[...]
    bd = bd_ref[...]

    k = _head_norm_rope(_project(h_bf, w_in_ref, _OFF_K, KV_W), bd, kg_ref[...], rope)
    v = _project(h_bf, w_in_ref, _OFF_V, KV_W)
    knew_ref[...] = k
    vnew_ref[...] = v
    kn_exp = _pair_expand(k)
    vn_exp = _pair_expand(v)
    kc_exp = _pair_expand(ck_ref[...].reshape(nseq * wbuf, KV_W))
    vc_exp = _pair_expand(cv_ref[...].reshape(nseq * wbuf, KV_W))

    qrow = lax.broadcasted_iota(jnp.int32, (rows, nseq * wbuf), 0)
    ccol = lax.broadcasted_iota(jnp.int32, (rows, nseq * wbuf), 1)
    mask_c = ((ccol // wbuf) == (qrow // dseq)) & ((ccol % wbuf) > (qrow % dseq) + (wbuf - WINDOW))
    qrow_n = lax.broadcasted_iota(jnp.int32, (rows, rows), 0)
    ncol = lax.broadcasted_iota(jnp.int32, (rows, rows), 1)
    mask_n = ((ncol // dseq) == (qrow_n // dseq)) & ((ncol % dseq) <= (qrow_n % dseq))

    o_tiles = []
    for t in range(Q_W // LANES):
        c = t // 2
        qt = _head_norm_rope(_project(h_bf, w_in_ref, _OFF_Q + t * LANES, LANES), bd, qg_ref[...], rope)
        qt = (qt * ATTN_SCALE).astype(BF16)
        o_t = None
        for half in range(2):
            s_c = jnp.where(mask_c, _dot_nt(qt, kc_exp[c][half]), NEG_INF)
            s_n = jnp.where(mask_n, _dot_nt(qt, kn_exp[c][half]), NEG_INF)
            sink = sinks_ref[c * GROUP + (t % 2) * 2 + half]
            part = _softmax_pv([s_c, s_n], [vc_exp[c][half], vn_exp[c][half]], sink)
            o_t = part if o_t is None else o_t + part
        o_tiles.append(o_t.astype(BF16))
    o_bf = jnp.concatenate(o_tiles, axis=1)

    z = _project(h_bf, w_in_ref, _OFF_GC, CONV_CH) * _project(h_bf, w_in_ref, _OFF_XC, CONV_CH)
    z_ref[...] = z
    r = lax.broadcasted_iota(jnp.int32, z.shape, 0) % dseq
    st0 = st_ref[0]
    st1 = st_ref[1]
    z1 = jnp.where(r == 0, st1, pltpu.roll(z, 1, 0))
    z2 = jnp.where(r == 0, st0, jnp.where(r == 1, st1, pltpu.roll(z, 2, 0)))
    conv = cb_ref[...] + cw_ref[0:1, :] * z2 + cw_ref[1:2, :] * z1 + cw_ref[2:3, :] * z
    y_conv = _project(h_bf, w_in_ref, _OFF_GB, CONV_CH) * conv

    x1, h2, topi, topg = _merge_and_route(x, o_bf, y_conv, h_bf, mod, n2g_ref, w_in_ref, w_ao_ref, w_co_ref,
                                          w_o_ref, wr_hi_ref, wr_lo_ref, br_ref)
    x1_ref[...] = x1
    h2_ref[...] = h2
    topi_ref[...] = topi
    topg_ref[...] = topg


def _sample_mixer(x, mod, rope, cache_k, cache_v, state, sinks, weights, nseq):
    ntok = x.shape[0]
    nall, wbuf, _ = cache_k.shape
    dseq = ntok // nall
    rows = nseq * dseq
    consts = weights
    in_specs = [
        pl.BlockSpec((rows, D_MODEL), lambda i, s: (i, 0)),
        pl.BlockSpec((6, rows, D_MODEL), lambda i, s: (0, i, 0)),
        pl.BlockSpec((rows, 3 * LANES), lambda i, s: (i, 0)),
        pl.BlockSpec((nseq, wbuf, KV_W), lambda i, s: (i, 0, 0)),
        pl.BlockSpec((nseq, wbuf, KV_W), lambda i, s: (i, 0, 0)),
        pl.BlockSpec((2, rows, CONV_CH), lambda i, s: (0, i, 0)),
    ] + [_const_spec(w.shape) for w in consts]
    out_specs = [
        pl.BlockSpec((rows, D_MODEL), lambda i, s: (i, 0)),
        pl.BlockSpec((rows, D_MODEL), lambda i, s: (i, 0)),
        pl.BlockSpec((SUBLANES, rows), lambda i, s: (0, i)),
        pl.BlockSpec((SUBLANES, rows), lambda i, s: (0, i)),
        pl.BlockSpec((rows, KV_W), lambda i, s: (i, 0)),
        pl.BlockSpec((rows, KV_W), lambda i, s: (i, 0)),
        pl.BlockSpec((rows, CONV_CH), lambda i, s: (i, 0)),
    ]
    out_shape = [
        jax.ShapeDtypeStruct((ntok, D_MODEL), F32),
        jax.ShapeDtypeStruct((ntok, D_MODEL), F32),
        jax.ShapeDtypeStruct((SUBLANES, ntok), jnp.int32),
        jax.ShapeDtypeStruct((SUBLANES, ntok), F32),
        jax.ShapeDtypeStruct((ntok, KV_W), F32),
        jax.ShapeDtypeStruct((ntok, KV_W), F32),
        jax.ShapeDtypeStruct((ntok, CONV_CH), F32),
    ]
    return pl.pallas_call(
        functools.partial(_sample_body, nseq=nseq, dseq=dseq),
        grid_spec=pltpu.PrefetchScalarGridSpec(
            num_scalar_prefetch=1, grid=(ntok // rows,),
            in_specs=in_specs, out_specs=out_specs),
        out_shape=out_shape,
        compiler_params=pltpu.CompilerParams(
            dimension_semantics=("arbitrary",), vmem_limit_bytes=VMEM_LIMIT),
        name="sample_mixer",
    )(sinks, x, mod, rope, cache_k, cache_v, state, *consts)


def _row_copy(src_hbm, dst, sem, idx, r):
    return pltpu.make_async_copy(src_hbm.at[pl.ds(idx, 1)], dst.at[pl.ds(r, 1)], sem)


def _start_row_gather(idx_smem, src_hbm, dst, sem, n):
    def issue(r, carry):
        _row_copy(src_hbm, dst, sem, idx_smem[r], r).start()
        return carry
    lax.fori_loop(0, n, issue, 0)


def _wait_row_gather(src_hbm, dst, sem, n):
    pltpu.make_async_copy(src_hbm.at[pl.ds(0, n)], dst, sem).wait()


def _experts_body(blk_exp_ref, nact_ref, tok_hbm, h_hbm, wg_ref, bg_ref, wu_ref, bu_ref, wd_ref, bd_ref,
                  ys_ref, idx_smem, xbuf, sem_idx, sem_x):
    b = pl.program_id(0)
    nact = nact_ref[0]
    slot = b % 2

    def fetch(blk, s):
        cp = pltpu.make_async_copy(tok_hbm.at[blk], idx_smem.at[s], sem_idx.at[s])
        cp.start()
        cp.wait()
        _start_row_gather(idx_smem.at[s], h_hbm, xbuf.at[s], sem_x.at[s], MOE_BLOCK)

    @pl.when(b == 0)
    def _():
        fetch(0, 0)

    @pl.when(b + 1 < nact)
    def _():
        fetch(b + 1, 1 - slot)

    @pl.when(b < nact)
    def _():
        _wait_row_gather(h_hbm, xbuf.at[slot], sem_x.at[slot], MOE_BLOCK)
        xb = xbuf[slot].astype(BF16)
        g = jnp.minimum(_dot(xb, wg_ref[0]) + bg_ref[0], SWIGLU_LIMIT)
        u = jnp.clip(_dot(xb, wu_ref[0]) + bu_ref[0], -SWIGLU_LIMIT, SWIGLU_LIMIT)
        a = g * _sigmoid(SWIGLU_ALPHA * g) * (u + 1.0)
        ys_ref[...] = _dot(a.astype(BF16), wd_ref[0]) + bd_ref[0]

    @pl.when(b >= nact)
    def _():
        ys_ref[...] = jnp.zeros(ys_ref.shape, F32)


def _experts(blk_exp, nact, slot_tok, h_all, wg, bg, wu, bu, wd, bd):
    n_blocks = blk_exp.shape[0]
    wspec = pl.BlockSpec((1, D_MODEL, D_MODEL), lambda b, be, na: (be[b], 0, 0))
    bspec = pl.BlockSpec((1, 1, D_MODEL), lambda b, be, na: (be[b], 0, 0))
    return pl.pallas_call(
        _experts_body,
        grid_spec=pltpu.PrefetchScalarGridSpec(
            num_scalar_prefetch=2, grid=(n_blocks,),
            in_specs=[pl.BlockSpec(memory_space=pl.ANY), pl.BlockSpec(memory_space=pl.ANY),
                      wspec, bspec, wspec, bspec, wspec, bspec],
            out_specs=pl.BlockSpec((MOE_BLOCK, D_MODEL), lambda b, be, na: (b, 0)),
            scratch_shapes=[pltpu.SMEM((2, MOE_BLOCK), jnp.int32),
                            pltpu.VMEM((2, MOE_BLOCK, D_MODEL), F32),
                            pltpu.SemaphoreType.DMA((2,)),
                            pltpu.SemaphoreType.DMA((2,))]),
        out_shape=jax.ShapeDtypeStruct((n_blocks * MOE_BLOCK, D_MODEL), F32),
        compiler_params=pltpu.CompilerParams(
            dimension_semantics=("arbitrary",), vmem_limit_bytes=VMEM_LIMIT),
        name="experts",
    )(blk_exp, nact, slot_tok, h_all, wg, bg, wu, bu, wd, bd)


def _combine_body(dest_hbm, ys_hbm, x1_ref, g2_ref, gate_ref, y_ref, idx_smem, buf, sem_idx, sem_x, *, blk0):
    i = pl.program_id(0)
    nrow = MOE_BLOCK * TOP_K
    cp = pltpu.make_async_copy(dest_hbm.at[blk0 + i], idx_smem, sem_idx)
    cp.start()
    cp.wait()
    _start_row_gather(idx_smem, ys_hbm, buf, sem_x, nrow)
    _wait_row_gather(ys_hbm, buf, sem_x, nrow)
    gate = gate_ref[...]
    acc = None
    for k in range(TOP_K):
        part = gate[:, k:k + 1] * buf[k * MOE_BLOCK:(k + 1) * MOE_BLOCK, :]
        acc = part if acc is None else acc + part
    y_ref[...] = x1_ref[...] + g2_ref[...] * acc


def _combine(dest_blocks, ys, x1, g2, gates, blk0, g2_spec):
    ntok = x1.shape[0]
    return pl.pallas_call(
        functools.partial(_combine_body, blk0=blk0),
        grid=(ntok // MOE_BLOCK,),
        in_specs=[pl.BlockSpec(memory_space=pl.ANY), pl.BlockSpec(memory_space=pl.ANY),
                  pl.BlockSpec((MOE_BLOCK, D_MODEL), lambda i: (i, 0)),
                  g2_spec,
                  pl.BlockSpec((MOE_BLOCK, TOP_K), lambda i: (i, 0))],
        out_specs=pl.BlockSpec((MOE_BLOCK, D_MODEL), lambda i: (i, 0)),
        scratch_shapes=[pltpu.SMEM((MOE_BLOCK * TOP_K,), jnp.int32),
                        pltpu.VMEM((MOE_BLOCK * TOP_K, D_MODEL), F32),
                        pltpu.SemaphoreType.DMA,
                        pltpu.SemaphoreType.DMA],
        out_shape=jax.ShapeDtypeStruct((ntok, D_MODEL), F32),
        compiler_params=pltpu.CompilerParams(
            dimension_semantics=("arbitrary",), vmem_limit_bytes=VMEM_LIMIT),
        name="combine",
    )(dest_blocks, ys, x1, g2, gates)


def _rope_table(pos):
    half = ROT_DIM // 2
    inv_freq = ROPE_THETA ** (-jnp.arange(0, ROT_DIM, 2, dtype=F32) / ROT_DIM)
    ang = pos.astype(F32)[:, None] * inv_freq[None, :]
    cos, sin = jnp.cos(ang), jnp.sin(ang)
    n = pos.shape[0]
    rest = HEAD_DIM - ROT_DIM
    c = jnp.concatenate([cos, cos, jnp.ones((n, rest), F32)], axis=1)
    s_next = jnp.concatenate([-sin, jnp.zeros((n, half + rest), F32)], axis=1)
    s_prev = jnp.concatenate([jnp.zeros((n, half), F32), sin, jnp.zeros((n, rest), F32)], axis=1)
    rep = LANES // HEAD_DIM
    return jnp.concatenate([jnp.tile(c, (1, rep)), jnp.tile(s_next, (1, rep)), jnp.tile(s_prev, (1, rep))], axis=1)


def _routing_plan(e_all):
    ntok = e_all.shape[0]
    onehot = (e_all[:, :, None] == jnp.arange(N_EXPERTS, dtype=jnp.int32)[None, None, :]).astype(jnp.int32).sum(axis=1)
    counts = onehot.sum(axis=0)
    before = jnp.cumsum(onehot, axis=0) - onehot
    rank = jnp.take_along_axis(before, e_all, axis=1)
    padded = ((counts + MOE_BLOCK - 1) // MOE_BLOCK) * MOE_BLOCK
    pad_end = jnp.cumsum(padded)
    pad_start = pad_end - padded
    dest = pad_start[e_all] + rank
    n_blocks = -(-(ntok * TOP_K) // MOE_BLOCK) + N_EXPERTS
    tok = jnp.broadcast_to(jnp.arange(ntok, dtype=jnp.int32)[:, None], (ntok, TOP_K))
    slot_tok = jnp.zeros((n_blocks * MOE_BLOCK,), jnp.int32).at[dest.reshape(-1)].set(
        tok.reshape(-1), unique_indices=True)
    blk_start = jnp.arange(n_blocks, dtype=jnp.int32) * MOE_BLOCK
    blk_exp = jnp.minimum(jnp.searchsorted(pad_end, blk_start, side='right'), N_EXPERTS - 1).astype(jnp.int32)
    nact = (pad_end[-1] // MOE_BLOCK).astype(jnp.int32).reshape(1)
    return dest.astype(jnp.int32), slot_tok.reshape(n_blocks, MOE_BLOCK), blk_exp, nact


def kernel(x_prompt, x_sample, cache_k_win, cache_v_win, state_conv, c_prompt, c_sample, w_ada, b_ada, norm1_g, norm2_g, w_in, q_norm_g, k_norm_g, attn_sinks, w_attn_out, conv_w, conv_b, w_conv_out, w_o, w_router, b_router, w_gate, b_gate, w_up, b_up, w_down, b_down):
    nb, seq, _ = x_prompt.shape
    ns, dseq, _ = x_sample.shape
    wbuf = cache_k_win.shape[2]
    assert w_ada.shape[0] == 1, "single-layer step"
    l = 0

    mod = _adaln(jnp.concatenate([c_prompt, c_sample], axis=0), w_ada[l], b_ada[l])
    mod_p = mod[:nb].reshape(nb, 6, D_MODEL)
    mod_s = jnp.repeat(mod[nb:].reshape(ns, 6, D_MODEL).transpose(1, 0, 2), dseq, axis=1)

    bd = jnp.asarray(np.kron(np.eye(LANES // HEAD_DIM), np.ones((HEAD_DIM, HEAD_DIM))), BF16)
    rep = LANES // HEAD_DIM
    wr_t = w_router[l].T
    wr_hi = wr_t.astype(BF16)
    wr_lo = (wr_t - wr_hi.astype(F32)).astype(BF16)
    weights = (norm1_g[l].reshape(1, D_MODEL), norm2_g[l].reshape(1, D_MODEL), w_in[l].astype(BF16), bd,
               jnp.tile(q_norm_g[l], rep).reshape(1, LANES), jnp.tile(k_norm_g[l], rep).reshape(1, LANES),
               w_attn_out[l].astype(BF16), conv_w[l], conv_b[l].reshape(1, CONV_CH), w_conv_out[l].astype(BF16),
               w_o[l].astype(BF16), wr_hi, wr_lo, b_router[l].reshape(N_EXPERTS, 1))
    sinks = attn_sinks[l]

    tb = 256
    (x1_p, h2_p, topi_p, topg_p, kwin_p, vwin_p, zwin_p) = _prompt_mixer(
        x_prompt, mod_p, _rope_table(jnp.arange(seq)), sinks, weights, tb)
    pos_s = jnp.tile(PAST_LEN + jnp.arange(dseq), ns)
    state_rows = jnp.repeat(state_conv[l].transpose(1, 0, 2), dseq, axis=1)
    (x1_s, h2_s, topi_s, topg_s, knew, vnew, z_s) = _sample_mixer(
        x_sample.reshape(ns * dseq, D_MODEL), mod_s, _rope_table(pos_s),
        cache_k_win[l].reshape(ns, wbuf, KV_W), cache_v_win[l].reshape(ns, wbuf, KV_W),
        state_rows, sinks, weights, 16)

    ntok_p = nb * seq
    e_all = jnp.concatenate([topi_p[:, :TOP_K, :].transpose(0, 2, 1).reshape(ntok_p, TOP_K),
                             topi_s[:TOP_K, :].T], axis=0)
    g_all = jnp.concatenate([topg_p[:, :TOP_K, :].transpose(0, 2, 1).reshape(ntok_p, TOP_K),
                             topg_s[:TOP_K, :].T], axis=0)
    h_all = jnp.concatenate([h2_p.reshape(ntok_p, D_MODEL), h2_s], axis=0)
    dest, slot_tok, blk_exp, nact = _routing_plan(e_all)
    ys = _experts(blk_exp, nact, slot_tok, h_all,
                  w_gate[l].astype(BF16), b_gate[l].reshape(N_EXPERTS, 1, D_MODEL),
                  w_up[l].astype(BF16), b_up[l].reshape(N_EXPERTS, 1, D_MODEL),
                  w_down[l].astype(BF16), b_down[l].reshape(N_EXPERTS, 1, D_MODEL))
    dest_blocks = dest.reshape(-1, MOE_BLOCK, TOP_K).transpose(0, 2, 1).reshape(-1, TOP_K * MOE_BLOCK)
    blocks_per_seq = seq // MOE_BLOCK
    y_p = _combine(dest_blocks, ys, x1_p.reshape(ntok_p, D_MODEL), mod_p[:, 5:6, :], g_all[:ntok_p], 0,
                   pl.BlockSpec((None, 1, D_MODEL), lambda i: (i // blocks_per_seq, 0, 0)))
    y_s = _combine(dest_blocks, ys, x1_s, mod_s[5], g_all[ntok_p:], ntok_p // MOE_BLOCK,
                   pl.BlockSpec((MOE_BLOCK, D_MODEL), lambda i: (i, 0)))

    n_kv_shape = (N_KV, HEAD_DIM)
    k_win_p = kwin_p.reshape(1, nb, WINDOW, *n_kv_shape)
    v_win_p = vwin_p.reshape(1, nb, WINDOW, *n_kv_shape)
    conv_p = zwin_p[:, SUBLANES - (CONV_K - 1):, :][None]
    k_win_s = jnp.concatenate([cache_k_win[l][:, dseq:], knew.reshape(ns, dseq, *n_kv_shape)], axis=1)[None]
    v_win_s = jnp.concatenate([cache_v_win[l][:, dseq:], vnew.reshape(ns, dseq, *n_kv_shape)], axis=1)[None]
    conv_s = z_s.reshape(ns, dseq, CONV_CH)[:, dseq - (CONV_K - 1):, :][None]
    return (y_p.reshape(nb, seq, D_MODEL), y_s.reshape(ns, dseq, D_MODEL),
            k_win_p, v_win_p, conv_p, k_win_s, v_win_s, conv_s)
```

```python
import functools

import numpy as np
import jax
import jax.numpy as jnp
from jax import lax
from jax.experimental import pallas as pl
from jax.experimental.pallas import tpu as pltpu

D_MODEL = 1024
HEAD_DIM = 64
N_HEADS = 8
N_KV = 2
GROUP = N_HEADS // N_KV
Q_W = N_HEADS * HEAD_DIM
KV_W = N_KV * HEAD_DIM
WINDOW = 128
ROT_DIM = HEAD_DIM // 4
ROPE_THETA = 500000.0
ATTN_SCALE = HEAD_DIM ** -0.5
CONV_CH = D_MODEL // 2
CONV_K = 3
N_EXPERTS = 32
TOP_K = 4
SWIGLU_ALPHA = 1.702
SWIGLU_LIMIT = 7.0
MOE_BLOCK = 128
NORM_EPS = 1e-5
QK_EPS = 1e-6
NEG_INF = -1e30
PAST_LEN = 16384

LANES = 128
SUBLANES = 8
VMEM_LIMIT = 56 * 1024 * 1024

_OFF_Q = 0
_OFF_K = _OFF_Q + Q_W
_OFF_V = _OFF_K + KV_W
_OFF_GB = _OFF_V + KV_W
_OFF_GC = _OFF_GB + CONV_CH
_OFF_XC = _OFF_GC + CONV_CH
_OFF_BA = _OFF_XC + CONV_CH
_OFF_BC = _OFF_BA + D_MODEL
IN_COLS = _OFF_BC + D_MODEL

BF16 = jnp.bfloat16
F32 = jnp.float32
_NT = (((1,), (1,)), ((), ()))


def _dot(a, b):
    return jnp.dot(a, b, preferred_element_type=F32)


def _dot_nt(a, b):
    return lax.dot_general(a, b, _NT, preferred_element_type=F32)


def _sigmoid(x):
    return 1.0 / (1.0 + jnp.exp(-x))


def _split_bf16(x):
    hi = x.astype(BF16)
    lo = (x - hi.astype(F32)).astype(BF16)
    return hi, lo


def _const_spec(shape):
    nd = len(shape)
    return pl.BlockSpec(shape, lambda *_: (0,) * nd, pipeline_mode=pl.Buffered(1))


def _adaln_body(c_ref, w_ref, b_ref, o_ref):
    c = c_ref[...]
    s = c * _sigmoid(c)
    o_ref[...] = _dot(s.astype(BF16), w_ref[...].astype(BF16)) + b_ref[...]


def _adaln(c, w_ada, b_ada):
    n = c.shape[0]
    cols = w_ada.shape[1]
    bn = 1536
    return pl.pallas_call(
        _adaln_body,
        grid=(cols // bn,),
        in_specs=[pl.BlockSpec((n, D_MODEL), lambda i: (0, 0)),
                  pl.BlockSpec((D_MODEL, bn), lambda i: (0, i)),
                  pl.BlockSpec((1, bn), lambda i: (0, i))],
        out_specs=pl.BlockSpec((n, bn), lambda i: (0, i)),
        out_shape=jax.ShapeDtypeStruct((n, cols), F32),
        compiler_params=pltpu.CompilerParams(vmem_limit_bytes=VMEM_LIMIT),
        name="adaln",
    )(c, w_ada, b_ada.reshape(1, cols))


def _modulated_norm(x, g, shift, scale):
    y = x * lax.rsqrt(jnp.mean(x * x, axis=-1, keepdims=True) + NORM_EPS)
    return (y * g) * (1.0 + scale) + shift


def _head_norm_rope(t, bd, g, rope):
    sq_hi, sq_lo = _split_bf16(t * t)
    ms = (_dot(sq_hi, bd) + _dot(sq_lo, bd)) * (1.0 / HEAD_DIM)
    y = t * lax.rsqrt(ms + QK_EPS) * g
    c = rope[:, 0:LANES]
    s_next = rope[:, LANES:2 * LANES]
    s_prev = rope[:, 2 * LANES:3 * LANES]
    half = ROT_DIM // 2
    return y * c + pltpu.roll(y, LANES - half, 1) * s_next + pltpu.roll(y, half, 1) * s_prev


def _pair_expand(t):
    lane = lax.broadcasted_iota(jnp.int32, t.shape, 1)
    lo = lane < HEAD_DIM
    r = pltpu.roll(t, HEAD_DIM, 1)
    zero = jnp.zeros_like(t)
    a0 = jnp.where(lo, t, zero).astype(BF16)
    b0 = jnp.where(lo, zero, r).astype(BF16)
    a1 = jnp.where(lo, r, zero).astype(BF16)
    b1 = jnp.where(lo, zero, t).astype(BF16)
    return ((a0, b0), (a1, b1))


def _softmax_pv(s_list, v_list, sink):
    m = jnp.full((s_list[0].shape[0], 1), sink, F32)
    for s in s_list:
        m = jnp.maximum(m, jnp.max(s, axis=-1, keepdims=True))
    den = jnp.exp(sink - m)
    acc = None
    for s, v in zip(s_list, v_list):
        e = jnp.exp(s - m)
        den = den + jnp.sum(e, axis=-1, keepdims=True)
        pv = _dot(e.astype(BF16), v)
        acc = pv if acc is None else acc + pv
    return acc * (1.0 / den)


def _route(h2, wr_hi, wr_lo, br):
    hi, lo = _split_bf16(h2)
    logits = _dot_nt(wr_hi, hi) + _dot_nt(wr_hi, lo) + _dot_nt(wr_lo, hi) + br
    eid = lax.broadcasted_iota(jnp.int32, logits.shape, 0).astype(F32)
    vals, idxs = [], []
    for _ in range(TOP_K):
        m = jnp.max(logits, axis=0, keepdims=True)
        idx = jnp.min(jnp.where(logits == m, eid, float(N_EXPERTS)), axis=0, keepdims=True)
        vals.append(m)
        idxs.append(idx)
        logits = jnp.where(eid == idx, -jnp.inf, logits)
    ex = [jnp.exp(v - vals[0]) for v in vals]
    tot = ex[0] + ex[1] + ex[2] + ex[3]
    inv = 1.0 / tot
    gates = [e * inv for e in ex]
    rowid = lax.broadcasted_iota(jnp.int32, (SUBLANES, h2.shape[0]), 0)
    topi = jnp.zeros((SUBLANES, h2.shape[0]), F32)
    topg = jnp.zeros((SUBLANES, h2.shape[0]), F32)
    for k in range(TOP_K):
        topi = jnp.where(rowid == k, idxs[k], topi)
        topg = jnp.where(rowid == k, gates[k], topg)
    return topi.astype(jnp.int32), topg


def _project(h_bf, w_in_ref, off, width):
    return _dot(h_bf, w_in_ref[:, off:off + width])


def _merge_and_route(x, o_bf, y_conv, h_bf, mod, n2g_ref, w_in_ref, w_ao_ref, w_co_ref, w_o_ref,
                     wr_hi_ref, wr_lo_ref, br_ref):
    sh1, sc1, g1, sh2, sc2, g2 = mod
    a = _dot(o_bf, w_ao_ref[...])
    c = _dot(y_conv.astype(BF16), w_co_ref[...])
    br_a = _project(h_bf, w_in_ref, _OFF_BA, D_MODEL)
    br_c = _project(h_bf, w_in_ref, _OFF_BC, D_MODEL)
    m = _sigmoid(br_a) * a + _sigmoid(br_c) * c
    x1 = x + g1 * _dot(m.astype(BF16), w_o_ref[...])
    h2 = _modulated_norm(x1, n2g_ref[...], sh2, sc2)
    topi, topg = _route(h2, wr_hi_ref[...], wr_lo_ref[...], br_ref[...])
    return x1, h2, topi, topg


def _prompt_body(sinks_ref, x_ref, mod_ref, rope_ref, n1g_ref, n2g_ref, w_in_ref, bd_ref, qg_ref, kg_ref,
                 w_ao_ref, cw_ref, cb_ref, w_co_ref, w_o_ref, wr_hi_ref, wr_lo_ref, br_ref,
                 x1_ref, h2_ref, topi_ref, topg_ref, kwin_ref, vwin_ref, zwin_ref,
                 kbuf, vbuf, zbuf, *, tb):
    j = pl.program_id(1)
    nsub = tb // WINDOW

    @pl.when(j == 0)
    def _():
        kbuf[0:WINDOW, :] = jnp.zeros((WINDOW, KV_W), F32)
        vbuf[0:WINDOW, :] = jnp.zeros((WINDOW, KV_W), F32)
        zbuf[0:SUBLANES, :] = jnp.zeros((SUBLANES, CONV_CH), F32)

    x = x_ref[0]
    mod = tuple(mod_ref[0, i:i + 1, :] for i in range(6))
    h_bf = _modulated_norm(x, n1g_ref[...], mod[0], mod[1]).astype(BF16)
    rope = rope_ref[...]
    bd = bd_ref[...]

    k = _head_norm_rope(_project(h_bf, w_in_ref, _OFF_K, KV_W), bd, kg_ref[...], rope)
    v = _project(h_bf, w_in_ref, _OFF_V, KV_W)
    kbuf[WINDOW:WINDOW + tb, :] = k
    vbuf[WINDOW:WINDOW + tb, :] = v
    kwin_ref[0] = k[tb - WINDOW:tb]
    vwin_ref[0] = v[tb - WINDOW:tb]
    k_exp = _pair_expand(kbuf[...])
    v_exp = _pair_expand(vbuf[...])

    q_tiles = []
    for t in range(Q_W // LANES):
        qt = _head_norm_rope(_project(h_bf, w_in_ref, _OFF_Q + t * LANES, LANES), bd, qg_ref[...], rope)
        q_tiles.append((qt * ATTN_SCALE).astype(BF16))

    row = lax.broadcasted_iota(jnp.int32, (WINDOW, 2 * WINDOW), 0)
    col = lax.broadcasted_iota(jnp.int32, (WINDOW, 2 * WINDOW), 1)
    band = (col > row) & (col <= row + WINDOW)
    first_key = jnp.where(j > 0, 0, WINDOW)
    band_first = band & (col >= first_key)
    o_rows = []
    for i in range(nsub):
        mask = band_first if i == 0 else band
        keys = slice(i * WINDOW, (i + 2) * WINDOW)
        o_tiles = []
        for t in range(Q_W // LANES):
            c = t // 2
            qt = q_tiles[t][i * WINDOW:(i + 1) * WINDOW]
            o_t = None
            for half in range(2):
                s = jnp.where(mask, _dot_nt(qt, k_exp[c][half][keys]), NEG_INF)
                sink = sinks_ref[c * GROUP + (t % 2) * 2 + half]
                part = _softmax_pv([s], [v_exp[c][half][keys]], sink)
                o_t = part if o_t is None else o_t + part
            o_tiles.append(o_t.astype(BF16))
        o_rows.append(jnp.concatenate(o_tiles, axis=1))
    o_bf = jnp.concatenate(o_rows, axis=0) if nsub > 1 else o_rows[0]
    kbuf[0:WINDOW, :] = kbuf[tb:tb + WINDOW, :]
    vbuf[0:WINDOW, :] = vbuf[tb:tb + WINDOW, :]

    z = _project(h_bf, w_in_ref, _OFF_GC, CONV_CH) * _project(h_bf, w_in_ref, _OFF_XC, CONV_CH)
    zbuf[SUBLANES:SUBLANES + tb, :] = z
    zwin_ref[0] = z[tb - SUBLANES:tb]
    z1 = zbuf[SUBLANES - 1:SUBLANES - 1 + tb, :]
    z2 = zbuf[SUBLANES - 2:SUBLANES - 2 + tb, :]
    conv = cb_ref[...] + cw_ref[0:1, :] * z2 + cw_ref[1:2, :] * z1 + cw_ref[2:3, :] * z
    y_conv = _project(h_bf, w_in_ref, _OFF_GB, CONV_CH) * conv
    zbuf[0:SUBLANES, :] = zbuf[tb:tb + SUBLANES, :]

    x1, h2, topi, topg = _merge_and_route(x, o_bf, y_conv, h_bf, mod, n2g_ref, w_in_ref, w_ao_ref, w_co_ref,
                                          w_o_ref, wr_hi_ref, wr_lo_ref, br_ref)
    x1_ref[0] = x1
    h2_ref[0] = h2.astype(BF16)
    topi_ref[0] = topi
    topg_ref[0] = topg


def _prompt_mixer(x, mod, rope, sinks, weights, tb):
    nb, seq, _ = x.shape
    (n1g, n2g, w_in, bd, qg, kg, w_ao, cw, cb, w_co, w_o, wr_hi, wr_lo, br) = weights
    consts = (n1g, n2g, w_in, bd, qg, kg, w_ao, cw, cb, w_co, w_o, wr_hi, wr_lo, br)
    in_specs = [
        pl.BlockSpec((1, tb, D_MODEL), lambda b, j, s: (b, j, 0)),
        pl.BlockSpec((1, 6, D_MODEL), lambda b, j, s: (b, 0, 0)),
        pl.BlockSpec((tb, 3 * LANES), lambda b, j, s: (j, 0)),
    ] + [_const_spec(w.shape) for w in consts]
    out_specs = [
        pl.BlockSpec((1, tb, D_MODEL), lambda b, j, s: (b, j, 0)),
        pl.BlockSpec((1, tb, D_MODEL), lambda b, j, s: (b, j, 0)),
        pl.BlockSpec((1, SUBLANES, tb), lambda b, j, s: (b, 0, j)),
        pl.BlockSpec((1, SUBLANES, tb), lambda b, j, s: (b, 0, j)),
        pl.BlockSpec((1, WINDOW, KV_W), lambda b, j, s: (b, 0, 0)),
        pl.BlockSpec((1, WINDOW, KV_W), lambda b, j, s: (b, 0, 0)),
        pl.BlockSpec((1, SUBLANES, CONV_CH), lambda b, j, s: (b, 0, 0)),
    ]
    out_shape = [
        jax.ShapeDtypeStruct((nb, seq, D_MODEL), F32),
        jax.ShapeDtypeStruct((nb, seq, D_MODEL), BF16),
        jax.ShapeDtypeStruct((nb, SUBLANES, seq), jnp.int32),
        jax.ShapeDtypeStruct((nb, SUBLANES, seq), F32),
        jax.ShapeDtypeStruct((nb, WINDOW, KV_W), F32),
        jax.ShapeDtypeStruct((nb, WINDOW, KV_W), F32),
        jax.ShapeDtypeStruct((nb, SUBLANES, CONV_CH), F32),
    ]
    return pl.pallas_call(
        functools.partial(_prompt_body, tb=tb),
        grid_spec=pltpu.PrefetchScalarGridSpec(
            num_scalar_prefetch=1, grid=(nb, seq // tb),
            in_specs=in_specs, out_specs=out_specs,
            scratch_shapes=[pltpu.VMEM((tb + WINDOW, KV_W), F32),
                            pltpu.VMEM((tb + WINDOW, KV_W), F32),
                            pltpu.VMEM((tb + SUBLANES, CONV_CH), F32)]),
        out_shape=out_shape,
        compiler_params=pltpu.CompilerParams(
            dimension_semantics=("arbitrary", "arbitrary"), vmem_limit_bytes=VMEM_LIMIT),
        name="prompt_mixer",
    )(sinks, x, mod, rope, *consts)


def _sample_body(sinks_ref, x_ref, mod_ref, rope_ref, ck_ref, cv_ref, st_ref,
                 n1g_ref, n2g_ref, w_in_ref, bd_ref, qg_ref, kg_ref,
                 w_ao_ref, cw_ref, cb_ref, w_co_ref, w_o_ref, wr_hi_ref, wr_lo_ref, br_ref,
                 x1_ref, h2_ref, topi_ref, topg_ref, knew_ref, vnew_ref, z_ref, *, nseq, dseq):
    rows = nseq * dseq
    wbuf = ck_ref.shape[1]
    x = x_ref[...]
    mod = tuple(mod_ref[i] for i in range(6))
    h_bf = _modulated_norm(x, n1g_ref[...], mod[0], mod[1]).astype(BF16)
    rope = rope_ref[...]
    bd = bd_ref[...]

    k = _head_norm_rope(_project(h_bf, w_in_ref, _OFF_K, KV_W), bd, kg_ref[...], rope)
    v = _project(h_bf, w_in_ref, _OFF_V, KV_W)
    knew_ref[...] = k
    vnew_ref[...] = v
    kn_exp = _pair_expand(k)
    vn_exp = _pair_expand(v)
    kc_exp = _pair_expand(ck_ref[...].reshape(nseq * wbuf, KV_W))
    vc_exp = _pair_expand(cv_ref[...].reshape(nseq * wbuf, KV_W))

    qrow = lax.broadcasted_iota(jnp.int32, (rows, nseq * wbuf), 0)
    ccol = lax.broadcasted_iota(jnp.int32, (rows, nseq * wbuf), 1)
    mask_c = ((ccol // wbuf) == (qrow // dseq)) & ((ccol % wbuf) > (qrow % dseq) + (wbuf - WINDOW))
    qrow_n = lax.broadcasted_iota(jnp.int32, (rows, rows), 0)
    ncol = lax.broadcasted_iota(jnp.int32, (rows, rows), 1)
    mask_n = ((ncol // dseq) == (qrow_n // dseq)) & ((ncol % dseq) <= (qrow_n % dseq))

    o_tiles = []
    for t in range(Q_W // LANES):
        c = t // 2
        qt = _head_norm_rope(_project(h_bf, w_in_ref, _OFF_Q + t * LANES, LANES), bd, qg_ref[...], rope)
        qt = (qt * ATTN_SCALE).astype(BF16)
        o_t = None
        for half in range(2):
            s_c = jnp.where(mask_c, _dot_nt(qt, kc_exp[c][half]), NEG_INF)
            s_n = jnp.where(mask_n, _dot_nt(qt, kn_exp[c][half]), NEG_INF)
            sink = sinks_ref[c * GROUP + (t % 2) * 2 + half]
            part = _softmax_pv([s_c, s_n], [vc_exp[c][half], vn_exp[c][half]], sink)
            o_t = part if o_t is None else o_t + part
        o_tiles.append(o_t.astype(BF16))
    o_bf = jnp.concatenate(o_tiles, axis=1)

    z = _project(h_bf, w_in_ref, _OFF_GC, CONV_CH) * _project(h_bf, w_in_ref, _OFF_XC, CONV_CH)
    z_ref[...] = z
    r = lax.broadcasted_iota(jnp.int32, z.shape, 0) % dseq
    st0 = st_ref[0]
    st1 = st_ref[1]
    z1 = jnp.where(r == 0, st1, pltpu.roll(z, 1, 0))
    z2 = jnp.where(r == 0, st0, jnp.where(r == 1, st1, pltpu.roll(z, 2, 0)))
    conv = cb_ref[...] + cw_ref[0:1, :] * z2 + cw_ref[1:2, :] * z1 + cw_ref[2:3, :] * z
    y_conv = _project(h_bf, w_in_ref, _OFF_GB, CONV_CH) * conv

    x1, h2, topi, topg = _merge_and_route(x, o_bf, y_conv, h_bf, mod, n2g_ref, w_in_ref, w_ao_ref, w_co_ref,
                                          w_o_ref, wr_hi_ref, wr_lo_ref, br_ref)
    x1_ref[...] = x1
    h2_ref[...] = h2.astype(BF16)
    topi_ref[...] = topi
    topg_ref[...] = topg


def _sample_mixer(x, mod, rope, cache_k, cache_v, state, sinks, weights, nseq):
    ntok = x.shape[0]
    nall, wbuf, _ = cache_k.shape
    dseq = ntok // nall
    rows = nseq * dseq
    consts = weights
    in_specs = [
        pl.BlockSpec((rows, D_MODEL), lambda i, s: (i, 0)),
        pl.BlockSpec((6, rows, D_MODEL), lambda i, s: (0, i, 0)),
        pl.BlockSpec((rows, 3 * LANES), lambda i, s: (i, 0)),
        pl.BlockSpec((nseq, wbuf, KV_W), lambda i, s: (i, 0, 0)),
        pl.BlockSpec((nseq, wbuf, KV_W), lambda i, s: (i, 0, 0)),
        pl.BlockSpec((2, rows, CONV_CH), lambda i, s: (0, i, 0)),
    ] + [_const_spec(w.shape) for w in consts]
    out_specs = [
        pl.BlockSpec((rows, D_MODEL), lambda i, s: (i, 0)),
        pl.BlockSpec((rows, D_MODEL), lambda i, s: (i, 0)),
        pl.BlockSpec((SUBLANES, rows), lambda i, s: (0, i)),
        pl.BlockSpec((SUBLANES, rows), lambda i, s: (0, i)),
        pl.BlockSpec((rows, KV_W), lambda i, s: (i, 0)),
        pl.BlockSpec((rows, KV_W), lambda i, s: (i, 0)),
        pl.BlockSpec((rows, CONV_CH), lambda i, s: (i, 0)),
    ]
    out_shape = [
        jax.ShapeDtypeStruct((ntok, D_MODEL), F32),
        jax.ShapeDtypeStruct((ntok, D_MODEL), BF16),
        jax.ShapeDtypeStruct((SUBLANES, ntok), jnp.int32),
        jax.ShapeDtypeStruct((SUBLANES, ntok), F32),
        jax.ShapeDtypeStruct((ntok, KV_W), F32),
        jax.ShapeDtypeStruct((ntok, KV_W), F32),
        jax.ShapeDtypeStruct((ntok, CONV_CH), F32),
    ]
    return pl.pallas_call(
        functools.partial(_sample_body, nseq=nseq, dseq=dseq),
        grid_spec=pltpu.PrefetchScalarGridSpec(
            num_scalar_prefetch=1, grid=(ntok // rows,),
            in_specs=in_specs, out_specs=out_specs),
        out_shape=out_shape,
        compiler_params=pltpu.CompilerParams(
            dimension_semantics=("arbitrary",), vmem_limit_bytes=VMEM_LIMIT),
        name="sample_mixer",
    )(sinks, x, mod, rope, cache_k, cache_v, state, *consts)


SORT_TOKENS = 256
RUN_ALIGN = SUBLANES
EXPERT_TILE = 256
HALF_D = D_MODEL // 2
LOCAL_ROWS = -(-(SORT_TOKENS * TOP_K + N_EXPERTS * (RUN_ALIGN - 1)) // LANES) * LANES
_HI_MASK = 0xFFFF0000


def _pack_halves(x):
    lo = pltpu.bitcast(x[:, :HALF_D], jnp.uint32)
    hi = pltpu.bitcast(x[:, HALF_D:], jnp.uint32)
    return (hi & jnp.uint32(_HI_MASK)) | (lo >> 16)


def _unpack_halves(w):
    lo = pltpu.bitcast(w << 16, F32).astype(BF16)
    hi = pltpu.bitcast(w & jnp.uint32(_HI_MASK), F32).astype(BF16)
    return lo, hi


def _local_positions(topi, lstart_col, tri):
    ntok = topi.shape[1]
    eid = lax.broadcasted_iota(jnp.int32, (N_EXPERTS, ntok), 0)
    hits = [eid == topi[k:k + 1, :] for k in range(TOP_K)]
    chosen = jnp.zeros((N_EXPERTS, ntok), F32)
    for h in hits:
        chosen = jnp.where(h, 1.0, chosen)
    base = _dot(chosen.astype(BF16), tri) + lstart_col
    return [jnp.sum(jnp.where(h, base, 0.0), axis=0, keepdims=True) for h in hits]


def _run_copy(loc, glob, sem, lofs, gofs, n, to_global):
    lo = loc.at[pl.ds(pl.multiple_of(lofs, RUN_ALIGN), n)]
    gl = glob.at[pl.ds(pl.multiple_of(gofs, RUN_ALIGN), n)]
    return pltpu.make_async_copy(lo, gl, sem) if to_global else pltpu.make_async_copy(gl, lo, sem)


def _start_runs(meta, blk, loc, glob, sem, to_global):
    lstart_ref, goff_ref, cpad_ref = meta

    def issue(e, carry):
        n = pl.multiple_of(cpad_ref[blk * N_EXPERTS + e], RUN_ALIGN)

        @pl.when(n > 0)
        def _():
            _run_copy(loc, glob, sem, lstart_ref[blk * N_EXPERTS + e], goff_ref[blk * N_EXPERTS + e], n,
                      to_global).start()
        return carry
    lax.fori_loop(0, N_EXPERTS, issue, 0)


def _wait_runs(nrows, loc, glob, sem, to_global):
    n = pl.multiple_of(nrows, RUN_ALIGN)

    @pl.when(n > 0)
    def _():
        _run_copy(loc, glob, sem, 0, 0, n, to_global).wait()


def _sort_body(lstart_ref, goff_ref, cpad_ref, ltot_ref, fstart_ref, flen_ref, nact_ref,
               hp_ref, hs_ref, topi_ref, lcol_ref, tri_ref, xs_hbm, buf, zbuf, sems, fsem, *, nblk_p, nblk):
    i = pl.program_id(0)
    slot = i % 2
    meta = (lstart_ref, goff_ref, cpad_ref)

    @pl.when(i == 0)
    def _():
        zbuf[...] = jnp.zeros(zbuf.shape, jnp.uint32)

        def fill(e, tot):
            n = pl.multiple_of(flen_ref[e], RUN_ALIGN)

            @pl.when(n > 0)
            def _():
                _run_copy(zbuf, xs_hbm, fsem, 0, fstart_ref[e], n, True).start()
            return tot + n
        total = lax.fori_loop(0, N_EXPERTS, fill, 0)
        _wait_runs(total, zbuf, xs_hbm, fsem, True)

        def fill_tile(t, carry):
            cp = _run_copy(zbuf, xs_hbm, fsem, 0, t * EXPERT_TILE, EXPERT_TILE, True)
            cp.start()
            cp.wait()
            return carry
        lax.fori_loop(nact_ref[0], xs_hbm.shape[0] // EXPERT_TILE, fill_tile, 0)

    def run(h_ref):
        pos = _local_positions(topi_ref[...], lcol_ref[...], tri_ref[...])
        rowid = lax.broadcasted_iota(jnp.int32, (LOCAL_ROWS, SORT_TOKENS), 0)
        onehot = jnp.zeros((LOCAL_ROWS, SORT_TOKENS), F32)
        for p in pos:
            onehot = jnp.where(rowid == p.astype(jnp.int32), 1.0, onehot)
        buf[slot] = _pack_halves(_dot(onehot.astype(BF16), h_ref[...]))

    @pl.when(i < nblk_p)
    def _():
        run(hp_ref)

    @pl.when(i >= nblk_p)
    def _():
        run(hs_ref)

    _start_runs(meta, i, buf.at[slot], xs_hbm, sems.at[slot], True)

    @pl.when(i > 0)
    def _():
        _wait_runs(ltot_ref[jnp.maximum(i - 1, 0)], buf.at[1 - slot], xs_hbm, sems.at[1 - slot], True)

    @pl.when(i == nblk - 1)
    def _():
        _wait_runs(ltot_ref[i], buf.at[slot], xs_hbm, sems.at[slot], True)


def _sort_tokens(plan, h_p, h_s, topi_all, tri, n_slots):
    nblk_p = h_p.shape[0] // SORT_TOKENS
    nblk = nblk_p + h_s.shape[0] // SORT_TOKENS
    return pl.pallas_call(
        functools.partial(_sort_body, nblk_p=nblk_p, nblk=nblk),
        grid_spec=pltpu.PrefetchScalarGridSpec(
            num_scalar_prefetch=7, grid=(nblk,),
            in_specs=[pl.BlockSpec((SORT_TOKENS, D_MODEL), lambda i, *_: (jnp.minimum(i, nblk_p - 1), 0)),
                      pl.BlockSpec((SORT_TOKENS, D_MODEL), lambda i, *_: (jnp.maximum(i - nblk_p, 0), 0)),
                      pl.BlockSpec((SUBLANES, SORT_TOKENS), lambda i, *_: (0, i)),
                      pl.BlockSpec((None, N_EXPERTS, 1), lambda i, *_: (i, 0, 0)),
                      _const_spec(tri.shape)],
            out_specs=pl.BlockSpec(memory_space=pl.ANY),
            scratch_shapes=[pltpu.VMEM((2, LOCAL_ROWS, HALF_D), jnp.uint32),
                            pltpu.VMEM((EXPERT_TILE, HALF_D), jnp.uint32),
                            pltpu.SemaphoreType.DMA((2,)),
                            pltpu.SemaphoreType.DMA]),
        out_shape=jax.ShapeDtypeStruct((n_slots, HALF_D), jnp.uint32),
        compiler_params=pltpu.CompilerParams(
            dimension_semantics=("arbitrary",), vmem_limit_bytes=VMEM_LIMIT),
        name="moe_sort",
    )(plan["lstart"], plan["goff"], plan["cpad"], plan["ltot"], plan["fill_start"], plan["fill_len"], plan["nact"],
      h_p, h_s, topi_all, plan["lstart_col"], tri)


def _experts_body(tile_exp_ref, nact_ref, xs_ref, wg_ref, bg_ref, wu_ref, bu_ref, wd_ref, bd_ref,
                  ys_ref, wg_bf, wu_bf, wd_bf):
    t = pl.program_id(0)
    nact = nact_ref[0]

    @pl.when(t < nact)
    def _():
        prev = tile_exp_ref[jnp.maximum(t - 1, 0)]

        @pl.when((t == 0) | (tile_exp_ref[t] != prev))
        def _():
            wg_bf[...] = wg_ref[0].astype(BF16)
            wu_bf[...] = wu_ref[0].astype(BF16)
            wd_bf[...] = wd_ref[0].astype(BF16)

        x_lo, x_hi = _unpack_halves(xs_ref[...])
        g = _dot(x_lo, wg_bf[:HALF_D, :]) + _dot(x_hi, wg_bf[HALF_D:, :]) + bg_ref[0]
        u = _dot(x_lo, wu_bf[:HALF_D, :]) + _dot(x_hi, wu_bf[HALF_D:, :]) + bu_ref[0]
        g = jnp.minimum(g, SWIGLU_LIMIT)
        u = jnp.clip(u, -SWIGLU_LIMIT, SWIGLU_LIMIT)
        a = g * _sigmoid(SWIGLU_ALPHA * g) * (u + 1.0)
        out = _dot(a.astype(BF16), wd_bf[...]) + bd_ref[0]
        ys_ref[...] = _pack_halves(out.astype(BF16).astype(F32))

    @pl.when(t >= nact)
    def _():
        ys_ref[...] = jnp.zeros(ys_ref.shape, jnp.uint32)


def _experts(tile_exp, nact, xs, wg, bg, wu, bu, wd, bd):
    n_tiles = tile_exp.shape[0]

    def active(t, na):
        return jnp.minimum(t, na[0] - 1)
    wspec = pl.BlockSpec((1, D_MODEL, D_MODEL), lambda t, te, na: (te[active(t, na)], 0, 0))
    bspec = pl.BlockSpec((1, 1, D_MODEL), lambda t, te, na: (te[active(t, na)], 0, 0))
    xspec = pl.BlockSpec((EXPERT_TILE, HALF_D), lambda t, te, na: (active(t, na), 0))
    return pl.pallas_call(
        _experts_body,
        grid_spec=pltpu.PrefetchScalarGridSpec(
            num_scalar_prefetch=2, grid=(n_tiles,),
            in_specs=[xspec, wspec, bspec, wspec, bspec, wspec, bspec],
            out_specs=pl.BlockSpec((EXPERT_TILE, HALF_D), lambda t, te, na: (t, 0)),
            scratch_shapes=[pltpu.VMEM((D_MODEL, D_MODEL), BF16)] * 3),
        out_shape=jax.ShapeDtypeStruct(xs.shape, jnp.uint32),
        compiler_params=pltpu.CompilerParams(
            dimension_semantics=("arbitrary",), vmem_limit_bytes=VMEM_LIMIT),
        name="experts",
    )(tile_exp, nact, xs, wg, bg, wu, bu, wd, bd)


def _unsort_body(lstart_ref, goff_ref, cpad_ref, ltot_ref,
                 ys_hbm, x1_ref, g2_ref, topi_ref, topg_ref, lcol_ref, tri_ref, y_ref, buf, sems, *, blk0, nblk):
    i = pl.program_id(0)
    b = blk0 + i
    slot = i % 2
    meta = (lstart_ref, goff_ref, cpad_ref)

    @pl.when(i == 0)
    def _():
        buf[...] = jnp.zeros(buf.shape, jnp.uint32)
        _start_runs(meta, b, buf.at[0], ys_hbm, sems.at[0], False)

    @pl.when(i + 1 < nblk)
    def _():
        _start_runs(meta, b + 1, buf.at[1 - slot], ys_hbm, sems.at[1 - slot], False)

    pos = _local_positions(topi_ref[...], lcol_ref[...], tri_ref[...])
    rowid = lax.broadcasted_iota(jnp.int32, (SUBLANES, SORT_TOKENS), 0)
    pos_rows = jnp.zeros((SUBLANES, SORT_TOKENS), F32)
    for k in range(TOP_K):
        pos_rows = jnp.where(rowid == k, pos[k], pos_rows)
    pos_cols = pos_rows.T.astype(jnp.int32)
    gate_cols = topg_ref[...].T
    colid = lax.broadcasted_iota(jnp.int32, (SORT_TOKENS, LOCAL_ROWS), 1)
    weights = jnp.zeros((SORT_TOKENS, LOCAL_ROWS), F32)
    for k in range(TOP_K):
        weights = jnp.where(colid == pos_cols[:, k:k + 1], gate_cols[:, k:k + 1], weights)
    weights = weights.astype(BF16)

    _wait_runs(ltot_ref[b], buf.at[slot], ys_hbm, sems.at[slot], False)
    y_lo, y_hi = _unpack_halves(buf[slot])
    g2 = g2_ref[...]
    y_ref[:, :HALF_D] = x1_ref[:, :HALF_D] + g2[:, :HALF_D] * _dot(weights, y_lo)
    y_ref[:, HALF_D:] = x1_ref[:, HALF_D:] + g2[:, HALF_D:] * _dot(weights, y_hi)


def _unsort_tokens(plan, ys, x1, g2, g2_spec, topi_all, topg_all, tri, blk0):
    nblk = x1.shape[0] // SORT_TOKENS
    return pl.pallas_call(
        functools.partial(_unsort_body, blk0=blk0, nblk=nblk),
        grid_spec=pltpu.PrefetchScalarGridSpec(
            num_scalar_prefetch=4, grid=(nblk,),
            in_specs=[pl.BlockSpec(memory_space=pl.ANY),
                      pl.BlockSpec((SORT_TOKENS, D_MODEL), lambda i, *_: (i, 0)),
                      g2_spec,
                      pl.BlockSpec((SUBLANES, SORT_TOKENS), lambda i, *_: (0, blk0 + i)),
                      pl.BlockSpec((SUBLANES, SORT_TOKENS), lambda i, *_: (0, blk0 + i)),
                      pl.BlockSpec((None, N_EXPERTS, 1), lambda i, *_: (blk0 + i, 0, 0)),
                      _const_spec(tri.shape)],
            out_specs=pl.BlockSpec((SORT_TOKENS, D_MODEL), lambda i, *_: (i, 0)),
            scratch_shapes=[pltpu.VMEM((2, LOCAL_ROWS, HALF_D), jnp.uint32),
                            pltpu.SemaphoreType.DMA((2,))]),
        out_shape=jax.ShapeDtypeStruct(x1.shape, F32),
        compiler_params=pltpu.CompilerParams(
            dimension_semantics=("arbitrary",), vmem_limit_bytes=VMEM_LIMIT),
        name="moe_unsort",
    )(plan["lstart"], plan["goff"], plan["cpad"], plan["ltot"],
      ys, x1, g2, topi_all, topg_all, plan["lstart_col"], tri)


def _rope_table(pos):
    half = ROT_DIM // 2
    inv_freq = ROPE_THETA ** (-jnp.arange(0, ROT_DIM, 2, dtype=F32) / ROT_DIM)
    ang = pos.astype(F32)[:, None] * inv_freq[None, :]
    cos, sin = jnp.cos(ang), jnp.sin(ang)
    n = pos.shape[0]
    rest = HEAD_DIM - ROT_DIM
    c = jnp.concatenate([cos, cos, jnp.ones((n, rest), F32)], axis=1)
    s_next = jnp.concatenate([-sin, jnp.zeros((n, half + rest), F32)], axis=1)
    s_prev = jnp.concatenate([jnp.zeros((n, half), F32), sin, jnp.zeros((n, rest), F32)], axis=1)
    rep = LANES // HEAD_DIM
    return jnp.concatenate([jnp.tile(c, (1, rep)), jnp.tile(s_next, (1, rep)), jnp.tile(s_prev, (1, rep))], axis=1)


def _num_expert_tiles(ntok):
    nblk = ntok // SORT_TOKENS
    worst = ntok * TOP_K + nblk * N_EXPERTS * (RUN_ALIGN - 1) + N_EXPERTS * (EXPERT_TILE - 1)
    return -(-worst // EXPERT_TILE)


def _routing_plan(topi_all):
    ntok = topi_all.shape[1]
    nblk = ntok // SORT_TOKENS
    n_tiles = _num_expert_tiles(ntok)
    choice = topi_all[:TOP_K].reshape(TOP_K, nblk, SORT_TOKENS)
    experts = jnp.arange(N_EXPERTS, dtype=jnp.int32)
    cnt = (choice[:, :, :, None] == experts).astype(jnp.int32).sum(axis=(0, 2))
    cpad = -(-cnt // RUN_ALIGN) * RUN_ALIGN
    lstart = jnp.cumsum(cpad, axis=1) - cpad
    tot = cpad.sum(axis=0)
    region = -(-tot // EXPERT_TILE) * EXPERT_TILE
    region_end = jnp.cumsum(region)
    gstart = region_end - region
    goff = gstart[None, :] + jnp.cumsum(cpad, axis=0) - cpad
    tile_start = jnp.arange(n_tiles, dtype=jnp.int32) * EXPERT_TILE
    tile_exp = jnp.minimum(jnp.searchsorted(region_end, tile_start, side='right'), N_EXPERTS - 1)
    i32 = lambda a: a.astype(jnp.int32)
    return dict(
        lstart=i32(lstart).reshape(-1), goff=i32(goff).reshape(-1), cpad=i32(cpad).reshape(-1),
        ltot=i32(cpad.sum(axis=1)), fill_start=i32(gstart + tot), fill_len=i32(region - tot),
        lstart_col=lstart.astype(F32).reshape(nblk, N_EXPERTS, 1),
        tile_exp=i32(tile_exp), nact=i32(region_end[-1] // EXPERT_TILE).reshape(1))


def kernel(x_prompt, x_sample, cache_k_win, cache_v_win, state_conv, c_prompt, c_sample, w_ada, b_ada, norm1_g, norm2_g, w_in, q_norm_g, k_norm_g, attn_sinks, w_attn_out, conv_w, conv_b, w_conv_out, w_o, w_router, b_router, w_gate, b_gate, w_up, b_up, w_down, b_down):
    nb, seq, _ = x_prompt.shape
    ns, dseq, _ = x_sample.shape
    wbuf = cache_k_win.shape[2]
    assert w_ada.shape[0] == 1, "single-layer step"
    l = 0

    mod = _adaln(jnp.concatenate([c_prompt, c_sample], axis=0), w_ada[l], b_ada[l])
    mod_p = mod[:nb].reshape(nb, 6, D_MODEL)
    mod_s = jnp.repeat(mod[nb:].reshape(ns, 6, D_MODEL).transpose(1, 0, 2), dseq, axis=1)

    bd = jnp.asarray(np.kron(np.eye(LANES // HEAD_DIM), np.ones((HEAD_DIM, HEAD_DIM))), BF16)
    rep = LANES // HEAD_DIM
    wr_t = w_router[l].T
    wr_hi = wr_t.astype(BF16)
    wr_lo = (wr_t - wr_hi.astype(F32)).astype(BF16)
    weights = (norm1_g[l].reshape(1, D_MODEL), norm2_g[l].reshape(1, D_MODEL), w_in[l].astype(BF16), bd,
               jnp.tile(q_norm_g[l], rep).reshape(1, LANES), jnp.tile(k_norm_g[l], rep).reshape(1, LANES),
               w_attn_out[l].astype(BF16), conv_w[l], conv_b[l].reshape(1, CONV_CH), w_conv_out[l].astype(BF16),
               w_o[l].astype(BF16), wr_hi, wr_lo, b_router[l].reshape(N_EXPERTS, 1))
    sinks = attn_sinks[l]

    tb = 256
    (x1_p, h2_p, topi_p, topg_p, kwin_p, vwin_p, zwin_p) = _prompt_mixer(
        x_prompt, mod_p, _rope_table(jnp.arange(seq)), sinks, weights, tb)
    pos_s = jnp.tile(PAST_LEN + jnp.arange(dseq), ns)
    state_rows = jnp.repeat(state_conv[l].transpose(1, 0, 2), dseq, axis=1)
    (x1_s, h2_s, topi_s, topg_s, knew, vnew, z_s) = _sample_mixer(
        x_sample.reshape(ns * dseq, D_MODEL), mod_s, _rope_table(pos_s),
        cache_k_win[l].reshape(ns, wbuf, KV_W), cache_v_win[l].reshape(ns, wbuf, KV_W),
        state_rows, sinks, weights, 16)

    ntok_p = nb * seq
    ntok = ntok_p + ns * dseq
    topi_all = jnp.concatenate([topi_p.transpose(1, 0, 2).reshape(SUBLANES, ntok_p), topi_s], axis=1)
    topg_all = jnp.concatenate([topg_p.transpose(1, 0, 2).reshape(SUBLANES, ntok_p), topg_s], axis=1)
    plan = _routing_plan(topi_all)
    tri = jnp.asarray(np.triu(np.ones((SORT_TOKENS, SORT_TOKENS)), k=1), BF16)
    xs = _sort_tokens(plan, h2_p.reshape(ntok_p, D_MODEL), h2_s, topi_all, tri,
                      _num_expert_tiles(ntok) * EXPERT_TILE)
    ys = _experts(plan["tile_exp"], plan["nact"], xs,
                  w_gate[l], b_gate[l].reshape(N_EXPERTS, 1, D_MODEL),
                  w_up[l], b_up[l].reshape(N_EXPERTS, 1, D_MODEL),
                  w_down[l], b_down[l].reshape(N_EXPERTS, 1, D_MODEL))
    blocks_per_seq = seq // SORT_TOKENS
    y_p = _unsort_tokens(plan, ys, x1_p.reshape(ntok_p, D_MODEL), mod_p[:, 5:6, :],
                         pl.BlockSpec((None, 1, D_MODEL), lambda i, *_: (i // blocks_per_seq, 0, 0)),
                         topi_all, topg_all, tri, 0)
    y_s = _unsort_tokens(plan, ys, x1_s, mod_s[5],
                         pl.BlockSpec((SORT_TOKENS, D_MODEL), lambda i, *_: (i, 0)),
                         topi_all, topg_all, tri, ntok_p // SORT_TOKENS)

    n_kv_shape = (N_KV, HEAD_DIM)
    k_win_p = kwin_p.reshape(1, nb, WINDOW, *n_kv_shape)
    v_win_p = vwin_p.reshape(1, nb, WINDOW, *n_kv_shape)
    conv_p = zwin_p[:, SUBLANES - (CONV_K - 1):, :][None]
    k_win_s = jnp.concatenate([cache_k_win[l][:, dseq:], knew.reshape(ns, dseq, *n_kv_shape)], axis=1)[None]
    v_win_s = jnp.concatenate([cache_v_win[l][:, dseq:], vnew.reshape(ns, dseq, *n_kv_shape)], axis=1)[None]
    conv_s = z_s.reshape(ns, dseq, CONV_CH)[:, dseq - (CONV_K - 1):, :][None]
    return (y_p.reshape(nb, seq, D_MODEL), y_s.reshape(ns, dseq, D_MODEL),
            k_win_p, v_win_p, conv_p, k_win_s, v_win_s, conv_s)
```

```python
import functools

import numpy as np
import jax
import jax.numpy as jnp
from jax import lax
from jax.experimental import pallas as pl
from jax.experimental.pallas import tpu as pltpu

D_MODEL = 1024
HEAD_DIM = 64
N_HEADS = 8
N_KV = 2
GROUP = N_HEADS // N_KV
Q_W = N_HEADS * HEAD_DIM
KV_W = N_KV * HEAD_DIM
WINDOW = 128
ROT_DIM = HEAD_DIM // 4
ROPE_THETA = 500000.0
ATTN_SCALE = HEAD_DIM ** -0.5
CONV_CH = D_MODEL // 2
CONV_K = 3
N_EXPERTS = 32
TOP_K = 4
SWIGLU_ALPHA = 1.702
SWIGLU_LIMIT = 7.0
MOE_BLOCK = 128
NORM_EPS = 1e-5
QK_EPS = 1e-6
NEG_INF = -1e30
PAST_LEN = 16384

LANES = 128
SUBLANES = 8
VMEM_LIMIT = 56 * 1024 * 1024

_OFF_Q = 0
_OFF_K = _OFF_Q + Q_W
_OFF_V = _OFF_K + KV_W
_OFF_GB = _OFF_V + KV_W
_OFF_GC = _OFF_GB + CONV_CH
_OFF_XC = _OFF_GC + CONV_CH
_OFF_BA = _OFF_XC + CONV_CH
_OFF_BC = _OFF_BA + D_MODEL
IN_COLS = _OFF_BC + D_MODEL

BF16 = jnp.bfloat16
F32 = jnp.float32
_NT = (((1,), (1,)), ((), ()))


def _dot(a, b):
    return jnp.dot(a, b, preferred_element_type=F32)


def _dot_nt(a, b):
    return lax.dot_general(a, b, _NT, preferred_element_type=F32)


def _sigmoid(x):
    return 1.0 / (1.0 + jnp.exp(-x))


def _split_bf16(x):
    hi = x.astype(BF16)
    lo = (x - hi.astype(F32)).astype(BF16)
    return hi, lo


def _const_spec(shape):
    nd = len(shape)
    return pl.BlockSpec(shape, lambda *_: (0,) * nd, pipeline_mode=pl.Buffered(1))


def _adaln_body(c_ref, w_ref, b_ref, o_ref):
    c = c_ref[...]
    s = c * _sigmoid(c)
    o_ref[...] = _dot(s.astype(BF16), w_ref[...].astype(BF16)) + b_ref[...]


def _adaln(c, w_ada, b_ada):
    n = c.shape[0]
    cols = w_ada.shape[1]
    bn = 1536
    return pl.pallas_call(
        _adaln_body,
        grid=(cols // bn,),
        in_specs=[pl.BlockSpec((n, D_MODEL), lambda i: (0, 0)),
                  pl.BlockSpec((D_MODEL, bn), lambda i: (0, i)),
                  pl.BlockSpec((1, bn), lambda i: (0, i))],
        out_specs=pl.BlockSpec((n, bn), lambda i: (0, i)),
        out_shape=jax.ShapeDtypeStruct((n, cols), F32),
        compiler_params=pltpu.CompilerParams(vmem_limit_bytes=VMEM_LIMIT),
        name="adaln",
    )(c, w_ada, b_ada.reshape(1, cols))


def _modulated_norm(x, g, shift, scale):
    y = x * lax.rsqrt(jnp.mean(x * x, axis=-1, keepdims=True) + NORM_EPS)
    return (y * g) * (1.0 + scale) + shift


def _head_norm_rope(t, bd, g, rope):
    sq_hi, sq_lo = _split_bf16(t * t)
    ms = (_dot(sq_hi, bd) + _dot(sq_lo, bd)) * (1.0 / HEAD_DIM)
    y = t * lax.rsqrt(ms + QK_EPS) * g
    c = rope[:, 0:LANES]
    s_next = rope[:, LANES:2 * LANES]
    s_prev = rope[:, 2 * LANES:3 * LANES]
    half = ROT_DIM // 2
    return y * c + pltpu.roll(y, LANES - half, 1) * s_next + pltpu.roll(y, half, 1) * s_prev


def _pair_expand(t):
    lane = lax.broadcasted_iota(jnp.int32, t.shape, 1)
    lo = lane < HEAD_DIM
    r = pltpu.roll(t, HEAD_DIM, 1)
    zero = jnp.zeros_like(t)
    a0 = jnp.where(lo, t, zero).astype(BF16)
    b0 = jnp.where(lo, zero, r).astype(BF16)
    a1 = jnp.where(lo, r, zero).astype(BF16)
    b1 = jnp.where(lo, zero, t).astype(BF16)
    return ((a0, b0), (a1, b1))


def _softmax_pv(s_list, v_list, sink):
    m = jnp.full((s_list[0].shape[0], 1), sink, F32)
    for s in s_list:
        m = jnp.maximum(m, jnp.max(s, axis=-1, keepdims=True))
    den = jnp.exp(sink - m)
    acc = None
    for s, v in zip(s_list, v_list):
        e = jnp.exp(s - m)
        den = den + jnp.sum(e, axis=-1, keepdims=True)
        pv = _dot(e.astype(BF16), v)
        acc = pv if acc is None else acc + pv
    return acc * (1.0 / den)


def _route(h2, wr_hi, wr_lo, br):
    hi, lo = _split_bf16(h2)
    logits = _dot_nt(wr_hi, hi) + _dot_nt(wr_hi, lo) + _dot_nt(wr_lo, hi) + br
    eid = lax.broadcasted_iota(jnp.int32, logits.shape, 0).astype(F32)
    vals, idxs = [], []
    for _ in range(TOP_K):
        m = jnp.max(logits, axis=0, keepdims=True)
        idx = jnp.min(jnp.where(logits == m, eid, float(N_EXPERTS)), axis=0, keepdims=True)
        vals.append(m)
        idxs.append(idx)
        logits = jnp.where(eid == idx, -jnp.inf, logits)
    ex = [jnp.exp(v - vals[0]) for v in vals]
    tot = ex[0] + ex[1] + ex[2] + ex[3]
    inv = 1.0 / tot
    gates = [e * inv for e in ex]
    rowid = lax.broadcasted_iota(jnp.int32, (SUBLANES, h2.shape[0]), 0)
    topi = jnp.zeros((SUBLANES, h2.shape[0]), F32)
    topg = jnp.zeros((SUBLANES, h2.shape[0]), F32)
    for k in range(TOP_K):
        topi = jnp.where(rowid == k, idxs[k], topi)
        topg = jnp.where(rowid == k, gates[k], topg)
    return topi.astype(jnp.int32), topg


def _project(h_bf, w_in_ref, off, width):
    return _dot(h_bf, w_in_ref[:, off:off + width])


def _merge_and_route(x, o_bf, y_conv, h_bf, mod, n2g_ref, w_in_ref, w_ao_ref, w_co_ref, w_o_ref,
                     wr_hi_ref, wr_lo_ref, br_ref):
    sh1, sc1, g1, sh2, sc2, g2 = mod
    a = _dot(o_bf, w_ao_ref[...])
    c = _dot(y_conv.astype(BF16), w_co_ref[...])
    br_a = _project(h_bf, w_in_ref, _OFF_BA, D_MODEL)
    br_c = _project(h_bf, w_in_ref, _OFF_BC, D_MODEL)
    m = _sigmoid(br_a) * a + _sigmoid(br_c) * c
    x1 = x + g1 * _dot(m.astype(BF16), w_o_ref[...])
    h2 = _modulated_norm(x1, n2g_ref[...], sh2, sc2)
    topi, topg = _route(h2, wr_hi_ref[...], wr_lo_ref[...], br_ref[...])
    return x1, h2, topi, topg


def _prompt_body(sinks_ref, x_ref, mod_ref, rope_ref, n1g_ref, n2g_ref, w_in_ref, bd_ref, qg_ref, kg_ref,
                 w_ao_ref, cw_ref, cb_ref, w_co_ref, w_o_ref, wr_hi_ref, wr_lo_ref, br_ref,
                 x1_ref, h2_ref, topi_ref, topg_ref, kwin_ref, vwin_ref, zwin_ref,
                 kbuf, vbuf, zbuf, *, tb):
    j = pl.program_id(1)
    nsub = tb // WINDOW

    @pl.when(j == 0)
    def _():
        kbuf[0:WINDOW, :] = jnp.zeros((WINDOW, KV_W), F32)
        vbuf[0:WINDOW, :] = jnp.zeros((WINDOW, KV_W), F32)
        zbuf[0:SUBLANES, :] = jnp.zeros((SUBLANES, CONV_CH), F32)

    x = x_ref[0]
    mod = tuple(mod_ref[0, i:i + 1, :] for i in range(6))
    h_bf = _modulated_norm(x, n1g_ref[...], mod[0], mod[1]).astype(BF16)
    rope = rope_ref[...]
    bd = bd_ref[...]

    k = _head_norm_rope(_project(h_bf, w_in_ref, _OFF_K, KV_W), bd, kg_ref[...], rope)
    v = _project(h_bf, w_in_ref, _OFF_V, KV_W)
    kbuf[WINDOW:WINDOW + tb, :] = k
    vbuf[WINDOW:WINDOW + tb, :] = v
    kwin_ref[0] = k[tb - WINDOW:tb]
    vwin_ref[0] = v[tb - WINDOW:tb]
    k_exp = _pair_expand(kbuf[...])
    v_exp = _pair_expand(vbuf[...])

    q_tiles = []
    for t in range(Q_W // LANES):
        qt = _head_norm_rope(_project(h_bf, w_in_ref, _OFF_Q + t * LANES, LANES), bd, qg_ref[...], rope)
        q_tiles.append((qt * ATTN_SCALE).astype(BF16))

    row = lax.broadcasted_iota(jnp.int32, (WINDOW, 2 * WINDOW), 0)
    col = lax.broadcasted_iota(jnp.int32, (WINDOW, 2 * WINDOW), 1)
    band = (col > row) & (col <= row + WINDOW)
    first_key = jnp.where(j > 0, 0, WINDOW)
    band_first = band & (col >= first_key)
    o_rows = []
    for i in range(nsub):
        mask = band_first if i == 0 else band
        keys = slice(i * WINDOW, (i + 2) * WINDOW)
        o_tiles = []
        for t in range(Q_W // LANES):
            c = t // 2
            qt = q_tiles[t][i * WINDOW:(i + 1) * WINDOW]
            o_t = None
            for half in range(2):
                s = jnp.where(mask, _dot_nt(qt, k_exp[c][half][keys]), NEG_INF)
                sink = sinks_ref[c * GROUP + (t % 2) * 2 + half]
                part = _softmax_pv([s], [v_exp[c][half][keys]], sink)
                o_t = part if o_t is None else o_t + part
            o_tiles.append(o_t.astype(BF16))
        o_rows.append(jnp.concatenate(o_tiles, axis=1))
    o_bf = jnp.concatenate(o_rows, axis=0) if nsub > 1 else o_rows[0]
    kbuf[0:WINDOW, :] = kbuf[tb:tb + WINDOW, :]
    vbuf[0:WINDOW, :] = vbuf[tb:tb + WINDOW, :]

    z = _project(h_bf, w_in_ref, _OFF_GC, CONV_CH) * _project(h_bf, w_in_ref, _OFF_XC, CONV_CH)
    zbuf[SUBLANES:SUBLANES + tb, :] = z
    zwin_ref[0] = z[tb - SUBLANES:tb]
    z1 = zbuf[SUBLANES - 1:SUBLANES - 1 + tb, :]
    z2 = zbuf[SUBLANES - 2:SUBLANES - 2 + tb, :]
    conv = cb_ref[...] + cw_ref[0:1, :] * z2 + cw_ref[1:2, :] * z1 + cw_ref[2:3, :] * z
    y_conv = _project(h_bf, w_in_ref, _OFF_GB, CONV_CH) * conv
    zbuf[0:SUBLANES, :] = zbuf[tb:tb + SUBLANES, :]

    x1, h2, topi, topg = _merge_and_route(x, o_bf, y_conv, h_bf, mod, n2g_ref, w_in_ref, w_ao_ref, w_co_ref,
                                          w_o_ref, wr_hi_ref, wr_lo_ref, br_ref)
    x1_ref[0] = x1
    h2_ref[0] = h2.astype(BF16)
    topi_ref[0] = topi
    topg_ref[0] = topg


def _prompt_mixer(x, mod, rope, sinks, weights, tb):
    nb, seq, _ = x.shape
    (n1g, n2g, w_in, bd, qg, kg, w_ao, cw, cb, w_co, w_o, wr_hi, wr_lo, br) = weights
    consts = (n1g, n2g, w_in, bd, qg, kg, w_ao, cw, cb, w_co, w_o, wr_hi, wr_lo, br)
    in_specs = [
        pl.BlockSpec((1, tb, D_MODEL), lambda b, j, s: (b, j, 0)),
        pl.BlockSpec((1, 6, D_MODEL), lambda b, j, s: (b, 0, 0)),
        pl.BlockSpec((tb, 3 * LANES), lambda b, j, s: (j, 0)),
    ] + [_const_spec(w.shape) for w in consts]
    out_specs = [
        pl.BlockSpec((1, tb, D_MODEL), lambda b, j, s: (b, j, 0)),
        pl.BlockSpec((1, tb, D_MODEL), lambda b, j, s: (b, j, 0)),
        pl.BlockSpec((1, SUBLANES, tb), lambda b, j, s: (b, 0, j)),
        pl.BlockSpec((1, SUBLANES, tb), lambda b, j, s: (b, 0, j)),
        pl.BlockSpec((1, WINDOW, KV_W), lambda b, j, s: (b, 0, 0)),
        pl.BlockSpec((1, WINDOW, KV_W), lambda b, j, s: (b, 0, 0)),
        pl.BlockSpec((1, SUBLANES, CONV_CH), lambda b, j, s: (b, 0, 0)),
    ]
    out_shape = [
        jax.ShapeDtypeStruct((nb, seq, D_MODEL), F32),
        jax.ShapeDtypeStruct((nb, seq, D_MODEL), BF16),
        jax.ShapeDtypeStruct((nb, SUBLANES, seq), jnp.int32),
        jax.ShapeDtypeStruct((nb, SUBLANES, seq), F32),
        jax.ShapeDtypeStruct((nb, WINDOW, KV_W), F32),
        jax.ShapeDtypeStruct((nb, WINDOW, KV_W), F32),
        jax.ShapeDtypeStruct((nb, SUBLANES, CONV_CH), F32),
    ]
    return pl.pallas_call(
        functools.partial(_prompt_body, tb=tb),
        grid_spec=pltpu.PrefetchScalarGridSpec(
            num_scalar_prefetch=1, grid=(nb, seq // tb),
            in_specs=in_specs, out_specs=out_specs,
            scratch_shapes=[pltpu.VMEM((tb + WINDOW, KV_W), F32),
                            pltpu.VMEM((tb + WINDOW, KV_W), F32),
                            pltpu.VMEM((tb + SUBLANES, CONV_CH), F32)]),
        out_shape=out_shape,
        compiler_params=pltpu.CompilerParams(
            dimension_semantics=("arbitrary", "arbitrary"), vmem_limit_bytes=VMEM_LIMIT),
        name="prompt_mixer",
    )(sinks, x, mod, rope, *consts)


def _sample_body(sinks_ref, x_ref, mod_ref, rope_ref, ck_ref, cv_ref, st_ref,
                 n1g_ref, n2g_ref, w_in_ref, bd_ref, qg_ref, kg_ref,
                 w_ao_ref, cw_ref, cb_ref, w_co_ref, w_o_ref, wr_hi_ref, wr_lo_ref, br_ref,
                 x1_ref, h2_ref, topi_ref, topg_ref, knew_ref, vnew_ref, z_ref, *, nseq, dseq):
    rows = nseq * dseq
    wbuf = ck_ref.shape[1]
    x = x_ref[...]
    mod = tuple(mod_ref[i] for i in range(6))
    h_bf = _modulated_norm(x, n1g_ref[...], mod[0], mod[1]).astype(BF16)
    rope = rope_ref[...]
    bd = bd_ref[...]

    k = _head_norm_rope(_project(h_bf, w_in_ref, _OFF_K, KV_W), bd, kg_ref[...], rope)
    v = _project(h_bf, w_in_ref, _OFF_V, KV_W)
    knew_ref[...] = k
    vnew_ref[...] = v
    kn_exp = _pair_expand(k)
    vn_exp = _pair_expand(v)
    kc_exp = _pair_expand(ck_ref[...].reshape(nseq * wbuf, KV_W))
    vc_exp = _pair_expand(cv_ref[...].reshape(nseq * wbuf, KV_W))

    qrow = lax.broadcasted_iota(jnp.int32, (rows, nseq * wbuf), 0)
    ccol = lax.broadcasted_iota(jnp.int32, (rows, nseq * wbuf), 1)
    mask_c = ((ccol // wbuf) == (qrow // dseq)) & ((ccol % wbuf) > (qrow % dseq) + (wbuf - WINDOW))
    qrow_n = lax.broadcasted_iota(jnp.int32, (rows, rows), 0)
    ncol = lax.broadcasted_iota(jnp.int32, (rows, rows), 1)
    mask_n = ((ncol // dseq) == (qrow_n // dseq)) & ((ncol % dseq) <= (qrow_n % dseq))

    o_tiles = []
    for t in range(Q_W // LANES):
        c = t // 2
        qt = _head_norm_rope(_project(h_bf, w_in_ref, _OFF_Q + t * LANES, LANES), bd, qg_ref[...], rope)
        qt = (qt * ATTN_SCALE).astype(BF16)
        o_t = None
        for half in range(2):
            s_c = jnp.where(mask_c, _dot_nt(qt, kc_exp[c][half]), NEG_INF)
            s_n = jnp.where(mask_n, _dot_nt(qt, kn_exp[c][half]), NEG_INF)
            sink = sinks_ref[c * GROUP + (t % 2) * 2 + half]
            part = _softmax_pv([s_c, s_n], [vc_exp[c][half], vn_exp[c][half]], sink)
            o_t = part if o_t is None else o_t + part
        o_tiles.append(o_t.astype(BF16))
    o_bf = jnp.concatenate(o_tiles, axis=1)

    z = _project(h_bf, w_in_ref, _OFF_GC, CONV_CH) * _project(h_bf, w_in_ref, _OFF_XC, CONV_CH)
    z_ref[...] = z
    r = lax.broadcasted_iota(jnp.int32, z.shape, 0) % dseq
    st0 = st_ref[0]
    st1 = st_ref[1]
    z1 = jnp.where(r == 0, st1, pltpu.roll(z, 1, 0))
    z2 = jnp.where(r == 0, st0, jnp.where(r == 1, st1, pltpu.roll(z, 2, 0)))
    conv = cb_ref[...] + cw_ref[0:1, :] * z2 + cw_ref[1:2, :] * z1 + cw_ref[2:3, :] * z
    y_conv = _project(h_bf, w_in_ref, _OFF_GB, CONV_CH) * conv

    x1, h2, topi, topg = _merge_and_route(x, o_bf, y_conv, h_bf, mod, n2g_ref, w_in_ref, w_ao_ref, w_co_ref,
                                          w_o_ref, wr_hi_ref, wr_lo_ref, br_ref)
    x1_ref[...] = x1
    h2_ref[...] = h2.astype(BF16)
    topi_ref[...] = topi
    topg_ref[...] = topg


def _sample_mixer(x, mod, rope, cache_k, cache_v, state, sinks, weights, nseq):
    ntok = x.shape[0]
    nall, wbuf, _ = cache_k.shape
    dseq = ntok // nall
    rows = nseq * dseq
    consts = weights
    in_specs = [
        pl.BlockSpec((rows, D_MODEL), lambda i, s: (i, 0)),
        pl.BlockSpec((6, rows, D_MODEL), lambda i, s: (0, i, 0)),
        pl.BlockSpec((rows, 3 * LANES), lambda i, s: (i, 0)),
        pl.BlockSpec((nseq, wbuf, KV_W), lambda i, s: (i, 0, 0)),
        pl.BlockSpec((nseq, wbuf, KV_W), lambda i, s: (i, 0, 0)),
        pl.BlockSpec((2, rows, CONV_CH), lambda i, s: (0, i, 0)),
    ] + [_const_spec(w.shape) for w in consts]
    out_specs = [
        pl.BlockSpec((rows, D_MODEL), lambda i, s: (i, 0)),
        pl.BlockSpec((rows, D_MODEL), lambda i, s: (i, 0)),
        pl.BlockSpec((SUBLANES, rows), lambda i, s: (0, i)),
        pl.BlockSpec((SUBLANES, rows), lambda i, s: (0, i)),
        pl.BlockSpec((rows, KV_W), lambda i, s: (i, 0)),
        pl.BlockSpec((rows, KV_W), lambda i, s: (i, 0)),
        pl.BlockSpec((rows, CONV_CH), lambda i, s: (i, 0)),
    ]
    out_shape = [
        jax.ShapeDtypeStruct((ntok, D_MODEL), F32),
        jax.ShapeDtypeStruct((ntok, D_MODEL), BF16),
        jax.ShapeDtypeStruct((SUBLANES, ntok), jnp.int32),
        jax.ShapeDtypeStruct((SUBLANES, ntok), F32),
        jax.ShapeDtypeStruct((ntok, KV_W), F32),
        jax.ShapeDtypeStruct((ntok, KV_W), F32),
        jax.ShapeDtypeStruct((ntok, CONV_CH), F32),
    ]
    return pl.pallas_call(
        functools.partial(_sample_body, nseq=nseq, dseq=dseq),
        grid_spec=pltpu.PrefetchScalarGridSpec(
            num_scalar_prefetch=1, grid=(ntok // rows,),
            in_specs=in_specs, out_specs=out_specs),
        out_shape=out_shape,
        compiler_params=pltpu.CompilerParams(
            dimension_semantics=("arbitrary",), vmem_limit_bytes=VMEM_LIMIT),
        name="sample_mixer",
    )(sinks, x, mod, rope, cache_k, cache_v, state, *consts)


SORT_TOKENS = 256
RUN_ALIGN = SUBLANES
EXPERT_TILE = 512
HALF_D = D_MODEL // 2
LOCAL_ROWS = -(-(SORT_TOKENS * TOP_K + N_EXPERTS * (RUN_ALIGN - 1)) // LANES) * LANES
_HI_MASK = 0xFFFF0000


def _pack_halves(x):
    lo = pltpu.bitcast(x[:, :HALF_D], jnp.uint32)
    hi = pltpu.bitcast(x[:, HALF_D:], jnp.uint32)
    return hi | (lo >> 16)


def _unpack_halves(w):
    lo = pltpu.bitcast(w << 16, F32).astype(BF16)
    hi = pltpu.bitcast(w & jnp.uint32(_HI_MASK), F32).astype(BF16)
    return lo, hi


def _local_positions(topi, lstart_col, tri):
    ntok = topi.shape[1]
    eid = lax.broadcasted_iota(jnp.int32, (N_EXPERTS, ntok), 0)
    hits = [eid == topi[k:k + 1, :] for k in range(TOP_K)]
    chosen = jnp.zeros((N_EXPERTS, ntok), F32)
    for h in hits:
        chosen = jnp.where(h, 1.0, chosen)
    base = _dot(chosen.astype(BF16), tri) + lstart_col
    return [jnp.sum(jnp.where(h, base, 0.0), axis=0, keepdims=True) for h in hits]


def _run_copy(loc, glob, sem, lofs, gofs, n, to_global):
    lo = loc.at[pl.ds(pl.multiple_of(lofs, RUN_ALIGN), n)]
    gl = glob.at[pl.ds(pl.multiple_of(gofs, RUN_ALIGN), n)]
    return pltpu.make_async_copy(lo, gl, sem) if to_global else pltpu.make_async_copy(gl, lo, sem)


def _start_runs(meta, blk, loc, glob, sem, to_global):
    lstart_ref, goff_ref, cpad_ref = meta

    def issue(e, carry):
        n = pl.multiple_of(cpad_ref[blk * N_EXPERTS + e], RUN_ALIGN)

        @pl.when(n > 0)
        def _():
            _run_copy(loc, glob, sem, lstart_ref[blk * N_EXPERTS + e], goff_ref[blk * N_EXPERTS + e], n,
                      to_global).start()
        return carry
    lax.fori_loop(0, N_EXPERTS, issue, 0)


def _wait_runs(nrows, loc, glob, sem, to_global):
    n = pl.multiple_of(nrows, RUN_ALIGN)

    @pl.when(n > 0)
    def _():
        _run_copy(loc, glob, sem, 0, 0, n, to_global).wait()


def _sort_body(lstart_ref, goff_ref, cpad_ref, ltot_ref, fstart_ref, flen_ref, nact_ref,
               hp_ref, hs_ref, topi_ref, lcol_ref, tri_ref, xs_hbm, buf, zbuf, sems, fsem, *, nblk_p, nblk):
    i = pl.program_id(0)
    slot = i % 2
    meta = (lstart_ref, goff_ref, cpad_ref)

    @pl.when(i == 0)
    def _():
        zbuf[...] = jnp.zeros(zbuf.shape, jnp.uint32)

        def fill(e, tot):
            n = pl.multiple_of(flen_ref[e], RUN_ALIGN)

            @pl.when(n > 0)
            def _():
                _run_copy(zbuf, xs_hbm, fsem, 0, fstart_ref[e], n, True).start()
            return tot + n
        total = lax.fori_loop(0, N_EXPERTS, fill, 0)
        _wait_runs(total, zbuf, xs_hbm, fsem, True)

        def fill_tile(t, carry):
            cp = _run_copy(zbuf, xs_hbm, fsem, 0, t * EXPERT_TILE, EXPERT_TILE, True)
            cp.start()
            cp.wait()
            return carry
        lax.fori_loop(nact_ref[0], xs_hbm.shape[0] // EXPERT_TILE, fill_tile, 0)

    def run(h_ref):
        pos = _local_positions(topi_ref[...], lcol_ref[...], tri_ref[...])
        rowid = lax.broadcasted_iota(jnp.int32, (LOCAL_ROWS, SORT_TOKENS), 0)
        onehot = jnp.zeros((LOCAL_ROWS, SORT_TOKENS), F32)
        for p in pos:
            onehot = jnp.where(rowid == p.astype(jnp.int32), 1.0, onehot)
        buf[slot] = _pack_halves(_dot(onehot.astype(BF16), h_ref[...]))

    @pl.when(i < nblk_p)
    def _():
        run(hp_ref)

    @pl.when(i >= nblk_p)
    def _():
        run(hs_ref)

    _start_runs(meta, i, buf.at[slot], xs_hbm, sems.at[slot], True)

    @pl.when(i > 0)
    def _():
        _wait_runs(ltot_ref[jnp.maximum(i - 1, 0)], buf.at[1 - slot], xs_hbm, sems.at[1 - slot], True)

    @pl.when(i == nblk - 1)
    def _():
        _wait_runs(ltot_ref[i], buf.at[slot], xs_hbm, sems.at[slot], True)


def _sort_tokens(plan, h_p, h_s, topi_all, tri, n_slots):
    nblk_p = h_p.shape[0] // SORT_TOKENS
    nblk = nblk_p + h_s.shape[0] // SORT_TOKENS
    return pl.pallas_call(
        functools.partial(_sort_body, nblk_p=nblk_p, nblk=nblk),
        grid_spec=pltpu.PrefetchScalarGridSpec(
            num_scalar_prefetch=7, grid=(nblk,),
            in_specs=[pl.BlockSpec((SORT_TOKENS, D_MODEL), lambda i, *_: (jnp.minimum(i, nblk_p - 1), 0)),
                      pl.BlockSpec((SORT_TOKENS, D_MODEL), lambda i, *_: (jnp.maximum(i - nblk_p, 0), 0)),
                      pl.BlockSpec((SUBLANES, SORT_TOKENS), lambda i, *_: (0, i)),
                      pl.BlockSpec((None, N_EXPERTS, 1), lambda i, *_: (i, 0, 0)),
                      _const_spec(tri.shape)],
            out_specs=pl.BlockSpec(memory_space=pl.ANY),
            scratch_shapes=[pltpu.VMEM((2, LOCAL_ROWS, HALF_D), jnp.uint32),
                            pltpu.VMEM((EXPERT_TILE, HALF_D), jnp.uint32),
                            pltpu.SemaphoreType.DMA((2,)),
                            pltpu.SemaphoreType.DMA]),
        out_shape=jax.ShapeDtypeStruct((n_slots, HALF_D), jnp.uint32),
        compiler_params=pltpu.CompilerParams(
            dimension_semantics=("arbitrary",), vmem_limit_bytes=VMEM_LIMIT),
        name="moe_sort",
    )(plan["lstart"], plan["goff"], plan["cpad"], plan["ltot"], plan["fill_start"], plan["fill_len"], plan["nact"],
      h_p, h_s, topi_all, plan["lstart_col"], tri)


def _experts_body(tile_exp_ref, nact_ref, xs_ref, wg_ref, bg_ref, wu_ref, bu_ref, wd_ref, bd_ref,
                  ys_ref, wg_bf, wu_bf, wd_bf):
    t = pl.program_id(0)
    nact = nact_ref[0]

    @pl.when(t < nact)
    def _():
        prev = tile_exp_ref[jnp.maximum(t - 1, 0)]

        @pl.when((t == 0) | (tile_exp_ref[t] != prev))
        def _():
            wg_bf[...] = wg_ref[0].astype(BF16)
            wu_bf[...] = wu_ref[0].astype(BF16)
            wd_bf[...] = wd_ref[0].astype(BF16)

        x_lo, x_hi = _unpack_halves(xs_ref[...])
        g = _dot(x_lo, wg_bf[:HALF_D, :]) + _dot(x_hi, wg_bf[HALF_D:, :]) + bg_ref[0]
        u = _dot(x_lo, wu_bf[:HALF_D, :]) + _dot(x_hi, wu_bf[HALF_D:, :]) + bu_ref[0]
        g = jnp.minimum(g, SWIGLU_LIMIT)
        u = jnp.clip(u, -SWIGLU_LIMIT, SWIGLU_LIMIT)
        a = g * _sigmoid(SWIGLU_ALPHA * g) * (u + 1.0)
        out = _dot(a.astype(BF16), wd_bf[...]) + bd_ref[0]
        ys_ref[...] = _pack_halves(out.astype(BF16).astype(F32))

    @pl.when(t >= nact)
    def _():
        ys_ref[...] = jnp.zeros(ys_ref.shape, jnp.uint32)


def _experts(tile_exp, nact, xs, wg, bg, wu, bu, wd, bd):
    n_tiles = tile_exp.shape[0]

    def active(t, na):
        return jnp.minimum(t, na[0] - 1)
    wspec = pl.BlockSpec((1, D_MODEL, D_MODEL), lambda t, te, na: (te[active(t, na)], 0, 0))
    bspec = pl.BlockSpec((1, 1, D_MODEL), lambda t, te, na: (te[active(t, na)], 0, 0))
    xspec = pl.BlockSpec((EXPERT_TILE, HALF_D), lambda t, te, na: (active(t, na), 0))
    return pl.pallas_call(
        _experts_body,
        grid_spec=pltpu.PrefetchScalarGridSpec(
            num_scalar_prefetch=2, grid=(n_tiles,),
            in_specs=[xspec, wspec, bspec, wspec, bspec, wspec, bspec],
            out_specs=pl.BlockSpec((EXPERT_TILE, HALF_D), lambda t, te, na: (t, 0)),
            scratch_shapes=[pltpu.VMEM((D_MODEL, D_MODEL), BF16)] * 3),
        out_shape=jax.ShapeDtypeStruct(xs.shape, jnp.uint32),
        compiler_params=pltpu.CompilerParams(
            dimension_semantics=("arbitrary",), vmem_limit_bytes=VMEM_LIMIT),
        name="experts",
    )(tile_exp, nact, xs, wg, bg, wu, bu, wd, bd)


def _unsort_body(lstart_ref, goff_ref, cpad_ref, ltot_ref,
                 ys_hbm, x1_ref, g2_ref, topi_ref, topg_ref, lcol_ref, tri_ref, y_ref, buf, sems, *, blk0, nblk):
    i = pl.program_id(0)
    b = blk0 + i
    slot = i % 2
    meta = (lstart_ref, goff_ref, cpad_ref)

    @pl.when(i == 0)
    def _():
        buf[...] = jnp.zeros(buf.shape, jnp.uint32)
        _start_runs(meta, b, buf.at[0], ys_hbm, sems.at[0], False)

    @pl.when(i + 1 < nblk)
    def _():
        _start_runs(meta, b + 1, buf.at[1 - slot], ys_hbm, sems.at[1 - slot], False)

    pos = _local_positions(topi_ref[...], lcol_ref[...], tri_ref[...])
    rowid = lax.broadcasted_iota(jnp.int32, (SUBLANES, SORT_TOKENS), 0)
    pos_rows = jnp.zeros((SUBLANES, SORT_TOKENS), F32)
    for k in range(TOP_K):
        pos_rows = jnp.where(rowid == k, pos[k], pos_rows)
    pos_cols = pos_rows.T.astype(jnp.int32)
    gate_cols = topg_ref[...].T
    colid = lax.broadcasted_iota(jnp.int32, (SORT_TOKENS, LOCAL_ROWS), 1)
    weights = jnp.zeros((SORT_TOKENS, LOCAL_ROWS), F32)
    for k in range(TOP_K):
        weights = jnp.where(colid == pos_cols[:, k:k + 1], gate_cols[:, k:k + 1], weights)
    weights = weights.astype(BF16)

    _wait_runs(ltot_ref[b], buf.at[slot], ys_hbm, sems.at[slot], False)
    y_lo, y_hi = _unpack_halves(buf[slot])
    g2 = g2_ref[...]
    y_ref[:, :HALF_D] = x1_ref[:, :HALF_D] + g2[:, :HALF_D] * _dot(weights, y_lo)
    y_ref[:, HALF_D:] = x1_ref[:, HALF_D:] + g2[:, HALF_D:] * _dot(weights, y_hi)


def _unsort_tokens(plan, ys, x1, g2, g2_spec, topi_all, topg_all, tri, blk0):
    nblk = x1.shape[0] // SORT_TOKENS
    return pl.pallas_call(
        functools.partial(_unsort_body, blk0=blk0, nblk=nblk),
        grid_spec=pltpu.PrefetchScalarGridSpec(
            num_scalar_prefetch=4, grid=(nblk,),
            in_specs=[pl.BlockSpec(memory_space=pl.ANY),
                      pl.BlockSpec((SORT_TOKENS, D_MODEL), lambda i, *_: (i, 0)),
                      g2_spec,
                      pl.BlockSpec((SUBLANES, SORT_TOKENS), lambda i, *_: (0, blk0 + i)),
                      pl.BlockSpec((SUBLANES, SORT_TOKENS), lambda i, *_: (0, blk0 + i)),
                      pl.BlockSpec((None, N_EXPERTS, 1), lambda i, *_: (blk0 + i, 0, 0)),
                      _const_spec(tri.shape)],
            out_specs=pl.BlockSpec((SORT_TOKENS, D_MODEL), lambda i, *_: (i, 0)),
            scratch_shapes=[pltpu.VMEM((2, LOCAL_ROWS, HALF_D), jnp.uint32),
                            pltpu.SemaphoreType.DMA((2,))]),
        out_shape=jax.ShapeDtypeStruct(x1.shape, F32),
        compiler_params=pltpu.CompilerParams(
            dimension_semantics=("arbitrary",), vmem_limit_bytes=VMEM_LIMIT),
        name="moe_unsort",
    )(plan["lstart"], plan["goff"], plan["cpad"], plan["ltot"],
      ys, x1, g2, topi_all, topg_all, plan["lstart_col"], tri)


def _rope_table(pos):
    half = ROT_DIM // 2
    inv_freq = ROPE_THETA ** (-jnp.arange(0, ROT_DIM, 2, dtype=F32) / ROT_DIM)
    ang = pos.astype(F32)[:, None] * inv_freq[None, :]
    cos, sin = jnp.cos(ang), jnp.sin(ang)
    n = pos.shape[0]
    rest = HEAD_DIM - ROT_DIM
    c = jnp.concatenate([cos, cos, jnp.ones((n, rest), F32)], axis=1)
    s_next = jnp.concatenate([-sin, jnp.zeros((n, half + rest), F32)], axis=1)
    s_prev = jnp.concatenate([jnp.zeros((n, half), F32), sin, jnp.zeros((n, rest), F32)], axis=1)
    rep = LANES // HEAD_DIM
    return jnp.concatenate([jnp.tile(c, (1, rep)), jnp.tile(s_next, (1, rep)), jnp.tile(s_prev, (1, rep))], axis=1)


def _num_expert_tiles(ntok):
    nblk = ntok // SORT_TOKENS
    worst = ntok * TOP_K + nblk * N_EXPERTS * (RUN_ALIGN - 1) + N_EXPERTS * (EXPERT_TILE - 1)
    return -(-worst // EXPERT_TILE)


def _routing_plan(topi_all):
    ntok = topi_all.shape[1]
    nblk = ntok // SORT_TOKENS
    n_tiles = _num_expert_tiles(ntok)
    choice = topi_all[:TOP_K].reshape(TOP_K, nblk, SORT_TOKENS)
    experts = jnp.arange(N_EXPERTS, dtype=jnp.int32)
    cnt = (choice[:, :, :, None] == experts).astype(jnp.int32).sum(axis=(0, 2))
    cpad = -(-cnt // RUN_ALIGN) * RUN_ALIGN
    lstart = jnp.cumsum(cpad, axis=1) - cpad
    tot = cpad.sum(axis=0)
    region = -(-tot // EXPERT_TILE) * EXPERT_TILE
    region_end = jnp.cumsum(region)
    gstart = region_end - region
    goff = gstart[None, :] + jnp.cumsum(cpad, axis=0) - cpad
    tile_start = jnp.arange(n_tiles, dtype=jnp.int32) * EXPERT_TILE
    tile_exp = jnp.minimum((tile_start[:, None] >= region_end[None, :]).astype(jnp.int32).sum(axis=1), N_EXPERTS - 1)
    i32 = lambda a: a.astype(jnp.int32)
    return dict(
        lstart=i32(lstart).reshape(-1), goff=i32(goff).reshape(-1), cpad=i32(cpad).reshape(-1),
        ltot=i32(cpad.sum(axis=1)), fill_start=i32(gstart + tot), fill_len=i32(region - tot),
        lstart_col=lstart.astype(F32).reshape(nblk, N_EXPERTS, 1),
        tile_exp=i32(tile_exp), nact=i32(region_end[-1] // EXPERT_TILE).reshape(1))


def kernel(x_prompt, x_sample, cache_k_win, cache_v_win, state_conv, c_prompt, c_sample, w_ada, b_ada, norm1_g, norm2_g, w_in, q_norm_g, k_norm_g, attn_sinks, w_attn_out, conv_w, conv_b, w_conv_out, w_o, w_router, b_router, w_gate, b_gate, w_up, b_up, w_down, b_down):
    nb, seq, _ = x_prompt.shape
    ns, dseq, _ = x_sample.shape
    wbuf = cache_k_win.shape[2]
    assert w_ada.shape[0] == 1, "single-layer step"
    l = 0

    mod = _adaln(jnp.concatenate([c_prompt, c_sample], axis=0), w_ada[l], b_ada[l])
    mod_p = mod[:nb].reshape(nb, 6, D_MODEL)
    mod_s = jnp.repeat(mod[nb:].reshape(ns, 6, D_MODEL).transpose(1, 0, 2), dseq, axis=1)

    bd = jnp.asarray(np.kron(np.eye(LANES // HEAD_DIM), np.ones((HEAD_DIM, HEAD_DIM))), BF16)
    rep = LANES // HEAD_DIM
    wr_t = w_router[l].T
    wr_hi = wr_t.astype(BF16)
    wr_lo = (wr_t - wr_hi.astype(F32)).astype(BF16)
    weights = (norm1_g[l].reshape(1, D_MODEL), norm2_g[l].reshape(1, D_MODEL), w_in[l].astype(BF16), bd,
               jnp.tile(q_norm_g[l], rep).reshape(1, LANES), jnp.tile(k_norm_g[l], rep).reshape(1, LANES),
               w_attn_out[l].astype(BF16), conv_w[l], conv_b[l].reshape(1, CONV_CH), w_conv_out[l].astype(BF16),
               w_o[l].astype(BF16), wr_hi, wr_lo, b_router[l].reshape(N_EXPERTS, 1))
    sinks = attn_sinks[l]

    tb = 256
    (x1_p, h2_p, topi_p, topg_p, kwin_p, vwin_p, zwin_p) = _prompt_mixer(
        x_prompt, mod_p, _rope_table(jnp.arange(seq)), sinks, weights, tb)
    pos_s = jnp.tile(PAST_LEN + jnp.arange(dseq), ns)
    state_rows = jnp.repeat(state_conv[l].transpose(1, 0, 2), dseq, axis=1)
    (x1_s, h2_s, topi_s, topg_s, knew, vnew, z_s) = _sample_mixer(
        x_sample.reshape(ns * dseq, D_MODEL), mod_s, _rope_table(pos_s),
        cache_k_win[l].reshape(ns, wbuf, KV_W), cache_v_win[l].reshape(ns, wbuf, KV_W),
        state_rows, sinks, weights, 16)

    ntok_p = nb * seq
    ntok = ntok_p + ns * dseq
    topi_all = jnp.concatenate([topi_p.transpose(1, 0, 2).reshape(SUBLANES, ntok_p), topi_s], axis=1)
    topg_all = jnp.concatenate([topg_p.transpose(1, 0, 2).reshape(SUBLANES, ntok_p), topg_s], axis=1)
    plan = _routing_plan(topi_all)
    tri = jnp.asarray(np.triu(np.ones((SORT_TOKENS, SORT_TOKENS)), k=1), BF16)
    xs = _sort_tokens(plan, h2_p.reshape(ntok_p, D_MODEL), h2_s, topi_all, tri,
                      _num_expert_tiles(ntok) * EXPERT_TILE)
    ys = _experts(plan["tile_exp"], plan["nact"], xs,
                  w_gate[l], b_gate[l].reshape(N_EXPERTS, 1, D_MODEL),
                  w_up[l], b_up[l].reshape(N_EXPERTS, 1, D_MODEL),
                  w_down[l], b_down[l].reshape(N_EXPERTS, 1, D_MODEL))
    blocks_per_seq = seq // SORT_TOKENS
    y_p = _unsort_tokens(plan, ys, x1_p.reshape(ntok_p, D_MODEL), mod_p[:, 5:6, :],
                         pl.BlockSpec((None, 1, D_MODEL), lambda i, *_: (i // blocks_per_seq, 0, 0)),
                         topi_all, topg_all, tri, 0)
    y_s = _unsort_tokens(plan, ys, x1_s, mod_s[5],
                         pl.BlockSpec((SORT_TOKENS, D_MODEL), lambda i, *_: (i, 0)),
                         topi_all, topg_all, tri, ntok_p // SORT_TOKENS)

    n_kv_shape = (N_KV, HEAD_DIM)
    k_win_p = kwin_p.reshape(1, nb, WINDOW, *n_kv_shape)
    v_win_p = vwin_p.reshape(1, nb, WINDOW, *n_kv_shape)
    conv_p = zwin_p[:, SUBLANES - (CONV_K - 1):, :][None]
    k_win_s = jnp.concatenate([cache_k_win[l][:, dseq:], knew.reshape(ns, dseq, *n_kv_shape)], axis=1)[None]
    v_win_s = jnp.concatenate([cache_v_win[l][:, dseq:], vnew.reshape(ns, dseq, *n_kv_shape)], axis=1)[None]
    conv_s = z_s.reshape(ns, dseq, CONV_CH)[:, dseq - (CONV_K - 1):, :][None]
    return (y_p.reshape(nb, seq, D_MODEL), y_s.reshape(ns, dseq, D_MODEL),
            k_win_p, v_win_p, conv_p, k_win_s, v_win_s, conv_s)
```

```python
import functools

import numpy as np
import jax
import jax.numpy as jnp
from jax import lax
from jax.experimental import pallas as pl
from jax.experimental.pallas import tpu as pltpu

D_MODEL = 1024
HEAD_DIM = 64
N_HEADS = 8
N_KV = 2
GROUP = N_HEADS // N_KV
Q_W = N_HEADS * HEAD_DIM
KV_W = N_KV * HEAD_DIM
WINDOW = 128
ROT_DIM = HEAD_DIM // 4
ROPE_THETA = 500000.0
ATTN_SCALE = HEAD_DIM ** -0.5
CONV_CH = D_MODEL // 2
CONV_K = 3
N_EXPERTS = 32
TOP_K = 4
SWIGLU_ALPHA = 1.702
SWIGLU_LIMIT = 7.0
MOE_BLOCK = 128
NORM_EPS = 1e-5
QK_EPS = 1e-6
NEG_INF = -1e30
PAST_LEN = 16384

LANES = 128
SUBLANES = 8
VMEM_LIMIT = 56 * 1024 * 1024

_OFF_Q = 0
_OFF_K = _OFF_Q + Q_W
_OFF_V = _OFF_K + KV_W
_OFF_GB = _OFF_V + KV_W
_OFF_GC = _OFF_GB + CONV_CH
_OFF_XC = _OFF_GC + CONV_CH
_OFF_BA = _OFF_XC + CONV_CH
_OFF_BC = _OFF_BA + D_MODEL
IN_COLS = _OFF_BC + D_MODEL

BF16 = jnp.bfloat16
F32 = jnp.float32
_NT = (((1,), (1,)), ((), ()))


def _dot(a, b):
    return jnp.dot(a, b, preferred_element_type=F32)


def _dot_nt(a, b):
    return lax.dot_general(a, b, _NT, preferred_element_type=F32)


def _sigmoid(x):
    return 1.0 / (1.0 + jnp.exp(-x))


def _split_bf16(x):
    hi = x.astype(BF16)
    lo = (x - hi.astype(F32)).astype(BF16)
    return hi, lo


def _const_spec(shape):
    nd = len(shape)
    return pl.BlockSpec(shape, lambda *_: (0,) * nd, pipeline_mode=pl.Buffered(1))


def _adaln_body(c_ref, w_ref, b_ref, o_ref):
    c = c_ref[...]
    s = c * _sigmoid(c)
    o_ref[...] = _dot(s.astype(BF16), w_ref[...].astype(BF16)) + b_ref[...]


def _adaln(c, w_ada, b_ada):
    n = c.shape[0]
    cols = w_ada.shape[1]
    bn = 1536
    return pl.pallas_call(
        _adaln_body,
        grid=(cols // bn,),
        in_specs=[pl.BlockSpec((n, D_MODEL), lambda i: (0, 0)),
                  pl.BlockSpec((D_MODEL, bn), lambda i: (0, i)),
                  pl.BlockSpec((1, bn), lambda i: (0, i))],
        out_specs=pl.BlockSpec((n, bn), lambda i: (0, i)),
        out_shape=jax.ShapeDtypeStruct((n, cols), F32),
        compiler_params=pltpu.CompilerParams(vmem_limit_bytes=VMEM_LIMIT),
        name="adaln",
    )(c, w_ada, b_ada.reshape(1, cols))


def _modulated_norm(x, g, shift, scale):
    y = x * lax.rsqrt(jnp.mean(x * x, axis=-1, keepdims=True) + NORM_EPS)
    return (y * g) * (1.0 + scale) + shift


def _head_norm(t, bd, g):
    w = t.shape[1]
    sq_hi, sq_lo = _split_bf16(t * t)
    blk = bd[:w, :w]
    ms = (_dot(sq_hi, blk) + _dot(sq_lo, blk)) * (1.0 / HEAD_DIM)
    return t * lax.rsqrt(ms + QK_EPS) * g[:, :w]


def _rope(y, rope):
    c = rope[:, 0:LANES]
    s_next = rope[:, LANES:2 * LANES]
    s_prev = rope[:, 2 * LANES:3 * LANES]
    half = ROT_DIM // 2
    return y * c + pltpu.roll(y, LANES - half, 1) * s_next + pltpu.roll(y, half, 1) * s_prev


def _pair_expand(t):
    lane = lax.broadcasted_iota(jnp.int32, t.shape, 1)
    lo = lane < HEAD_DIM
    r = pltpu.roll(t, HEAD_DIM, 1)
    zero = jnp.zeros_like(t)
    a0 = jnp.where(lo, t, zero).astype(BF16)
    b0 = jnp.where(lo, zero, r).astype(BF16)
    a1 = jnp.where(lo, r, zero).astype(BF16)
    b1 = jnp.where(lo, zero, t).astype(BF16)
    return ((a0, b0), (a1, b1))


def _softmax_pv(s_list, v_list, sink):
    m = jnp.full((s_list[0].shape[0], 1), sink, F32)
    for s in s_list:
        m = jnp.maximum(m, jnp.max(s, axis=-1, keepdims=True))
    den = jnp.exp(sink - m)
    acc = None
    for s, v in zip(s_list, v_list):
        e = jnp.exp(s - m)
        den = den + jnp.sum(e, axis=-1, keepdims=True)
        pv = _dot(e.astype(BF16), v)
        acc = pv if acc is None else acc + pv
    return acc * (1.0 / den)


def _route(h2, wr_hi, wr_lo, br):
    hi, lo = _split_bf16(h2)
    logits = _dot_nt(wr_hi, hi) + _dot_nt(wr_hi, lo) + _dot_nt(wr_lo, hi) + br
    eid = lax.broadcasted_iota(jnp.int32, logits.shape, 0).astype(F32)
    vals, idxs = [], []
    for _ in range(TOP_K):
        m = jnp.max(logits, axis=0, keepdims=True)
        idx = jnp.min(jnp.where(logits == m, eid, float(N_EXPERTS)), axis=0, keepdims=True)
        vals.append(m)
        idxs.append(idx)
        logits = jnp.where(eid == idx, -jnp.inf, logits)
    ex = [jnp.exp(v - vals[0]) for v in vals]
    tot = ex[0] + ex[1] + ex[2] + ex[3]
    inv = 1.0 / tot
    gates = [e * inv for e in ex]
    rowid = lax.broadcasted_iota(jnp.int32, (SUBLANES, h2.shape[0]), 0)
    topi = jnp.zeros((SUBLANES, h2.shape[0]), F32)
    topg = jnp.zeros((SUBLANES, h2.shape[0]), F32)
    for k in range(TOP_K):
        topi = jnp.where(rowid == k, idxs[k], topi)
        topg = jnp.where(rowid == k, gates[k], topg)
    return topi.astype(jnp.int32), topg


_QKV_COLS = (_OFF_Q, Q_W + 2 * KV_W)
_LATE_COLS = ((_OFF_GB, 2 * CONV_CH), (_OFF_XC, CONV_CH), (_OFF_BA, D_MODEL), (_OFF_BC, D_MODEL))


def _in_proj(proj, h_bf, w_in_ref, group):
    off, width = group
    proj[off] = _dot(h_bf, w_in_ref[:, off:off + width])


def _cols(proj, off, width):
    for start, piece in proj.items():
        if start <= off and off + width <= start + piece.shape[1]:
            return piece[:, off - start:off - start + width]
    raise KeyError(off)


def _merge_and_route(x, o_bf, y_conv, proj, mod, n2g_ref, w_ao_ref, w_co_ref, w_o_ref,
                     wr_hi_ref, wr_lo_ref, br_ref):
    sh1, sc1, g1, sh2, sc2, g2 = mod
    a = _dot(o_bf, w_ao_ref[...])
    c = _dot(y_conv.astype(BF16), w_co_ref[...])
    m = _sigmoid(_cols(proj, _OFF_BA, D_MODEL)) * a + _sigmoid(_cols(proj, _OFF_BC, D_MODEL)) * c
    x1 = x + g1 * _dot(m.astype(BF16), w_o_ref[...])
    h2 = _modulated_norm(x1, n2g_ref[...], sh2, sc2)
    topi, topg = _route(h2, wr_hi_ref[...], wr_lo_ref[...], br_ref[...])
    return x1, h2, topi, topg


def _prompt_body(sinks_ref, x_ref, mod_ref, rope_ref, n1g_ref, n2g_ref, w_in_ref, bd_ref, qg_ref, kg_ref,
                 w_ao_ref, cw_ref, cb_ref, w_co_ref, w_o_ref, wr_hi_ref, wr_lo_ref, br_ref,
                 x1_ref, h2_ref, topi_ref, topg_ref, kwin_ref, vwin_ref, zwin_ref,
                 kbuf, vbuf, zbuf, *, tb):
    j = pl.program_id(1)
    nsub = tb // WINDOW

    @pl.when(j == 0)
    def _():
        kbuf[0:WINDOW, :] = jnp.zeros((WINDOW, KV_W), F32)
        vbuf[0:WINDOW, :] = jnp.zeros((WINDOW, KV_W), F32)
        zbuf[0:SUBLANES, :] = jnp.zeros((SUBLANES, CONV_CH), F32)

    x = x_ref[0]
    mod = tuple(mod_ref[0, i:i + 1, :] for i in range(6))
    h_bf = _modulated_norm(x, n1g_ref[...], mod[0], mod[1]).astype(BF16)
    proj = {}
    _in_proj(proj, h_bf, w_in_ref, _QKV_COLS)
    _in_proj(proj, h_bf, w_in_ref, _LATE_COLS[0])
    rope = rope_ref[...]
    bd = bd_ref[...]

    k = _rope(_head_norm(_cols(proj, _OFF_K, KV_W), bd, kg_ref[...]), rope)
    v = _cols(proj, _OFF_V, KV_W)
    kbuf[WINDOW:WINDOW + tb, :] = k
    vbuf[WINDOW:WINDOW + tb, :] = v
    kwin_ref[0] = k[tb - WINDOW:tb]
    vwin_ref[0] = v[tb - WINDOW:tb]
    k_exp = _pair_expand(kbuf[...])
    v_exp = _pair_expand(vbuf[...])

    _in_proj(proj, h_bf, w_in_ref, _LATE_COLS[1])
    qn = _head_norm(_cols(proj, _OFF_Q, Q_W), bd, qg_ref[...])
    q_tiles = []
    for t in range(Q_W // LANES):
        qt = _rope(qn[:, t * LANES:(t + 1) * LANES], rope)
        q_tiles.append((qt * ATTN_SCALE).astype(BF16))

    row = lax.broadcasted_iota(jnp.int32, (WINDOW, 2 * WINDOW), 0)
    col = lax.broadcasted_iota(jnp.int32, (WINDOW, 2 * WINDOW), 1)
    band = (col > row) & (col <= row + WINDOW)
    first_key = jnp.where(j > 0, 0, WINDOW)
    band_first = band & (col >= first_key)
    o_rows = []
    for i in range(nsub):
        for group in _LATE_COLS[2:][i::nsub]:
            _in_proj(proj, h_bf, w_in_ref, group)
        mask = band_first if i == 0 else band
        keys = slice(i * WINDOW, (i + 2) * WINDOW)
        o_tiles = []
        for t in range(Q_W // LANES):
            c = t // 2
            qt = q_tiles[t][i * WINDOW:(i + 1) * WINDOW]
            o_t = None
            for half in range(2):
                s = jnp.where(mask, _dot_nt(qt, k_exp[c][half][keys]), NEG_INF)
                sink = sinks_ref[c * GROUP + (t % 2) * 2 + half]
                part = _softmax_pv([s], [v_exp[c][half][keys]], sink)
                o_t = part if o_t is None else o_t + part
            o_tiles.append(o_t.astype(BF16))
        o_rows.append(jnp.concatenate(o_tiles, axis=1))
    o_bf = jnp.concatenate(o_rows, axis=0) if nsub > 1 else o_rows[0]
    kbuf[0:WINDOW, :] = kbuf[tb:tb + WINDOW, :]
    vbuf[0:WINDOW, :] = vbuf[tb:tb + WINDOW, :]

    z = _cols(proj, _OFF_GC, CONV_CH) * _cols(proj, _OFF_XC, CONV_CH)
    zbuf[SUBLANES:SUBLANES + tb, :] = z
    zwin_ref[0] = z[tb - SUBLANES:tb]
    z1 = zbuf[SUBLANES - 1:SUBLANES - 1 + tb, :]
    z2 = zbuf[SUBLANES - 2:SUBLANES - 2 + tb, :]
    conv = cb_ref[...] + cw_ref[0:1, :] * z2 + cw_ref[1:2, :] * z1 + cw_ref[2:3, :] * z
    y_conv = _cols(proj, _OFF_GB, CONV_CH) * conv
    zbuf[0:SUBLANES, :] = zbuf[tb:tb + SUBLANES, :]

    x1, h2, topi, topg = _merge_and_route(x, o_bf, y_conv, proj, mod, n2g_ref, w_ao_ref, w_co_ref,
                                          w_o_ref, wr_hi_ref, wr_lo_ref, br_ref)
    x1_ref[0] = x1
    h2_ref[0] = h2.astype(BF16)
    topi_ref[0] = topi
    topg_ref[0] = topg


def _prompt_mixer(x, mod, rope, sinks, weights, tb):
    nb, seq, _ = x.shape
    (n1g, n2g, w_in, bd, qg, kg, w_ao, cw, cb, w_co, w_o, wr_hi, wr_lo, br) = weights
    consts = (n1g, n2g, w_in, bd, qg, kg, w_ao, cw, cb, w_co, w_o, wr_hi, wr_lo, br)
    in_specs = [
        pl.BlockSpec((1, tb, D_MODEL), lambda b, j, s: (b, j, 0)),
        pl.BlockSpec((1, 6, D_MODEL), lambda b, j, s: (b, 0, 0)),
        pl.BlockSpec((tb, 3 * LANES), lambda b, j, s: (j, 0)),
    ] + [_const_spec(w.shape) for w in consts]
    out_specs = [
        pl.BlockSpec((1, tb, D_MODEL), lambda b, j, s: (b, j, 0)),
        pl.BlockSpec((1, tb, D_MODEL), lambda b, j, s: (b, j, 0)),
        pl.BlockSpec((1, SUBLANES, tb), lambda b, j, s: (b, 0, j)),
        pl.BlockSpec((1, SUBLANES, tb), lambda b, j, s: (b, 0, j)),
        pl.BlockSpec((1, WINDOW, KV_W), lambda b, j, s: (b, 0, 0)),
        pl.BlockSpec((1, WINDOW, KV_W), lambda b, j, s: (b, 0, 0)),
        pl.BlockSpec((1, SUBLANES, CONV_CH), lambda b, j, s: (b, 0, 0)),
    ]
    out_shape = [
        jax.ShapeDtypeStruct((nb, seq, D_MODEL), F32),
        jax.ShapeDtypeStruct((nb, seq, D_MODEL), BF16),
        jax.ShapeDtypeStruct((nb, SUBLANES, seq), jnp.int32),
        jax.ShapeDtypeStruct((nb, SUBLANES, seq), F32),
        jax.ShapeDtypeStruct((nb, WINDOW, KV_W), F32),
        jax.ShapeDtypeStruct((nb, WINDOW, KV_W), F32),
        jax.ShapeDtypeStruct((nb, SUBLANES, CONV_CH), F32),
    ]
    return pl.pallas_call(
        functools.partial(_prompt_body, tb=tb),
        grid_spec=pltpu.PrefetchScalarGridSpec(
            num_scalar_prefetch=1, grid=(nb, seq // tb),
            in_specs=in_specs, out_specs=out_specs,
            scratch_shapes=[pltpu.VMEM((tb + WINDOW, KV_W), F32),
                            pltpu.VMEM((tb + WINDOW, KV_W), F32),
                            pltpu.VMEM((tb + SUBLANES, CONV_CH), F32)]),
        out_shape=out_shape,
        compiler_params=pltpu.CompilerParams(
            dimension_semantics=("arbitrary", "arbitrary"), vmem_limit_bytes=VMEM_LIMIT),
        name="prompt_mixer",
    )(sinks, x, mod, rope, *consts)


def _sample_body(sinks_ref, x_ref, mod_ref, rope_ref, ck_ref, cv_ref, st_ref,
                 n1g_ref, n2g_ref, w_in_ref, bd_ref, qg_ref, kg_ref,
                 w_ao_ref, cw_ref, cb_ref, w_co_ref, w_o_ref, wr_hi_ref, wr_lo_ref, br_ref,
                 x1_ref, h2_ref, topi_ref, topg_ref, knew_ref, vnew_ref, z_ref, *, nseq, dseq):
    rows = nseq * dseq
    wbuf = ck_ref.shape[1]
    x = x_ref[...]
    mod = tuple(mod_ref[i] for i in range(6))
    h_bf = _modulated_norm(x, n1g_ref[...], mod[0], mod[1]).astype(BF16)
    proj = {}
    for group in (_QKV_COLS,) + _LATE_COLS:
        _in_proj(proj, h_bf, w_in_ref, group)
    rope = rope_ref[...]
    bd = bd_ref[...]

    k = _rope(_head_norm(_cols(proj, _OFF_K, KV_W), bd, kg_ref[...]), rope)
    v = _cols(proj, _OFF_V, KV_W)
    knew_ref[...] = k
    vnew_ref[...] = v
    kn_exp = _pair_expand(k)
    vn_exp = _pair_expand(v)
    kc_exp = _pair_expand(ck_ref[...].reshape(nseq * wbuf, KV_W))
    vc_exp = _pair_expand(cv_ref[...].reshape(nseq * wbuf, KV_W))

    qrow = lax.broadcasted_iota(jnp.int32, (rows, nseq * wbuf), 0)
    ccol = lax.broadcasted_iota(jnp.int32, (rows, nseq * wbuf), 1)
    mask_c = ((ccol // wbuf) == (qrow // dseq)) & ((ccol % wbuf) > (qrow % dseq) + (wbuf - WINDOW))
    qrow_n = lax.broadcasted_iota(jnp.int32, (rows, rows), 0)
    ncol = lax.broadcasted_iota(jnp.int32, (rows, rows), 1)
    mask_n = ((ncol // dseq) == (qrow_n // dseq)) & ((ncol % dseq) <= (qrow_n % dseq))

    qn = _head_norm(_cols(proj, _OFF_Q, Q_W), bd, qg_ref[...])
    o_tiles = []
    for t in range(Q_W // LANES):
        c = t // 2
        qt = (_rope(qn[:, t * LANES:(t + 1) * LANES], rope) * ATTN_SCALE).astype(BF16)
        o_t = None
        for half in range(2):
            s_c = jnp.where(mask_c, _dot_nt(qt, kc_exp[c][half]), NEG_INF)
            s_n = jnp.where(mask_n, _dot_nt(qt, kn_exp[c][half]), NEG_INF)
            sink = sinks_ref[c * GROUP + (t % 2) * 2 + half]
            part = _softmax_pv([s_c, s_n], [vc_exp[c][half], vn_exp[c][half]], sink)
            o_t = part if o_t is None else o_t + part
        o_tiles.append(o_t.astype(BF16))
    o_bf = jnp.concatenate(o_tiles, axis=1)

    z = _cols(proj, _OFF_GC, CONV_CH) * _cols(proj, _OFF_XC, CONV_CH)
    z_ref[...] = z
    r = lax.broadcasted_iota(jnp.int32, z.shape, 0) % dseq
    st0 = st_ref[0]
    st1 = st_ref[1]
    z1 = jnp.where(r == 0, st1, pltpu.roll(z, 1, 0))
    z2 = jnp.where(r == 0, st0, jnp.where(r == 1, st1, pltpu.roll(z, 2, 0)))
    conv = cb_ref[...] + cw_ref[0:1, :] * z2 + cw_ref[1:2, :] * z1 + cw_ref[2:3, :] * z
    y_conv = _cols(proj, _OFF_GB, CONV_CH) * conv

    x1, h2, topi, topg = _merge_and_route(x, o_bf, y_conv, proj, mod, n2g_ref, w_ao_ref, w_co_ref,
                                          w_o_ref, wr_hi_ref, wr_lo_ref, br_ref)
    x1_ref[...] = x1
    h2_ref[...] = h2.astype(BF16)
    topi_ref[...] = topi
    topg_ref[...] = topg


def _sample_mixer(x, mod, rope, cache_k, cache_v, state, sinks, weights, nseq):
    ntok = x.shape[0]
    nall, wbuf, _ = cache_k.shape
    dseq = ntok // nall
    rows = nseq * dseq
    consts = weights
    in_specs = [
        pl.BlockSpec((rows, D_MODEL), lambda i, s: (i, 0)),
        pl.BlockSpec((6, rows, D_MODEL), lambda i, s: (0, i, 0)),
        pl.BlockSpec((rows, 3 * LANES), lambda i, s: (i, 0)),
        pl.BlockSpec((nseq, wbuf, KV_W), lambda i, s: (i, 0, 0)),
        pl.BlockSpec((nseq, wbuf, KV_W), lambda i, s: (i, 0, 0)),
        pl.BlockSpec((2, rows, CONV_CH), lambda i, s: (0, i, 0)),
    ] + [_const_spec(w.shape) for w in consts]
    out_specs = [
        pl.BlockSpec((rows, D_MODEL), lambda i, s: (i, 0)),
        pl.BlockSpec((rows, D_MODEL), lambda i, s: (i, 0)),
        pl.BlockSpec((SUBLANES, rows), lambda i, s: (0, i)),
        pl.BlockSpec((SUBLANES, rows), lambda i, s: (0, i)),
        pl.BlockSpec((rows, KV_W), lambda i, s: (i, 0)),
        pl.BlockSpec((rows, KV_W), lambda i, s: (i, 0)),
        pl.BlockSpec((rows, CONV_CH), lambda i, s: (i, 0)),
    ]
    out_shape = [
        jax.ShapeDtypeStruct((ntok, D_MODEL), F32),
        jax.ShapeDtypeStruct((ntok, D_MODEL), BF16),
        jax.ShapeDtypeStruct((SUBLANES, ntok), jnp.int32),
        jax.ShapeDtypeStruct((SUBLANES, ntok), F32),
        jax.ShapeDtypeStruct((ntok, KV_W), F32),
        jax.ShapeDtypeStruct((ntok, KV_W), F32),
        jax.ShapeDtypeStruct((ntok, CONV_CH), F32),
    ]
    return pl.pallas_call(
        functools.partial(_sample_body, nseq=nseq, dseq=dseq),
        grid_spec=pltpu.PrefetchScalarGridSpec(
            num_scalar_prefetch=1, grid=(ntok // rows,),
            in_specs=in_specs, out_specs=out_specs),
        out_shape=out_shape,
        compiler_params=pltpu.CompilerParams(
            dimension_semantics=("arbitrary",), vmem_limit_bytes=VMEM_LIMIT),
        name="sample_mixer",
    )(sinks, x, mod, rope, cache_k, cache_v, state, *consts)


SORT_TOKENS = 256
RUN_ALIGN = SUBLANES
EXPERT_TILE = 512
HALF_D = D_MODEL // 2
LOCAL_ROWS = -(-(SORT_TOKENS * TOP_K + N_EXPERTS * (RUN_ALIGN - 1)) // LANES) * LANES
_HI_MASK = 0xFFFF0000


def _pack_halves(x):
    lo = pltpu.bitcast(x[:, :HALF_D], jnp.uint32)
    hi = pltpu.bitcast(x[:, HALF_D:], jnp.uint32)
    return hi | (lo >> 16)


def _unpack_halves(w):
    lo = pltpu.bitcast(w << 16, F32).astype(BF16)
    hi = pltpu.bitcast(w & jnp.uint32(_HI_MASK), F32).astype(BF16)
    return lo, hi


def _local_positions(topi, lstart_col, tri):
    ntok = topi.shape[1]
    eid = lax.broadcasted_iota(jnp.int32, (N_EXPERTS, ntok), 0)
    hits = [eid == topi[k:k + 1, :] for k in range(TOP_K)]
    chosen = jnp.zeros((N_EXPERTS, ntok), F32)
    for h in hits:
        chosen = jnp.where(h, 1.0, chosen)
    base = _dot(chosen.astype(BF16), tri) + lstart_col
    return [jnp.sum(jnp.where(h, base, 0.0), axis=0, keepdims=True) for h in hits]


def _run_copy(loc, glob, sem, lofs, gofs, n, to_global):
    lo = loc.at[pl.ds(pl.multiple_of(lofs, RUN_ALIGN), n)]
    gl = glob.at[pl.ds(pl.multiple_of(gofs, RUN_ALIGN), n)]
    return pltpu.make_async_copy(lo, gl, sem) if to_global else pltpu.make_async_copy(gl, lo, sem)


def _start_runs(meta, blk, loc, glob, sem, to_global):
    lstart_ref, goff_ref, cpad_ref = meta

    def issue(e, carry):
        n = pl.multiple_of(cpad_ref[blk * N_EXPERTS + e], RUN_ALIGN)

        @pl.when(n > 0)
        def _():
            _run_copy(loc, glob, sem, lstart_ref[blk * N_EXPERTS + e], goff_ref[blk * N_EXPERTS + e], n,
                      to_global).start()
        return carry
    lax.fori_loop(0, N_EXPERTS, issue, 0)


def _wait_runs(nrows, loc, glob, sem, to_global):
    n = pl.multiple_of(nrows, RUN_ALIGN)

    @pl.when(n > 0)
    def _():
        _run_copy(loc, glob, sem, 0, 0, n, to_global).wait()


def _sort_body(lstart_ref, goff_ref, cpad_ref, ltot_ref, fstart_ref, flen_ref, nact_ref,
               hp_ref, hs_ref, topi_ref, lcol_ref, tri_ref, xs_hbm, buf, zbuf, sems, fsem, *, nblk_p, nblk):
    i = pl.program_id(0)
    slot = i % 2
    meta = (lstart_ref, goff_ref, cpad_ref)

    @pl.when(i == 0)
    def _():
        zbuf[...] = jnp.zeros(zbuf.shape, jnp.uint32)

        def fill(e, tot):
            n = pl.multiple_of(flen_ref[e], RUN_ALIGN)

            @pl.when(n > 0)
            def _():
                _run_copy(zbuf, xs_hbm, fsem, 0, fstart_ref[e], n, True).start()
            return tot + n
        total = lax.fori_loop(0, N_EXPERTS, fill, 0)
        _wait_runs(total, zbuf, xs_hbm, fsem, True)

        def fill_tile(t, carry):
            cp = _run_copy(zbuf, xs_hbm, fsem, 0, t * EXPERT_TILE, EXPERT_TILE, True)
            cp.start()
            cp.wait()
            return carry
        lax.fori_loop(nact_ref[0], xs_hbm.shape[0] // EXPERT_TILE, fill_tile, 0)

    def run(h_ref):
        pos = _local_positions(topi_ref[...], lcol_ref[...], tri_ref[...])
        rowid = lax.broadcasted_iota(jnp.int32, (LOCAL_ROWS, SORT_TOKENS), 0)
        onehot = jnp.zeros((LOCAL_ROWS, SORT_TOKENS), F32)
        for p in pos:
            onehot = jnp.where(rowid == p.astype(jnp.int32), 1.0, onehot)
        buf[slot] = _pack_halves(_dot(onehot.astype(BF16), h_ref[...]))

    @pl.when(i < nblk_p)
    def _():
        run(hp_ref)

    @pl.when(i >= nblk_p)
    def _():
        run(hs_ref)

    _start_runs(meta, i, buf.at[slot], xs_hbm, sems.at[slot], True)

    @pl.when(i > 0)
    def _():
        _wait_runs(ltot_ref[jnp.maximum(i - 1, 0)], buf.at[1 - slot], xs_hbm, sems.at[1 - slot], True)

    @pl.when(i == nblk - 1)
    def _():
        _wait_runs(ltot_ref[i], buf.at[slot], xs_hbm, sems.at[slot], True)


def _sort_tokens(plan, h_p, h_s, topi_all, tri, n_slots):
    nblk_p = h_p.shape[0] // SORT_TOKENS
    nblk = nblk_p + h_s.shape[0] // SORT_TOKENS
    return pl.pallas_call(
        functools.partial(_sort_body, nblk_p=nblk_p, nblk=nblk),
        grid_spec=pltpu.PrefetchScalarGridSpec(
            num_scalar_prefetch=7, grid=(nblk,),
            in_specs=[pl.BlockSpec((SORT_TOKENS, D_MODEL), lambda i, *_: (jnp.minimum(i, nblk_p - 1), 0)),
                      pl.BlockSpec((SORT_TOKENS, D_MODEL), lambda i, *_: (jnp.maximum(i - nblk_p, 0), 0)),
                      pl.BlockSpec((SUBLANES, SORT_TOKENS), lambda i, *_: (0, i)),
                      pl.BlockSpec((None, N_EXPERTS, 1), lambda i, *_: (i, 0, 0)),
                      _const_spec(tri.shape)],
            out_specs=pl.BlockSpec(memory_space=pl.ANY),
            scratch_shapes=[pltpu.VMEM((2, LOCAL_ROWS, HALF_D), jnp.uint32),
                            pltpu.VMEM((EXPERT_TILE, HALF_D), jnp.uint32),
                            pltpu.SemaphoreType.DMA((2,)),
                            pltpu.SemaphoreType.DMA]),
        out_shape=jax.ShapeDtypeStruct((n_slots, HALF_D), jnp.uint32),
        compiler_params=pltpu.CompilerParams(
            dimension_semantics=("arbitrary",), vmem_limit_bytes=VMEM_LIMIT),
        name="moe_sort",
    )(plan["lstart"], plan["goff"], plan["cpad"], plan["ltot"], plan["fill_start"], plan["fill_len"], plan["nact"],
      h_p, h_s, topi_all, plan["lstart_col"], tri)


def _experts_body(tile_exp_ref, nact_ref, xs_ref, wg_ref, bg_ref, wu_ref, bu_ref, wd_ref, bd_ref,
                  ys_ref, wg_bf, wu_bf, wd_bf):
    t = pl.program_id(0)
    nact = nact_ref[0]

    @pl.when(t < nact)
    def _():
        prev = tile_exp_ref[jnp.maximum(t - 1, 0)]

        @pl.when((t == 0) | (tile_exp_ref[t] != prev))
        def _():
            wg_bf[...] = wg_ref[0].astype(BF16)
            wu_bf[...] = wu_ref[0].astype(BF16)
            wd_bf[...] = wd_ref[0].astype(BF16)

        x_lo, x_hi = _unpack_halves(xs_ref[...])
        g = _dot(x_lo, wg_bf[:HALF_D, :]) + _dot(x_hi, wg_bf[HALF_D:, :]) + bg_ref[0]
        u = _dot(x_lo, wu_bf[:HALF_D, :]) + _dot(x_hi, wu_bf[HALF_D:, :]) + bu_ref[0]
        g = jnp.minimum(g, SWIGLU_LIMIT)
        u = jnp.clip(u, -SWIGLU_LIMIT, SWIGLU_LIMIT)
        a = g * _sigmoid(SWIGLU_ALPHA * g) * (u + 1.0)
        out = _dot(a.astype(BF16), wd_bf[...]) + bd_ref[0]
        ys_ref[...] = _pack_halves(out.astype(BF16).astype(F32))

    @pl.when(t >= nact)
    def _():
        ys_ref[...] = jnp.zeros(ys_ref.shape, jnp.uint32)


def _experts(tile_exp, nact, xs, wg, bg, wu, bu, wd, bd):
    n_tiles = tile_exp.shape[0]

    def active(t, na):
        return jnp.minimum(t, na[0] - 1)
    wspec = pl.BlockSpec((1, D_MODEL, D_MODEL), lambda t, te, na: (te[active(t, na)], 0, 0))
    bspec = pl.BlockSpec((1, 1, D_MODEL), lambda t, te, na: (te[active(t, na)], 0, 0))
    xspec = pl.BlockSpec((EXPERT_TILE, HALF_D), lambda t, te, na: (active(t, na), 0))
    return pl.pallas_call(
        _experts_body,
        grid_spec=pltpu.PrefetchScalarGridSpec(
            num_scalar_prefetch=2, grid=(n_tiles,),
            in_specs=[xspec, wspec, bspec, wspec, bspec, wspec, bspec],
            out_specs=pl.BlockSpec((EXPERT_TILE, HALF_D), lambda t, te, na: (t, 0)),
            scratch_shapes=[pltpu.VMEM((D_MODEL, D_MODEL), BF16)] * 3),
        out_shape=jax.ShapeDtypeStruct(xs.shape, jnp.uint32),
        compiler_params=pltpu.CompilerParams(
            dimension_semantics=("arbitrary",), vmem_limit_bytes=VMEM_LIMIT),
        name="experts",
    )(tile_exp, nact, xs, wg, bg, wu, bu, wd, bd)


def _unsort_body(lstart_ref, goff_ref, cpad_ref, ltot_ref,
                 ys_hbm, x1_ref, g2_ref, topi_ref, topg_ref, lcol_ref, tri_ref, y_ref, buf, sems, *, blk0, nblk):
    i = pl.program_id(0)
    b = blk0 + i
    slot = i % 2
    meta = (lstart_ref, goff_ref, cpad_ref)

    @pl.when(i == 0)
    def _():
        buf[...] = jnp.zeros(buf.shape, jnp.uint32)
        _start_runs(meta, b, buf.at[0], ys_hbm, sems.at[0], False)

    @pl.when(i + 1 < nblk)
    def _():
        _start_runs(meta, b + 1, buf.at[1 - slot], ys_hbm, sems.at[1 - slot], False)

    pos = _local_positions(topi_ref[...], lcol_ref[...], tri_ref[...])
    rowid = lax.broadcasted_iota(jnp.int32, (SUBLANES, SORT_TOKENS), 0)
    pos_rows = jnp.zeros((SUBLANES, SORT_TOKENS), F32)
    for k in range(TOP_K):
        pos_rows = jnp.where(rowid == k, pos[k], pos_rows)
    pos_cols = pos_rows.T.astype(jnp.int32)
    gate_cols = topg_ref[...].T
    colid = lax.broadcasted_iota(jnp.int32, (SORT_TOKENS, LOCAL_ROWS), 1)
    weights = jnp.zeros((SORT_TOKENS, LOCAL_ROWS), F32)
    for k in range(TOP_K):
        weights = jnp.where(colid == pos_cols[:, k:k + 1], gate_cols[:, k:k + 1], weights)
    weights = weights.astype(BF16)

    _wait_runs(ltot_ref[b], buf.at[slot], ys_hbm, sems.at[slot], False)
    y_lo, y_hi = _unpack_halves(buf[slot])
    g2 = g2_ref[...]
    y_ref[:, :HALF_D] = x1_ref[:, :HALF_D] + g2[:, :HALF_D] * _dot(weights, y_lo)
    y_ref[:, HALF_D:] = x1_ref[:, HALF_D:] + g2[:, HALF_D:] * _dot(weights, y_hi)


def _unsort_tokens(plan, ys, x1, g2, g2_spec, topi_all, topg_all, tri, blk0):
    nblk = x1.shape[0] // SORT_TOKENS
    return pl.pallas_call(
        functools.partial(_unsort_body, blk0=blk0, nblk=nblk),
        grid_spec=pltpu.PrefetchScalarGridSpec(
            num_scalar_prefetch=4, grid=(nblk,),
            in_specs=[pl.BlockSpec(memory_space=pl.ANY),
                      pl.BlockSpec((SORT_TOKENS, D_MODEL), lambda i, *_: (i, 0)),
                      g2_spec,
                      pl.BlockSpec((SUBLANES, SORT_TOKENS), lambda i, *_: (0, blk0 + i)),
                      pl.BlockSpec((SUBLANES, SORT_TOKENS), lambda i, *_: (0, blk0 + i)),
                      pl.BlockSpec((None, N_EXPERTS, 1), lambda i, *_: (blk0 + i, 0, 0)),
                      _const_spec(tri.shape)],
            out_specs=pl.BlockSpec((SORT_TOKENS, D_MODEL), lambda i, *_: (i, 0)),
            scratch_shapes=[pltpu.VMEM((2, LOCAL_ROWS, HALF_D), jnp.uint32),
                            pltpu.SemaphoreType.DMA((2,))]),
        out_shape=jax.ShapeDtypeStruct(x1.shape, F32),
        compiler_params=pltpu.CompilerParams(
            dimension_semantics=("arbitrary",), vmem_limit_bytes=VMEM_LIMIT),
        name="moe_unsort",
    )(plan["lstart"], plan["goff"], plan["cpad"], plan["ltot"],
      ys, x1, g2, topi_all, topg_all, plan["lstart_col"], tri)


def _rope_table(pos):
    half = ROT_DIM // 2
    inv_freq = ROPE_THETA ** (-jnp.arange(0, ROT_DIM, 2, dtype=F32) / ROT_DIM)
    ang = pos.astype(F32)[:, None] * inv_freq[None, :]
    cos, sin = jnp.cos(ang), jnp.sin(ang)
    n = pos.shape[0]
    rest = HEAD_DIM - ROT_DIM
    c = jnp.concatenate([cos, cos, jnp.ones((n, rest), F32)], axis=1)
    s_next = jnp.concatenate([-sin, jnp.zeros((n, half + rest), F32)], axis=1)
    s_prev = jnp.concatenate([jnp.zeros((n, half), F32), sin, jnp.zeros((n, rest), F32)], axis=1)
    rep = LANES // HEAD_DIM
    return jnp.concatenate([jnp.tile(c, (1, rep)), jnp.tile(s_next, (1, rep)), jnp.tile(s_prev, (1, rep))], axis=1)


def _num_expert_tiles(ntok):
    nblk = ntok // SORT_TOKENS
    worst = ntok * TOP_K + nblk * N_EXPERTS * (RUN_ALIGN - 1) + N_EXPERTS * (EXPERT_TILE - 1)
    return -(-worst // EXPERT_TILE)


def _routing_plan(topi_all):
    ntok = topi_all.shape[1]
    nblk = ntok // SORT_TOKENS
    n_tiles = _num_expert_tiles(ntok)
    choice = topi_all[:TOP_K].reshape(TOP_K, nblk, SORT_TOKENS)
    experts = jnp.arange(N_EXPERTS, dtype=jnp.int32)
    cnt = (choice[:, :, :, None] == experts).astype(jnp.int32).sum(axis=(0, 2))
    cpad = -(-cnt // RUN_ALIGN) * RUN_ALIGN
    lstart = jnp.cumsum(cpad, axis=1) - cpad
    tot = cpad.sum(axis=0)
    region = -(-tot // EXPERT_TILE) * EXPERT_TILE
    region_end = jnp.cumsum(region)
    gstart = region_end - region
    goff = gstart[None, :] + jnp.cumsum(cpad, axis=0) - cpad
    tile_start = jnp.arange(n_tiles, dtype=jnp.int32) * EXPERT_TILE
    tile_exp = jnp.minimum((tile_start[:, None] >= region_end[None, :]).astype(jnp.int32).sum(axis=1), N_EXPERTS - 1)
    i32 = lambda a: a.astype(jnp.int32)
    return dict(
        lstart=i32(lstart).reshape(-1), goff=i32(goff).reshape(-1), cpad=i32(cpad).reshape(-1),
        ltot=i32(cpad.sum(axis=1)), fill_start=i32(gstart + tot), fill_len=i32(region - tot),
        lstart_col=lstart.astype(F32).reshape(nblk, N_EXPERTS, 1),
        tile_exp=i32(tile_exp), nact=i32(region_end[-1] // EXPERT_TILE).reshape(1))


def kernel(x_prompt, x_sample, cache_k_win, cache_v_win, state_conv, c_prompt, c_sample, w_ada, b_ada, norm1_g, norm2_g, w_in, q_norm_g, k_norm_g, attn_sinks, w_attn_out, conv_w, conv_b, w_conv_out, w_o, w_router, b_router, w_gate, b_gate, w_up, b_up, w_down, b_down):
    nb, seq, _ = x_prompt.shape
    ns, dseq, _ = x_sample.shape
    wbuf = cache_k_win.shape[2]
    assert w_ada.shape[0] == 1, "single-layer step"
    l = 0

    mod = _adaln(jnp.concatenate([c_prompt, c_sample], axis=0), w_ada[l], b_ada[l])
    mod_p = mod[:nb].reshape(nb, 6, D_MODEL)
    mod_s = jnp.repeat(mod[nb:].reshape(ns, 6, D_MODEL).transpose(1, 0, 2), dseq, axis=1)

    bd = jnp.asarray(np.kron(np.eye(N_HEADS), np.ones((HEAD_DIM, HEAD_DIM))), BF16)
    rep = LANES // HEAD_DIM
    wr_t = w_router[l].T
    wr_hi = wr_t.astype(BF16)
    wr_lo = (wr_t - wr_hi.astype(F32)).astype(BF16)
    weights = (norm1_g[l].reshape(1, D_MODEL), norm2_g[l].reshape(1, D_MODEL), w_in[l].astype(BF16), bd,
               jnp.tile(q_norm_g[l], N_HEADS).reshape(1, Q_W), jnp.tile(k_norm_g[l], N_KV).reshape(1, KV_W),
               w_attn_out[l].astype(BF16), conv_w[l], conv_b[l].reshape(1, CONV_CH), w_conv_out[l].astype(BF16),
               w_o[l].astype(BF16), wr_hi, wr_lo, b_router[l].reshape(N_EXPERTS, 1))
    sinks = attn_sinks[l]

    tb = 512
    (x1_p, h2_p, topi_p, topg_p, kwin_p, vwin_p, zwin_p) = _prompt_mixer(
        x_prompt, mod_p, _rope_table(jnp.arange(seq)), sinks, weights, tb)
    pos_s = jnp.tile(PAST_LEN + jnp.arange(dseq), ns)
    state_rows = jnp.repeat(state_conv[l].transpose(1, 0, 2), dseq, axis=1)
    (x1_s, h2_s, topi_s, topg_s, knew, vnew, z_s) = _sample_mixer(
        x_sample.reshape(ns * dseq, D_MODEL), mod_s, _rope_table(pos_s),
        cache_k_win[l].reshape(ns, wbuf, KV_W), cache_v_win[l].reshape(ns, wbuf, KV_W),
        state_rows, sinks, weights, 16)

    ntok_p = nb * seq
    ntok = ntok_p + ns * dseq
    topi_all = jnp.concatenate([topi_p.transpose(1, 0, 2).reshape(SUBLANES, ntok_p), topi_s], axis=1)
    topg_all = jnp.concatenate([topg_p.transpose(1, 0, 2).reshape(SUBLANES, ntok_p), topg_s], axis=1)
    plan = _routing_plan(topi_all)
    tri = jnp.asarray(np.triu(np.ones((SORT_TOKENS, SORT_TOKENS)), k=1), BF16)
    xs = _sort_tokens(plan, h2_p.reshape(ntok_p, D_MODEL), h2_s, topi_all, tri,
                      _num_expert_tiles(ntok) * EXPERT_TILE)
    ys = _experts(plan["tile_exp"], plan["nact"], xs,
                  w_gate[l], b_gate[l].reshape(N_EXPERTS, 1, D_MODEL),
                  w_up[l], b_up[l].reshape(N_EXPERTS, 1, D_MODEL),
                  w_down[l], b_down[l].reshape(N_EXPERTS, 1, D_MODEL))
    blocks_per_seq = seq // SORT_TOKENS
    y_p = _unsort_tokens(plan, ys, x1_p.reshape(ntok_p, D_MODEL), mod_p[:, 5:6, :],
                         pl.BlockSpec((None, 1, D_MODEL), lambda i, *_: (i // blocks_per_seq, 0, 0)),
                         topi_all, topg_all, tri, 0)
    y_s = _unsort_tokens(plan, ys, x1_s, mod_s[5],
                         pl.BlockSpec((SORT_TOKENS, D_MODEL), lambda i, *_: (i, 0)),
                         topi_all, topg_all, tri, ntok_p // SORT_TOKENS)

    n_kv_shape = (N_KV, HEAD_DIM)
    k_win_p = kwin_p.reshape(1, nb, WINDOW, *n_kv_shape)
    v_win_p = vwin_p.reshape(1, nb, WINDOW, *n_kv_shape)
    conv_p = zwin_p[:, SUBLANES - (CONV_K - 1):, :][None]
    k_win_s = jnp.concatenate([cache_k_win[l][:, dseq:], knew.reshape(ns, dseq, *n_kv_shape)], axis=1)[None]
    v_win_s = jnp.concatenate([cache_v_win[l][:, dseq:], vnew.reshape(ns, dseq, *n_kv_shape)], axis=1)[None]
    conv_s = z_s.reshape(ns, dseq, CONV_CH)[:, dseq - (CONV_K - 1):, :][None]
    return (y_p.reshape(nb, seq, D_MODEL), y_s.reshape(ns, dseq, D_MODEL),
            k_win_p, v_win_p, conv_p, k_win_s, v_win_s, conv_s)
```

```python
import functools

import numpy as np
import jax
import jax.numpy as jnp
from jax import lax
from jax.experimental import pallas as pl
from jax.experimental.pallas import tpu as pltpu

D_MODEL = 1024
HEAD_DIM = 64
N_HEADS = 8
N_KV = 2
GROUP = N_HEADS // N_KV
Q_W = N_HEADS * HEAD_DIM
KV_W = N_KV * HEAD_DIM
WINDOW = 128
ROT_DIM = HEAD_DIM // 4
ROPE_THETA = 500000.0
ATTN_SCALE = HEAD_DIM ** -0.5
CONV_CH = D_MODEL // 2
CONV_K = 3
N_EXPERTS = 32
TOP_K = 4
SWIGLU_ALPHA = 1.702
SWIGLU_LIMIT = 7.0
MOE_BLOCK = 128
NORM_EPS = 1e-5
QK_EPS = 1e-6
NEG_INF = -1e30
PAST_LEN = 16384

LANES = 128
SUBLANES = 8
VMEM_LIMIT = 56 * 1024 * 1024

_OFF_Q = 0
_OFF_K = _OFF_Q + Q_W
_OFF_V = _OFF_K + KV_W
_OFF_GB = _OFF_V + KV_W
_OFF_GC = _OFF_GB + CONV_CH
_OFF_XC = _OFF_GC + CONV_CH
_OFF_BA = _OFF_XC + CONV_CH
_OFF_BC = _OFF_BA + D_MODEL
IN_COLS = _OFF_BC + D_MODEL

BF16 = jnp.bfloat16
F32 = jnp.float32
_NT = (((1,), (1,)), ((), ()))


def _dot(a, b):
    return jnp.dot(a, b, preferred_element_type=F32)


def _dot_nt(a, b):
    return lax.dot_general(a, b, _NT, preferred_element_type=F32)


def _sigmoid(x):
    return 1.0 / (1.0 + jnp.exp(-x))


def _split_bf16(x):
    hi = x.astype(BF16)
    lo = (x - hi.astype(F32)).astype(BF16)
    return hi, lo


def _const_spec(shape):
    nd = len(shape)
    return pl.BlockSpec(shape, lambda *_: (0,) * nd, pipeline_mode=pl.Buffered(1))


def _adaln_body(c_ref, w_ref, b_ref, o_ref):
    c = c_ref[...]
    s = c * _sigmoid(c)
    o_ref[...] = _dot(s.astype(BF16), w_ref[...].astype(BF16)) + b_ref[...]


def _adaln(c, w_ada, b_ada):
    n = c.shape[0]
    cols = w_ada.shape[1]
    bn = 1536
    return pl.pallas_call(
        _adaln_body,
        grid=(cols // bn,),
        in_specs=[pl.BlockSpec((n, D_MODEL), lambda i: (0, 0)),
                  pl.BlockSpec((D_MODEL, bn), lambda i: (0, i)),
                  pl.BlockSpec((1, bn), lambda i: (0, i))],
        out_specs=pl.BlockSpec((n, bn), lambda i: (0, i)),
        out_shape=jax.ShapeDtypeStruct((n, cols), F32),
        compiler_params=pltpu.CompilerParams(vmem_limit_bytes=VMEM_LIMIT),
        name="adaln",
    )(c, w_ada, b_ada.reshape(1, cols))


def _modulated_norm(x, g, shift, scale):
    y = x * lax.rsqrt(jnp.mean(x * x, axis=-1, keepdims=True) + NORM_EPS)
    return (y * g) * (1.0 + scale) + shift


def _head_norm(t, bd, g):
    w = t.shape[1]
    sq_hi, sq_lo = _split_bf16(t * t)
    blk = bd[:w, :w]
    ms = (_dot(sq_hi, blk) + _dot(sq_lo, blk)) * (1.0 / HEAD_DIM)
    return t * lax.rsqrt(ms + QK_EPS) * g[:, :w]


def _rope(y, rope):
    c = rope[:, 0:LANES]
    s_next = rope[:, LANES:2 * LANES]
    s_prev = rope[:, 2 * LANES:3 * LANES]
    half = ROT_DIM // 2
    return y * c + pltpu.roll(y, LANES - half, 1) * s_next + pltpu.roll(y, half, 1) * s_prev


def _pair_expand(t):
    lane = lax.broadcasted_iota(jnp.int32, t.shape, 1)
    lo = lane < HEAD_DIM
    r = pltpu.roll(t, HEAD_DIM, 1)
    zero = jnp.zeros_like(t)
    a0 = jnp.where(lo, t, zero).astype(BF16)
    b0 = jnp.where(lo, zero, r).astype(BF16)
    a1 = jnp.where(lo, r, zero).astype(BF16)
    b1 = jnp.where(lo, zero, t).astype(BF16)
    return ((a0, b0), (a1, b1))


def _softmax_pv(s_list, v_list, sink):
    m = jnp.full((s_list[0].shape[0], 1), sink, F32)
    for s in s_list:
        m = jnp.maximum(m, jnp.max(s, axis=-1, keepdims=True))
    den = jnp.exp(sink - m)
    acc = None
    for s, v in zip(s_list, v_list):
        e = jnp.exp(s - m)
        den = den + jnp.sum(e, axis=-1, keepdims=True)
        pv = _dot(e.astype(BF16), v)
        acc = pv if acc is None else acc + pv
    return acc * (1.0 / den)


def _route(h2, wr_hi, wr_lo, br):
    hi, lo = _split_bf16(h2)
    logits = _dot_nt(wr_hi, hi) + _dot_nt(wr_hi, lo) + _dot_nt(wr_lo, hi) + br
    eid = lax.broadcasted_iota(jnp.int32, logits.shape, 0).astype(F32)
    vals, idxs = [], []
    for _ in range(TOP_K):
        m = jnp.max(logits, axis=0, keepdims=True)
        idx = jnp.min(jnp.where(logits == m, eid, float(N_EXPERTS)), axis=0, keepdims=True)
        vals.append(m)
        idxs.append(idx)
        logits = jnp.where(eid == idx, -jnp.inf, logits)
    ex = [jnp.exp(v - vals[0]) for v in vals]
    tot = ex[0] + ex[1] + ex[2] + ex[3]
    inv = 1.0 / tot
    gates = [e * inv for e in ex]
    rowid = lax.broadcasted_iota(jnp.int32, (SUBLANES, h2.shape[0]), 0)
    topi = jnp.zeros((SUBLANES, h2.shape[0]), F32)
    topg = jnp.zeros((SUBLANES, h2.shape[0]), F32)
    for k in range(TOP_K):
        topi = jnp.where(rowid == k, idxs[k], topi)
        topg = jnp.where(rowid == k, gates[k], topg)
    return topi.astype(jnp.int32), topg


_QKV_COLS = (_OFF_Q, Q_W + 2 * KV_W)
_LATE_COLS = ((_OFF_GB, 2 * CONV_CH), (_OFF_XC, CONV_CH), (_OFF_BA, D_MODEL), (_OFF_BC, D_MODEL))


def _in_proj(proj, h_bf, w_in_ref, group):
    off, width = group
    proj[off] = _dot(h_bf, w_in_ref[:, off:off + width])


def _cols(proj, off, width):
    for start, piece in proj.items():
        if start <= off and off + width <= start + piece.shape[1]:
            return piece[:, off - start:off - start + width]
    raise KeyError(off)


def _merge_and_route(x, o_bf, y_conv, proj, mod, n2g_ref, w_ao_ref, w_co_ref, w_o_ref,
                     wr_hi_ref, wr_lo_ref, br_ref):
    sh1, sc1, g1, sh2, sc2, g2 = mod
    a = _dot(o_bf, w_ao_ref[...])
    c = _dot(y_conv.astype(BF16), w_co_ref[...])
    m = _sigmoid(_cols(proj, _OFF_BA, D_MODEL)) * a + _sigmoid(_cols(proj, _OFF_BC, D_MODEL)) * c
    x1 = x + g1 * _dot(m.astype(BF16), w_o_ref[...])
    h2 = _modulated_norm(x1, n2g_ref[...], sh2, sc2)
    topi, topg = _route(h2, wr_hi_ref[...], wr_lo_ref[...], br_ref[...])
    return x1, h2, topi, topg


def _prompt_body(sinks_ref, x_ref, mod_ref, rope_ref, n1g_ref, n2g_ref, w_in_ref, bd_ref, qg_ref, kg_ref,
                 w_ao_ref, cw_ref, cb_ref, w_co_ref, w_o_ref, wr_hi_ref, wr_lo_ref, br_ref,
                 x1_ref, h2_ref, topi_ref, topg_ref, kwin_ref, vwin_ref, zwin_ref,
                 kbuf, vbuf, zbuf, *, tb):
    j = pl.program_id(1)
    nsub = tb // WINDOW

    @pl.when(j == 0)
    def _():
        kbuf[0:WINDOW, :] = jnp.zeros((WINDOW, KV_W), F32)
        vbuf[0:WINDOW, :] = jnp.zeros((WINDOW, KV_W), F32)
        zbuf[0:SUBLANES, :] = jnp.zeros((SUBLANES, CONV_CH), F32)

    x = x_ref[0]
    mod = tuple(mod_ref[0, i:i + 1, :] for i in range(6))
    h_bf = _modulated_norm(x, n1g_ref[...], mod[0], mod[1]).astype(BF16)
    proj = {}
    _in_proj(proj, h_bf, w_in_ref, _QKV_COLS)
    _in_proj(proj, h_bf, w_in_ref, _LATE_COLS[0])
    rope = rope_ref[...]
    bd = bd_ref[...]

    k = _rope(_head_norm(_cols(proj, _OFF_K, KV_W), bd, kg_ref[...]), rope)
    v = _cols(proj, _OFF_V, KV_W)
    kbuf[WINDOW:WINDOW + tb, :] = k
    vbuf[WINDOW:WINDOW + tb, :] = v
    kwin_ref[0] = k[tb - WINDOW:tb]
    vwin_ref[0] = v[tb - WINDOW:tb]
    k_exp = _pair_expand(kbuf[...])
    v_exp = _pair_expand(vbuf[...])

    _in_proj(proj, h_bf, w_in_ref, _LATE_COLS[1])
    qn = _head_norm(_cols(proj, _OFF_Q, Q_W), bd, qg_ref[...])
    q_tiles = []
    for t in range(Q_W // LANES):
        qt = _rope(qn[:, t * LANES:(t + 1) * LANES], rope)
        q_tiles.append((qt * ATTN_SCALE).astype(BF16))

    row = lax.broadcasted_iota(jnp.int32, (WINDOW, 2 * WINDOW), 0)
    col = lax.broadcasted_iota(jnp.int32, (WINDOW, 2 * WINDOW), 1)
    band = (col > row) & (col <= row + WINDOW)
    first_key = jnp.where(j > 0, 0, WINDOW)
    band_first = band & (col >= first_key)
    o_rows = []
    for i in range(nsub):
        for group in _LATE_COLS[2:][i::nsub]:
            _in_proj(proj, h_bf, w_in_ref, group)
        mask = band_first if i == 0 else band
        keys = slice(i * WINDOW, (i + 2) * WINDOW)
        o_tiles = []
        for t in range(Q_W // LANES):
            c = t // 2
            qt = q_tiles[t][i * WINDOW:(i + 1) * WINDOW]
            o_t = None
            for half in range(2):
                s = jnp.where(mask, _dot_nt(qt, k_exp[c][half][keys]), NEG_INF)
                sink = sinks_ref[c * GROUP + (t % 2) * 2 + half]
                part = _softmax_pv([s], [v_exp[c][half][keys]], sink)
                o_t = part if o_t is None else o_t + part
            o_tiles.append(o_t.astype(BF16))
        o_rows.append(jnp.concatenate(o_tiles, axis=1))
    o_bf = jnp.concatenate(o_rows, axis=0) if nsub > 1 else o_rows[0]
    kbuf[0:WINDOW, :] = kbuf[tb:tb + WINDOW, :]
    vbuf[0:WINDOW, :] = vbuf[tb:tb + WINDOW, :]

    z = _cols(proj, _OFF_GC, CONV_CH) * _cols(proj, _OFF_XC, CONV_CH)
    zbuf[SUBLANES:SUBLANES + tb, :] = z
    zwin_ref[0] = z[tb - SUBLANES:tb]
    z1 = zbuf[SUBLANES - 1:SUBLANES - 1 + tb, :]
    z2 = zbuf[SUBLANES - 2:SUBLANES - 2 + tb, :]
    conv = cb_ref[...] + cw_ref[0:1, :] * z2 + cw_ref[1:2, :] * z1 + cw_ref[2:3, :] * z
    y_conv = _cols(proj, _OFF_GB, CONV_CH) * conv
    zbuf[0:SUBLANES, :] = zbuf[tb:tb + SUBLANES, :]

    x1, h2, topi, topg = _merge_and_route(x, o_bf, y_conv, proj, mod, n2g_ref, w_ao_ref, w_co_ref,
                                          w_o_ref, wr_hi_ref, wr_lo_ref, br_ref)
    x1_ref[0] = x1
    h2_ref[0] = h2.astype(BF16)
    topi_ref[0] = topi
    topg_ref[0] = topg


def _prompt_mixer(x, mod, rope, sinks, weights, tb):
    nb, seq, _ = x.shape
    (n1g, n2g, w_in, bd, qg, kg, w_ao, cw, cb, w_co, w_o, wr_hi, wr_lo, br) = weights
    consts = (n1g, n2g, w_in, bd, qg, kg, w_ao, cw, cb, w_co, w_o, wr_hi, wr_lo, br)
    in_specs = [
        pl.BlockSpec((1, tb, D_MODEL), lambda b, j, s: (b, j, 0)),
        pl.BlockSpec((1, 6, D_MODEL), lambda b, j, s: (b, 0, 0)),
        pl.BlockSpec((tb, 3 * LANES), lambda b, j, s: (j, 0)),
    ] + [_const_spec(w.shape) for w in consts]
    out_specs = [
        pl.BlockSpec((1, tb, D_MODEL), lambda b, j, s: (b, j, 0)),
        pl.BlockSpec((1, tb, D_MODEL), lambda b, j, s: (b, j, 0)),
        pl.BlockSpec((1, SUBLANES, tb), lambda b, j, s: (b, 0, j)),
        pl.BlockSpec((1, SUBLANES, tb), lambda b, j, s: (b, 0, j)),
        pl.BlockSpec((1, WINDOW, KV_W), lambda b, j, s: (b, 0, 0)),
        pl.BlockSpec((1, WINDOW, KV_W), lambda b, j, s: (b, 0, 0)),
        pl.BlockSpec((1, SUBLANES, CONV_CH), lambda b, j, s: (b, 0, 0)),
    ]
    out_shape = [
        jax.ShapeDtypeStruct((nb, seq, D_MODEL), F32),
        jax.ShapeDtypeStruct((nb, seq, D_MODEL), BF16),
        jax.ShapeDtypeStruct((nb, SUBLANES, seq), jnp.int32),
        jax.ShapeDtypeStruct((nb, SUBLANES, seq), F32),
        jax.ShapeDtypeStruct((nb, WINDOW, KV_W), F32),
        jax.ShapeDtypeStruct((nb, WINDOW, KV_W), F32),
        jax.ShapeDtypeStruct((nb, SUBLANES, CONV_CH), F32),
    ]
    return pl.pallas_call(
        functools.partial(_prompt_body, tb=tb),
        grid_spec=pltpu.PrefetchScalarGridSpec(
            num_scalar_prefetch=1, grid=(nb, seq // tb),
            in_specs=in_specs, out_specs=out_specs,
            scratch_shapes=[pltpu.VMEM((tb + WINDOW, KV_W), F32),
                            pltpu.VMEM((tb + WINDOW, KV_W), F32),
                            pltpu.VMEM((tb + SUBLANES, CONV_CH), F32)]),
        out_shape=out_shape,
        compiler_params=pltpu.CompilerParams(
            dimension_semantics=("arbitrary", "arbitrary"), vmem_limit_bytes=VMEM_LIMIT),
        name="prompt_mixer",
    )(sinks, x, mod, rope, *consts)


def _sample_body(sinks_ref, x_ref, mod_ref, rope_ref, ck_ref, cv_ref, st_ref,
                 n1g_ref, n2g_ref, w_in_ref, bd_ref, qg_ref, kg_ref,
                 w_ao_ref, cw_ref, cb_ref, w_co_ref, w_o_ref, wr_hi_ref, wr_lo_ref, br_ref,
                 x1_ref, h2_ref, topi_ref, topg_ref, knew_ref, vnew_ref, z_ref, *, nseq, dseq):
    rows = nseq * dseq
    wbuf = ck_ref.shape[1]
    x = x_ref[...]
    mod = tuple(mod_ref[i] for i in range(6))
    h_bf = _modulated_norm(x, n1g_ref[...], mod[0], mod[1]).astype(BF16)
    proj = {}
    for group in (_QKV_COLS,) + _LATE_COLS:
        _in_proj(proj, h_bf, w_in_ref, group)
    rope = rope_ref[...]
    bd = bd_ref[...]

    k = _rope(_head_norm(_cols(proj, _OFF_K, KV_W), bd, kg_ref[...]), rope)
    v = _cols(proj, _OFF_V, KV_W)
    for win_ref, cache_ref, new in ((knew_ref, ck_ref, k), (vnew_ref, cv_ref, v)):
        win_ref[:, 0:wbuf - dseq, :] = cache_ref[:, dseq:, :]
        win_ref[:, wbuf - dseq:, :] = new.reshape(nseq, dseq, KV_W)
    kn_exp = _pair_expand(k)
    vn_exp = _pair_expand(v)
    kc_exp = _pair_expand(ck_ref[...].reshape(nseq * wbuf, KV_W))
    vc_exp = _pair_expand(cv_ref[...].reshape(nseq * wbuf, KV_W))

    qrow = lax.broadcasted_iota(jnp.int32, (rows, nseq * wbuf), 0)
    ccol = lax.broadcasted_iota(jnp.int32, (rows, nseq * wbuf), 1)
    mask_c = ((ccol // wbuf) == (qrow // dseq)) & ((ccol % wbuf) > (qrow % dseq) + (wbuf - WINDOW))
    qrow_n = lax.broadcasted_iota(jnp.int32, (rows, rows), 0)
    ncol = lax.broadcasted_iota(jnp.int32, (rows, rows), 1)
    mask_n = ((ncol // dseq) == (qrow_n // dseq)) & ((ncol % dseq) <= (qrow_n % dseq))

    qn = _head_norm(_cols(proj, _OFF_Q, Q_W), bd, qg_ref[...])
    o_tiles = []
    for t in range(Q_W // LANES):
        c = t // 2
        qt = (_rope(qn[:, t * LANES:(t + 1) * LANES], rope) * ATTN_SCALE).astype(BF16)
        o_t = None
        for half in range(2):
            s_c = jnp.where(mask_c, _dot_nt(qt, kc_exp[c][half]), NEG_INF)
            s_n = jnp.where(mask_n, _dot_nt(qt, kn_exp[c][half]), NEG_INF)
            sink = sinks_ref[c * GROUP + (t % 2) * 2 + half]
            part = _softmax_pv([s_c, s_n], [vc_exp[c][half], vn_exp[c][half]], sink)
            o_t = part if o_t is None else o_t + part
        o_tiles.append(o_t.astype(BF16))
    o_bf = jnp.concatenate(o_tiles, axis=1)

    z = _cols(proj, _OFF_GC, CONV_CH) * _cols(proj, _OFF_XC, CONV_CH)
    z_ref[...] = z
    r = lax.broadcasted_iota(jnp.int32, z.shape, 0) % dseq
    st0 = st_ref[0]
    st1 = st_ref[1]
    z1 = jnp.where(r == 0, st1, pltpu.roll(z, 1, 0))
    z2 = jnp.where(r == 0, st0, jnp.where(r == 1, st1, pltpu.roll(z, 2, 0)))
    conv = cb_ref[...] + cw_ref[0:1, :] * z2 + cw_ref[1:2, :] * z1 + cw_ref[2:3, :] * z
    y_conv = _cols(proj, _OFF_GB, CONV_CH) * conv

    x1, h2, topi, topg = _merge_and_route(x, o_bf, y_conv, proj, mod, n2g_ref, w_ao_ref, w_co_ref,
                                          w_o_ref, wr_hi_ref, wr_lo_ref, br_ref)
    x1_ref[...] = x1
    h2_ref[...] = h2.astype(BF16)
    topi_ref[...] = topi
    topg_ref[...] = topg


def _sample_mixer(x, mod, rope, cache_k, cache_v, state, sinks, weights, nseq):
    ntok = x.shape[0]
    nall, wbuf, _ = cache_k.shape
    dseq = ntok // nall
    rows = nseq * dseq
    consts = weights
    in_specs = [
        pl.BlockSpec((rows, D_MODEL), lambda i, s: (i, 0)),
        pl.BlockSpec((6, rows, D_MODEL), lambda i, s: (0, i, 0)),
        pl.BlockSpec((rows, 3 * LANES), lambda i, s: (i, 0)),
        pl.BlockSpec((nseq, wbuf, KV_W), lambda i, s: (i, 0, 0)),
        pl.BlockSpec((nseq, wbuf, KV_W), lambda i, s: (i, 0, 0)),
        pl.BlockSpec((2, rows, CONV_CH), lambda i, s: (0, i, 0)),
    ] + [_const_spec(w.shape) for w in consts]
    out_specs = [
        pl.BlockSpec((rows, D_MODEL), lambda i, s: (i, 0)),
        pl.BlockSpec((rows, D_MODEL), lambda i, s: (i, 0)),
        pl.BlockSpec((SUBLANES, rows), lambda i, s: (0, i)),
        pl.BlockSpec((SUBLANES, rows), lambda i, s: (0, i)),
        pl.BlockSpec((nseq, wbuf, KV_W), lambda i, s: (i, 0, 0)),
        pl.BlockSpec((nseq, wbuf, KV_W), lambda i, s: (i, 0, 0)),
        pl.BlockSpec((rows, CONV_CH), lambda i, s: (i, 0)),
    ]
    out_shape = [
        jax.ShapeDtypeStruct((ntok, D_MODEL), F32),
        jax.ShapeDtypeStruct((ntok, D_MODEL), BF16),
        jax.ShapeDtypeStruct((SUBLANES, ntok), jnp.int32),
        jax.ShapeDtypeStruct((SUBLANES, ntok), F32),
        jax.ShapeDtypeStruct((nall, wbuf, KV_W), F32),
        jax.ShapeDtypeStruct((nall, wbuf, KV_W), F32),
        jax.ShapeDtypeStruct((ntok, CONV_CH), F32),
    ]
    return pl.pallas_call(
        functools.partial(_sample_body, nseq=nseq, dseq=dseq),
        grid_spec=pltpu.PrefetchScalarGridSpec(
            num_scalar_prefetch=1, grid=(ntok // rows,),
            in_specs=in_specs, out_specs=out_specs),
        out_shape=out_shape,
        compiler_params=pltpu.CompilerParams(
            dimension_semantics=("arbitrary",), vmem_limit_bytes=VMEM_LIMIT),
        name="sample_mixer",
    )(sinks, x, mod, rope, cache_k, cache_v, state, *consts)


SORT_TOKENS = 256
RUN_ALIGN = SUBLANES
EXPERT_TILE = 512
HALF_D = D_MODEL // 2
LOCAL_ROWS = SORT_TOKENS * TOP_K + N_EXPERTS * RUN_ALIGN
UNSORT_CHUNK = 256
assert LOCAL_ROWS % UNSORT_CHUNK == 0
_HI_MASK = 0xFFFF0000


def _pack_halves(x):
    lo = pltpu.bitcast(x[:, :HALF_D], jnp.uint32)
    hi = pltpu.bitcast(x[:, HALF_D:], jnp.uint32)
    return hi | (lo >> 16)


def _unpack_halves(w):
    lo = pltpu.bitcast(w << 16, F32).astype(BF16)
    hi = pltpu.bitcast(w & jnp.uint32(_HI_MASK), F32).astype(BF16)
    return lo, hi


def _local_positions(topi, lstart_col, tri):
    ntok = topi.shape[1]
    eid = lax.broadcasted_iota(jnp.int32, (N_EXPERTS, ntok), 0)
    hits = [eid == topi[k:k + 1, :] for k in range(TOP_K)]
    chosen = jnp.zeros((N_EXPERTS, ntok), F32)
    for h in hits:
        chosen = jnp.where(h, 1.0, chosen)
    base = _dot(chosen.astype(BF16), tri) + lstart_col
    return [jnp.sum(jnp.where(h, base, 0.0), axis=0, keepdims=True) for h in hits]


def _run_copy(loc, glob, sem, lofs, gofs, n, to_global):
    lo = loc.at[pl.ds(pl.multiple_of(lofs, RUN_ALIGN), n)]
    gl = glob.at[pl.ds(pl.multiple_of(gofs, RUN_ALIGN), n)]
    return pltpu.make_async_copy(lo, gl, sem) if to_global else pltpu.make_async_copy(gl, lo, sem)


def _start_runs(meta, blk, loc, glob, sem, to_global):
    lstart_ref, goff_ref, cpad_ref = meta
    for e in range(N_EXPERTS):
        n = pl.multiple_of(cpad_ref[blk * N_EXPERTS + e], RUN_ALIGN)
        _run_copy(loc, glob, sem, lstart_ref[blk * N_EXPERTS + e], goff_ref[blk * N_EXPERTS + e], n,
                  to_global).start()


def _wait_runs(nrows, loc, glob, sem, to_global):
    _run_copy(loc, glob, sem, 0, 0, pl.multiple_of(nrows, RUN_ALIGN), to_global).wait()


def _sort_body(lstart_ref, goff_ref, cpad_ref, ltot_ref, fstart_ref, flen_ref, nact_ref,
               hp_ref, hs_ref, topi_ref, lcol_ref, tri_ref, xs_hbm, pos_ref, buf, zbuf, sems, fsem, *, nblk_p, nblk):
    i = pl.program_id(0)
    slot = i % 2
    meta = (lstart_ref, goff_ref, cpad_ref)

    @pl.when(i == 0)
    def _():
        zbuf[...] = jnp.zeros(zbuf.shape, jnp.uint32)

        def fill(e, tot):
            n = pl.multiple_of(flen_ref[e], RUN_ALIGN)

            @pl.when(n > 0)
            def _():
                _run_copy(zbuf, xs_hbm, fsem, 0, fstart_ref[e], n, True).start()
            return tot + n
        total = lax.fori_loop(0, N_EXPERTS, fill, 0)

        @pl.when(total > 0)
        def _():
            _wait_runs(total, zbuf, xs_hbm, fsem, True)

        def fill_tile(t, carry):
            cp = _run_copy(zbuf, xs_hbm, fsem, 0, t * EXPERT_TILE, EXPERT_TILE, True)
            cp.start()
            cp.wait()
            return carry
        lax.fori_loop(nact_ref[0], xs_hbm.shape[0] // EXPERT_TILE, fill_tile, 0)

    def run(h_ref):
        pos = [p.astype(jnp.int32) for p in _local_positions(topi_ref[...], lcol_ref[...], tri_ref[...])]
        rowid = lax.broadcasted_iota(jnp.int32, (LOCAL_ROWS, SORT_TOKENS), 0)
        onehot = jnp.zeros((LOCAL_ROWS, SORT_TOKENS), F32)
        for p in pos:
            onehot = jnp.where(rowid == p, 1.0, onehot)
        buf[slot] = _pack_halves(_dot(onehot.astype(BF16), h_ref[...]))
        krow = lax.broadcasted_iota(jnp.int32, (SUBLANES, SORT_TOKENS), 0)
        pos_rows = jnp.zeros((SUBLANES, SORT_TOKENS), jnp.int32)
        for k in range(TOP_K):
            pos_rows = jnp.where(krow == k, pos[k], pos_rows)
        pos_ref[...] = pos_rows

    @pl.when(i < nblk_p)
    def _():
        run(hp_ref)

    @pl.when(i >= nblk_p)
    def _():
        run(hs_ref)

    _start_runs(meta, i, buf.at[slot], xs_hbm, sems.at[slot], True)

    @pl.when(i > 0)
    def _():
        _wait_runs(ltot_ref[jnp.maximum(i - 1, 0)], buf.at[1 - slot], xs_hbm, sems.at[1 - slot], True)

    @pl.when(i == nblk - 1)
    def _():
        _wait_runs(ltot_ref[i], buf.at[slot], xs_hbm, sems.at[slot], True)


def _sort_tokens(plan, h_p, h_s, topi_all, tri, n_slots):
    nblk_p = h_p.shape[0] // SORT_TOKENS
    nblk = nblk_p + h_s.shape[0] // SORT_TOKENS
    return pl.pallas_call(
        functools.partial(_sort_body, nblk_p=nblk_p, nblk=nblk),
        grid_spec=pltpu.PrefetchScalarGridSpec(
            num_scalar_prefetch=7, grid=(nblk,),
            in_specs=[pl.BlockSpec((SORT_TOKENS, D_MODEL), lambda i, *_: (jnp.minimum(i, nblk_p - 1), 0)),
                      pl.BlockSpec((SORT_TOKENS, D_MODEL), lambda i, *_: (jnp.maximum(i - nblk_p, 0), 0)),
                      pl.BlockSpec((SUBLANES, SORT_TOKENS), lambda i, *_: (0, i)),
                      pl.BlockSpec((None, N_EXPERTS, 1), lambda i, *_: (i, 0, 0)),
                      _const_spec(tri.shape)],
            out_specs=[pl.BlockSpec(memory_space=pl.ANY),
                       pl.BlockSpec((SUBLANES, SORT_TOKENS), lambda i, *_: (0, i))],
            scratch_shapes=[pltpu.VMEM((2, LOCAL_ROWS, HALF_D), jnp.uint32),
                            pltpu.VMEM((EXPERT_TILE, HALF_D), jnp.uint32),
                            pltpu.SemaphoreType.DMA((2,)),
                            pltpu.SemaphoreType.DMA]),
        out_shape=[jax.ShapeDtypeStruct((n_slots, HALF_D), jnp.uint32),
                   jax.ShapeDtypeStruct(topi_all.shape, jnp.int32)],
        compiler_params=pltpu.CompilerParams(
            dimension_semantics=("arbitrary",), vmem_limit_bytes=VMEM_LIMIT),
        name="moe_sort",
    )(plan["lstart"], plan["goff"], plan["cpad"], plan["ltot"], plan["fill_start"], plan["fill_len"], plan["nact"],
      h_p, h_s, topi_all, plan["lstart_col"], tri)


def _experts_body(tile_exp_ref, nact_ref, xs_ref, wg_ref, bg_ref, wu_ref, bu_ref, wd_ref, bd_ref,
                  ys_ref, wg_bf, wu_bf, wd_bf):
    t = pl.program_id(0)
    nact = nact_ref[0]

    @pl.when(t < nact)
    def _():
        prev = tile_exp_ref[jnp.maximum(t - 1, 0)]

        @pl.when((t == 0) | (tile_exp_ref[t] != prev))
        def _():
            wg_bf[...] = wg_ref[0].astype(BF16)
            wu_bf[...] = wu_ref[0].astype(BF16)
            wd_bf[...] = wd_ref[0].astype(BF16)

        x = jnp.concatenate(_unpack_halves(xs_ref[...]), axis=1)
        g = _dot(x, wg_bf[...]) + bg_ref[0]
        u = _dot(x, wu_bf[...]) + bu_ref[0]
        g = jnp.minimum(g, SWIGLU_LIMIT)
        u = jnp.clip(u, -SWIGLU_LIMIT, SWIGLU_LIMIT)
        a = g * _sigmoid(SWIGLU_ALPHA * g) * (u + 1.0)
        out = _dot(a.astype(BF16), wd_bf[...]) + bd_ref[0]
        ys_ref[...] = _pack_halves(out.astype(BF16).astype(F32))

    @pl.when(t >= nact)
    def _():
        ys_ref[...] = jnp.zeros(ys_ref.shape, jnp.uint32)


def _experts(tile_exp, nact, xs, wg, bg, wu, bu, wd, bd):
    n_tiles = tile_exp.shape[0]

    def active(t, na):
        return jnp.minimum(t, na[0] - 1)
    wspec = pl.BlockSpec((1, D_MODEL, D_MODEL), lambda t, te, na: (te[active(t, na)], 0, 0))
    bspec = pl.BlockSpec((1, 1, D_MODEL), lambda t, te, na: (te[active(t, na)], 0, 0))
    xspec = pl.BlockSpec((EXPERT_TILE, HALF_D), lambda t, te, na: (active(t, na), 0))
    return pl.pallas_call(
        _experts_body,
        grid_spec=pltpu.PrefetchScalarGridSpec(
            num_scalar_prefetch=2, grid=(n_tiles,),
            in_specs=[xspec, wspec, bspec, wspec, bspec, wspec, bspec],
            out_specs=pl.BlockSpec((EXPERT_TILE, HALF_D), lambda t, te, na: (t, 0)),
            scratch_shapes=[pltpu.VMEM((D_MODEL, D_MODEL), BF16)] * 3),
        out_shape=jax.ShapeDtypeStruct(xs.shape, jnp.uint32),
        compiler_params=pltpu.CompilerParams(
            dimension_semantics=("arbitrary",), vmem_limit_bytes=VMEM_LIMIT),
        name="experts",
    )(tile_exp, nact, xs, wg, bg, wu, bu, wd, bd)


def _unsort_body(lstart_ref, goff_ref, cpad_ref, ltot_ref,
                 ys_hbm, x1_ref, g2_ref, pos_ref, gate_ref, y_ref, buf, sems, *, blk0, nblk):
    i = pl.program_id(0)
    b = blk0 + i
    slot = i % 2
    meta = (lstart_ref, goff_ref, cpad_ref)

    @pl.when(i == 0)
    def _():
        buf[...] = jnp.zeros(buf.shape, jnp.uint32)
        _start_runs(meta, b, buf.at[0], ys_hbm, sems.at[0], False)

    @pl.when(i + 1 < nblk)
    def _():
        _start_runs(meta, b + 1, buf.at[1 - slot], ys_hbm, sems.at[1 - slot], False)

    _wait_runs(ltot_ref[b], buf.at[slot], ys_hbm, sems.at[slot], False)
    pos_cols = pos_ref[...]
    gate_cols = gate_ref[...]
    acc_lo = acc_hi = None
    for c0 in range(0, LOCAL_ROWS, UNSORT_CHUNK):
        colid = lax.broadcasted_iota(jnp.int32, (SORT_TOKENS, UNSORT_CHUNK), 1) + c0
        weights = jnp.zeros((SORT_TOKENS, UNSORT_CHUNK), F32)
        for k in range(TOP_K):
            weights = jnp.where(colid == pos_cols[:, k:k + 1], gate_cols[:, k:k + 1], weights)
        weights = weights.astype(BF16)
        y_lo, y_hi = _unpack_halves(buf[slot, c0:c0 + UNSORT_CHUNK, :])
        acc_lo = _dot(weights, y_lo) if acc_lo is None else acc_lo + _dot(weights, y_lo)
        acc_hi = _dot(weights, y_hi) if acc_hi is None else acc_hi + _dot(weights, y_hi)
    g2 = g2_ref[...]
    y_ref[:, :HALF_D] = x1_ref[:, :HALF_D] + g2[:, :HALF_D] * acc_lo
    y_ref[:, HALF_D:] = x1_ref[:, HALF_D:] + g2[:, HALF_D:] * acc_hi


def _unsort_tokens(plan, ys, x1, g2, g2_spec, pos_cols, gate_cols, blk0):
    nblk = x1.shape[0] // SORT_TOKENS
    return pl.pallas_call(
        functools.partial(_unsort_body, blk0=blk0, nblk=nblk),
        grid_spec=pltpu.PrefetchScalarGridSpec(
            num_scalar_prefetch=4, grid=(nblk,),
            in_specs=[pl.BlockSpec(memory_space=pl.ANY),
                      pl.BlockSpec((SORT_TOKENS, D_MODEL), lambda i, *_: (i, 0)),
                      g2_spec,
                      pl.BlockSpec((SORT_TOKENS, SUBLANES), lambda i, *_: (blk0 + i, 0)),
                      pl.BlockSpec((SORT_TOKENS, SUBLANES), lambda i, *_: (blk0 + i, 0))],
            out_specs=pl.BlockSpec((SORT_TOKENS, D_MODEL), lambda i, *_: (i, 0)),
            scratch_shapes=[pltpu.VMEM((2, LOCAL_ROWS, HALF_D), jnp.uint32),
                            pltpu.SemaphoreType.DMA((2,))]),
        out_shape=jax.ShapeDtypeStruct(x1.shape, F32),
        compiler_params=pltpu.CompilerParams(
            dimension_semantics=("arbitrary",), vmem_limit_bytes=VMEM_LIMIT),
        name="moe_unsort",
    )(plan["lstart"], plan["goff"], plan["cpad"], plan["ltot"],
      ys, x1, g2, pos_cols, gate_cols)


def _rope_table(pos):
    half = ROT_DIM // 2
    inv_freq = ROPE_THETA ** (-jnp.arange(0, ROT_DIM, 2, dtype=F32) / ROT_DIM)
    ang = pos.astype(F32)[:, None] * inv_freq[None, :]
    cos, sin = jnp.cos(ang), jnp.sin(ang)
    n = pos.shape[0]
    rest = HEAD_DIM - ROT_DIM
    c = jnp.concatenate([cos, cos, jnp.ones((n, rest), F32)], axis=1)
    s_next = jnp.concatenate([-sin, jnp.zeros((n, half + rest), F32)], axis=1)
    s_prev = jnp.concatenate([jnp.zeros((n, half), F32), sin, jnp.zeros((n, rest), F32)], axis=1)
    rep = LANES // HEAD_DIM
    return jnp.concatenate([jnp.tile(c, (1, rep)), jnp.tile(s_next, (1, rep)), jnp.tile(s_prev, (1, rep))], axis=1)


def _num_expert_tiles(ntok):
    nblk = ntok // SORT_TOKENS
    worst = ntok * TOP_K + nblk * N_EXPERTS * RUN_ALIGN + N_EXPERTS * (EXPERT_TILE - 1)
    return -(-worst // EXPERT_TILE)


def _routing_plan(topi_all):
    ntok = topi_all.shape[1]
    nblk = ntok // SORT_TOKENS
    n_tiles = _num_expert_tiles(ntok)
    choice = topi_all[:TOP_K].reshape(TOP_K, nblk, SORT_TOKENS)
    experts = jnp.arange(N_EXPERTS, dtype=jnp.int32)
    cnt = (choice[:, :, :, None] == experts).astype(jnp.int32).sum(axis=(0, 2))
    cpad = jnp.maximum(-(-cnt // RUN_ALIGN) * RUN_ALIGN, RUN_ALIGN)
    lstart = jnp.cumsum(cpad, axis=1) - cpad
    tot = cpad.sum(axis=0)
    region = -(-tot // EXPERT_TILE) * EXPERT_TILE
    region_end = jnp.cumsum(region)
    gstart = region_end - region
    goff = gstart[None, :] + jnp.cumsum(cpad, axis=0) - cpad
    tile_start = jnp.arange(n_tiles, dtype=jnp.int32) * EXPERT_TILE
    tile_exp = jnp.minimum((tile_start[:, None] >= region_end[None, :]).astype(jnp.int32).sum(axis=1), N_EXPERTS - 1)
    i32 = lambda a: a.astype(jnp.int32)
    return dict(
        lstart=i32(lstart).reshape(-1), goff=i32(goff).reshape(-1), cpad=i32(cpad).reshape(-1),
        ltot=i32(cpad.sum(axis=1)), fill_start=i32(gstart + tot), fill_len=i32(region - tot),
        lstart_col=lstart.astype(F32).reshape(nblk, N_EXPERTS, 1),
        tile_exp=i32(tile_exp), nact=i32(region_end[-1] // EXPERT_TILE).reshape(1))


def kernel(x_prompt, x_sample, cache_k_win, cache_v_win, state_conv, c_prompt, c_sample, w_ada, b_ada, norm1_g, norm2_g, w_in, q_norm_g, k_norm_g, attn_sinks, w_attn_out, conv_w, conv_b, w_conv_out, w_o, w_router, b_router, w_gate, b_gate, w_up, b_up, w_down, b_down):
    nb, seq, _ = x_prompt.shape
    ns, dseq, _ = x_sample.shape
    wbuf = cache_k_win.shape[2]
    assert w_ada.shape[0] == 1, "single-layer step"
    l = 0

    mod = _adaln(jnp.concatenate([c_prompt, c_sample], axis=0), w_ada[l], b_ada[l])
    mod_p = mod[:nb].reshape(nb, 6, D_MODEL)
    mod_s = jnp.repeat(mod[nb:].reshape(ns, 6, D_MODEL).transpose(1, 0, 2), dseq, axis=1)

    bd = jnp.asarray(np.kron(np.eye(N_HEADS), np.ones((HEAD_DIM, HEAD_DIM))), BF16)
    rep = LANES // HEAD_DIM
    wr_t = w_router[l].T
    wr_hi = wr_t.astype(BF16)
    wr_lo = (wr_t - wr_hi.astype(F32)).astype(BF16)
    weights = (norm1_g[l].reshape(1, D_MODEL), norm2_g[l].reshape(1, D_MODEL), w_in[l].astype(BF16), bd,
               jnp.tile(q_norm_g[l], N_HEADS).reshape(1, Q_W), jnp.tile(k_norm_g[l], N_KV).reshape(1, KV_W),
               w_attn_out[l].astype(BF16), conv_w[l], conv_b[l].reshape(1, CONV_CH), w_conv_out[l].astype(BF16),
               w_o[l].astype(BF16), wr_hi, wr_lo, b_router[l].reshape(N_EXPERTS, 1))
    sinks = attn_sinks[l]

    tb = 512
    (x1_p, h2_p, topi_p, topg_p, kwin_p, vwin_p, zwin_p) = _prompt_mixer(
        x_prompt, mod_p, _rope_table(jnp.arange(seq)), sinks, weights, tb)
    pos_s = jnp.tile(PAST_LEN + jnp.arange(dseq), ns)
    state_rows = jnp.repeat(state_conv[l].transpose(1, 0, 2), dseq, axis=1)
    (x1_s, h2_s, topi_s, topg_s, knew, vnew, z_s) = _sample_mixer(
        x_sample.reshape(ns * dseq, D_MODEL), mod_s, _rope_table(pos_s),
        cache_k_win[l].reshape(ns, wbuf, KV_W), cache_v_win[l].reshape(ns, wbuf, KV_W),
        state_rows, sinks, weights, 16)

    ntok_p = nb * seq
    ntok = ntok_p + ns * dseq
    topi_all = jnp.concatenate([topi_p.transpose(1, 0, 2).reshape(SUBLANES, ntok_p), topi_s], axis=1)
    topg_all = jnp.concatenate([topg_p.transpose(1, 0, 2).reshape(SUBLANES, ntok_p), topg_s], axis=1)
    plan = _routing_plan(topi_all)
    tri = jnp.asarray(np.triu(np.ones((SORT_TOKENS, SORT_TOKENS)), k=1), BF16)
    xs, pos_all = _sort_tokens(plan, h2_p.reshape(ntok_p, D_MODEL), h2_s, topi_all, tri,
                               _num_expert_tiles(ntok) * EXPERT_TILE)
    pos_cols, gate_cols = pos_all.T, topg_all.T
    ys = _experts(plan["tile_exp"], plan["nact"], xs,
                  w_gate[l], b_gate[l].reshape(N_EXPERTS, 1, D_MODEL),
                  w_up[l], b_up[l].reshape(N_EXPERTS, 1, D_MODEL),
                  w_down[l], b_down[l].reshape(N_EXPERTS, 1, D_MODEL))
    blocks_per_seq = seq // SORT_TOKENS
    y_p = _unsort_tokens(plan, ys, x1_p.reshape(ntok_p, D_MODEL), mod_p[:, 5:6, :],
                         pl.BlockSpec((None, 1, D_MODEL), lambda i, *_: (i // blocks_per_seq, 0, 0)),
                         pos_cols, gate_cols, 0)
    y_s = _unsort_tokens(plan, ys, x1_s, mod_s[5],
                         pl.BlockSpec((SORT_TOKENS, D_MODEL), lambda i, *_: (i, 0)),
                         pos_cols, gate_cols, ntok_p // SORT_TOKENS)

    n_kv_shape = (N_KV, HEAD_DIM)
    k_win_p = kwin_p.reshape(1, nb, WINDOW, *n_kv_shape)
    v_win_p = vwin_p.reshape(1, nb, WINDOW, *n_kv_shape)
    conv_p = zwin_p[:, SUBLANES - (CONV_K - 1):, :][None]
    k_win_s = knew.reshape(1, ns, wbuf, *n_kv_shape)
    v_win_s = vnew.reshape(1, ns, wbuf, *n_kv_shape)
    conv_s = z_s.reshape(ns, dseq, CONV_CH)[:, dseq - (CONV_K - 1):, :][None]
    return (y_p.reshape(nb, seq, D_MODEL), y_s.reshape(ns, dseq, D_MODEL),
            k_win_p, v_win_p, conv_p, k_win_s, v_win_s, conv_s)
```

```python
import functools

import numpy as np
import jax
import jax.numpy as jnp
from jax import lax
from jax.experimental import pallas as pl
from jax.experimental.pallas import tpu as pltpu

D_MODEL = 1024
HEAD_DIM = 64
N_HEADS = 8
N_KV = 2
GROUP = N_HEADS // N_KV
Q_W = N_HEADS * HEAD_DIM
KV_W = N_KV * HEAD_DIM
WINDOW = 128
ROT_DIM = HEAD_DIM // 4
ROPE_THETA = 500000.0
ATTN_SCALE = HEAD_DIM ** -0.5
CONV_CH = D_MODEL // 2
CONV_K = 3
N_EXPERTS = 32
TOP_K = 4
SWIGLU_ALPHA = 1.702
SWIGLU_LIMIT = 7.0
MOE_BLOCK = 128
NORM_EPS = 1e-5
QK_EPS = 1e-6
NEG_INF = -1e30
PAST_LEN = 16384

LANES = 128
SUBLANES = 8
VMEM_LIMIT = 56 * 1024 * 1024

_OFF_Q = 0
_OFF_K = _OFF_Q + Q_W
_OFF_V = _OFF_K + KV_W
_OFF_GB = _OFF_V + KV_W
_OFF_GC = _OFF_GB + CONV_CH
_OFF_XC = _OFF_GC + CONV_CH
_OFF_BA = _OFF_XC + CONV_CH
_OFF_BC = _OFF_BA + D_MODEL
IN_COLS = _OFF_BC + D_MODEL

BF16 = jnp.bfloat16
F32 = jnp.float32
_NT = (((1,), (1,)), ((), ()))


def _dot(a, b):
    return jnp.dot(a, b, preferred_element_type=F32)


def _dot_nt(a, b):
    return lax.dot_general(a, b, _NT, preferred_element_type=F32)


def _sigmoid(x):
    return 1.0 / (1.0 + jnp.exp(-x))


def _split_bf16(x):
    hi = x.astype(BF16)
    lo = (x - hi.astype(F32)).astype(BF16)
    return hi, lo


def _const_spec(shape):
    nd = len(shape)
    return pl.BlockSpec(shape, lambda *_: (0,) * nd, pipeline_mode=pl.Buffered(1))


def _adaln_body(c_ref, w_ref, b_ref, o_ref):
    c = c_ref[...]
    s = c * _sigmoid(c)
    o_ref[...] = _dot(s.astype(BF16), w_ref[...].astype(BF16)) + b_ref[...]


def _adaln(c, w_ada, b_ada):
    n = c.shape[0]
    cols = w_ada.shape[1]
    bn = 1536
    return pl.pallas_call(
        _adaln_body,
        grid=(cols // bn,),
        in_specs=[pl.BlockSpec((n, D_MODEL), lambda i: (0, 0)),
                  pl.BlockSpec((D_MODEL, bn), lambda i: (0, i)),
                  pl.BlockSpec((1, bn), lambda i: (0, i))],
        out_specs=pl.BlockSpec((n, bn), lambda i: (0, i)),
        out_shape=jax.ShapeDtypeStruct((n, cols), F32),
        compiler_params=pltpu.CompilerParams(vmem_limit_bytes=VMEM_LIMIT),
        name="adaln",
    )(c, w_ada, b_ada.reshape(1, cols))


def _modulated_norm(x, g, shift, scale):
    y = x * lax.rsqrt(jnp.mean(x * x, axis=-1, keepdims=True) + NORM_EPS)
    return (y * g) * (1.0 + scale) + shift


def _head_norm(t, bd, g):
    w = t.shape[1]
    sq_hi, sq_lo = _split_bf16(t * t)
    blk = bd[:w, :w]
    ms = (_dot(sq_hi, blk) + _dot(sq_lo, blk)) * (1.0 / HEAD_DIM)
    return t * lax.rsqrt(ms + QK_EPS) * g[:, :w]


def _rope(y, rope):
    c = rope[:, 0:LANES]
    s_next = rope[:, LANES:2 * LANES]
    s_prev = rope[:, 2 * LANES:3 * LANES]
    half = ROT_DIM // 2
    return y * c + pltpu.roll(y, LANES - half, 1) * s_next + pltpu.roll(y, half, 1) * s_prev


def _pair_expand(t):
    lane = lax.broadcasted_iota(jnp.int32, t.shape, 1)
    lo = lane < HEAD_DIM
    r = pltpu.roll(t, HEAD_DIM, 1)
    zero = jnp.zeros_like(t)
    a0 = jnp.where(lo, t, zero).astype(BF16)
    b0 = jnp.where(lo, zero, r).astype(BF16)
    a1 = jnp.where(lo, r, zero).astype(BF16)
    b1 = jnp.where(lo, zero, t).astype(BF16)
    return ((a0, b0), (a1, b1))


def _softmax_pv(s_list, v_list, sink):
    m = jnp.full((s_list[0].shape[0], 1), sink, F32)
    for s in s_list:
        m = jnp.maximum(m, jnp.max(s, axis=-1, keepdims=True))
    den = jnp.exp(sink - m)
    acc = None
    for s, v in zip(s_list, v_list):
        e = jnp.exp(s - m)
        den = den + jnp.sum(e, axis=-1, keepdims=True)
        pv = _dot(e.astype(BF16), v)
        acc = pv if acc is None else acc + pv
    return acc * (1.0 / den)


def _route(h2, wr_hi, wr_lo, br):
    hi, lo = _split_bf16(h2)
    logits = _dot_nt(wr_hi, hi) + _dot_nt(wr_hi, lo) + _dot_nt(wr_lo, hi) + br
    eid = lax.broadcasted_iota(jnp.int32, logits.shape, 0).astype(F32)
    vals, idxs = [], []
    for _ in range(TOP_K):
        m = jnp.max(logits, axis=0, keepdims=True)
        idx = jnp.min(jnp.where(logits == m, eid, float(N_EXPERTS)), axis=0, keepdims=True)
        vals.append(m)
        idxs.append(idx)
        logits = jnp.where(eid == idx, -jnp.inf, logits)
    ex = [jnp.exp(v - vals[0]) for v in vals]
    tot = ex[0] + ex[1] + ex[2] + ex[3]
    inv = 1.0 / tot
    gates = [e * inv for e in ex]
    rowid = lax.broadcasted_iota(jnp.int32, (SUBLANES, h2.shape[0]), 0)
    topi = jnp.zeros((SUBLANES, h2.shape[0]), F32)
    topg = jnp.zeros((SUBLANES, h2.shape[0]), F32)
    for k in range(TOP_K):
        topi = jnp.where(rowid == k, idxs[k], topi)
        topg = jnp.where(rowid == k, gates[k], topg)
    return topi.astype(jnp.int32), topg


_QKV_COLS = (_OFF_Q, Q_W + 2 * KV_W)
_LATE_COLS = ((_OFF_GB, 2 * CONV_CH), (_OFF_XC, CONV_CH), (_OFF_BA, D_MODEL), (_OFF_BC, D_MODEL))


def _in_proj(proj, h_bf, w_in_ref, group):
    off, width = group
    proj[off] = _dot(h_bf, w_in_ref[:, off:off + width])


def _cols(proj, off, width):
    for start, piece in proj.items():
        if start <= off and off + width <= start + piece.shape[1]:
            return piece[:, off - start:off - start + width]
    raise KeyError(off)


def _merge_and_route(x, o_bf, y_conv, proj, mod, n2g_ref, w_ao_ref, w_co_ref, w_o_ref,
                     wr_hi_ref, wr_lo_ref, br_ref):
    sh1, sc1, g1, sh2, sc2, g2 = mod
    a = _dot(o_bf, w_ao_ref[...])
    c = _dot(y_conv.astype(BF16), w_co_ref[...])
    m = _sigmoid(_cols(proj, _OFF_BA, D_MODEL)) * a + _sigmoid(_cols(proj, _OFF_BC, D_MODEL)) * c
    x1 = x + g1 * _dot(m.astype(BF16), w_o_ref[...])
    h2 = _modulated_norm(x1, n2g_ref[...], sh2, sc2)
    topi, topg = _route(h2, wr_hi_ref[...], wr_lo_ref[...], br_ref[...])
    return x1, h2, topi, topg


def _prompt_body(sinks_ref, x_ref, mod_ref, rope_ref, n1g_ref, n2g_ref, w_in_ref, bd_ref, qg_ref, kg_ref,
                 w_ao_ref, cw_ref, cb_ref, w_co_ref, w_o_ref, wr_hi_ref, wr_lo_ref, br_ref,
                 x1_ref, h2_ref, topi_ref, topg_ref, kwin_ref, vwin_ref, zwin_ref,
                 kbuf, vbuf, zbuf, *, tb):
    j = pl.program_id(1)
    nsub = tb // WINDOW

    @pl.when(j == 0)
    def _():
        kbuf[0:WINDOW, :] = jnp.zeros((WINDOW, KV_W), F32)
        vbuf[0:WINDOW, :] = jnp.zeros((WINDOW, KV_W), F32)
        zbuf[0:SUBLANES, :] = jnp.zeros((SUBLANES, CONV_CH), F32)

    x = x_ref[0]
    mod = tuple(mod_ref[0, i:i + 1, :] for i in range(6))
    h_bf = _modulated_norm(x, n1g_ref[...], mod[0], mod[1]).astype(BF16)
    proj = {}
    _in_proj(proj, h_bf, w_in_ref, _QKV_COLS)
    _in_proj(proj, h_bf, w_in_ref, _LATE_COLS[0])
    rope = rope_ref[...]
    bd = bd_ref[...]

    k = _rope(_head_norm(_cols(proj, _OFF_K, KV_W), bd, kg_ref[...]), rope)
    v = _cols(proj, _OFF_V, KV_W)
    kbuf[WINDOW:WINDOW + tb, :] = k
    vbuf[WINDOW:WINDOW + tb, :] = v
    kwin_ref[0] = k[tb - WINDOW:tb]
    vwin_ref[0] = v[tb - WINDOW:tb]
    k_exp = _pair_expand(kbuf[...])
    v_exp = _pair_expand(vbuf[...])

    _in_proj(proj, h_bf, w_in_ref, _LATE_COLS[1])
    qn = _head_norm(_cols(proj, _OFF_Q, Q_W), bd, qg_ref[...])
    q_tiles = []
    for t in range(Q_W // LANES):
        qt = _rope(qn[:, t * LANES:(t + 1) * LANES], rope)
        q_tiles.append((qt * ATTN_SCALE).astype(BF16))

    row = lax.broadcasted_iota(jnp.int32, (WINDOW, 2 * WINDOW), 0)
    col = lax.broadcasted_iota(jnp.int32, (WINDOW, 2 * WINDOW), 1)
    band = (col > row) & (col <= row + WINDOW)
    first_key = jnp.where(j > 0, 0, WINDOW)
    band_first = band & (col >= first_key)
    o_rows = []
    for i in range(nsub):
        for group in _LATE_COLS[2:][i::nsub]:
            _in_proj(proj, h_bf, w_in_ref, group)
        mask = band_first if i == 0 else band
        keys = slice(i * WINDOW, (i + 2) * WINDOW)
        o_tiles = []
        for t in range(Q_W // LANES):
            c = t // 2
            qt = q_tiles[t][i * WINDOW:(i + 1) * WINDOW]
            o_t = None
            for half in range(2):
                s = jnp.where(mask, _dot_nt(qt, k_exp[c][half][keys]), NEG_INF)
                sink = sinks_ref[c * GROUP + (t % 2) * 2 + half]
                part = _softmax_pv([s], [v_exp[c][half][keys]], sink)
                o_t = part if o_t is None else o_t + part
            o_tiles.append(o_t.astype(BF16))
        o_rows.append(jnp.concatenate(o_tiles, axis=1))
    o_bf = jnp.concatenate(o_rows, axis=0) if nsub > 1 else o_rows[0]
    kbuf[0:WINDOW, :] = kbuf[tb:tb + WINDOW, :]
    vbuf[0:WINDOW, :] = vbuf[tb:tb + WINDOW, :]

    z = _cols(proj, _OFF_GC, CONV_CH) * _cols(proj, _OFF_XC, CONV_CH)
    zbuf[SUBLANES:SUBLANES + tb, :] = z
    zwin_ref[0] = z[tb - SUBLANES:tb]
    z1 = zbuf[SUBLANES - 1:SUBLANES - 1 + tb, :]
    z2 = zbuf[SUBLANES - 2:SUBLANES - 2 + tb, :]
    conv = cb_ref[...] + cw_ref[0:1, :] * z2 + cw_ref[1:2, :] * z1 + cw_ref[2:3, :] * z
    y_conv = _cols(proj, _OFF_GB, CONV_CH) * conv
    zbuf[0:SUBLANES, :] = zbuf[tb:tb + SUBLANES, :]

    x1, h2, topi, topg = _merge_and_route(x, o_bf, y_conv, proj, mod, n2g_ref, w_ao_ref, w_co_ref,
                                          w_o_ref, wr_hi_ref, wr_lo_ref, br_ref)
    x1_ref[0] = x1
    h2_ref[0] = h2.astype(BF16)
    topi_ref[0] = topi
    topg_ref[0] = topg


def _prompt_mixer(x, mod, rope, sinks, weights, tb):
    nb, seq, _ = x.shape
    (n1g, n2g, w_in, bd, qg, kg, w_ao, cw, cb, w_co, w_o, wr_hi, wr_lo, br) = weights
    consts = (n1g, n2g, w_in, bd, qg, kg, w_ao, cw, cb, w_co, w_o, wr_hi, wr_lo, br)
    in_specs = [
        pl.BlockSpec((1, tb, D_MODEL), lambda b, j, s: (b, j, 0)),
        pl.BlockSpec((1, 6, D_MODEL), lambda b, j, s: (b, 0, 0)),
        pl.BlockSpec((tb, 3 * LANES), lambda b, j, s: (j, 0)),
    ] + [_const_spec(w.shape) for w in consts]
    out_specs = [
        pl.BlockSpec((1, tb, D_MODEL), lambda b, j, s: (b, j, 0)),
        pl.BlockSpec((1, tb, D_MODEL), lambda b, j, s: (b, j, 0)),
        pl.BlockSpec((1, SUBLANES, tb), lambda b, j, s: (b, 0, j)),
        pl.BlockSpec((1, SUBLANES, tb), lambda b, j, s: (b, 0, j)),
        pl.BlockSpec((1, WINDOW, KV_W), lambda b, j, s: (b, 0, 0)),
        pl.BlockSpec((1, WINDOW, KV_W), lambda b, j, s: (b, 0, 0)),
        pl.BlockSpec((1, SUBLANES, CONV_CH), lambda b, j, s: (b, 0, 0)),
    ]
    out_shape = [
        jax.ShapeDtypeStruct((nb, seq, D_MODEL), F32),
        jax.ShapeDtypeStruct((nb, seq, D_MODEL), BF16),
        jax.ShapeDtypeStruct((nb, SUBLANES, seq), jnp.int32),
        jax.ShapeDtypeStruct((nb, SUBLANES, seq), F32),
        jax.ShapeDtypeStruct((nb, WINDOW, KV_W), F32),
        jax.ShapeDtypeStruct((nb, WINDOW, KV_W), F32),
        jax.ShapeDtypeStruct((nb, SUBLANES, CONV_CH), F32),
    ]
    return pl.pallas_call(
        functools.partial(_prompt_body, tb=tb),
        grid_spec=pltpu.PrefetchScalarGridSpec(
            num_scalar_prefetch=1, grid=(nb, seq // tb),
            in_specs=in_specs, out_specs=out_specs,
            scratch_shapes=[pltpu.VMEM((tb + WINDOW, KV_W), F32),
                            pltpu.VMEM((tb + WINDOW, KV_W), F32),
                            pltpu.VMEM((tb + SUBLANES, CONV_CH), F32)]),
        out_shape=out_shape,
        compiler_params=pltpu.CompilerParams(
            dimension_semantics=("arbitrary", "arbitrary"), vmem_limit_bytes=VMEM_LIMIT),
        name="prompt_mixer",
    )(sinks, x, mod, rope, *consts)


def _sample_body(sinks_ref, x_ref, mod_ref, rope_ref, ck_ref, cv_ref, st_ref,
                 n1g_ref, n2g_ref, w_in_ref, bd_ref, qg_ref, kg_ref,
                 w_ao_ref, cw_ref, cb_ref, w_co_ref, w_o_ref, wr_hi_ref, wr_lo_ref, br_ref,
                 x1_ref, h2_ref, topi_ref, topg_ref, knew_ref, vnew_ref, z_ref, *, nseq, dseq):
    rows = nseq * dseq
    wbuf = ck_ref.shape[1]
    x = x_ref[...]
    mod = tuple(mod_ref[i] for i in range(6))
    h_bf = _modulated_norm(x, n1g_ref[...], mod[0], mod[1]).astype(BF16)
    proj = {}
    for group in (_QKV_COLS,) + _LATE_COLS:
        _in_proj(proj, h_bf, w_in_ref, group)
    rope = rope_ref[...]
    bd = bd_ref[...]

    k = _rope(_head_norm(_cols(proj, _OFF_K, KV_W), bd, kg_ref[...]), rope)
    v = _cols(proj, _OFF_V, KV_W)
    for win_ref, cache_ref, new in ((knew_ref, ck_ref, k), (vnew_ref, cv_ref, v)):
        win_ref[:, 0:wbuf - dseq, :] = cache_ref[:, dseq:, :]
        win_ref[:, wbuf - dseq:, :] = new.reshape(nseq, dseq, KV_W)
    kn_exp = _pair_expand(k)
    vn_exp = _pair_expand(v)
    kc_exp = _pair_expand(ck_ref[...].reshape(nseq * wbuf, KV_W))
    vc_exp = _pair_expand(cv_ref[...].reshape(nseq * wbuf, KV_W))

    qrow = lax.broadcasted_iota(jnp.int32, (rows, nseq * wbuf), 0)
    ccol = lax.broadcasted_iota(jnp.int32, (rows, nseq * wbuf), 1)
    mask_c = ((ccol // wbuf) == (qrow // dseq)) & ((ccol % wbuf) > (qrow % dseq) + (wbuf - WINDOW))
    qrow_n = lax.broadcasted_iota(jnp.int32, (rows, rows), 0)
    ncol = lax.broadcasted_iota(jnp.int32, (rows, rows), 1)
    mask_n = ((ncol // dseq) == (qrow_n // dseq)) & ((ncol % dseq) <= (qrow_n % dseq))

    qn = _head_norm(_cols(proj, _OFF_Q, Q_W), bd, qg_ref[...])
    o_tiles = []
    for t in range(Q_W // LANES):
        c = t // 2
        qt = (_rope(qn[:, t * LANES:(t + 1) * LANES], rope) * ATTN_SCALE).astype(BF16)
        o_t = None
        for half in range(2):
            s_c = jnp.where(mask_c, _dot_nt(qt, kc_exp[c][half]), NEG_INF)
            s_n = jnp.where(mask_n, _dot_nt(qt, kn_exp[c][half]), NEG_INF)
            sink = sinks_ref[c * GROUP + (t % 2) * 2 + half]
            part = _softmax_pv([s_c, s_n], [vc_exp[c][half], vn_exp[c][half]], sink)
            o_t = part if o_t is None else o_t + part
        o_tiles.append(o_t.astype(BF16))
    o_bf = jnp.concatenate(o_tiles, axis=1)

    z = _cols(proj, _OFF_GC, CONV_CH) * _cols(proj, _OFF_XC, CONV_CH)
    z_ref[...] = z
    r = lax.broadcasted_iota(jnp.int32, z.shape, 0) % dseq
    st0 = st_ref[0]
    st1 = st_ref[1]
    z1 = jnp.where(r == 0, st1, pltpu.roll(z, 1, 0))
    z2 = jnp.where(r == 0, st0, jnp.where(r == 1, st1, pltpu.roll(z, 2, 0)))
    conv = cb_ref[...] + cw_ref[0:1, :] * z2 + cw_ref[1:2, :] * z1 + cw_ref[2:3, :] * z
    y_conv = _cols(proj, _OFF_GB, CONV_CH) * conv

    x1, h2, topi, topg = _merge_and_route(x, o_bf, y_conv, proj, mod, n2g_ref, w_ao_ref, w_co_ref,
                                          w_o_ref, wr_hi_ref, wr_lo_ref, br_ref)
    x1_ref[...] = x1
    h2_ref[...] = h2.astype(BF16)
    topi_ref[...] = topi
    topg_ref[...] = topg


def _sample_mixer(x, mod, rope, cache_k, cache_v, state, sinks, weights, nseq):
    ntok = x.shape[0]
    nall, wbuf, _ = cache_k.shape
    dseq = ntok // nall
    rows = nseq * dseq
    consts = weights
    in_specs = [
        pl.BlockSpec((rows, D_MODEL), lambda i, s: (i, 0)),
        pl.BlockSpec((6, rows, D_MODEL), lambda i, s: (0, i, 0)),
        pl.BlockSpec((rows, 3 * LANES), lambda i, s: (i, 0)),
        pl.BlockSpec((nseq, wbuf, KV_W), lambda i, s: (i, 0, 0)),
        pl.BlockSpec((nseq, wbuf, KV_W), lambda i, s: (i, 0, 0)),
        pl.BlockSpec((2, rows, CONV_CH), lambda i, s: (0, i, 0)),
    ] + [_const_spec(w.shape) for w in consts]
    out_specs = [
        pl.BlockSpec((rows, D_MODEL), lambda i, s: (i, 0)),
        pl.BlockSpec((rows, D_MODEL), lambda i, s: (i, 0)),
        pl.BlockSpec((SUBLANES, rows), lambda i, s: (0, i)),
        pl.BlockSpec((SUBLANES, rows), lambda i, s: (0, i)),
        pl.BlockSpec((nseq, wbuf, KV_W), lambda i, s: (i, 0, 0)),
        pl.BlockSpec((nseq, wbuf, KV_W), lambda i, s: (i, 0, 0)),
        pl.BlockSpec((rows, CONV_CH), lambda i, s: (i, 0)),
    ]
    out_shape = [
        jax.ShapeDtypeStruct((ntok, D_MODEL), F32),
        jax.ShapeDtypeStruct((ntok, D_MODEL), BF16),
        jax.ShapeDtypeStruct((SUBLANES, ntok), jnp.int32),
        jax.ShapeDtypeStruct((SUBLANES, ntok), F32),
        jax.ShapeDtypeStruct((nall, wbuf, KV_W), F32),
        jax.ShapeDtypeStruct((nall, wbuf, KV_W), F32),
        jax.ShapeDtypeStruct((ntok, CONV_CH), F32),
    ]
    return pl.pallas_call(
        functools.partial(_sample_body, nseq=nseq, dseq=dseq),
        grid_spec=pltpu.PrefetchScalarGridSpec(
            num_scalar_prefetch=1, grid=(ntok // rows,),
            in_specs=in_specs, out_specs=out_specs),
        out_shape=out_shape,
        compiler_params=pltpu.CompilerParams(
            dimension_semantics=("arbitrary",), vmem_limit_bytes=VMEM_LIMIT),
        name="sample_mixer",
    )(sinks, x, mod, rope, cache_k, cache_v, state, *consts)


SORT_TOKENS = 256
RUN_ALIGN = SUBLANES
EXPERT_TILE = 512
HALF_D = D_MODEL // 2
LOCAL_ROWS = SORT_TOKENS * TOP_K + N_EXPERTS * RUN_ALIGN
UNSORT_CHUNK = 256
assert LOCAL_ROWS % UNSORT_CHUNK == 0
_HI_MASK = 0xFFFF0000


def _pack_halves(x):
    lo = pltpu.bitcast(x[:, :HALF_D], jnp.uint32)
    hi = pltpu.bitcast(x[:, HALF_D:], jnp.uint32)
    return hi | (lo >> 16)


def _unpack_halves(w):
    lo = pltpu.bitcast(w << 16, F32).astype(BF16)
    hi = pltpu.bitcast(w & jnp.uint32(_HI_MASK), F32).astype(BF16)
    return lo, hi


def _local_positions(topi, lstart_col, tri):
    ntok = topi.shape[1]
    eid = lax.broadcasted_iota(jnp.int32, (N_EXPERTS, ntok), 0)
    hits = [eid == topi[k:k + 1, :] for k in range(TOP_K)]
    chosen = jnp.zeros((N_EXPERTS, ntok), F32)
    for h in hits:
        chosen = jnp.where(h, 1.0, chosen)
    base = _dot(chosen.astype(BF16), tri) + lstart_col
    return [jnp.sum(jnp.where(h, base, 0.0), axis=0, keepdims=True) for h in hits]


def _run_copy(loc, glob, sem, lofs, gofs, n, to_global):
    lo = loc.at[pl.ds(pl.multiple_of(lofs, RUN_ALIGN), n)]
    gl = glob.at[pl.ds(pl.multiple_of(gofs, RUN_ALIGN), n)]
    return pltpu.make_async_copy(lo, gl, sem) if to_global else pltpu.make_async_copy(gl, lo, sem)


def _start_runs(meta, blk, loc, glob, sem, to_global):
    lstart_ref, goff_ref, cpad_ref = meta
    for e in range(N_EXPERTS):
        n = pl.multiple_of(cpad_ref[blk * N_EXPERTS + e], RUN_ALIGN)
        _run_copy(loc, glob, sem, lstart_ref[blk * N_EXPERTS + e], goff_ref[blk * N_EXPERTS + e], n,
                  to_global).start()


def _wait_runs(nrows, loc, glob, sem, to_global):
    _run_copy(loc, glob, sem, 0, 0, pl.multiple_of(nrows, RUN_ALIGN), to_global).wait()


def _sort_body(lstart_ref, goff_ref, cpad_ref, ltot_ref, fstart_ref, flen_ref, nact_ref,
               hp_ref, hs_ref, topi_ref, lcol_ref, tri_ref, xs_hbm, pos_ref, buf, zbuf, sems, fsem, *, nblk_p, nblk):
    i = pl.program_id(0)
    slot = i % 2
    meta = (lstart_ref, goff_ref, cpad_ref)

    @pl.when(i == 0)
    def _():
        zbuf[...] = jnp.zeros(zbuf.shape, jnp.uint32)

        def fill(e, tot):
            n = pl.multiple_of(flen_ref[e], RUN_ALIGN)

            @pl.when(n > 0)
            def _():
                _run_copy(zbuf, xs_hbm, fsem, 0, fstart_ref[e], n, True).start()
            return tot + n
        total = lax.fori_loop(0, N_EXPERTS, fill, 0)

        @pl.when(total > 0)
        def _():
            _wait_runs(total, zbuf, xs_hbm, fsem, True)

        def fill_tile(t, carry):
            cp = _run_copy(zbuf, xs_hbm, fsem, 0, t * EXPERT_TILE, EXPERT_TILE, True)
            cp.start()
            cp.wait()
            return carry
        lax.fori_loop(nact_ref[0], xs_hbm.shape[0] // EXPERT_TILE, fill_tile, 0)

    def run(h_ref):
        pos = [p.astype(jnp.int32) for p in _local_positions(topi_ref[...], lcol_ref[...], tri_ref[...])]
        rowid = lax.broadcasted_iota(jnp.int32, (LOCAL_ROWS, SORT_TOKENS), 0)
        onehot = jnp.zeros((LOCAL_ROWS, SORT_TOKENS), F32)
        for p in pos:
            onehot = jnp.where(rowid == p, 1.0, onehot)
        buf[slot] = _pack_halves(_dot(onehot.astype(BF16), h_ref[...]))
        krow = lax.broadcasted_iota(jnp.int32, (SUBLANES, SORT_TOKENS), 0)
        pos_rows = jnp.zeros((SUBLANES, SORT_TOKENS), jnp.int32)
        for k in range(TOP_K):
            pos_rows = jnp.where(krow == k, pos[k], pos_rows)
        pos_ref[...] = pos_rows

    @pl.when(i < nblk_p)
    def _():
        run(hp_ref)

    @pl.when(i >= nblk_p)
    def _():
        run(hs_ref)

    _start_runs(meta, i, buf.at[slot], xs_hbm, sems.at[slot], True)

    @pl.when(i > 0)
    def _():
        _wait_runs(ltot_ref[jnp.maximum(i - 1, 0)], buf.at[1 - slot], xs_hbm, sems.at[1 - slot], True)

    @pl.when(i == nblk - 1)
    def _():
        _wait_runs(ltot_ref[i], buf.at[slot], xs_hbm, sems.at[slot], True)


def _sort_tokens(plan, h_p, h_s, topi_all, tri, n_slots):
    nblk_p = h_p.shape[0] // SORT_TOKENS
    nblk = nblk_p + h_s.shape[0] // SORT_TOKENS
    return pl.pallas_call(
        functools.partial(_sort_body, nblk_p=nblk_p, nblk=nblk),
        grid_spec=pltpu.PrefetchScalarGridSpec(
            num_scalar_prefetch=7, grid=(nblk,),
            in_specs=[pl.BlockSpec((SORT_TOKENS, D_MODEL), lambda i, *_: (jnp.minimum(i, nblk_p - 1), 0)),
                      pl.BlockSpec((SORT_TOKENS, D_MODEL), lambda i, *_: (jnp.maximum(i - nblk_p, 0), 0)),
                      pl.BlockSpec((SUBLANES, SORT_TOKENS), lambda i, *_: (0, i)),
                      pl.BlockSpec((None, N_EXPERTS, 1), lambda i, *_: (i, 0, 0)),
                      _const_spec(tri.shape)],
            out_specs=[pl.BlockSpec(memory_space=pl.ANY),
                       pl.BlockSpec((SUBLANES, SORT_TOKENS), lambda i, *_: (0, i))],
            scratch_shapes=[pltpu.VMEM((2, LOCAL_ROWS, HALF_D), jnp.uint32),
                            pltpu.VMEM((EXPERT_TILE, HALF_D), jnp.uint32),
                            pltpu.SemaphoreType.DMA((2,)),
                            pltpu.SemaphoreType.DMA]),
        out_shape=[jax.ShapeDtypeStruct((n_slots, HALF_D), jnp.uint32),
                   jax.ShapeDtypeStruct(topi_all.shape, jnp.int32)],
        compiler_params=pltpu.CompilerParams(
            dimension_semantics=("arbitrary",), vmem_limit_bytes=VMEM_LIMIT),
        name="moe_sort",
    )(plan["lstart"], plan["goff"], plan["cpad"], plan["ltot"], plan["fill_start"], plan["fill_len"], plan["nact"],
      h_p, h_s, topi_all, plan["lstart_col"], tri)


def _experts_body(tile_exp_ref, nact_ref, xs_ref, wg_hbm, bg_ref, wu_hbm, bu_ref, wd_hbm, bd_ref,
                  ys_ref, w_f32, w_bf, sems):
    t = pl.program_id(0)
    nact = nact_ref[0]
    w_hbm = (wg_hbm, wu_hbm, wd_hbm)

    def weight_copies(e, slot):
        return [pltpu.make_async_copy(w.at[e], w_f32.at[slot, j], sems.at[slot, j]) for j, w in enumerate(w_hbm)]

    @pl.when(t < nact)
    def _():
        e = tile_exp_ref[t]
        slot = e % 2

        @pl.when(t == 0)
        def _():
            for cp in weight_copies(e, slot):
                cp.start()

        @pl.when((t == 0) | (e != tile_exp_ref[jnp.maximum(t - 1, 0)]))
        def _():
            @pl.when(e + 1 < N_EXPERTS)
            def _():
                for cp in weight_copies(e + 1, 1 - slot):
                    cp.start()
            for j, cp in enumerate(weight_copies(e, slot)):
                cp.wait()
                w_bf[j] = w_f32[slot, j].astype(BF16)

        x = jnp.concatenate(_unpack_halves(xs_ref[...]), axis=1)
        g = _dot(x, w_bf[0]) + bg_ref[0]
        u = _dot(x, w_bf[1]) + bu_ref[0]
        g = jnp.minimum(g, SWIGLU_LIMIT)
        u = jnp.clip(u, -SWIGLU_LIMIT, SWIGLU_LIMIT)
        a = g * _sigmoid(SWIGLU_ALPHA * g) * (u + 1.0)
        out = _dot(a.astype(BF16), w_bf[2]) + bd_ref[0]
        ys_ref[...] = _pack_halves(out.astype(BF16).astype(F32))

    @pl.when(t >= nact)
    def _():
        ys_ref[...] = jnp.zeros(ys_ref.shape, jnp.uint32)


def _experts(tile_exp, nact, xs, wg, bg, wu, bu, wd, bd):
    n_tiles = tile_exp.shape[0]

    def active(t, na):
        return jnp.minimum(t, na[0] - 1)
    wspec = pl.BlockSpec(memory_space=pl.ANY)
    bspec = pl.BlockSpec((1, 1, D_MODEL), lambda t, te, na: (te[active(t, na)], 0, 0))
    xspec = pl.BlockSpec((EXPERT_TILE, HALF_D), lambda t, te, na: (active(t, na), 0))
    return pl.pallas_call(
        _experts_body,
        grid_spec=pltpu.PrefetchScalarGridSpec(
            num_scalar_prefetch=2, grid=(n_tiles,),
            in_specs=[xspec, wspec, bspec, wspec, bspec, wspec, bspec],
            out_specs=pl.BlockSpec((EXPERT_TILE, HALF_D), lambda t, te, na: (t, 0)),
            scratch_shapes=[pltpu.VMEM((2, 3, D_MODEL, D_MODEL), F32),
                            pltpu.VMEM((3, D_MODEL, D_MODEL), BF16),
                            pltpu.SemaphoreType.DMA((2, 3))]),
        out_shape=jax.ShapeDtypeStruct(xs.shape, jnp.uint32),
        compiler_params=pltpu.CompilerParams(
            dimension_semantics=("arbitrary",), vmem_limit_bytes=VMEM_LIMIT),
        name="experts",
    )(tile_exp, nact, xs, wg, bg, wu, bu, wd, bd)


def _unsort_body(lstart_ref, goff_ref, cpad_ref, ltot_ref,
                 ys_hbm, x1_ref, g2_ref, pos_ref, gate_ref, y_ref, buf, sems, *, blk0, nblk):
    i = pl.program_id(0)
    b = blk0 + i
    slot = i % 2
    meta = (lstart_ref, goff_ref, cpad_ref)

    @pl.when(i == 0)
    def _():
        buf[...] = jnp.zeros(buf.shape, jnp.uint32)
        _start_runs(meta, b, buf.at[0], ys_hbm, sems.at[0], False)

    @pl.when(i + 1 < nblk)
    def _():
        _start_runs(meta, b + 1, buf.at[1 - slot], ys_hbm, sems.at[1 - slot], False)

    _wait_runs(ltot_ref[b], buf.at[slot], ys_hbm, sems.at[slot], False)
    pos_cols = pos_ref[...]
    gate_cols = gate_ref[...]
    acc_lo = acc_hi = None
    for c0 in range(0, LOCAL_ROWS, UNSORT_CHUNK):
        colid = lax.broadcasted_iota(jnp.int32, (SORT_TOKENS, UNSORT_CHUNK), 1) + c0
        weights = jnp.zeros((SORT_TOKENS, UNSORT_CHUNK), F32)
        for k in range(TOP_K):
            weights = jnp.where(colid == pos_cols[:, k:k + 1], gate_cols[:, k:k + 1], weights)
        weights = weights.astype(BF16)
        y_lo, y_hi = _unpack_halves(buf[slot, c0:c0 + UNSORT_CHUNK, :])
        acc_lo = _dot(weights, y_lo) if acc_lo is None else acc_lo + _dot(weights, y_lo)
        acc_hi = _dot(weights, y_hi) if acc_hi is None else acc_hi + _dot(weights, y_hi)
    g2 = g2_ref[...]
    y_ref[:, :HALF_D] = x1_ref[:, :HALF_D] + g2[:, :HALF_D] * acc_lo
    y_ref[:, HALF_D:] = x1_ref[:, HALF_D:] + g2[:, HALF_D:] * acc_hi


def _unsort_tokens(plan, ys, x1, g2, g2_spec, pos_cols, gate_cols, blk0):
    nblk = x1.shape[0] // SORT_TOKENS
    return pl.pallas_call(
        functools.partial(_unsort_body, blk0=blk0, nblk=nblk),
        grid_spec=pltpu.PrefetchScalarGridSpec(
            num_scalar_prefetch=4, grid=(nblk,),
            in_specs=[pl.BlockSpec(memory_space=pl.ANY),
                      pl.BlockSpec((SORT_TOKENS, D_MODEL), lambda i, *_: (i, 0)),
                      g2_spec,
                      pl.BlockSpec((SORT_TOKENS, SUBLANES), lambda i, *_: (blk0 + i, 0)),
                      pl.BlockSpec((SORT_TOKENS, SUBLANES), lambda i, *_: (blk0 + i, 0))],
            out_specs=pl.BlockSpec((SORT_TOKENS, D_MODEL), lambda i, *_: (i, 0)),
            scratch_shapes=[pltpu.VMEM((2, LOCAL_ROWS, HALF_D), jnp.uint32),
                            pltpu.SemaphoreType.DMA((2,))]),
        out_shape=jax.ShapeDtypeStruct(x1.shape, F32),
        compiler_params=pltpu.CompilerParams(
            dimension_semantics=("arbitrary",), vmem_limit_bytes=VMEM_LIMIT),
        name="moe_unsort",
    )(plan["lstart"], plan["goff"], plan["cpad"], plan["ltot"],
      ys, x1, g2, pos_cols, gate_cols)


def _rope_table(pos):
    half = ROT_DIM // 2
    inv_freq = ROPE_THETA ** (-jnp.arange(0, ROT_DIM, 2, dtype=F32) / ROT_DIM)
    ang = pos.astype(F32)[:, None] * inv_freq[None, :]
    cos, sin = jnp.cos(ang), jnp.sin(ang)
    n = pos.shape[0]
    rest = HEAD_DIM - ROT_DIM
    c = jnp.concatenate([cos, cos, jnp.ones((n, rest), F32)], axis=1)
    s_next = jnp.concatenate([-sin, jnp.zeros((n, half + rest), F32)], axis=1)
    s_prev = jnp.concatenate([jnp.zeros((n, half), F32), sin, jnp.zeros((n, rest), F32)], axis=1)
    rep = LANES // HEAD_DIM
    return jnp.concatenate([jnp.tile(c, (1, rep)), jnp.tile(s_next, (1, rep)), jnp.tile(s_prev, (1, rep))], axis=1)


def _num_expert_tiles(ntok):
    nblk = ntok // SORT_TOKENS
    worst = ntok * TOP_K + nblk * N_EXPERTS * RUN_ALIGN + N_EXPERTS * (EXPERT_TILE - 1)
    return -(-worst // EXPERT_TILE)


def _routing_plan(topi_all):
    ntok = topi_all.shape[1]
    nblk = ntok // SORT_TOKENS
    n_tiles = _num_expert_tiles(ntok)
    choice = topi_all[:TOP_K].reshape(TOP_K, nblk, SORT_TOKENS)
    experts = jnp.arange(N_EXPERTS, dtype=jnp.int32)
    cnt = (choice[:, :, :, None] == experts).astype(jnp.int32).sum(axis=(0, 2))
    cpad = jnp.maximum(-(-cnt // RUN_ALIGN) * RUN_ALIGN, RUN_ALIGN)
    lstart = jnp.cumsum(cpad, axis=1) - cpad
    tot = cpad.sum(axis=0)
    region = -(-tot // EXPERT_TILE) * EXPERT_TILE
    region_end = jnp.cumsum(region)
    gstart = region_end - region
    goff = gstart[None, :] + jnp.cumsum(cpad, axis=0) - cpad
    tile_start = jnp.arange(n_tiles, dtype=jnp.int32) * EXPERT_TILE
    tile_exp = jnp.minimum((tile_start[:, None] >= region_end[None, :]).astype(jnp.int32).sum(axis=1), N_EXPERTS - 1)
    i32 = lambda a: a.astype(jnp.int32)
    return dict(
        lstart=i32(lstart).reshape(-1), goff=i32(goff).reshape(-1), cpad=i32(cpad).reshape(-1),
        ltot=i32(cpad.sum(axis=1)), fill_start=i32(gstart + tot), fill_len=i32(region - tot),
        lstart_col=lstart.astype(F32).reshape(nblk, N_EXPERTS, 1),
        tile_exp=i32(tile_exp), nact=i32(region_end[-1] // EXPERT_TILE).reshape(1))


def kernel(x_prompt, x_sample, cache_k_win, cache_v_win, state_conv, c_prompt, c_sample, w_ada, b_ada, norm1_g, norm2_g, w_in, q_norm_g, k_norm_g, attn_sinks, w_attn_out, conv_w, conv_b, w_conv_out, w_o, w_router, b_router, w_gate, b_gate, w_up, b_up, w_down, b_down):
    nb, seq, _ = x_prompt.shape
    ns, dseq, _ = x_sample.shape
    wbuf = cache_k_win.shape[2]
    assert w_ada.shape[0] == 1, "single-layer step"
    l = 0

    mod = _adaln(jnp.concatenate([c_prompt, c_sample], axis=0), w_ada[l], b_ada[l])
    mod_p = mod[:nb].reshape(nb, 6, D_MODEL)
    mod_s = jnp.repeat(mod[nb:].reshape(ns, 6, D_MODEL).transpose(1, 0, 2), dseq, axis=1)

    bd = jnp.asarray(np.kron(np.eye(N_HEADS), np.ones((HEAD_DIM, HEAD_DIM))), BF16)
    rep = LANES // HEAD_DIM
    wr_t = w_router[l].T
    wr_hi = wr_t.astype(BF16)
    wr_lo = (wr_t - wr_hi.astype(F32)).astype(BF16)
    weights = (norm1_g[l].reshape(1, D_MODEL), norm2_g[l].reshape(1, D_MODEL), w_in[l].astype(BF16), bd,
               jnp.tile(q_norm_g[l], N_HEADS).reshape(1, Q_W), jnp.tile(k_norm_g[l], N_KV).reshape(1, KV_W),
               w_attn_out[l].astype(BF16), conv_w[l], conv_b[l].reshape(1, CONV_CH), w_conv_out[l].astype(BF16),
               w_o[l].astype(BF16), wr_hi, wr_lo, b_router[l].reshape(N_EXPERTS, 1))
    sinks = attn_sinks[l]

    tb = 512
    (x1_p, h2_p, topi_p, topg_p, kwin_p, vwin_p, zwin_p) = _prompt_mixer(
        x_prompt, mod_p, _rope_table(jnp.arange(seq)), sinks, weights, tb)
    pos_s = jnp.tile(PAST_LEN + jnp.arange(dseq), ns)
    state_rows = jnp.repeat(state_conv[l].transpose(1, 0, 2), dseq, axis=1)
    (x1_s, h2_s, topi_s, topg_s, knew, vnew, z_s) = _sample_mixer(
        x_sample.reshape(ns * dseq, D_MODEL), mod_s, _rope_table(pos_s),
        cache_k_win[l].reshape(ns, wbuf, KV_W), cache_v_win[l].reshape(ns, wbuf, KV_W),
        state_rows, sinks, weights, 16)

    ntok_p = nb * seq
    ntok = ntok_p + ns * dseq
    topi_all = jnp.concatenate([topi_p.transpose(1, 0, 2).reshape(SUBLANES, ntok_p), topi_s], axis=1)
    topg_all = jnp.concatenate([topg_p.transpose(1, 0, 2).reshape(SUBLANES, ntok_p), topg_s], axis=1)
    plan = _routing_plan(topi_all)
    tri = jnp.asarray(np.triu(np.ones((SORT_TOKENS, SORT_TOKENS)), k=1), BF16)
    xs, pos_all = _sort_tokens(plan, h2_p.reshape(ntok_p, D_MODEL), h2_s, topi_all, tri,
                               _num_expert_tiles(ntok) * EXPERT_TILE)
    pos_cols, gate_cols = pos_all.T, topg_all.T
    ys = _experts(plan["tile_exp"], plan["nact"], xs,
                  w_gate[l], b_gate[l].reshape(N_EXPERTS, 1, D_MODEL),
                  w_up[l], b_up[l].reshape(N_EXPERTS, 1, D_MODEL),
                  w_down[l], b_down[l].reshape(N_EXPERTS, 1, D_MODEL))
    blocks_per_seq = seq // SORT_TOKENS
    y_p = _unsort_tokens(plan, ys, x1_p.reshape(ntok_p, D_MODEL), mod_p[:, 5:6, :],
                         pl.BlockSpec((None, 1, D_MODEL), lambda i, *_: (i // blocks_per_seq, 0, 0)),
                         pos_cols, gate_cols, 0)
    y_s = _unsort_tokens(plan, ys, x1_s, mod_s[5],
                         pl.BlockSpec((SORT_TOKENS, D_MODEL), lambda i, *_: (i, 0)),
                         pos_cols, gate_cols, ntok_p // SORT_TOKENS)

    n_kv_shape = (N_KV, HEAD_DIM)
    k_win_p = kwin_p.reshape(1, nb, WINDOW, *n_kv_shape)
    v_win_p = vwin_p.reshape(1, nb, WINDOW, *n_kv_shape)
    conv_p = zwin_p[:, SUBLANES - (CONV_K - 1):, :][None]
    k_win_s = knew.reshape(1, ns, wbuf, *n_kv_shape)
    v_win_s = vnew.reshape(1, ns, wbuf, *n_kv_shape)
    conv_s = z_s.reshape(ns, dseq, CONV_CH)[:, dseq - (CONV_K - 1):, :][None]
    return (y_p.reshape(nb, seq, D_MODEL), y_s.reshape(ns, dseq, D_MODEL),
            k_win_p, v_win_p, conv_p, k_win_s, v_win_s, conv_s)
```

```python
import functools

import numpy as np
import jax
import jax.numpy as jnp
from jax import lax
from jax.experimental import pallas as pl
from jax.experimental.pallas import tpu as pltpu

D_MODEL = 1024
HEAD_DIM = 64
N_HEADS = 8
N_KV = 2
GROUP = N_HEADS // N_KV
Q_W = N_HEADS * HEAD_DIM
KV_W = N_KV * HEAD_DIM
WINDOW = 128
ROT_DIM = HEAD_DIM // 4
ROPE_THETA = 500000.0
ATTN_SCALE = HEAD_DIM ** -0.5
CONV_CH = D_MODEL // 2
CONV_K = 3
N_EXPERTS = 32
TOP_K = 4
SWIGLU_ALPHA = 1.702
SWIGLU_LIMIT = 7.0
MOE_BLOCK = 128
NORM_EPS = 1e-5
QK_EPS = 1e-6
NEG_INF = -1e30
PAST_LEN = 16384

LANES = 128
SUBLANES = 8
VMEM_LIMIT = 56 * 1024 * 1024

_OFF_Q = 0
_OFF_K = _OFF_Q + Q_W
_OFF_V = _OFF_K + KV_W
_OFF_GB = _OFF_V + KV_W
_OFF_GC = _OFF_GB + CONV_CH
_OFF_XC = _OFF_GC + CONV_CH
_OFF_BA = _OFF_XC + CONV_CH
_OFF_BC = _OFF_BA + D_MODEL
IN_COLS = _OFF_BC + D_MODEL

BF16 = jnp.bfloat16
F32 = jnp.float32
_NT = (((1,), (1,)), ((), ()))


def _dot(a, b):
    return jnp.dot(a, b, preferred_element_type=F32)


def _dot_nt(a, b):
    return lax.dot_general(a, b, _NT, preferred_element_type=F32)


def _sigmoid(x):
    return 1.0 / (1.0 + jnp.exp(-x))


def _split_bf16(x):
    hi = x.astype(BF16)
    lo = (x - hi.astype(F32)).astype(BF16)
    return hi, lo


def _const_spec(shape):
    nd = len(shape)
    return pl.BlockSpec(shape, lambda *_: (0,) * nd, pipeline_mode=pl.Buffered(1))


def _adaln_body(c_ref, w_ref, b_ref, o_ref):
    c = c_ref[...]
    s = c * _sigmoid(c)
    o_ref[...] = _dot(s.astype(BF16), w_ref[...].astype(BF16)) + b_ref[...]


def _adaln(c, w_ada, b_ada):
    n = c.shape[0]
    cols = w_ada.shape[1]
    bn = 1536
    return pl.pallas_call(
        _adaln_body,
        grid=(cols // bn,),
        in_specs=[pl.BlockSpec((n, D_MODEL), lambda i: (0, 0)),
                  pl.BlockSpec((D_MODEL, bn), lambda i: (0, i)),
                  pl.BlockSpec((1, bn), lambda i: (0, i))],
        out_specs=pl.BlockSpec((n, bn), lambda i: (0, i)),
        out_shape=jax.ShapeDtypeStruct((n, cols), F32),
        compiler_params=pltpu.CompilerParams(vmem_limit_bytes=VMEM_LIMIT),
        name="adaln",
    )(c, w_ada, b_ada.reshape(1, cols))


def _modulated_norm(x, g, shift, scale):
    y = x * lax.rsqrt(jnp.mean(x * x, axis=-1, keepdims=True) + NORM_EPS)
    return (y * g) * (1.0 + scale) + shift


def _head_norm(t, bd, g):
    w = t.shape[1]
    sq_hi, sq_lo = _split_bf16(t * t)
    blk = bd[:w, :w]
    ms = (_dot(sq_hi, blk) + _dot(sq_lo, blk)) * (1.0 / HEAD_DIM)
    return t * lax.rsqrt(ms + QK_EPS) * g[:, :w]


def _rope(y, rope):
    c = rope[:, 0:LANES]
    s_next = rope[:, LANES:2 * LANES]
    s_prev = rope[:, 2 * LANES:3 * LANES]
    half = ROT_DIM // 2
    return y * c + pltpu.roll(y, LANES - half, 1) * s_next + pltpu.roll(y, half, 1) * s_prev


def _pair_expand(t):
    lane = lax.broadcasted_iota(jnp.int32, t.shape, 1)
    lo = lane < HEAD_DIM
    r = pltpu.roll(t, HEAD_DIM, 1)
    zero = jnp.zeros_like(t)
    a0 = jnp.where(lo, t, zero).astype(BF16)
    b0 = jnp.where(lo, zero, r).astype(BF16)
    a1 = jnp.where(lo, r, zero).astype(BF16)
    b1 = jnp.where(lo, zero, t).astype(BF16)
    return ((a0, b0), (a1, b1))


def _softmax_pv(s_list, v_list, sink):
    m = jnp.full((s_list[0].shape[0], 1), sink, F32)
    for s in s_list:
        m = jnp.maximum(m, jnp.max(s, axis=-1, keepdims=True))
    den = jnp.exp(sink - m)
    acc = None
    for s, v in zip(s_list, v_list):
        e = jnp.exp(s - m)
        den = den + jnp.sum(e, axis=-1, keepdims=True)
        pv = _dot(e.astype(BF16), v)
        acc = pv if acc is None else acc + pv
    return acc * (1.0 / den)


def _route(h2, wr_hi, wr_lo, br):
    hi, lo = _split_bf16(h2)
    logits = _dot_nt(wr_hi, hi) + _dot_nt(wr_hi, lo) + _dot_nt(wr_lo, hi) + br
    eid = lax.broadcasted_iota(jnp.int32, logits.shape, 0).astype(F32)
    vals, idxs = [], []
    for _ in range(TOP_K):
        m = jnp.max(logits, axis=0, keepdims=True)
        idx = jnp.min(jnp.where(logits == m, eid, float(N_EXPERTS)), axis=0, keepdims=True)
        vals.append(m)
        idxs.append(idx)
        logits = jnp.where(eid == idx, -jnp.inf, logits)
    ex = [jnp.exp(v - vals[0]) for v in vals]
    tot = ex[0] + ex[1] + ex[2] + ex[3]
    inv = 1.0 / tot
    gates = [e * inv for e in ex]
    rowid = lax.broadcasted_iota(jnp.int32, (SUBLANES, h2.shape[0]), 0)
    topi = jnp.zeros((SUBLANES, h2.shape[0]), F32)
    topg = jnp.zeros((SUBLANES, h2.shape[0]), F32)
    for k in range(TOP_K):
        topi = jnp.where(rowid == k, idxs[k], topi)
        topg = jnp.where(rowid == k, gates[k], topg)
    return topi.astype(jnp.int32), topg


_QKV_COLS = (_OFF_Q, Q_W + 2 * KV_W)
_LATE_COLS = ((_OFF_GB, 2 * CONV_CH), (_OFF_XC, CONV_CH), (_OFF_BA, D_MODEL), (_OFF_BC, D_MODEL))


def _in_proj(proj, h_bf, w_in_ref, group):
    off, width = group
    proj[off] = _dot(h_bf, w_in_ref[:, off:off + width])


def _cols(proj, off, width):
    for start, piece in proj.items():
        if start <= off and off + width <= start + piece.shape[1]:
            return piece[:, off - start:off - start + width]
    raise KeyError(off)


def _merge_and_route(x, o_bf, y_conv, proj, mod, n2g_ref, w_ao_ref, w_co_ref, w_o_ref,
                     wr_hi_ref, wr_lo_ref, br_ref):
    sh1, sc1, g1, sh2, sc2, g2 = mod
    a = _dot(o_bf, w_ao_ref[...])
    c = _dot(y_conv.astype(BF16), w_co_ref[...])
    m = _sigmoid(_cols(proj, _OFF_BA, D_MODEL)) * a + _sigmoid(_cols(proj, _OFF_BC, D_MODEL)) * c
    x1 = x + g1 * _dot(m.astype(BF16), w_o_ref[...])
    h2 = _modulated_norm(x1, n2g_ref[...], sh2, sc2)
    topi, topg = _route(h2, wr_hi_ref[...], wr_lo_ref[...], br_ref[...])
    return x1, h2, topi, topg


def _prompt_body(sinks_ref, x_ref, mod_ref, rope_ref, n1g_ref, n2g_ref, w_in_ref, bd_ref, qg_ref, kg_ref,
                 w_ao_ref, cw_ref, cb_ref, w_co_ref, w_o_ref, wr_hi_ref, wr_lo_ref, br_ref,
                 x1_ref, h2_ref, topi_ref, topg_ref, kwin_ref, vwin_ref, zwin_ref,
                 kbuf, vbuf, zbuf, *, tb):
    j = pl.program_id(1)
    nsub = tb // WINDOW

    @pl.when(j == 0)
    def _():
        kbuf[0:WINDOW, :] = jnp.zeros((WINDOW, KV_W), F32)
        vbuf[0:WINDOW, :] = jnp.zeros((WINDOW, KV_W), F32)
        zbuf[0:SUBLANES, :] = jnp.zeros((SUBLANES, CONV_CH), F32)

    x = x_ref[0]
    mod = tuple(mod_ref[0, i:i + 1, :] for i in range(6))
    h_bf = _modulated_norm(x, n1g_ref[...], mod[0], mod[1]).astype(BF16)
    proj = {}
    _in_proj(proj, h_bf, w_in_ref, _QKV_COLS)
    _in_proj(proj, h_bf, w_in_ref, _LATE_COLS[0])
    rope = rope_ref[...]
    bd = bd_ref[...]

    k = _rope(_head_norm(_cols(proj, _OFF_K, KV_W), bd, kg_ref[...]), rope)
    v = _cols(proj, _OFF_V, KV_W)
    kbuf[WINDOW:WINDOW + tb, :] = k
    vbuf[WINDOW:WINDOW + tb, :] = v
    kwin_ref[0] = k[tb - WINDOW:tb]
    vwin_ref[0] = v[tb - WINDOW:tb]
    k_exp = _pair_expand(kbuf[...])
    v_exp = _pair_expand(vbuf[...])

    _in_proj(proj, h_bf, w_in_ref, _LATE_COLS[1])
    qn = _head_norm(_cols(proj, _OFF_Q, Q_W), bd, qg_ref[...])
    q_tiles = []
    for t in range(Q_W // LANES):
        qt = _rope(qn[:, t * LANES:(t + 1) * LANES], rope)
        q_tiles.append((qt * ATTN_SCALE).astype(BF16))

    row = lax.broadcasted_iota(jnp.int32, (WINDOW, 2 * WINDOW), 0)
    col = lax.broadcasted_iota(jnp.int32, (WINDOW, 2 * WINDOW), 1)
    band = (col > row) & (col <= row + WINDOW)
    first_key = jnp.where(j > 0, 0, WINDOW)
    band_first = band & (col >= first_key)
    o_rows = []
    for i in range(nsub):
        for group in _LATE_COLS[2:][i::nsub]:
            _in_proj(proj, h_bf, w_in_ref, group)
        mask = band_first if i == 0 else band
        keys = slice(i * WINDOW, (i + 2) * WINDOW)
        o_tiles = []
        for t in range(Q_W // LANES):
            c = t // 2
            qt = q_tiles[t][i * WINDOW:(i + 1) * WINDOW]
            o_t = None
            for half in range(2):
                s = jnp.where(mask, _dot_nt(qt, k_exp[c][half][keys]), NEG_INF)
                sink = sinks_ref[c * GROUP + (t % 2) * 2 + half]
                part = _softmax_pv([s], [v_exp[c][half][keys]], sink)
                o_t = part if o_t is None else o_t + part
            o_tiles.append(o_t.astype(BF16))
        o_rows.append(jnp.concatenate(o_tiles, axis=1))
    o_bf = jnp.concatenate(o_rows, axis=0) if nsub > 1 else o_rows[0]
    kbuf[0:WINDOW, :] = kbuf[tb:tb + WINDOW, :]
    vbuf[0:WINDOW, :] = vbuf[tb:tb + WINDOW, :]

    z = _cols(proj, _OFF_GC, CONV_CH) * _cols(proj, _OFF_XC, CONV_CH)
    zbuf[SUBLANES:SUBLANES + tb, :] = z
    zwin_ref[0] = z[tb - SUBLANES:tb]
    z1 = zbuf[SUBLANES - 1:SUBLANES - 1 + tb, :]
    z2 = zbuf[SUBLANES - 2:SUBLANES - 2 + tb, :]
    conv = cb_ref[...] + cw_ref[0:1, :] * z2 + cw_ref[1:2, :] * z1 + cw_ref[2:3, :] * z
    y_conv = _cols(proj, _OFF_GB, CONV_CH) * conv
    zbuf[0:SUBLANES, :] = zbuf[tb:tb + SUBLANES, :]

    x1, h2, topi, topg = _merge_and_route(x, o_bf, y_conv, proj, mod, n2g_ref, w_ao_ref, w_co_ref,
                                          w_o_ref, wr_hi_ref, wr_lo_ref, br_ref)
    x1_ref[0] = x1
    h2_ref[0] = h2.astype(BF16)
    topi_ref[0] = topi
    topg_ref[0] = topg


def _prompt_mixer(x, mod, rope, sinks, weights, tb):
    nb, seq, _ = x.shape
    (n1g, n2g, w_in, bd, qg, kg, w_ao, cw, cb, w_co, w_o, wr_hi, wr_lo, br) = weights
    consts = (n1g, n2g, w_in, bd, qg, kg, w_ao, cw, cb, w_co, w_o, wr_hi, wr_lo, br)
    in_specs = [
        pl.BlockSpec((1, tb, D_MODEL), lambda b, j, s: (b, j, 0)),
        pl.BlockSpec((1, 6, D_MODEL), lambda b, j, s: (b, 0, 0)),
        pl.BlockSpec((tb, 3 * LANES), lambda b, j, s: (j, 0)),
    ] + [_const_spec(w.shape) for w in consts]
    out_specs = [
        pl.BlockSpec((1, tb, D_MODEL), lambda b, j, s: (b, j, 0)),
        pl.BlockSpec((1, tb, D_MODEL), lambda b, j, s: (b, j, 0)),
        pl.BlockSpec((1, SUBLANES, tb), lambda b, j, s: (b, 0, j)),
        pl.BlockSpec((1, SUBLANES, tb), lambda b, j, s: (b, 0, j)),
        pl.BlockSpec((1, WINDOW, KV_W), lambda b, j, s: (b, 0, 0)),
        pl.BlockSpec((1, WINDOW, KV_W), lambda b, j, s: (b, 0, 0)),
        pl.BlockSpec((1, SUBLANES, CONV_CH), lambda b, j, s: (b, 0, 0)),
    ]
    out_shape = [
        jax.ShapeDtypeStruct((nb, seq, D_MODEL), F32),
        jax.ShapeDtypeStruct((nb, seq, D_MODEL), BF16),
        jax.ShapeDtypeStruct((nb, SUBLANES, seq), jnp.int32),
        jax.ShapeDtypeStruct((nb, SUBLANES, seq), F32),
        jax.ShapeDtypeStruct((nb, WINDOW, KV_W), F32),
        jax.ShapeDtypeStruct((nb, WINDOW, KV_W), F32),
        jax.ShapeDtypeStruct((nb, SUBLANES, CONV_CH), F32),
    ]
    return pl.pallas_call(
        functools.partial(_prompt_body, tb=tb),
        grid_spec=pltpu.PrefetchScalarGridSpec(
            num_scalar_prefetch=1, grid=(nb, seq // tb),
            in_specs=in_specs, out_specs=out_specs,
            scratch_shapes=[pltpu.VMEM((tb + WINDOW, KV_W), F32),
                            pltpu.VMEM((tb + WINDOW, KV_W), F32),
                            pltpu.VMEM((tb + SUBLANES, CONV_CH), F32)]),
        out_shape=out_shape,
        compiler_params=pltpu.CompilerParams(
            dimension_semantics=("arbitrary", "arbitrary"), vmem_limit_bytes=VMEM_LIMIT),
        name="prompt_mixer",
    )(sinks, x, mod, rope, *consts)


def _sample_body(sinks_ref, x_ref, mod_ref, rope_ref, ck_ref, cv_ref, st_ref,
                 n1g_ref, n2g_ref, w_in_ref, bd_ref, qg_ref, kg_ref,
                 w_ao_ref, cw_ref, cb_ref, w_co_ref, w_o_ref, wr_hi_ref, wr_lo_ref, br_ref,
                 x1_ref, h2_ref, topi_ref, topg_ref, knew_ref, vnew_ref, z_ref, *, nseq, dseq):
    rows = nseq * dseq
    wbuf = ck_ref.shape[1]
    x = x_ref[...]
    mod = tuple(mod_ref[i] for i in range(6))
    h_bf = _modulated_norm(x, n1g_ref[...], mod[0], mod[1]).astype(BF16)
    proj = {}
    for group in (_QKV_COLS,) + _LATE_COLS:
        _in_proj(proj, h_bf, w_in_ref, group)
    rope = rope_ref[...]
    bd = bd_ref[...]

    k = _rope(_head_norm(_cols(proj, _OFF_K, KV_W), bd, kg_ref[...]), rope)
    v = _cols(proj, _OFF_V, KV_W)
    for win_ref, cache_ref, new in ((knew_ref, ck_ref, k), (vnew_ref, cv_ref, v)):
        win_ref[:, 0:wbuf - dseq, :] = cache_ref[:, dseq:, :]
        win_ref[:, wbuf - dseq:, :] = new.reshape(nseq, dseq, KV_W)
    kn_exp = _pair_expand(k)
    vn_exp = _pair_expand(v)
    kc_exp = _pair_expand(ck_ref[...].reshape(nseq * wbuf, KV_W))
    vc_exp = _pair_expand(cv_ref[...].reshape(nseq * wbuf, KV_W))

    qrow = lax.broadcasted_iota(jnp.int32, (rows, nseq * wbuf), 0)
    ccol = lax.broadcasted_iota(jnp.int32, (rows, nseq * wbuf), 1)
    mask_c = ((ccol // wbuf) == (qrow // dseq)) & ((ccol % wbuf) > (qrow % dseq) + (wbuf - WINDOW))
    qrow_n = lax.broadcasted_iota(jnp.int32, (rows, rows), 0)
    ncol = lax.broadcasted_iota(jnp.int32, (rows, rows), 1)
    mask_n = ((ncol // dseq) == (qrow_n // dseq)) & ((ncol % dseq) <= (qrow_n % dseq))

    qn = _head_norm(_cols(proj, _OFF_Q, Q_W), bd, qg_ref[...])
    o_tiles = []
    for t in range(Q_W // LANES):
        c = t // 2
        qt = (_rope(qn[:, t * LANES:(t + 1) * LANES], rope) * ATTN_SCALE).astype(BF16)
        o_t = None
        for half in range(2):
            s_c = jnp.where(mask_c, _dot_nt(qt, kc_exp[c][half]), NEG_INF)
            s_n = jnp.where(mask_n, _dot_nt(qt, kn_exp[c][half]), NEG_INF)
            sink = sinks_ref[c * GROUP + (t % 2) * 2 + half]
            part = _softmax_pv([s_c, s_n], [vc_exp[c][half], vn_exp[c][half]], sink)
            o_t = part if o_t is None else o_t + part
        o_tiles.append(o_t.astype(BF16))
    o_bf = jnp.concatenate(o_tiles, axis=1)

    z = _cols(proj, _OFF_GC, CONV_CH) * _cols(proj, _OFF_XC, CONV_CH)
    z_ref[...] = z
    r = lax.broadcasted_iota(jnp.int32, z.shape, 0) % dseq
    st0 = st_ref[0]
    st1 = st_ref[1]
    z1 = jnp.where(r == 0, st1, pltpu.roll(z, 1, 0))
    z2 = jnp.where(r == 0, st0, jnp.where(r == 1, st1, pltpu.roll(z, 2, 0)))
    conv = cb_ref[...] + cw_ref[0:1, :] * z2 + cw_ref[1:2, :] * z1 + cw_ref[2:3, :] * z
    y_conv = _cols(proj, _OFF_GB, CONV_CH) * conv

    x1, h2, topi, topg = _merge_and_route(x, o_bf, y_conv, proj, mod, n2g_ref, w_ao_ref, w_co_ref,
                                          w_o_ref, wr_hi_ref, wr_lo_ref, br_ref)
    x1_ref[...] = x1
    h2_ref[...] = h2.astype(BF16)
    topi_ref[...] = topi
    topg_ref[...] = topg


def _sample_mixer(x, mod, rope, cache_k, cache_v, state, sinks, weights, nseq):
    ntok = x.shape[0]
    nall, wbuf, _ = cache_k.shape
    dseq = ntok // nall
    rows = nseq * dseq
    consts = weights
    in_specs = [
        pl.BlockSpec((rows, D_MODEL), lambda i, s: (i, 0)),
        pl.BlockSpec((6, rows, D_MODEL), lambda i, s: (0, i, 0)),
        pl.BlockSpec((rows, 3 * LANES), lambda i, s: (i, 0)),
        pl.BlockSpec((nseq, wbuf, KV_W), lambda i, s: (i, 0, 0)),
        pl.BlockSpec((nseq, wbuf, KV_W), lambda i, s: (i, 0, 0)),
        pl.BlockSpec((2, rows, CONV_CH), lambda i, s: (0, i, 0)),
    ] + [_const_spec(w.shape) for w in consts]
    out_specs = [
        pl.BlockSpec((rows, D_MODEL), lambda i, s: (i, 0)),
        pl.BlockSpec((rows, D_MODEL), lambda i, s: (i, 0)),
        pl.BlockSpec((SUBLANES, rows), lambda i, s: (0, i)),
        pl.BlockSpec((SUBLANES, rows), lambda i, s: (0, i)),
        pl.BlockSpec((nseq, wbuf, KV_W), lambda i, s: (i, 0, 0)),
        pl.BlockSpec((nseq, wbuf, KV_W), lambda i, s: (i, 0, 0)),
        pl.BlockSpec((rows, CONV_CH), lambda i, s: (i, 0)),
    ]
    out_shape = [
        jax.ShapeDtypeStruct((ntok, D_MODEL), F32),
        jax.ShapeDtypeStruct((ntok, D_MODEL), BF16),
        jax.ShapeDtypeStruct((SUBLANES, ntok), jnp.int32),
        jax.ShapeDtypeStruct((SUBLANES, ntok), F32),
        jax.ShapeDtypeStruct((nall, wbuf, KV_W), F32),
        jax.ShapeDtypeStruct((nall, wbuf, KV_W), F32),
        jax.ShapeDtypeStruct((ntok, CONV_CH), F32),
    ]
    return pl.pallas_call(
        functools.partial(_sample_body, nseq=nseq, dseq=dseq),
        grid_spec=pltpu.PrefetchScalarGridSpec(
            num_scalar_prefetch=1, grid=(ntok // rows,),
            in_specs=in_specs, out_specs=out_specs),
        out_shape=out_shape,
        compiler_params=pltpu.CompilerParams(
            dimension_semantics=("arbitrary",), vmem_limit_bytes=VMEM_LIMIT),
        name="sample_mixer",
    )(sinks, x, mod, rope, cache_k, cache_v, state, *consts)


SORT_TOKENS = 256
RUN_ALIGN = SUBLANES
EXPERT_TILE = 1024
HALF_D = D_MODEL // 2
LOCAL_ROWS = SORT_TOKENS * TOP_K + N_EXPERTS * RUN_ALIGN
UNSORT_CHUNK = 256
assert LOCAL_ROWS % UNSORT_CHUNK == 0
_HI_MASK = 0xFFFF0000


def _pack_halves(x):
    lo = pltpu.bitcast(x[:, :HALF_D], jnp.uint32)
    hi = pltpu.bitcast(x[:, HALF_D:], jnp.uint32)
    return hi | (lo >> 16)


def _unpack_halves(w):
    lo = pltpu.bitcast(w << 16, F32).astype(BF16)
    hi = pltpu.bitcast(w & jnp.uint32(_HI_MASK), F32).astype(BF16)
    return lo, hi


def _local_positions(topi, lstart_col, tri):
    ntok = topi.shape[1]
    eid = lax.broadcasted_iota(jnp.int32, (N_EXPERTS, ntok), 0)
    hits = [eid == topi[k:k + 1, :] for k in range(TOP_K)]
    chosen = jnp.zeros((N_EXPERTS, ntok), F32)
    for h in hits:
        chosen = jnp.where(h, 1.0, chosen)
    base = _dot(chosen.astype(BF16), tri) + lstart_col
    return [jnp.sum(jnp.where(h, base, 0.0), axis=0, keepdims=True) for h in hits]


def _run_copy(loc, glob, sem, lofs, gofs, n, to_global):
    lo = loc.at[pl.ds(pl.multiple_of(lofs, RUN_ALIGN), n)]
    gl = glob.at[pl.ds(pl.multiple_of(gofs, RUN_ALIGN), n)]
    return pltpu.make_async_copy(lo, gl, sem) if to_global else pltpu.make_async_copy(gl, lo, sem)


def _start_runs(meta, blk, loc, glob, sem, to_global):
    lstart_ref, goff_ref, cpad_ref = meta
    for e in range(N_EXPERTS):
        n = pl.multiple_of(cpad_ref[blk * N_EXPERTS + e], RUN_ALIGN)
        _run_copy(loc, glob, sem, lstart_ref[blk * N_EXPERTS + e], goff_ref[blk * N_EXPERTS + e], n,
                  to_global).start()


def _wait_runs(nrows, loc, glob, sem, to_global):
    _run_copy(loc, glob, sem, 0, 0, pl.multiple_of(nrows, RUN_ALIGN), to_global).wait()


def _sort_body(lstart_ref, goff_ref, cpad_ref, ltot_ref, fstart_ref, flen_ref, nact_ref,
               hp_ref, hs_ref, topi_ref, lcol_ref, tri_ref, xs_hbm, pos_ref, buf, zbuf, sems, fsem, *, nblk_p, nblk):
    i = pl.program_id(0)
    slot = i % 2
    meta = (lstart_ref, goff_ref, cpad_ref)

    @pl.when(i == 0)
    def _():
        zbuf[...] = jnp.zeros(zbuf.shape, jnp.uint32)

        def fill(e, tot):
            n = pl.multiple_of(flen_ref[e], RUN_ALIGN)

            @pl.when(n > 0)
            def _():
                _run_copy(zbuf, xs_hbm, fsem, 0, fstart_ref[e], n, True).start()
            return tot + n
        total = lax.fori_loop(0, N_EXPERTS, fill, 0)

        @pl.when(total > 0)
        def _():
            _wait_runs(total, zbuf, xs_hbm, fsem, True)

        def fill_tile(t, carry):
            cp = _run_copy(zbuf, xs_hbm, fsem, 0, t * EXPERT_TILE, EXPERT_TILE, True)
            cp.start()
            cp.wait()
            return carry
        lax.fori_loop(nact_ref[0], xs_hbm.shape[0] // EXPERT_TILE, fill_tile, 0)

    def run(h_ref):
        pos = [p.astype(jnp.int32) for p in _local_positions(topi_ref[...], lcol_ref[...], tri_ref[...])]
        rowid = lax.broadcasted_iota(jnp.int32, (LOCAL_ROWS, SORT_TOKENS), 0)
        onehot = jnp.zeros((LOCAL_ROWS, SORT_TOKENS), F32)
        for p in pos:
            onehot = jnp.where(rowid == p, 1.0, onehot)
        buf[slot] = _pack_halves(_dot(onehot.astype(BF16), h_ref[...]))
        krow = lax.broadcasted_iota(jnp.int32, (SUBLANES, SORT_TOKENS), 0)
        pos_rows = jnp.zeros((SUBLANES, SORT_TOKENS), jnp.int32)
        for k in range(TOP_K):
            pos_rows = jnp.where(krow == k, pos[k], pos_rows)
        pos_ref[...] = pos_rows

    @pl.when(i < nblk_p)
    def _():
        run(hp_ref)

    @pl.when(i >= nblk_p)
    def _():
        run(hs_ref)

    _start_runs(meta, i, buf.at[slot], xs_hbm, sems.at[slot], True)

    @pl.when(i > 0)
    def _():
        _wait_runs(ltot_ref[jnp.maximum(i - 1, 0)], buf.at[1 - slot], xs_hbm, sems.at[1 - slot], True)

    @pl.when(i == nblk - 1)
    def _():
        _wait_runs(ltot_ref[i], buf.at[slot], xs_hbm, sems.at[slot], True)


def _sort_tokens(plan, h_p, h_s, topi_all, tri, n_slots):
    nblk_p = h_p.shape[0] // SORT_TOKENS
    nblk = nblk_p + h_s.shape[0] // SORT_TOKENS
    return pl.pallas_call(
        functools.partial(_sort_body, nblk_p=nblk_p, nblk=nblk),
        grid_spec=pltpu.PrefetchScalarGridSpec(
            num_scalar_prefetch=7, grid=(nblk,),
            in_specs=[pl.BlockSpec((SORT_TOKENS, D_MODEL), lambda i, *_: (jnp.minimum(i, nblk_p - 1), 0)),
                      pl.BlockSpec((SORT_TOKENS, D_MODEL), lambda i, *_: (jnp.maximum(i - nblk_p, 0), 0)),
                      pl.BlockSpec((SUBLANES, SORT_TOKENS), lambda i, *_: (0, i)),
                      pl.BlockSpec((None, N_EXPERTS, 1), lambda i, *_: (i, 0, 0)),
                      _const_spec(tri.shape)],
            out_specs=[pl.BlockSpec(memory_space=pl.ANY),
                       pl.BlockSpec((SUBLANES, SORT_TOKENS), lambda i, *_: (0, i))],
            scratch_shapes=[pltpu.VMEM((2, LOCAL_ROWS, HALF_D), jnp.uint32),
                            pltpu.VMEM((EXPERT_TILE, HALF_D), jnp.uint32),
                            pltpu.SemaphoreType.DMA((2,)),
                            pltpu.SemaphoreType.DMA]),
        out_shape=[jax.ShapeDtypeStruct((n_slots, HALF_D), jnp.uint32),
                   jax.ShapeDtypeStruct(topi_all.shape, jnp.int32)],
        compiler_params=pltpu.CompilerParams(
            dimension_semantics=("arbitrary",), vmem_limit_bytes=VMEM_LIMIT),
        name="moe_sort",
    )(plan["lstart"], plan["goff"], plan["cpad"], plan["ltot"], plan["fill_start"], plan["fill_len"], plan["nact"],
      h_p, h_s, topi_all, plan["lstart_col"], tri)


def _experts_body(tile_exp_ref, nact_ref, rows_ref, xs_ref, wg_hbm, bg_ref, wu_hbm, bu_ref, wd_hbm, bd_ref,
                  ys_ref, w_f32, w_bf, sems):
    t = pl.program_id(0)
    nact = nact_ref[0]
    w_hbm = (wg_hbm, wu_hbm, wd_hbm)

    def weight_copies(e, slot):
        return [pltpu.make_async_copy(w.at[e], w_f32.at[slot, j], sems.at[slot, j]) for j, w in enumerate(w_hbm)]

    @pl.when(t < nact)
    def _():
        e = tile_exp_ref[t]
        slot = e % 2

        @pl.when(t == 0)
        def _():
            for cp in weight_copies(e, slot):
                cp.start()

        @pl.when((t == 0) | (e != tile_exp_ref[jnp.maximum(t - 1, 0)]))
        def _():
            @pl.when(e + 1 < N_EXPERTS)
            def _():
                for cp in weight_copies(e + 1, 1 - slot):
                    cp.start()
            for j, cp in enumerate(weight_copies(e, slot)):
                cp.wait()
                w_bf[j] = w_f32[slot, j].astype(BF16)

        def swiglu(rows):
            x = jnp.concatenate(_unpack_halves(xs_ref[0:rows, :]), axis=1)
            g = _dot(x, w_bf[0]) + bg_ref[0]
            u = _dot(x, w_bf[1]) + bu_ref[0]
            g = jnp.minimum(g, SWIGLU_LIMIT)
            u = jnp.clip(u, -SWIGLU_LIMIT, SWIGLU_LIMIT)
            a = g * _sigmoid(SWIGLU_ALPHA * g) * (u + 1.0)
            out = _dot(a.astype(BF16), w_bf[2]) + bd_ref[0]
            ys_ref[0:rows, :] = _pack_halves(out.astype(BF16).astype(F32))

        half = EXPERT_TILE // 2
        used = rows_ref[t]

        @pl.when(used > half)
        def _():
            swiglu(EXPERT_TILE)

        @pl.when(used <= half)
        def _():
            swiglu(half)
            ys_ref[half:, :] = jnp.zeros((EXPERT_TILE - half, HALF_D), jnp.uint32)

    @pl.when(t >= nact)
    def _():
        ys_ref[...] = jnp.zeros(ys_ref.shape, jnp.uint32)


def _experts(tile_exp, nact, tile_rows, xs, wg, bg, wu, bu, wd, bd):
    n_tiles = tile_exp.shape[0]

    def active(t, na):
        return jnp.minimum(t, na[0] - 1)
    wspec = pl.BlockSpec(memory_space=pl.ANY)
    bspec = pl.BlockSpec((1, 1, D_MODEL), lambda t, te, na, tr: (te[active(t, na)], 0, 0))
    xspec = pl.BlockSpec((EXPERT_TILE, HALF_D), lambda t, te, na, tr: (active(t, na), 0))
    return pl.pallas_call(
        _experts_body,
        grid_spec=pltpu.PrefetchScalarGridSpec(
            num_scalar_prefetch=3, grid=(n_tiles,),
            in_specs=[xspec, wspec, bspec, wspec, bspec, wspec, bspec],
            out_specs=pl.BlockSpec((EXPERT_TILE, HALF_D), lambda t, te, na, tr: (t, 0)),
            scratch_shapes=[pltpu.VMEM((2, 3, D_MODEL, D_MODEL), F32),
                            pltpu.VMEM((3, D_MODEL, D_MODEL), BF16),
                            pltpu.SemaphoreType.DMA((2, 3))]),
        out_shape=jax.ShapeDtypeStruct(xs.shape, jnp.uint32),
        compiler_params=pltpu.CompilerParams(
            dimension_semantics=("arbitrary",), vmem_limit_bytes=VMEM_LIMIT),
        name="experts",
    )(tile_exp, nact, tile_rows, xs, wg, bg, wu, bu, wd, bd)


def _unsort_body(lstart_ref, goff_ref, cpad_ref, ltot_ref,
                 ys_hbm, x1_ref, g2_ref, pos_ref, gate_ref, y_ref, buf, sems, *, blk0, nblk):
    i = pl.program_id(0)
    b = blk0 + i
    slot = i % 2
    meta = (lstart_ref, goff_ref, cpad_ref)

    @pl.when(i == 0)
    def _():
        buf[...] = jnp.zeros(buf.shape, jnp.uint32)
        _start_runs(meta, b, buf.at[0], ys_hbm, sems.at[0], False)

    @pl.when(i + 1 < nblk)
    def _():
        _start_runs(meta, b + 1, buf.at[1 - slot], ys_hbm, sems.at[1 - slot], False)

    _wait_runs(ltot_ref[b], buf.at[slot], ys_hbm, sems.at[slot], False)
    pos_cols = pos_ref[...]
    gate_cols = gate_ref[...]
    acc_lo = acc_hi = None
    for c0 in range(0, LOCAL_ROWS, UNSORT_CHUNK):
        colid = lax.broadcasted_iota(jnp.int32, (SORT_TOKENS, UNSORT_CHUNK), 1) + c0
        weights = jnp.zeros((SORT_TOKENS, UNSORT_CHUNK), F32)
        for k in range(TOP_K):
            weights = jnp.where(colid == pos_cols[:, k:k + 1], gate_cols[:, k:k + 1], weights)
        weights = weights.astype(BF16)
        y_lo, y_hi = _unpack_halves(buf[slot, c0:c0 + UNSORT_CHUNK, :])
        acc_lo = _dot(weights, y_lo) if acc_lo is None else acc_lo + _dot(weights, y_lo)
        acc_hi = _dot(weights, y_hi) if acc_hi is None else acc_hi + _dot(weights, y_hi)
    g2 = g2_ref[...]
    y_ref[:, :HALF_D] = x1_ref[:, :HALF_D] + g2[:, :HALF_D] * acc_lo
    y_ref[:, HALF_D:] = x1_ref[:, HALF_D:] + g2[:, HALF_D:] * acc_hi


def _unsort_tokens(plan, ys, x1, g2, g2_spec, pos_cols, gate_cols, blk0):
    nblk = x1.shape[0] // SORT_TOKENS
    return pl.pallas_call(
        functools.partial(_unsort_body, blk0=blk0, nblk=nblk),
        grid_spec=pltpu.PrefetchScalarGridSpec(
            num_scalar_prefetch=4, grid=(nblk,),
            in_specs=[pl.BlockSpec(memory_space=pl.ANY),
                      pl.BlockSpec((SORT_TOKENS, D_MODEL), lambda i, *_: (i, 0)),
                      g2_spec,
                      pl.BlockSpec((SORT_TOKENS, SUBLANES), lambda i, *_: (blk0 + i, 0)),
                      pl.BlockSpec((SORT_TOKENS, SUBLANES), lambda i, *_: (blk0 + i, 0))],
            out_specs=pl.BlockSpec((SORT_TOKENS, D_MODEL), lambda i, *_: (i, 0)),
            scratch_shapes=[pltpu.VMEM((2, LOCAL_ROWS, HALF_D), jnp.uint32),
                            pltpu.SemaphoreType.DMA((2,))]),
        out_shape=jax.ShapeDtypeStruct(x1.shape, F32),
        compiler_params=pltpu.CompilerParams(
            dimension_semantics=("arbitrary",), vmem_limit_bytes=VMEM_LIMIT),
        name="moe_unsort",
    )(plan["lstart"], plan["goff"], plan["cpad"], plan["ltot"],
      ys, x1, g2, pos_cols, gate_cols)


def _rope_table(pos):
    half = ROT_DIM // 2
    inv_freq = ROPE_THETA ** (-jnp.arange(0, ROT_DIM, 2, dtype=F32) / ROT_DIM)
    ang = pos.astype(F32)[:, None] * inv_freq[None, :]
    cos, sin = jnp.cos(ang), jnp.sin(ang)
    n = pos.shape[0]
    rest = HEAD_DIM - ROT_DIM
    c = jnp.concatenate([cos, cos, jnp.ones((n, rest), F32)], axis=1)
    s_next = jnp.concatenate([-sin, jnp.zeros((n, half + rest), F32)], axis=1)
    s_prev = jnp.concatenate([jnp.zeros((n, half), F32), sin, jnp.zeros((n, rest), F32)], axis=1)
    rep = LANES // HEAD_DIM
    return jnp.concatenate([jnp.tile(c, (1, rep)), jnp.tile(s_next, (1, rep)), jnp.tile(s_prev, (1, rep))], axis=1)


def _num_expert_tiles(ntok):
    nblk = ntok // SORT_TOKENS
    worst = ntok * TOP_K + nblk * N_EXPERTS * RUN_ALIGN + N_EXPERTS * (EXPERT_TILE - 1)
    return -(-worst // EXPERT_TILE)


def _routing_plan(topi_all):
    ntok = topi_all.shape[1]
    nblk = ntok // SORT_TOKENS
    n_tiles = _num_expert_tiles(ntok)
    choice = topi_all[:TOP_K].reshape(TOP_K, nblk, SORT_TOKENS)
    experts = jnp.arange(N_EXPERTS, dtype=jnp.int32)
    cnt = (choice[:, :, :, None] == experts).astype(jnp.int32).sum(axis=(0, 2))
    cpad = jnp.maximum(-(-cnt // RUN_ALIGN) * RUN_ALIGN, RUN_ALIGN)
    lstart = jnp.cumsum(cpad, axis=1) - cpad
    tot = cpad.sum(axis=0)
    region = -(-tot // EXPERT_TILE) * EXPERT_TILE
    region_end = jnp.cumsum(region)
    gstart = region_end - region
    goff = gstart[None, :] + jnp.cumsum(cpad, axis=0) - cpad
    tile_start = jnp.arange(n_tiles, dtype=jnp.int32) * EXPERT_TILE
    tile_exp = jnp.minimum((tile_start[:, None] >= region_end[None, :]).astype(jnp.int32).sum(axis=1), N_EXPERTS - 1)
    tile_rows = jnp.clip((gstart + tot)[tile_exp] - tile_start, 0, EXPERT_TILE)
    i32 = lambda a: a.astype(jnp.int32)
    return dict(
        tile_rows=i32(tile_rows),
        lstart=i32(lstart).reshape(-1), goff=i32(goff).reshape(-1), cpad=i32(cpad).reshape(-1),
        ltot=i32(cpad.sum(axis=1)), fill_start=i32(gstart + tot), fill_len=i32(region - tot),
        lstart_col=lstart.astype(F32).reshape(nblk, N_EXPERTS, 1),
        tile_exp=i32(tile_exp), nact=i32(region_end[-1] // EXPERT_TILE).reshape(1))


def kernel(x_prompt, x_sample, cache_k_win, cache_v_win, state_conv, c_prompt, c_sample, w_ada, b_ada, norm1_g, norm2_g, w_in, q_norm_g, k_norm_g, attn_sinks, w_attn_out, conv_w, conv_b, w_conv_out, w_o, w_router, b_router, w_gate, b_gate, w_up, b_up, w_down, b_down):
    nb, seq, _ = x_prompt.shape
    ns, dseq, _ = x_sample.shape
    wbuf = cache_k_win.shape[2]
    assert w_ada.shape[0] == 1, "single-layer step"
    l = 0

    mod = _adaln(jnp.concatenate([c_prompt, c_sample], axis=0), w_ada[l], b_ada[l])
    mod_p = mod[:nb].reshape(nb, 6, D_MODEL)
    mod_s = jnp.repeat(mod[nb:].reshape(ns, 6, D_MODEL).transpose(1, 0, 2), dseq, axis=1)

    bd = jnp.asarray(np.kron(np.eye(N_HEADS), np.ones((HEAD_DIM, HEAD_DIM))), BF16)
    rep = LANES // HEAD_DIM
    wr_t = w_router[l].T
    wr_hi = wr_t.astype(BF16)
    wr_lo = (wr_t - wr_hi.astype(F32)).astype(BF16)
    weights = (norm1_g[l].reshape(1, D_MODEL), norm2_g[l].reshape(1, D_MODEL), w_in[l].astype(BF16), bd,
               jnp.tile(q_norm_g[l], N_HEADS).reshape(1, Q_W), jnp.tile(k_norm_g[l], N_KV).reshape(1, KV_W),
               w_attn_out[l].astype(BF16), conv_w[l], conv_b[l].reshape(1, CONV_CH), w_conv_out[l].astype(BF16),
               w_o[l].astype(BF16), wr_hi, wr_lo, b_router[l].reshape(N_EXPERTS, 1))
    sinks = attn_sinks[l]

    tb = 512
    (x1_p, h2_p, topi_p, topg_p, kwin_p, vwin_p, zwin_p) = _prompt_mixer(
        x_prompt, mod_p, _rope_table(jnp.arange(seq)), sinks, weights, tb)
    pos_s = jnp.tile(PAST_LEN + jnp.arange(dseq), ns)
    state_rows = jnp.repeat(state_conv[l].transpose(1, 0, 2), dseq, axis=1)
    (x1_s, h2_s, topi_s, topg_s, knew, vnew, z_s) = _sample_mixer(
        x_sample.reshape(ns * dseq, D_MODEL), mod_s, _rope_table(pos_s),
        cache_k_win[l].reshape(ns, wbuf, KV_W), cache_v_win[l].reshape(ns, wbuf, KV_W),
        state_rows, sinks, weights, 16)

    ntok_p = nb * seq
    ntok = ntok_p + ns * dseq
    topi_all = jnp.concatenate([topi_p.transpose(1, 0, 2).reshape(SUBLANES, ntok_p), topi_s], axis=1)
    topg_all = jnp.concatenate([topg_p.transpose(1, 0, 2).reshape(SUBLANES, ntok_p), topg_s], axis=1)
    plan = _routing_plan(topi_all)
    tri = jnp.asarray(np.triu(np.ones((SORT_TOKENS, SORT_TOKENS)), k=1), BF16)
    xs, pos_all = _sort_tokens(plan, h2_p.reshape(ntok_p, D_MODEL), h2_s, topi_all, tri,
                               _num_expert_tiles(ntok) * EXPERT_TILE)
    pos_cols, gate_cols = pos_all.T, topg_all.T
    ys = _experts(plan["tile_exp"], plan["nact"], plan["tile_rows"], xs,
                  w_gate[l], b_gate[l].reshape(N_EXPERTS, 1, D_MODEL),
                  w_up[l], b_up[l].reshape(N_EXPERTS, 1, D_MODEL),
                  w_down[l], b_down[l].reshape(N_EXPERTS, 1, D_MODEL))
    blocks_per_seq = seq // SORT_TOKENS
    y_p = _unsort_tokens(plan, ys, x1_p.reshape(ntok_p, D_MODEL), mod_p[:, 5:6, :],
                         pl.BlockSpec((None, 1, D_MODEL), lambda i, *_: (i // blocks_per_seq, 0, 0)),
                         pos_cols, gate_cols, 0)
    y_s = _unsort_tokens(plan, ys, x1_s, mod_s[5],
                         pl.BlockSpec((SORT_TOKENS, D_MODEL), lambda i, *_: (i, 0)),
                         pos_cols, gate_cols, ntok_p // SORT_TOKENS)

    n_kv_shape = (N_KV, HEAD_DIM)
    k_win_p = kwin_p.reshape(1, nb, WINDOW, *n_kv_shape)
    v_win_p = vwin_p.reshape(1, nb, WINDOW, *n_kv_shape)
    conv_p = zwin_p[:, SUBLANES - (CONV_K - 1):, :][None]
    k_win_s = knew.reshape(1, ns, wbuf, *n_kv_shape)
    v_win_s = vnew.reshape(1, ns, wbuf, *n_kv_shape)
    conv_s = z_s.reshape(ns, dseq, CONV_CH)[:, dseq - (CONV_K - 1):, :][None]
    return (y_p.reshape(nb, seq, D_MODEL), y_s.reshape(ns, dseq, D_MODEL),
            k_win_p, v_win_p, conv_p, k_win_s, v_win_s, conv_s)
```

```python
import functools

import numpy as np
import jax
import jax.numpy as jnp
from jax import lax
from jax.experimental import pallas as pl
from jax.experimental.pallas import tpu as pltpu

D_MODEL = 1024
HEAD_DIM = 64
N_HEADS = 8
N_KV = 2
GROUP = N_HEADS // N_KV
Q_W = N_HEADS * HEAD_DIM
KV_W = N_KV * HEAD_DIM
WINDOW = 128
ROT_DIM = HEAD_DIM // 4
ROPE_THETA = 500000.0
ATTN_SCALE = HEAD_DIM ** -0.5
CONV_CH = D_MODEL // 2
CONV_K = 3
N_EXPERTS = 32
TOP_K = 4
SWIGLU_ALPHA = 1.702
SWIGLU_LIMIT = 7.0
MOE_BLOCK = 128
NORM_EPS = 1e-5
QK_EPS = 1e-6
NEG_INF = -1e30
PAST_LEN = 16384

LANES = 128
SUBLANES = 8
VMEM_LIMIT = 56 * 1024 * 1024

_OFF_Q = 0
_OFF_K = _OFF_Q + Q_W
_OFF_V = _OFF_K + KV_W
_OFF_GB = _OFF_V + KV_W
_OFF_GC = _OFF_GB + CONV_CH
_OFF_XC = _OFF_GC + CONV_CH
_OFF_BA = _OFF_XC + CONV_CH
_OFF_BC = _OFF_BA + D_MODEL
IN_COLS = _OFF_BC + D_MODEL

BF16 = jnp.bfloat16
F32 = jnp.float32
_NT = (((1,), (1,)), ((), ()))


def _dot(a, b):
    return jnp.dot(a, b, preferred_element_type=F32)


def _dot_nt(a, b):
    return lax.dot_general(a, b, _NT, preferred_element_type=F32)


def _sigmoid(x):
    return 1.0 / (1.0 + jnp.exp(-x))


def _split_bf16(x):
    hi = x.astype(BF16)
    lo = (x - hi.astype(F32)).astype(BF16)
    return hi, lo


def _const_spec(shape):
    nd = len(shape)
    return pl.BlockSpec(shape, lambda *_: (0,) * nd, pipeline_mode=pl.Buffered(1))


def _adaln_body(c_ref, w_ref, b_ref, o_ref):
    c = c_ref[...]
    s = c * _sigmoid(c)
    o_ref[...] = _dot(s.astype(BF16), w_ref[...].astype(BF16)) + b_ref[...]


def _adaln(c, w_ada, b_ada):
    n = c.shape[0]
    cols = w_ada.shape[1]
    bn = 1536
    return pl.pallas_call(
        _adaln_body,
        grid=(cols // bn,),
        in_specs=[pl.BlockSpec((n, D_MODEL), lambda i: (0, 0)),
                  pl.BlockSpec((D_MODEL, bn), lambda i: (0, i)),
                  pl.BlockSpec((1, bn), lambda i: (0, i))],
        out_specs=pl.BlockSpec((n, bn), lambda i: (0, i)),
        out_shape=jax.ShapeDtypeStruct((n, cols), F32),
        compiler_params=pltpu.CompilerParams(vmem_limit_bytes=VMEM_LIMIT),
        name="adaln",
    )(c, w_ada, b_ada.reshape(1, cols))


def _modulated_norm(x, g, shift, scale):
    y = x * lax.rsqrt(jnp.mean(x * x, axis=-1, keepdims=True) + NORM_EPS)
    return (y * g) * (1.0 + scale) + shift


def _head_norm(t, bd, g):
    w = t.shape[1]
    sq_hi, sq_lo = _split_bf16(t * t)
    blk = bd[:w, :w]
    ms = (_dot(sq_hi, blk) + _dot(sq_lo, blk)) * (1.0 / HEAD_DIM)
    return t * lax.rsqrt(ms + QK_EPS) * g[:, :w]


def _rope(y, rope):
    c = rope[:, 0:LANES]
    s_next = rope[:, LANES:2 * LANES]
    s_prev = rope[:, 2 * LANES:3 * LANES]
    half = ROT_DIM // 2
    return y * c + pltpu.roll(y, LANES - half, 1) * s_next + pltpu.roll(y, half, 1) * s_prev


def _pair_expand(t):
    lane = lax.broadcasted_iota(jnp.int32, t.shape, 1)
    lo = lane < HEAD_DIM
    r = pltpu.roll(t, HEAD_DIM, 1)
    zero = jnp.zeros_like(t)
    a0 = jnp.where(lo, t, zero).astype(BF16)
    b0 = jnp.where(lo, zero, r).astype(BF16)
    a1 = jnp.where(lo, r, zero).astype(BF16)
    b1 = jnp.where(lo, zero, t).astype(BF16)
    return ((a0, b0), (a1, b1))


def _softmax_pv(s_list, v_list, sink):
    m = jnp.full((s_list[0].shape[0], 1), sink, F32)
    for s in s_list:
        m = jnp.maximum(m, jnp.max(s, axis=-1, keepdims=True))
    den = jnp.exp(sink - m)
    acc = None
    for s, v in zip(s_list, v_list):
        e = jnp.exp(s - m)
        den = den + jnp.sum(e, axis=-1, keepdims=True)
        pv = _dot(e.astype(BF16), v)
        acc = pv if acc is None else acc + pv
    return acc * (1.0 / den)


def _route(h2, wr_hi, wr_lo, br):
    hi, lo = _split_bf16(h2)
    logits = _dot_nt(wr_hi, hi) + _dot_nt(wr_hi, lo) + _dot_nt(wr_lo, hi) + br
    eid = lax.broadcasted_iota(jnp.int32, logits.shape, 0).astype(F32)
    vals, idxs = [], []
    for _ in range(TOP_K):
        m = jnp.max(logits, axis=0, keepdims=True)
        idx = jnp.min(jnp.where(logits == m, eid, float(N_EXPERTS)), axis=0, keepdims=True)
        vals.append(m)
        idxs.append(idx)
        logits = jnp.where(eid == idx, -jnp.inf, logits)
    ex = [jnp.exp(v - vals[0]) for v in vals]
    tot = ex[0] + ex[1] + ex[2] + ex[3]
    inv = 1.0 / tot
    gates = [e * inv for e in ex]
    rowid = lax.broadcasted_iota(jnp.int32, (SUBLANES, h2.shape[0]), 0)
    topi = jnp.zeros((SUBLANES, h2.shape[0]), F32)
    topg = jnp.zeros((SUBLANES, h2.shape[0]), F32)
    for k in range(TOP_K):
        topi = jnp.where(rowid == k, idxs[k], topi)
        topg = jnp.where(rowid == k, gates[k], topg)
    return topi.astype(jnp.int32), topg


_QKV_COLS = (_OFF_Q, Q_W + 2 * KV_W)
_LATE_COLS = ((_OFF_GB, 2 * CONV_CH), (_OFF_XC, CONV_CH), (_OFF_BA, D_MODEL), (_OFF_BC, D_MODEL))


def _in_proj(proj, h_bf, w_in_ref, group):
    off, width = group
    proj[off] = _dot(h_bf, w_in_ref[:, off:off + width])


def _cols(proj, off, width):
    for start, piece in proj.items():
        if start <= off and off + width <= start + piece.shape[1]:
            return piece[:, off - start:off - start + width]
    raise KeyError(off)


def _merge_and_route(x, o_bf, y_conv, proj, mod, n2g_ref, w_ao_ref, w_co_ref, w_o_ref,
                     wr_hi_ref, wr_lo_ref, br_ref):
    sh1, sc1, g1, sh2, sc2, g2 = mod
    a = _dot(o_bf, w_ao_ref[...])
    c = _dot(y_conv.astype(BF16), w_co_ref[...])
    m = _sigmoid(_cols(proj, _OFF_BA, D_MODEL)) * a + _sigmoid(_cols(proj, _OFF_BC, D_MODEL)) * c
    x1 = x + g1 * _dot(m.astype(BF16), w_o_ref[...])
    h2 = _modulated_norm(x1, n2g_ref[...], sh2, sc2)
    topi, topg = _route(h2, wr_hi_ref[...], wr_lo_ref[...], br_ref[...])
    return x1, h2, topi, topg


def _prompt_body(sinks_ref, x_ref, mod_ref, rope_ref, n1g_ref, n2g_ref, w_in_ref, bd_ref, qg_ref, kg_ref,
                 w_ao_ref, cw_ref, cb_ref, w_co_ref, w_o_ref, wr_hi_ref, wr_lo_ref, br_ref,
                 x1_ref, h2_ref, topi_ref, topg_ref, kwin_ref, vwin_ref, zwin_ref,
                 kbuf, vbuf, zbuf, *, tb):
    j = pl.program_id(1)
    nsub = tb // WINDOW

    @pl.when(j == 0)
    def _():
        kbuf[0:WINDOW, :] = jnp.zeros((WINDOW, KV_W), F32)
        vbuf[0:WINDOW, :] = jnp.zeros((WINDOW, KV_W), F32)
        zbuf[0:SUBLANES, :] = jnp.zeros((SUBLANES, CONV_CH), F32)

    x = x_ref[0]
    mod = tuple(mod_ref[0, i:i + 1, :] for i in range(6))
    h_bf = _modulated_norm(x, n1g_ref[...], mod[0], mod[1]).astype(BF16)
    proj = {}
    _in_proj(proj, h_bf, w_in_ref, _QKV_COLS)
    _in_proj(proj, h_bf, w_in_ref, _LATE_COLS[0])
    rope = rope_ref[...]
    bd = bd_ref[...]

    k = _rope(_head_norm(_cols(proj, _OFF_K, KV_W), bd, kg_ref[...]), rope)
    v = _cols(proj, _OFF_V, KV_W)
    kbuf[WINDOW:WINDOW + tb, :] = k
    vbuf[WINDOW:WINDOW + tb, :] = v
    kwin_ref[0] = k[tb - WINDOW:tb]
    vwin_ref[0] = v[tb - WINDOW:tb]
    k_exp = _pair_expand(kbuf[...])
    v_exp = _pair_expand(vbuf[...])

    _in_proj(proj, h_bf, w_in_ref, _LATE_COLS[1])
    qn = _head_norm(_cols(proj, _OFF_Q, Q_W), bd, qg_ref[...])
    q_tiles = []
    for t in range(Q_W // LANES):
        qt = _rope(qn[:, t * LANES:(t + 1) * LANES], rope)
        q_tiles.append((qt * ATTN_SCALE).astype(BF16))

    row = lax.broadcasted_iota(jnp.int32, (WINDOW, 2 * WINDOW), 0)
    col = lax.broadcasted_iota(jnp.int32, (WINDOW, 2 * WINDOW), 1)
    band = (col > row) & (col <= row + WINDOW)
    first_key = jnp.where(j > 0, 0, WINDOW)
    band_first = band & (col >= first_key)
    o_rows = []
    for i in range(nsub):
        for group in _LATE_COLS[2:][i::nsub]:
            _in_proj(proj, h_bf, w_in_ref, group)
        mask = band_first if i == 0 else band
        keys = slice(i * WINDOW, (i + 2) * WINDOW)
        o_tiles = []
        for t in range(Q_W // LANES):
            c = t // 2
            qt = q_tiles[t][i * WINDOW:(i + 1) * WINDOW]
            o_t = None
            for half in range(2):
                s = jnp.where(mask, _dot_nt(qt, k_exp[c][half][keys]), NEG_INF)
                sink = sinks_ref[c * GROUP + (t % 2) * 2 + half]
                part = _softmax_pv([s], [v_exp[c][half][keys]], sink)
                o_t = part if o_t is None else o_t + part
            o_tiles.append(o_t.astype(BF16))
        o_rows.append(jnp.concatenate(o_tiles, axis=1))
    o_bf = jnp.concatenate(o_rows, axis=0) if nsub > 1 else o_rows[0]
    kbuf[0:WINDOW, :] = kbuf[tb:tb + WINDOW, :]
    vbuf[0:WINDOW, :] = vbuf[tb:tb + WINDOW, :]

    z = _cols(proj, _OFF_GC, CONV_CH) * _cols(proj, _OFF_XC, CONV_CH)
    zbuf[SUBLANES:SUBLANES + tb, :] = z
    zwin_ref[0] = z[tb - SUBLANES:tb]
    z1 = zbuf[SUBLANES - 1:SUBLANES - 1 + tb, :]
    z2 = zbuf[SUBLANES - 2:SUBLANES - 2 + tb, :]
    conv = cb_ref[...] + cw_ref[0:1, :] * z2 + cw_ref[1:2, :] * z1 + cw_ref[2:3, :] * z
    y_conv = _cols(proj, _OFF_GB, CONV_CH) * conv
    zbuf[0:SUBLANES, :] = zbuf[tb:tb + SUBLANES, :]

    x1, h2, topi, topg = _merge_and_route(x, o_bf, y_conv, proj, mod, n2g_ref, w_ao_ref, w_co_ref,
                                          w_o_ref, wr_hi_ref, wr_lo_ref, br_ref)
    x1_ref[0] = x1
    h2_ref[0] = h2.astype(BF16)
    topi_ref[0] = topi
    topg_ref[0] = topg


def _prompt_mixer(x, mod, rope, sinks, weights, tb):
    nb, seq, _ = x.shape
    (n1g, n2g, w_in, bd, qg, kg, w_ao, cw, cb, w_co, w_o, wr_hi, wr_lo, br) = weights
    consts = (n1g, n2g, w_in, bd, qg, kg, w_ao, cw, cb, w_co, w_o, wr_hi, wr_lo, br)
    in_specs = [
        pl.BlockSpec((1, tb, D_MODEL), lambda b, j, s: (b, j, 0)),
        pl.BlockSpec((1, 6, D_MODEL), lambda b, j, s: (b, 0, 0)),
        pl.BlockSpec((tb, 3 * LANES), lambda b, j, s: (j, 0)),
    ] + [_const_spec(w.shape) for w in consts]
    out_specs = [
        pl.BlockSpec((1, tb, D_MODEL), lambda b, j, s: (b, j, 0)),
        pl.BlockSpec((1, tb, D_MODEL), lambda b, j, s: (b, j, 0)),
        pl.BlockSpec((1, SUBLANES, tb), lambda b, j, s: (b, 0, j)),
        pl.BlockSpec((1, SUBLANES, tb), lambda b, j, s: (b, 0, j)),
        pl.BlockSpec((1, WINDOW, KV_W), lambda b, j, s: (b, 0, 0)),
        pl.BlockSpec((1, WINDOW, KV_W), lambda b, j, s: (b, 0, 0)),
        pl.BlockSpec((1, SUBLANES, CONV_CH), lambda b, j, s: (b, 0, 0)),
    ]
    out_shape = [
        jax.ShapeDtypeStruct((nb, seq, D_MODEL), F32),
        jax.ShapeDtypeStruct((nb, seq, D_MODEL), BF16),
        jax.ShapeDtypeStruct((nb, SUBLANES, seq), jnp.int32),
        jax.ShapeDtypeStruct((nb, SUBLANES, seq), F32),
        jax.ShapeDtypeStruct((nb, WINDOW, KV_W), F32),
        jax.ShapeDtypeStruct((nb, WINDOW, KV_W), F32),
        jax.ShapeDtypeStruct((nb, SUBLANES, CONV_CH), F32),
    ]
    return pl.pallas_call(
        functools.partial(_prompt_body, tb=tb),
        grid_spec=pltpu.PrefetchScalarGridSpec(
            num_scalar_prefetch=1, grid=(nb, seq // tb),
            in_specs=in_specs, out_specs=out_specs,
            scratch_shapes=[pltpu.VMEM((tb + WINDOW, KV_W), F32),
                            pltpu.VMEM((tb + WINDOW, KV_W), F32),
                            pltpu.VMEM((tb + SUBLANES, CONV_CH), F32)]),
        out_shape=out_shape,
        compiler_params=pltpu.CompilerParams(
            dimension_semantics=("arbitrary", "arbitrary"), vmem_limit_bytes=VMEM_LIMIT),
        name="prompt_mixer",
    )(sinks, x, mod, rope, *consts)


def _sample_body(sinks_ref, x_ref, mod_ref, rope_ref, ck_ref, cv_ref, st_ref,
                 n1g_ref, n2g_ref, w_in_ref, bd_ref, qg_ref, kg_ref,
                 w_ao_ref, cw_ref, cb_ref, w_co_ref, w_o_ref, wr_hi_ref, wr_lo_ref, br_ref,
                 x1_ref, h2_ref, topi_ref, topg_ref, knew_ref, vnew_ref, z_ref, *, nseq, dseq):
    rows = nseq * dseq
    wbuf = ck_ref.shape[1]
    x = x_ref[...]
    mod = tuple(mod_ref[i] for i in range(6))
    h_bf = _modulated_norm(x, n1g_ref[...], mod[0], mod[1]).astype(BF16)
    proj = {}
    for group in (_QKV_COLS,) + _LATE_COLS:
        _in_proj(proj, h_bf, w_in_ref, group)
    rope = rope_ref[...]
    bd = bd_ref[...]

    k = _rope(_head_norm(_cols(proj, _OFF_K, KV_W), bd, kg_ref[...]), rope)
    v = _cols(proj, _OFF_V, KV_W)
    for win_ref, cache_ref, new in ((knew_ref, ck_ref, k), (vnew_ref, cv_ref, v)):
        win_ref[:, 0:wbuf - dseq, :] = cache_ref[:, dseq:, :]
        win_ref[:, wbuf - dseq:, :] = new.reshape(nseq, dseq, KV_W)
    kn_exp = _pair_expand(k)
    vn_exp = _pair_expand(v)
    kc_exp = _pair_expand(ck_ref[...].reshape(nseq * wbuf, KV_W))
    vc_exp = _pair_expand(cv_ref[...].reshape(nseq * wbuf, KV_W))

    qrow = lax.broadcasted_iota(jnp.int32, (rows, nseq * wbuf), 0)
    ccol = lax.broadcasted_iota(jnp.int32, (rows, nseq * wbuf), 1)
    mask_c = ((ccol // wbuf) == (qrow // dseq)) & ((ccol % wbuf) > (qrow % dseq) + (wbuf - WINDOW))
    qrow_n = lax.broadcasted_iota(jnp.int32, (rows, rows), 0)
    ncol = lax.broadcasted_iota(jnp.int32, (rows, rows), 1)
    mask_n = ((ncol // dseq) == (qrow_n // dseq)) & ((ncol % dseq) <= (qrow_n % dseq))

    qn = _head_norm(_cols(proj, _OFF_Q, Q_W), bd, qg_ref[...])
    o_tiles = []
    for t in range(Q_W // LANES):
        c = t // 2
        qt = (_rope(qn[:, t * LANES:(t + 1) * LANES], rope) * ATTN_SCALE).astype(BF16)
        o_t = None
        for half in range(2):
            s_c = jnp.where(mask_c, _dot_nt(qt, kc_exp[c][half]), NEG_INF)
            s_n = jnp.where(mask_n, _dot_nt(qt, kn_exp[c][half]), NEG_INF)
            sink = sinks_ref[c * GROUP + (t % 2) * 2 + half]
            part = _softmax_pv([s_c, s_n], [vc_exp[c][half], vn_exp[c][half]], sink)
            o_t = part if o_t is None else o_t + part
        o_tiles.append(o_t.astype(BF16))
    o_bf = jnp.concatenate(o_tiles, axis=1)

    z = _cols(proj, _OFF_GC, CONV_CH) * _cols(proj, _OFF_XC, CONV_CH)
    z_ref[...] = z
    r = lax.broadcasted_iota(jnp.int32, z.shape, 0) % dseq
    st0 = st_ref[0]
    st1 = st_ref[1]
    z1 = jnp.where(r == 0, st1, pltpu.roll(z, 1, 0))
    z2 = jnp.where(r == 0, st0, jnp.where(r == 1, st1, pltpu.roll(z, 2, 0)))
    conv = cb_ref[...] + cw_ref[0:1, :] * z2 + cw_ref[1:2, :] * z1 + cw_ref[2:3, :] * z
    y_conv = _cols(proj, _OFF_GB, CONV_CH) * conv

    x1, h2, topi, topg = _merge_and_route(x, o_bf, y_conv, proj, mod, n2g_ref, w_ao_ref, w_co_ref,
                                          w_o_ref, wr_hi_ref, wr_lo_ref, br_ref)
    x1_ref[...] = x1
    h2_ref[...] = h2.astype(BF16)
    topi_ref[...] = topi
    topg_ref[...] = topg


def _sample_mixer(x, mod, rope, cache_k, cache_v, state, sinks, weights, nseq):
    ntok = x.shape[0]
    nall, wbuf, _ = cache_k.shape
    dseq = ntok // nall
    rows = nseq * dseq
    consts = weights
    in_specs = [
        pl.BlockSpec((rows, D_MODEL), lambda i, s: (i, 0)),
        pl.BlockSpec((6, rows, D_MODEL), lambda i, s: (0, i, 0)),
        pl.BlockSpec((rows, 3 * LANES), lambda i, s: (i, 0)),
        pl.BlockSpec((nseq, wbuf, KV_W), lambda i, s: (i, 0, 0)),
        pl.BlockSpec((nseq, wbuf, KV_W), lambda i, s: (i, 0, 0)),
        pl.BlockSpec((2, rows, CONV_CH), lambda i, s: (0, i, 0)),
    ] + [_const_spec(w.shape) for w in consts]
    out_specs = [
        pl.BlockSpec((rows, D_MODEL), lambda i, s: (i, 0)),
        pl.BlockSpec((rows, D_MODEL), lambda i, s: (i, 0)),
        pl.BlockSpec((SUBLANES, rows), lambda i, s: (0, i)),
        pl.BlockSpec((SUBLANES, rows), lambda i, s: (0, i)),
        pl.BlockSpec((nseq, wbuf, KV_W), lambda i, s: (i, 0, 0)),
        pl.BlockSpec((nseq, wbuf, KV_W), lambda i, s: (i, 0, 0)),
        pl.BlockSpec((rows, CONV_CH), lambda i, s: (i, 0)),
    ]
    out_shape = [
        jax.ShapeDtypeStruct((ntok, D_MODEL), F32),
        jax.ShapeDtypeStruct((ntok, D_MODEL), BF16),
        jax.ShapeDtypeStruct((SUBLANES, ntok), jnp.int32),
        jax.ShapeDtypeStruct((SUBLANES, ntok), F32),
        jax.ShapeDtypeStruct((nall, wbuf, KV_W), F32),
        jax.ShapeDtypeStruct((nall, wbuf, KV_W), F32),
        jax.ShapeDtypeStruct((ntok, CONV_CH), F32),
    ]
    return pl.pallas_call(
        functools.partial(_sample_body, nseq=nseq, dseq=dseq),
        grid_spec=pltpu.PrefetchScalarGridSpec(
            num_scalar_prefetch=1, grid=(ntok // rows,),
            in_specs=in_specs, out_specs=out_specs),
        out_shape=out_shape,
        compiler_params=pltpu.CompilerParams(
            dimension_semantics=("arbitrary",), vmem_limit_bytes=VMEM_LIMIT),
        name="sample_mixer",
    )(sinks, x, mod, rope, cache_k, cache_v, state, *consts)


SORT_TOKENS = 512
RUN_ALIGN = SUBLANES
EXPERT_TILE = 1024
HALF_D = D_MODEL // 2
LOCAL_ROWS = SORT_TOKENS * TOP_K + N_EXPERTS * RUN_ALIGN
UNSORT_CHUNK = 256
assert LOCAL_ROWS % UNSORT_CHUNK == 0
_HI_MASK = 0xFFFF0000


def _pack_halves(x):
    lo = pltpu.bitcast(x[:, :HALF_D], jnp.uint32)
    hi = pltpu.bitcast(x[:, HALF_D:], jnp.uint32)
    return hi | (lo >> 16)


def _unpack_halves(w):
    lo = pltpu.bitcast(w << 16, F32).astype(BF16)
    hi = pltpu.bitcast(w & jnp.uint32(_HI_MASK), F32).astype(BF16)
    return lo, hi


def _local_positions(topi, lstart_col, tri):
    ntok = topi.shape[1]
    eid = lax.broadcasted_iota(jnp.int32, (N_EXPERTS, ntok), 0)
    hits = [eid == topi[k:k + 1, :] for k in range(TOP_K)]
    chosen = jnp.zeros((N_EXPERTS, ntok), F32)
    for h in hits:
        chosen = jnp.where(h, 1.0, chosen)
    base = _dot(chosen.astype(BF16), tri) + lstart_col
    return [jnp.sum(jnp.where(h, base, 0.0), axis=0, keepdims=True) for h in hits]


def _run_copy(loc, glob, sem, lofs, gofs, n, to_global):
    lo = loc.at[pl.ds(pl.multiple_of(lofs, RUN_ALIGN), n)]
    gl = glob.at[pl.ds(pl.multiple_of(gofs, RUN_ALIGN), n)]
    return pltpu.make_async_copy(lo, gl, sem) if to_global else pltpu.make_async_copy(gl, lo, sem)


def _start_runs(meta, blk, loc, glob, sem, to_global):
    lstart_ref, goff_ref, cpad_ref = meta
    for e in range(N_EXPERTS):
        n = pl.multiple_of(cpad_ref[blk * N_EXPERTS + e], RUN_ALIGN)
        _run_copy(loc, glob, sem, lstart_ref[blk * N_EXPERTS + e], goff_ref[blk * N_EXPERTS + e], n,
                  to_global).start()


def _wait_runs(nrows, loc, glob, sem, to_global):
    _run_copy(loc, glob, sem, 0, 0, pl.multiple_of(nrows, RUN_ALIGN), to_global).wait()


def _sort_body(lstart_ref, goff_ref, cpad_ref, ltot_ref, fstart_ref, flen_ref, nact_ref,
               hp_ref, hs_ref, topi_ref, lcol_ref, tri_ref, xs_hbm, pos_ref, buf, zbuf, sems, fsem, *, nblk_p, nblk):
    i = pl.program_id(0)
    slot = i % 2
    meta = (lstart_ref, goff_ref, cpad_ref)

    @pl.when(i == 0)
    def _():
        zbuf[...] = jnp.zeros(zbuf.shape, jnp.uint32)

        def fill(e, tot):
            n = pl.multiple_of(flen_ref[e], RUN_ALIGN)

            @pl.when(n > 0)
            def _():
                _run_copy(zbuf, xs_hbm, fsem, 0, fstart_ref[e], n, True).start()
            return tot + n
        total = lax.fori_loop(0, N_EXPERTS, fill, 0)

        @pl.when(total > 0)
        def _():
            _wait_runs(total, zbuf, xs_hbm, fsem, True)

        def fill_tile(t, carry):
            cp = _run_copy(zbuf, xs_hbm, fsem, 0, t * EXPERT_TILE, EXPERT_TILE, True)
            cp.start()
            cp.wait()
            return carry
        lax.fori_loop(nact_ref[0], xs_hbm.shape[0] // EXPERT_TILE, fill_tile, 0)

    def run(h_ref):
        pos = [p.astype(jnp.int32) for p in _local_positions(topi_ref[...], lcol_ref[...], tri_ref[...])]
        rowid = lax.broadcasted_iota(jnp.int32, (LOCAL_ROWS, SORT_TOKENS), 0)
        onehot = jnp.zeros((LOCAL_ROWS, SORT_TOKENS), F32)
        for p in pos:
            onehot = jnp.where(rowid == p, 1.0, onehot)
        buf[slot] = _pack_halves(_dot(onehot.astype(BF16), h_ref[...]))
        krow = lax.broadcasted_iota(jnp.int32, (SUBLANES, SORT_TOKENS), 0)
        pos_rows = jnp.zeros((SUBLANES, SORT_TOKENS), jnp.int32)
        for k in range(TOP_K):
            pos_rows = jnp.where(krow == k, pos[k], pos_rows)
        pos_ref[...] = pos_rows

    @pl.when(i < nblk_p)
    def _():
        run(hp_ref)

    @pl.when(i >= nblk_p)
    def _():
        run(hs_ref)

    _start_runs(meta, i, buf.at[slot], xs_hbm, sems.at[slot], True)

    @pl.when(i > 0)
    def _():
        _wait_runs(ltot_ref[jnp.maximum(i - 1, 0)], buf.at[1 - slot], xs_hbm, sems.at[1 - slot], True)

    @pl.when(i == nblk - 1)
    def _():
        _wait_runs(ltot_ref[i], buf.at[slot], xs_hbm, sems.at[slot], True)


def _sort_tokens(plan, h_p, h_s, topi_all, tri, n_slots):
    nblk_p = h_p.shape[0] // SORT_TOKENS
    nblk = nblk_p + h_s.shape[0] // SORT_TOKENS
    return pl.pallas_call(
        functools.partial(_sort_body, nblk_p=nblk_p, nblk=nblk),
        grid_spec=pltpu.PrefetchScalarGridSpec(
            num_scalar_prefetch=7, grid=(nblk,),
            in_specs=[pl.BlockSpec((SORT_TOKENS, D_MODEL), lambda i, *_: (jnp.minimum(i, nblk_p - 1), 0)),
                      pl.BlockSpec((SORT_TOKENS, D_MODEL), lambda i, *_: (jnp.maximum(i - nblk_p, 0), 0)),
                      pl.BlockSpec((SUBLANES, SORT_TOKENS), lambda i, *_: (0, i)),
                      pl.BlockSpec((None, N_EXPERTS, 1), lambda i, *_: (i, 0, 0)),
                      _const_spec(tri.shape)],
            out_specs=[pl.BlockSpec(memory_space=pl.ANY),
                       pl.BlockSpec((SUBLANES, SORT_TOKENS), lambda i, *_: (0, i))],
            scratch_shapes=[pltpu.VMEM((2, LOCAL_ROWS, HALF_D), jnp.uint32),
                            pltpu.VMEM((EXPERT_TILE, HALF_D), jnp.uint32),
                            pltpu.SemaphoreType.DMA((2,)),
                            pltpu.SemaphoreType.DMA]),
        out_shape=[jax.ShapeDtypeStruct((n_slots, HALF_D), jnp.uint32),
                   jax.ShapeDtypeStruct(topi_all.shape, jnp.int32)],
        compiler_params=pltpu.CompilerParams(
            dimension_semantics=("arbitrary",), vmem_limit_bytes=VMEM_LIMIT),
        name="moe_sort",
    )(plan["lstart"], plan["goff"], plan["cpad"], plan["ltot"], plan["fill_start"], plan["fill_len"], plan["nact"],
      h_p, h_s, topi_all, plan["lstart_col"], tri)


def _experts_body(tile_exp_ref, nact_ref, rows_ref, xs_ref, wg_hbm, bg_ref, wu_hbm, bu_ref, wd_hbm, bd_ref,
                  ys_ref, w_f32, w_bf, sems):
    t = pl.program_id(0)
    nact = nact_ref[0]
    w_hbm = (wg_hbm, wu_hbm, wd_hbm)

    def weight_copies(e, slot):
        return [pltpu.make_async_copy(w.at[e], w_f32.at[slot, j], sems.at[slot, j]) for j, w in enumerate(w_hbm)]

    @pl.when(t < nact)
    def _():
        e = tile_exp_ref[t]
        slot = e % 2

        @pl.when(t == 0)
        def _():
            for cp in weight_copies(e, slot):
                cp.start()

        @pl.when((t == 0) | (e != tile_exp_ref[jnp.maximum(t - 1, 0)]))
        def _():
            @pl.when(e + 1 < N_EXPERTS)
            def _():
                for cp in weight_copies(e + 1, 1 - slot):
                    cp.start()
            for j, cp in enumerate(weight_copies(e, slot)):
                cp.wait()
                w_bf[j] = w_f32[slot, j].astype(BF16)

        def swiglu(rows):
            x = jnp.concatenate(_unpack_halves(xs_ref[0:rows, :]), axis=1)
            g = _dot(x, w_bf[0]) + bg_ref[0]
            u = _dot(x, w_bf[1]) + bu_ref[0]
            g = jnp.minimum(g, SWIGLU_LIMIT)
            u = jnp.clip(u, -SWIGLU_LIMIT, SWIGLU_LIMIT)
            a = g * _sigmoid(SWIGLU_ALPHA * g) * (u + 1.0)
            out = _dot(a.astype(BF16), w_bf[2]) + bd_ref[0]
            ys_ref[0:rows, :] = _pack_halves(out.astype(BF16).astype(F32))

        half = EXPERT_TILE // 2
        used = rows_ref[t]

        @pl.when(used > half)
        def _():
            swiglu(EXPERT_TILE)

        @pl.when(used <= half)
        def _():
            swiglu(half)
            ys_ref[half:, :] = jnp.zeros((EXPERT_TILE - half, HALF_D), jnp.uint32)

    @pl.when(t >= nact)
    def _():
        ys_ref[...] = jnp.zeros(ys_ref.shape, jnp.uint32)


def _experts(tile_exp, nact, tile_rows, xs, wg, bg, wu, bu, wd, bd):
    n_tiles = tile_exp.shape[0]

    def active(t, na):
        return jnp.minimum(t, na[0] - 1)
    wspec = pl.BlockSpec(memory_space=pl.ANY)
    bspec = pl.BlockSpec((1, 1, D_MODEL), lambda t, te, na, tr: (te[active(t, na)], 0, 0))
    xspec = pl.BlockSpec((EXPERT_TILE, HALF_D), lambda t, te, na, tr: (active(t, na), 0))
    return pl.pallas_call(
        _experts_body,
        grid_spec=pltpu.PrefetchScalarGridSpec(
            num_scalar_prefetch=3, grid=(n_tiles,),
            in_specs=[xspec, wspec, bspec, wspec, bspec, wspec, bspec],
            out_specs=pl.BlockSpec((EXPERT_TILE, HALF_D), lambda t, te, na, tr: (t, 0)),
            scratch_shapes=[pltpu.VMEM((2, 3, D_MODEL, D_MODEL), F32),
                            pltpu.VMEM((3, D_MODEL, D_MODEL), BF16),
                            pltpu.SemaphoreType.DMA((2, 3))]),
        out_shape=jax.ShapeDtypeStruct(xs.shape, jnp.uint32),
        compiler_params=pltpu.CompilerParams(
            dimension_semantics=("arbitrary",), vmem_limit_bytes=VMEM_LIMIT),
        name="experts",
    )(tile_exp, nact, tile_rows, xs, wg, bg, wu, bu, wd, bd)


def _unsort_body(lstart_ref, goff_ref, cpad_ref, ltot_ref,
                 ys_hbm, x1_ref, g2_ref, pos_ref, gate_ref, y_ref, buf, sems, *, blk0, nblk):
    i = pl.program_id(0)
    b = blk0 + i
    slot = i % 2
    meta = (lstart_ref, goff_ref, cpad_ref)

    @pl.when(i == 0)
    def _():
        buf[...] = jnp.zeros(buf.shape, jnp.uint32)
        _start_runs(meta, b, buf.at[0], ys_hbm, sems.at[0], False)

    @pl.when(i + 1 < nblk)
    def _():
        _start_runs(meta, b + 1, buf.at[1 - slot], ys_hbm, sems.at[1 - slot], False)

    _wait_runs(ltot_ref[b], buf.at[slot], ys_hbm, sems.at[slot], False)
    pos_cols = pos_ref[...]
    gate_cols = gate_ref[...]
    acc_lo = acc_hi = None
    for c0 in range(0, LOCAL_ROWS, UNSORT_CHUNK):
        colid = lax.broadcasted_iota(jnp.int32, (SORT_TOKENS, UNSORT_CHUNK), 1) + c0
        weights = jnp.zeros((SORT_TOKENS, UNSORT_CHUNK), F32)
        for k in range(TOP_K):
            weights = jnp.where(colid == pos_cols[:, k:k + 1], gate_cols[:, k:k + 1], weights)
        weights = weights.astype(BF16)
        y_lo, y_hi = _unpack_halves(buf[slot, c0:c0 + UNSORT_CHUNK, :])
        acc_lo = _dot(weights, y_lo) if acc_lo is None else acc_lo + _dot(weights, y_lo)
        acc_hi = _dot(weights, y_hi) if acc_hi is None else acc_hi + _dot(weights, y_hi)
    g2 = g2_ref[...]
    y_ref[:, :HALF_D] = x1_ref[:, :HALF_D] + g2[:, :HALF_D] * acc_lo
    y_ref[:, HALF_D:] = x1_ref[:, HALF_D:] + g2[:, HALF_D:] * acc_hi


def _unsort_tokens(plan, ys, x1, g2, g2_spec, pos_cols, gate_cols, blk0):
    nblk = x1.shape[0] // SORT_TOKENS
    return pl.pallas_call(
        functools.partial(_unsort_body, blk0=blk0, nblk=nblk),
        grid_spec=pltpu.PrefetchScalarGridSpec(
            num_scalar_prefetch=4, grid=(nblk,),
            in_specs=[pl.BlockSpec(memory_space=pl.ANY),
                      pl.BlockSpec((SORT_TOKENS, D_MODEL), lambda i, *_: (i, 0)),
                      g2_spec,
                      pl.BlockSpec((SORT_TOKENS, SUBLANES), lambda i, *_: (blk0 + i, 0)),
                      pl.BlockSpec((SORT_TOKENS, SUBLANES), lambda i, *_: (blk0 + i, 0))],
            out_specs=pl.BlockSpec((SORT_TOKENS, D_MODEL), lambda i, *_: (i, 0)),
            scratch_shapes=[pltpu.VMEM((2, LOCAL_ROWS, HALF_D), jnp.uint32),
                            pltpu.SemaphoreType.DMA((2,))]),
        out_shape=jax.ShapeDtypeStruct(x1.shape, F32),
        compiler_params=pltpu.CompilerParams(
            dimension_semantics=("arbitrary",), vmem_limit_bytes=VMEM_LIMIT),
        name="moe_unsort",
    )(plan["lstart"], plan["goff"], plan["cpad"], plan["ltot"],
      ys, x1, g2, pos_cols, gate_cols)


def _rope_table(pos):
    half = ROT_DIM // 2
    inv_freq = ROPE_THETA ** (-jnp.arange(0, ROT_DIM, 2, dtype=F32) / ROT_DIM)
    ang = pos.astype(F32)[:, None] * inv_freq[None, :]
    cos, sin = jnp.cos(ang), jnp.sin(ang)
    n = pos.shape[0]
    rest = HEAD_DIM - ROT_DIM
    c = jnp.concatenate([cos, cos, jnp.ones((n, rest), F32)], axis=1)
    s_next = jnp.concatenate([-sin, jnp.zeros((n, half + rest), F32)], axis=1)
    s_prev = jnp.concatenate([jnp.zeros((n, half), F32), sin, jnp.zeros((n, rest), F32)], axis=1)
    rep = LANES // HEAD_DIM
    return jnp.concatenate([jnp.tile(c, (1, rep)), jnp.tile(s_next, (1, rep)), jnp.tile(s_prev, (1, rep))], axis=1)


def _num_expert_tiles(ntok):
    nblk = ntok // SORT_TOKENS
    worst = ntok * TOP_K + nblk * N_EXPERTS * RUN_ALIGN + N_EXPERTS * (EXPERT_TILE - 1)
    return -(-worst // EXPERT_TILE)


def _routing_plan(topi_all):
    ntok = topi_all.shape[1]
    nblk = ntok // SORT_TOKENS
    n_tiles = _num_expert_tiles(ntok)
    choice = topi_all[:TOP_K].reshape(TOP_K, nblk, SORT_TOKENS)
    experts = jnp.arange(N_EXPERTS, dtype=jnp.int32)
    cnt = (choice[:, :, :, None] == experts).astype(jnp.int32).sum(axis=(0, 2))
    cpad = jnp.maximum(-(-cnt // RUN_ALIGN) * RUN_ALIGN, RUN_ALIGN)
    lstart = jnp.cumsum(cpad, axis=1) - cpad
    tot = cpad.sum(axis=0)
    region = -(-tot // EXPERT_TILE) * EXPERT_TILE
    region_end = jnp.cumsum(region)
    gstart = region_end - region
    goff = gstart[None, :] + jnp.cumsum(cpad, axis=0) - cpad
    tile_start = jnp.arange(n_tiles, dtype=jnp.int32) * EXPERT_TILE
    tile_exp = jnp.minimum((tile_start[:, None] >= region_end[None, :]).astype(jnp.int32).sum(axis=1), N_EXPERTS - 1)
    tile_rows = jnp.clip((gstart + tot)[tile_exp] - tile_start, 0, EXPERT_TILE)
    i32 = lambda a: a.astype(jnp.int32)
    return dict(
        tile_rows=i32(tile_rows),
        lstart=i32(lstart).reshape(-1), goff=i32(goff).reshape(-1), cpad=i32(cpad).reshape(-1),
        ltot=i32(cpad.sum(axis=1)), fill_start=i32(gstart + tot), fill_len=i32(region - tot),
        lstart_col=lstart.astype(F32).reshape(nblk, N_EXPERTS, 1),
        tile_exp=i32(tile_exp), nact=i32(region_end[-1] // EXPERT_TILE).reshape(1))


def kernel(x_prompt, x_sample, cache_k_win, cache_v_win, state_conv, c_prompt, c_sample, w_ada, b_ada, norm1_g, norm2_g, w_in, q_norm_g, k_norm_g, attn_sinks, w_attn_out, conv_w, conv_b, w_conv_out, w_o, w_router, b_router, w_gate, b_gate, w_up, b_up, w_down, b_down):
    nb, seq, _ = x_prompt.shape
    ns, dseq, _ = x_sample.shape
    wbuf = cache_k_win.shape[2]
    assert w_ada.shape[0] == 1, "single-layer step"
    l = 0

    mod = _adaln(jnp.concatenate([c_prompt, c_sample], axis=0), w_ada[l], b_ada[l])
    mod_p = mod[:nb].reshape(nb, 6, D_MODEL)
    mod_s = jnp.repeat(mod[nb:].reshape(ns, 6, D_MODEL).transpose(1, 0, 2), dseq, axis=1)

    bd = jnp.asarray(np.kron(np.eye(N_HEADS), np.ones((HEAD_DIM, HEAD_DIM))), BF16)
    rep = LANES // HEAD_DIM
    wr_t = w_router[l].T
    wr_hi = wr_t.astype(BF16)
    wr_lo = (wr_t - wr_hi.astype(F32)).astype(BF16)
    weights = (norm1_g[l].reshape(1, D_MODEL), norm2_g[l].reshape(1, D_MODEL), w_in[l].astype(BF16), bd,
               jnp.tile(q_norm_g[l], N_HEADS).reshape(1, Q_W), jnp.tile(k_norm_g[l], N_KV).reshape(1, KV_W),
               w_attn_out[l].astype(BF16), conv_w[l], conv_b[l].reshape(1, CONV_CH), w_conv_out[l].astype(BF16),
               w_o[l].astype(BF16), wr_hi, wr_lo, b_router[l].reshape(N_EXPERTS, 1))
    sinks = attn_sinks[l]

    tb = 512
    (x1_p, h2_p, topi_p, topg_p, kwin_p, vwin_p, zwin_p) = _prompt_mixer(
        x_prompt, mod_p, _rope_table(jnp.arange(seq)), sinks, weights, tb)
    pos_s = jnp.tile(PAST_LEN + jnp.arange(dseq), ns)
    state_rows = jnp.repeat(state_conv[l].transpose(1, 0, 2), dseq, axis=1)
    (x1_s, h2_s, topi_s, topg_s, knew, vnew, z_s) = _sample_mixer(
        x_sample.reshape(ns * dseq, D_MODEL), mod_s, _rope_table(pos_s),
        cache_k_win[l].reshape(ns, wbuf, KV_W), cache_v_win[l].reshape(ns, wbuf, KV_W),
        state_rows, sinks, weights, 16)

    ntok_p = nb * seq
    ntok = ntok_p + ns * dseq
    topi_all = jnp.concatenate([topi_p.transpose(1, 0, 2).reshape(SUBLANES, ntok_p), topi_s], axis=1)
    topg_all = jnp.concatenate([topg_p.transpose(1, 0, 2).reshape(SUBLANES, ntok_p), topg_s], axis=1)
    plan = _routing_plan(topi_all)
    tri = jnp.asarray(np.triu(np.ones((SORT_TOKENS, SORT_TOKENS)), k=1), BF16)
    xs, pos_all = _sort_tokens(plan, h2_p.reshape(ntok_p, D_MODEL), h2_s, topi_all, tri,
                               _num_expert_tiles(ntok) * EXPERT_TILE)
    pos_cols, gate_cols = pos_all.T, topg_all.T
    ys = _experts(plan["tile_exp"], plan["nact"], plan["tile_rows"], xs,
                  w_gate[l], b_gate[l].reshape(N_EXPERTS, 1, D_MODEL),
                  w_up[l], b_up[l].reshape(N_EXPERTS, 1, D_MODEL),
                  w_down[l], b_down[l].reshape(N_EXPERTS, 1, D_MODEL))
    blocks_per_seq = seq // SORT_TOKENS
    y_p = _unsort_tokens(plan, ys, x1_p.reshape(ntok_p, D_MODEL), mod_p[:, 5:6, :],
                         pl.BlockSpec((None, 1, D_MODEL), lambda i, *_: (i // blocks_per_seq, 0, 0)),
                         pos_cols, gate_cols, 0)
    y_s = _unsort_tokens(plan, ys, x1_s, mod_s[5],
                         pl.BlockSpec((SORT_TOKENS, D_MODEL), lambda i, *_: (i, 0)),
                         pos_cols, gate_cols, ntok_p // SORT_TOKENS)

    n_kv_shape = (N_KV, HEAD_DIM)
    k_win_p = kwin_p.reshape(1, nb, WINDOW, *n_kv_shape)
    v_win_p = vwin_p.reshape(1, nb, WINDOW, *n_kv_shape)
    conv_p = zwin_p[:, SUBLANES - (CONV_K - 1):, :][None]
    k_win_s = knew.reshape(1, ns, wbuf, *n_kv_shape)
    v_win_s = vnew.reshape(1, ns, wbuf, *n_kv_shape)
    conv_s = z_s.reshape(ns, dseq, CONV_CH)[:, dseq - (CONV_K - 1):, :][None]
    return (y_p.reshape(nb, seq, D_MODEL), y_s.reshape(ns, dseq, D_MODEL),
            k_win_p, v_win_p, conv_p, k_win_s, v_win_s, conv_s)
```

```python
import functools

import numpy as np
import jax
import jax.numpy as jnp
from jax import lax
from jax.experimental import pallas as pl
from jax.experimental.pallas import tpu as pltpu

D_MODEL = 1024
HEAD_DIM = 64
N_HEADS = 8
N_KV = 2
GROUP = N_HEADS // N_KV
Q_W = N_HEADS * HEAD_DIM
KV_W = N_KV * HEAD_DIM
WINDOW = 128
ROT_DIM = HEAD_DIM // 4
ROPE_THETA = 500000.0
ATTN_SCALE = HEAD_DIM ** -0.5
CONV_CH = D_MODEL // 2
CONV_K = 3
N_EXPERTS = 32
TOP_K = 4
SWIGLU_ALPHA = 1.702
SWIGLU_LIMIT = 7.0
MOE_BLOCK = 128
NORM_EPS = 1e-5
QK_EPS = 1e-6
NEG_INF = -1e30
PAST_LEN = 16384

LANES = 128
SUBLANES = 8
VMEM_LIMIT = 56 * 1024 * 1024

_OFF_Q = 0
_OFF_K = _OFF_Q + Q_W
_OFF_V = _OFF_K + KV_W
_OFF_GB = _OFF_V + KV_W
_OFF_GC = _OFF_GB + CONV_CH
_OFF_XC = _OFF_GC + CONV_CH
_OFF_BA = _OFF_XC + CONV_CH
_OFF_BC = _OFF_BA + D_MODEL
IN_COLS = _OFF_BC + D_MODEL

BF16 = jnp.bfloat16
F32 = jnp.float32
_NT = (((1,), (1,)), ((), ()))


def _dot(a, b):
    return jnp.dot(a, b, preferred_element_type=F32)


def _dot_nt(a, b):
    return lax.dot_general(a, b, _NT, preferred_element_type=F32)


def _sigmoid(x):
    return 1.0 / (1.0 + jnp.exp(-x))


def _split_bf16(x):
    hi = x.astype(BF16)
    lo = (x - hi.astype(F32)).astype(BF16)
    return hi, lo


def _const_spec(shape):
    nd = len(shape)
    return pl.BlockSpec(shape, lambda *_: (0,) * nd, pipeline_mode=pl.Buffered(1))


def _adaln_body(c_ref, w_ref, b_ref, o_ref):
    c = c_ref[...]
    s = c * _sigmoid(c)
    o_ref[...] = _dot(s.astype(BF16), w_ref[...].astype(BF16)) + b_ref[...]


def _adaln(c, w_ada, b_ada):
    n = c.shape[0]
    cols = w_ada.shape[1]
    bn = 1536
    return pl.pallas_call(
        _adaln_body,
        grid=(cols // bn,),
        in_specs=[pl.BlockSpec((n, D_MODEL), lambda i: (0, 0)),
                  pl.BlockSpec((D_MODEL, bn), lambda i: (0, i)),
                  pl.BlockSpec((1, bn), lambda i: (0, i))],
        out_specs=pl.BlockSpec((n, bn), lambda i: (0, i)),
        out_shape=jax.ShapeDtypeStruct((n, cols), F32),
        compiler_params=pltpu.CompilerParams(vmem_limit_bytes=VMEM_LIMIT),
        name="adaln",
    )(c, w_ada, b_ada.reshape(1, cols))


def _modulated_norm(x, g, shift, scale):
    y = x * lax.rsqrt(jnp.mean(x * x, axis=-1, keepdims=True) + NORM_EPS)
    return (y * g) * (1.0 + scale) + shift


def _head_norm(t, bd, g):
    w = t.shape[1]
    sq_hi, sq_lo = _split_bf16(t * t)
    blk = bd[:w, :w]
    ms = (_dot(sq_hi, blk) + _dot(sq_lo, blk)) * (1.0 / HEAD_DIM)
    return t * lax.rsqrt(ms + QK_EPS) * g[:, :w]


def _rope(y, rope):
    c = rope[:, 0:LANES]
    s_next = rope[:, LANES:2 * LANES]
    s_prev = rope[:, 2 * LANES:3 * LANES]
    half = ROT_DIM // 2
    return y * c + pltpu.roll(y, LANES - half, 1) * s_next + pltpu.roll(y, half, 1) * s_prev


def _pair_expand(t):
    lane = lax.broadcasted_iota(jnp.int32, t.shape, 1)
    lo = lane < HEAD_DIM
    r = pltpu.roll(t, HEAD_DIM, 1)
    zero = jnp.zeros_like(t)
    a0 = jnp.where(lo, t, zero).astype(BF16)
    b0 = jnp.where(lo, zero, r).astype(BF16)
    a1 = jnp.where(lo, r, zero).astype(BF16)
    b1 = jnp.where(lo, zero, t).astype(BF16)
    return ((a0, b0), (a1, b1))


def _softmax_pv(s_list, v_list, sink):
    m = jnp.full((s_list[0].shape[0], 1), sink, F32)
    for s in s_list:
        m = jnp.maximum(m, jnp.max(s, axis=-1, keepdims=True))
    den = jnp.exp(sink - m)
    acc = None
    for s, v in zip(s_list, v_list):
        e = jnp.exp(s - m)
        den = den + jnp.sum(e, axis=-1, keepdims=True)
        pv = _dot(e.astype(BF16), v)
        acc = pv if acc is None else acc + pv
    return acc * (1.0 / den)


def _route(h2, wr, br):
    logits = _dot_nt(wr, h2.astype(BF16)) + br
    eid = lax.broadcasted_iota(jnp.int32, logits.shape, 0).astype(F32)
    vals, idxs = [], []
    for _ in range(TOP_K):
        m = jnp.max(logits, axis=0, keepdims=True)
        idx = jnp.min(jnp.where(logits == m, eid, float(N_EXPERTS)), axis=0, keepdims=True)
        vals.append(m)
        idxs.append(idx)
        logits = jnp.where(eid == idx, -jnp.inf, logits)
    ex = [jnp.exp(v - vals[0]) for v in vals]
    tot = ex[0] + ex[1] + ex[2] + ex[3]
    inv = 1.0 / tot
    gates = [e * inv for e in ex]
    rowid = lax.broadcasted_iota(jnp.int32, (SUBLANES, h2.shape[0]), 0)
    topi = jnp.zeros((SUBLANES, h2.shape[0]), F32)
    topg = jnp.zeros((SUBLANES, h2.shape[0]), F32)
    for k in range(TOP_K):
        topi = jnp.where(rowid == k, idxs[k], topi)
        topg = jnp.where(rowid == k, gates[k], topg)
    return topi.astype(jnp.int32), topg


_QKV_COLS = (_OFF_Q, Q_W + 2 * KV_W)
_LATE_COLS = ((_OFF_GB, 2 * CONV_CH), (_OFF_XC, CONV_CH), (_OFF_BA, D_MODEL), (_OFF_BC, D_MODEL))


def _in_proj(proj, h_bf, w_in_ref, group):
    off, width = group
    proj[off] = _dot(h_bf, w_in_ref[:, off:off + width])


def _cols(proj, off, width):
    for start, piece in proj.items():
        if start <= off and off + width <= start + piece.shape[1]:
            return piece[:, off - start:off - start + width]
    raise KeyError(off)


def _merge_and_route(x, o_bf, y_conv, proj, mod, n2g_ref, w_ao_ref, w_co_ref, w_o_ref,
                     wr_ref, br_ref, before_tail=None):
    sh1, sc1, g1, sh2, sc2, g2 = mod
    a = _dot(o_bf, w_ao_ref[...])
    c = _dot(y_conv.astype(BF16), w_co_ref[...])
    m = _sigmoid(_cols(proj, _OFF_BA, D_MODEL)) * a + _sigmoid(_cols(proj, _OFF_BC, D_MODEL)) * c
    x1 = x + g1 * _dot(m.astype(BF16), w_o_ref[...])
    if before_tail is not None:
        before_tail()
    h2 = _modulated_norm(x1, n2g_ref[...], sh2, sc2)
    topi, topg = _route(h2, wr_ref[...], br_ref[...])
    return x1, h2, topi, topg


def _prompt_body(sinks_ref, x_ref, mod_ref, rope_ref, xn_ref, modn_ref,
                 n1g_ref, n2g_ref, w_in_ref, bd_ref, qg_ref, kg_ref,
                 w_ao_ref, cw_ref, cb_ref, w_co_ref, w_o_ref, wr_ref, br_ref,
                 x1_ref, h2_ref, topi_ref, topg_ref, kwin_ref, vwin_ref, zwin_ref,
                 kbuf, vbuf, zbuf, h_buf, qkv_buf, late_buf, *, tb):
    j = pl.program_id(1)
    nsub = tb // WINDOW

    @pl.when(j == 0)
    def _():
        kbuf[0:WINDOW, :] = jnp.zeros((WINDOW, KV_W), F32)
        vbuf[0:WINDOW, :] = jnp.zeros((WINDOW, KV_W), F32)
        zbuf[0:SUBLANES, :] = jnp.zeros((SUBLANES, CONV_CH), F32)

    def front(xr, mr):
        h = _modulated_norm(xr[0], n1g_ref[...], mr[0, 0:1, :], mr[0, 1:2, :]).astype(BF16)
        h_buf[...] = h
        for group, buf in ((_QKV_COLS, qkv_buf), (_LATE_COLS[0], late_buf)):
            buf[...] = _dot(h, w_in_ref[:, group[0]:group[0] + group[1]])

    @pl.when((pl.program_id(0) == 0) & (j == 0))
    def _():
        front(x_ref, mod_ref)

    x = x_ref[0]
    mod = tuple(mod_ref[0, i:i + 1, :] for i in range(6))
    h_bf = h_buf[...]
    proj = {_QKV_COLS[0]: qkv_buf[...], _LATE_COLS[0][0]: late_buf[...]}
    rope = rope_ref[...]
    bd = bd_ref[...]

    k = _rope(_head_norm(_cols(proj, _OFF_K, KV_W), bd, kg_ref[...]), rope)
    v = _cols(proj, _OFF_V, KV_W)
    kbuf[WINDOW:WINDOW + tb, :] = k
    vbuf[WINDOW:WINDOW + tb, :] = v
    kwin_ref[0] = k[tb - WINDOW:tb]
    vwin_ref[0] = v[tb - WINDOW:tb]
    k_exp = _pair_expand(kbuf[...])
    v_exp = _pair_expand(vbuf[...])

    _in_proj(proj, h_bf, w_in_ref, _LATE_COLS[1])
    qn = _head_norm(_cols(proj, _OFF_Q, Q_W), bd, qg_ref[...])
    q_tiles = []
    for t in range(Q_W // LANES):
        qt = _rope(qn[:, t * LANES:(t + 1) * LANES], rope)
        q_tiles.append((qt * ATTN_SCALE).astype(BF16))

    row = lax.broadcasted_iota(jnp.int32, (WINDOW, 2 * WINDOW), 0)
    col = lax.broadcasted_iota(jnp.int32, (WINDOW, 2 * WINDOW), 1)
    band = (col > row) & (col <= row + WINDOW)
    first_key = jnp.where(j > 0, 0, WINDOW)
    band_first = band & (col >= first_key)
    o_rows = []
    for i in range(nsub):
        for group in _LATE_COLS[2:][i::nsub]:
            _in_proj(proj, h_bf, w_in_ref, group)
        mask = band_first if i == 0 else band
        keys = slice(i * WINDOW, (i + 2) * WINDOW)
        o_tiles = []
        for t in range(Q_W // LANES):
            c = t // 2
            qt = q_tiles[t][i * WINDOW:(i + 1) * WINDOW]
            o_t = None
            for half in range(2):
                s = jnp.where(mask, _dot_nt(qt, k_exp[c][half][keys]), NEG_INF)
                sink = sinks_ref[c * GROUP + (t % 2) * 2 + half]
                part = _softmax_pv([s], [v_exp[c][half][keys]], sink)
                o_t = part if o_t is None else o_t + part
            o_tiles.append(o_t.astype(BF16))
        o_rows.append(jnp.concatenate(o_tiles, axis=1))
    o_bf = jnp.concatenate(o_rows, axis=0) if nsub > 1 else o_rows[0]
    kbuf[0:WINDOW, :] = kbuf[tb:tb + WINDOW, :]
    vbuf[0:WINDOW, :] = vbuf[tb:tb + WINDOW, :]

    z = _cols(proj, _OFF_GC, CONV_CH) * _cols(proj, _OFF_XC, CONV_CH)
    zbuf[SUBLANES:SUBLANES + tb, :] = z
    zwin_ref[0] = z[tb - SUBLANES:tb]
    z1 = zbuf[SUBLANES - 1:SUBLANES - 1 + tb, :]
    z2 = zbuf[SUBLANES - 2:SUBLANES - 2 + tb, :]
    conv = cb_ref[...] + cw_ref[0:1, :] * z2 + cw_ref[1:2, :] * z1 + cw_ref[2:3, :] * z
    y_conv = _cols(proj, _OFF_GB, CONV_CH) * conv
    zbuf[0:SUBLANES, :] = zbuf[tb:tb + SUBLANES, :]

    x1, h2, topi, topg = _merge_and_route(x, o_bf, y_conv, proj, mod, n2g_ref, w_ao_ref, w_co_ref,
                                          w_o_ref, wr_ref, br_ref,
                                          before_tail=lambda: front(xn_ref, modn_ref))
    x1_ref[0] = x1
    h2_ref[0] = h2.astype(BF16)
    topi_ref[0] = topi
    topg_ref[0] = topg


def _prompt_mixer(x, mod, rope, sinks, weights, tb):
    nb, seq, _ = x.shape
    nj = seq // tb
    consts = weights

    def nxt(b, j):
        s = jnp.minimum(b * nj + j + 1, nb * nj - 1)
        return s // nj, s % nj
    in_specs = [
        pl.BlockSpec((1, tb, D_MODEL), lambda b, j, s: (b, j, 0)),
        pl.BlockSpec((1, 6, D_MODEL), lambda b, j, s: (b, 0, 0)),
        pl.BlockSpec((tb, 3 * LANES), lambda b, j, s: (j, 0)),
        pl.BlockSpec((1, tb, D_MODEL), lambda b, j, s: (*nxt(b, j), 0)),
        pl.BlockSpec((1, 6, D_MODEL), lambda b, j, s: (nxt(b, j)[0], 0, 0)),
    ] + [_const_spec(w.shape) for w in consts]
    out_specs = [
        pl.BlockSpec((1, tb, D_MODEL), lambda b, j, s: (b, j, 0)),
        pl.BlockSpec((1, tb, D_MODEL), lambda b, j, s: (b, j, 0)),
        pl.BlockSpec((1, SUBLANES, tb), lambda b, j, s: (b, 0, j)),
        pl.BlockSpec((1, SUBLANES, tb), lambda b, j, s: (b, 0, j)),
        pl.BlockSpec((1, WINDOW, KV_W), lambda b, j, s: (b, 0, 0)),
        pl.BlockSpec((1, WINDOW, KV_W), lambda b, j, s: (b, 0, 0)),
        pl.BlockSpec((1, SUBLANES, CONV_CH), lambda b, j, s: (b, 0, 0)),
    ]
    out_shape = [
        jax.ShapeDtypeStruct((nb, seq, D_MODEL), F32),
        jax.ShapeDtypeStruct((nb, seq, D_MODEL), BF16),
        jax.ShapeDtypeStruct((nb, SUBLANES, seq), jnp.int32),
        jax.ShapeDtypeStruct((nb, SUBLANES, seq), F32),
        jax.ShapeDtypeStruct((nb, WINDOW, KV_W), F32),
        jax.ShapeDtypeStruct((nb, WINDOW, KV_W), F32),
        jax.ShapeDtypeStruct((nb, SUBLANES, CONV_CH), F32),
    ]
    return pl.pallas_call(
        functools.partial(_prompt_body, tb=tb),
        grid_spec=pltpu.PrefetchScalarGridSpec(
            num_scalar_prefetch=1, grid=(nb, seq // tb),
            in_specs=in_specs, out_specs=out_specs,
            scratch_shapes=[pltpu.VMEM((tb + WINDOW, KV_W), F32),
                            pltpu.VMEM((tb + WINDOW, KV_W), F32),
                            pltpu.VMEM((tb + SUBLANES, CONV_CH), F32),
                            pltpu.VMEM((tb, D_MODEL), BF16),
                            pltpu.VMEM((tb, _QKV_COLS[1]), F32),
                            pltpu.VMEM((tb, _LATE_COLS[0][1]), F32)]),
        out_shape=out_shape,
        compiler_params=pltpu.CompilerParams(
            dimension_semantics=("arbitrary", "arbitrary"), vmem_limit_bytes=VMEM_LIMIT),
        name="prompt_mixer",
    )(sinks, x, mod, rope, x, mod, *consts)


def _sample_body(sinks_ref, x_ref, mod_ref, rope_ref, ck_ref, cv_ref, st_ref,
                 n1g_ref, n2g_ref, w_in_ref, bd_ref, qg_ref, kg_ref,
                 w_ao_ref, cw_ref, cb_ref, w_co_ref, w_o_ref, wr_ref, br_ref,
                 x1_ref, h2_ref, topi_ref, topg_ref, knew_ref, vnew_ref, z_ref, *, nseq, dseq):
    rows = nseq * dseq
    wbuf = ck_ref.shape[1]
    x = x_ref[...]
    mod = tuple(mod_ref[i] for i in range(6))
    h_bf = _modulated_norm(x, n1g_ref[...], mod[0], mod[1]).astype(BF16)
    proj = {}
    for group in (_QKV_COLS,) + _LATE_COLS:
        _in_proj(proj, h_bf, w_in_ref, group)
    rope = rope_ref[...]
    bd = bd_ref[...]

    k = _rope(_head_norm(_cols(proj, _OFF_K, KV_W), bd, kg_ref[...]), rope)
    v = _cols(proj, _OFF_V, KV_W)
    for win_ref, cache_ref, new in ((knew_ref, ck_ref, k), (vnew_ref, cv_ref, v)):
        win_ref[:, 0:wbuf - dseq, :] = cache_ref[:, dseq:, :]
        win_ref[:, wbuf - dseq:, :] = new.reshape(nseq, dseq, KV_W)
    kn_exp = _pair_expand(k)
    vn_exp = _pair_expand(v)
    kc_exp = _pair_expand(ck_ref[...].reshape(nseq * wbuf, KV_W))
    vc_exp = _pair_expand(cv_ref[...].reshape(nseq * wbuf, KV_W))

    qrow = lax.broadcasted_iota(jnp.int32, (rows, nseq * wbuf), 0)
    ccol = lax.broadcasted_iota(jnp.int32, (rows, nseq * wbuf), 1)
    mask_c = ((ccol // wbuf) == (qrow // dseq)) & ((ccol % wbuf) > (qrow % dseq) + (wbuf - WINDOW))
    qrow_n = lax.broadcasted_iota(jnp.int32, (rows, rows), 0)
    ncol = lax.broadcasted_iota(jnp.int32, (rows, rows), 1)
    mask_n = ((ncol // dseq) == (qrow_n // dseq)) & ((ncol % dseq) <= (qrow_n % dseq))

    qn = _head_norm(_cols(proj, _OFF_Q, Q_W), bd, qg_ref[...])
    o_tiles = []
    for t in range(Q_W // LANES):
        c = t // 2
        qt = (_rope(qn[:, t * LANES:(t + 1) * LANES], rope) * ATTN_SCALE).astype(BF16)
        o_t = None
        for half in range(2):
            s_c = jnp.where(mask_c, _dot_nt(qt, kc_exp[c][half]), NEG_INF)
            s_n = jnp.where(mask_n, _dot_nt(qt, kn_exp[c][half]), NEG_INF)
            sink = sinks_ref[c * GROUP + (t % 2) * 2 + half]
            part = _softmax_pv([s_c, s_n], [vc_exp[c][half], vn_exp[c][half]], sink)
            o_t = part if o_t is None else o_t + part
        o_tiles.append(o_t.astype(BF16))
    o_bf = jnp.concatenate(o_tiles, axis=1)

    z = _cols(proj, _OFF_GC, CONV_CH) * _cols(proj, _OFF_XC, CONV_CH)
    z_ref[...] = z
    r = lax.broadcasted_iota(jnp.int32, z.shape, 0) % dseq
    st0 = st_ref[0]
    st1 = st_ref[1]
    z1 = jnp.where(r == 0, st1, pltpu.roll(z, 1, 0))
    z2 = jnp.where(r == 0, st0, jnp.where(r == 1, st1, pltpu.roll(z, 2, 0)))
    conv = cb_ref[...] + cw_ref[0:1, :] * z2 + cw_ref[1:2, :] * z1 + cw_ref[2:3, :] * z
    y_conv = _cols(proj, _OFF_GB, CONV_CH) * conv

    x1, h2, topi, topg = _merge_and_route(x, o_bf, y_conv, proj, mod, n2g_ref, w_ao_ref, w_co_ref,
                                          w_o_ref, wr_ref, br_ref)
    x1_ref[...] = x1
    h2_ref[...] = h2.astype(BF16)
    topi_ref[...] = topi
    topg_ref[...] = topg


def _sample_mixer(x, mod, rope, cache_k, cache_v, state, sinks, weights, nseq):
    ntok = x.shape[0]
    nall, wbuf, _ = cache_k.shape
    dseq = ntok // nall
    rows = nseq * dseq
    consts = weights
    in_specs = [
        pl.BlockSpec((rows, D_MODEL), lambda i, s: (i, 0)),
        pl.BlockSpec((6, rows, D_MODEL), lambda i, s: (0, i, 0)),
        pl.BlockSpec((rows, 3 * LANES), lambda i, s: (i, 0)),
        pl.BlockSpec((nseq, wbuf, KV_W), lambda i, s: (i, 0, 0)),
        pl.BlockSpec((nseq, wbuf, KV_W), lambda i, s: (i, 0, 0)),
        pl.BlockSpec((2, rows, CONV_CH), lambda i, s: (0, i, 0)),
    ] + [_const_spec(w.shape) for w in consts]
    out_specs = [
        pl.BlockSpec((rows, D_MODEL), lambda i, s: (i, 0)),
        pl.BlockSpec((rows, D_MODEL), lambda i, s: (i, 0)),
        pl.BlockSpec((SUBLANES, rows), lambda i, s: (0, i)),
        pl.BlockSpec((SUBLANES, rows), lambda i, s: (0, i)),
        pl.BlockSpec((nseq, wbuf, KV_W), lambda i, s: (i, 0, 0)),
        pl.BlockSpec((nseq, wbuf, KV_W), lambda i, s: (i, 0, 0)),
        pl.BlockSpec((rows, CONV_CH), lambda i, s: (i, 0)),
    ]
    out_shape = [
        jax.ShapeDtypeStruct((ntok, D_MODEL), F32),
        jax.ShapeDtypeStruct((ntok, D_MODEL), BF16),
        jax.ShapeDtypeStruct((SUBLANES, ntok), jnp.int32),
        jax.ShapeDtypeStruct((SUBLANES, ntok), F32),
        jax.ShapeDtypeStruct((nall, wbuf, KV_W), F32),
        jax.ShapeDtypeStruct((nall, wbuf, KV_W), F32),
        jax.ShapeDtypeStruct((ntok, CONV_CH), F32),
    ]
    return pl.pallas_call(
        functools.partial(_sample_body, nseq=nseq, dseq=dseq),
        grid_spec=pltpu.PrefetchScalarGridSpec(
            num_scalar_prefetch=1, grid=(ntok // rows,),
            in_specs=in_specs, out_specs=out_specs),
        out_shape=out_shape,
        compiler_params=pltpu.CompilerParams(
            dimension_semantics=("arbitrary",), vmem_limit_bytes=VMEM_LIMIT),
        name="sample_mixer",
    )(sinks, x, mod, rope, cache_k, cache_v, state, *consts)


SORT_TOKENS = 512
RUN_ALIGN = SUBLANES
EXPERT_TILE = 1024
HALF_D = D_MODEL // 2
LOCAL_ROWS = SORT_TOKENS * TOP_K + N_EXPERTS * RUN_ALIGN
UNSORT_CHUNK = 256
assert LOCAL_ROWS % UNSORT_CHUNK == 0
_HI_MASK = 0xFFFF0000


def _pack_halves(x):
    lo = pltpu.bitcast(x[:, :HALF_D], jnp.uint32)
    hi = pltpu.bitcast(x[:, HALF_D:], jnp.uint32)
    return hi | (lo >> 16)


def _unpack_halves(w):
    lo = pltpu.bitcast(w << 16, F32).astype(BF16)
    hi = pltpu.bitcast(w & jnp.uint32(_HI_MASK), F32).astype(BF16)
    return lo, hi


def _local_positions(topi, lstart_col, tri):
    ntok = topi.shape[1]
    eid = lax.broadcasted_iota(jnp.int32, (N_EXPERTS, ntok), 0)
    hits = [eid == topi[k:k + 1, :] for k in range(TOP_K)]
    chosen = jnp.zeros((N_EXPERTS, ntok), F32)
    for h in hits:
        chosen = jnp.where(h, 1.0, chosen)
    base = _dot(chosen.astype(BF16), tri) + lstart_col
    return [jnp.sum(jnp.where(h, base, 0.0), axis=0, keepdims=True) for h in hits]


def _run_copy(loc, glob, sem, lofs, gofs, n, to_global):
    lo = loc.at[pl.ds(pl.multiple_of(lofs, RUN_ALIGN), n)]
    gl = glob.at[pl.ds(pl.multiple_of(gofs, RUN_ALIGN), n)]
    return pltpu.make_async_copy(lo, gl, sem) if to_global else pltpu.make_async_copy(gl, lo, sem)


def _start_runs(meta, blk, loc, glob, sem, to_global):
    lstart_ref, goff_ref, cpad_ref = meta
    for e in range(N_EXPERTS):
        n = pl.multiple_of(cpad_ref[blk * N_EXPERTS + e], RUN_ALIGN)
        _run_copy(loc, glob, sem, lstart_ref[blk * N_EXPERTS + e], goff_ref[blk * N_EXPERTS + e], n,
                  to_global).start()


def _wait_runs(nrows, loc, glob, sem, to_global):
    _run_copy(loc, glob, sem, 0, 0, pl.multiple_of(nrows, RUN_ALIGN), to_global).wait()


def _sort_body(lstart_ref, goff_ref, cpad_ref, ltot_ref, fstart_ref, flen_ref, nact_ref,
               hp_ref, hs_ref, topi_ref, lcol_ref, tri_ref, xs_hbm, pos_ref, buf, zbuf, sems, fsem, *, nblk_p, nblk):
    i = pl.program_id(0)
    slot = i % 2
    meta = (lstart_ref, goff_ref, cpad_ref)

    @pl.when(i == 0)
    def _():
        zbuf[...] = jnp.zeros(zbuf.shape, jnp.uint32)

        def fill(e, tot):
            n = pl.multiple_of(flen_ref[e], RUN_ALIGN)

            @pl.when(n > 0)
            def _():
                _run_copy(zbuf, xs_hbm, fsem, 0, fstart_ref[e], n, True).start()
            return tot + n
        total = lax.fori_loop(0, N_EXPERTS, fill, 0)

        @pl.when(total > 0)
        def _():
            _wait_runs(total, zbuf, xs_hbm, fsem, True)

        def fill_tile(t, carry):
            cp = _run_copy(zbuf, xs_hbm, fsem, 0, t * EXPERT_TILE, EXPERT_TILE, True)
            cp.start()
            cp.wait()
            return carry
        lax.fori_loop(nact_ref[0], xs_hbm.shape[0] // EXPERT_TILE, fill_tile, 0)

    def run(h_ref):
        pos = [p.astype(jnp.int32) for p in _local_positions(topi_ref[...], lcol_ref[...], tri_ref[...])]
        rowid = lax.broadcasted_iota(jnp.int32, (LOCAL_ROWS, SORT_TOKENS), 0)
        onehot = jnp.zeros((LOCAL_ROWS, SORT_TOKENS), F32)
        for p in pos:
            onehot = jnp.where(rowid == p, 1.0, onehot)
        buf[slot] = _pack_halves(_dot(onehot.astype(BF16), h_ref[...]))
        krow = lax.broadcasted_iota(jnp.int32, (SUBLANES, SORT_TOKENS), 0)
        pos_rows = jnp.zeros((SUBLANES, SORT_TOKENS), jnp.int32)
        for k in range(TOP_K):
            pos_rows = jnp.where(krow == k, pos[k], pos_rows)
        pos_ref[...] = pos_rows

    @pl.when(i < nblk_p)
    def _():
        run(hp_ref)

    @pl.when(i >= nblk_p)
    def _():
        run(hs_ref)

    _start_runs(meta, i, buf.at[slot], xs_hbm, sems.at[slot], True)

    @pl.when(i > 0)
    def _():
        _wait_runs(ltot_ref[jnp.maximum(i - 1, 0)], buf.at[1 - slot], xs_hbm, sems.at[1 - slot], True)

    @pl.when(i == nblk - 1)
    def _():
        _wait_runs(ltot_ref[i], buf.at[slot], xs_hbm, sems.at[slot], True)


def _sort_tokens(plan, h_p, h_s, topi_all, tri, n_slots):
    nblk_p = h_p.shape[0] // SORT_TOKENS
    nblk = nblk_p + h_s.shape[0] // SORT_TOKENS
    return pl.pallas_call(
        functools.partial(_sort_body, nblk_p=nblk_p, nblk=nblk),
        grid_spec=pltpu.PrefetchScalarGridSpec(
            num_scalar_prefetch=7, grid=(nblk,),
            in_specs=[pl.BlockSpec((SORT_TOKENS, D_MODEL), lambda i, *_: (jnp.minimum(i, nblk_p - 1), 0)),
                      pl.BlockSpec((SORT_TOKENS, D_MODEL), lambda i, *_: (jnp.maximum(i - nblk_p, 0), 0)),
                      pl.BlockSpec((SUBLANES, SORT_TOKENS), lambda i, *_: (0, i)),
                      pl.BlockSpec((None, N_EXPERTS, 1), lambda i, *_: (i, 0, 0)),
                      _const_spec(tri.shape)],
            out_specs=[pl.BlockSpec(memory_space=pl.ANY),
                       pl.BlockSpec((SUBLANES, SORT_TOKENS), lambda i, *_: (0, i))],
            scratch_shapes=[pltpu.VMEM((2, LOCAL_ROWS, HALF_D), jnp.uint32),
                            pltpu.VMEM((EXPERT_TILE, HALF_D), jnp.uint32),
                            pltpu.SemaphoreType.DMA((2,)),
                            pltpu.SemaphoreType.DMA]),
        out_shape=[jax.ShapeDtypeStruct((n_slots, HALF_D), jnp.uint32),
                   jax.ShapeDtypeStruct(topi_all.shape, jnp.int32)],
        compiler_params=pltpu.CompilerParams(
            dimension_semantics=("arbitrary",), vmem_limit_bytes=VMEM_LIMIT),
        name="moe_sort",
    )(plan["lstart"], plan["goff"], plan["cpad"], plan["ltot"], plan["fill_start"], plan["fill_len"], plan["nact"],
      h_p, h_s, topi_all, plan["lstart_col"], tri)


def _experts_body(tile_exp_ref, nact_ref, rows_ref, xs_ref, wg_hbm, bg_ref, wu_hbm, bu_ref, wd_hbm, bd_ref,
                  ys_ref, w_f32, w_bf, sems):
    t = pl.program_id(0)
    nact = nact_ref[0]
    w_hbm = (wg_hbm, wu_hbm, wd_hbm)

    def weight_copies(e, slot):
        return [pltpu.make_async_copy(w.at[e], w_f32.at[slot, j], sems.at[slot, j]) for j, w in enumerate(w_hbm)]

    @pl.when(t < nact)
    def _():
        e = tile_exp_ref[t]
        slot = e % 2

        @pl.when(t == 0)
        def _():
            for cp in weight_copies(e, slot):
                cp.start()

        @pl.when((t == 0) | (e != tile_exp_ref[jnp.maximum(t - 1, 0)]))
        def _():
            @pl.when(e + 1 < N_EXPERTS)
            def _():
                for cp in weight_copies(e + 1, 1 - slot):
                    cp.start()
            for j, cp in enumerate(weight_copies(e, slot)):
                cp.wait()
                w_bf[j] = w_f32[slot, j].astype(BF16)

        def swiglu(rows):
            x = jnp.concatenate(_unpack_halves(xs_ref[0:rows, :]), axis=1)
            g = _dot(x, w_bf[0]) + bg_ref[0]
            u = _dot(x, w_bf[1]) + bu_ref[0]
            g = jnp.minimum(g, SWIGLU_LIMIT)
            u = jnp.clip(u, -SWIGLU_LIMIT, SWIGLU_LIMIT)
            a = g * _sigmoid(SWIGLU_ALPHA * g) * (u + 1.0)
            out = _dot(a.astype(BF16), w_bf[2]) + bd_ref[0]
            ys_ref[0:rows, :] = _pack_halves(out.astype(BF16).astype(F32))

        half = EXPERT_TILE // 2
        used = rows_ref[t]

        @pl.when(used > half)
        def _():
            swiglu(EXPERT_TILE)

        @pl.when(used <= half)
        def _():
            swiglu(half)
            ys_ref[half:, :] = jnp.zeros((EXPERT_TILE - half, HALF_D), jnp.uint32)

    @pl.when(t >= nact)
    def _():
        ys_ref[...] = jnp.zeros(ys_ref.shape, jnp.uint32)


def _experts(tile_exp, nact, tile_rows, xs, wg, bg, wu, bu, wd, bd):
    n_tiles = tile_exp.shape[0]

    def active(t, na):
        return jnp.minimum(t, na[0] - 1)
    wspec = pl.BlockSpec(memory_space=pl.ANY)
    bspec = pl.BlockSpec((1, 1, D_MODEL), lambda t, te, na, tr: (te[active(t, na)], 0, 0))
    xspec = pl.BlockSpec((EXPERT_TILE, HALF_D), lambda t, te, na, tr: (active(t, na), 0))
    return pl.pallas_call(
        _experts_body,
        grid_spec=pltpu.PrefetchScalarGridSpec(
            num_scalar_prefetch=3, grid=(n_tiles,),
            in_specs=[xspec, wspec, bspec, wspec, bspec, wspec, bspec],
            out_specs=pl.BlockSpec((EXPERT_TILE, HALF_D), lambda t, te, na, tr: (t, 0)),
            scratch_shapes=[pltpu.VMEM((2, 3, D_MODEL, D_MODEL), F32),
                            pltpu.VMEM((3, D_MODEL, D_MODEL), BF16),
                            pltpu.SemaphoreType.DMA((2, 3))]),
        out_shape=jax.ShapeDtypeStruct(xs.shape, jnp.uint32),
        compiler_params=pltpu.CompilerParams(
            dimension_semantics=("arbitrary",), vmem_limit_bytes=VMEM_LIMIT),
        name="experts",
    )(tile_exp, nact, tile_rows, xs, wg, bg, wu, bu, wd, bd)


def _unsort_body(lstart_ref, goff_ref, cpad_ref, ltot_ref,
                 ys_hbm, x1_ref, g2_ref, pos_ref, gate_ref, y_ref, buf, sems, *, blk0, nblk):
    i = pl.program_id(0)
    b = blk0 + i
    slot = i % 2
    meta = (lstart_ref, goff_ref, cpad_ref)

    @pl.when(i == 0)
    def _():
        buf[...] = jnp.zeros(buf.shape, jnp.uint32)
        _start_runs(meta, b, buf.at[0], ys_hbm, sems.at[0], False)

    @pl.when(i + 1 < nblk)
    def _():
        _start_runs(meta, b + 1, buf.at[1 - slot], ys_hbm, sems.at[1 - slot], False)

    _wait_runs(ltot_ref[b], buf.at[slot], ys_hbm, sems.at[slot], False)
    pos_cols = pos_ref[...]
    gate_cols = gate_ref[...]
    acc_lo = acc_hi = None
    for c0 in range(0, LOCAL_ROWS, UNSORT_CHUNK):
        colid = lax.broadcasted_iota(jnp.int32, (SORT_TOKENS, UNSORT_CHUNK), 1) + c0
        weights = jnp.zeros((SORT_TOKENS, UNSORT_CHUNK), F32)
        for k in range(TOP_K):
            weights = jnp.where(colid == pos_cols[:, k:k + 1], gate_cols[:, k:k + 1], weights)
        weights = weights.astype(BF16)
        y_lo, y_hi = _unpack_halves(buf[slot, c0:c0 + UNSORT_CHUNK, :])
        acc_lo = _dot(weights, y_lo) if acc_lo is None else acc_lo + _dot(weights, y_lo)
        acc_hi = _dot(weights, y_hi) if acc_hi is None else acc_hi + _dot(weights, y_hi)
    g2 = g2_ref[...]
    y_ref[:, :HALF_D] = x1_ref[:, :HALF_D] + g2[:, :HALF_D] * acc_lo
    y_ref[:, HALF_D:] = x1_ref[:, HALF_D:] + g2[:, HALF_D:] * acc_hi


def _unsort_tokens(plan, ys, x1, g2, g2_spec, pos_cols, gate_cols, blk0):
    nblk = x1.shape[0] // SORT_TOKENS
    return pl.pallas_call(
        functools.partial(_unsort_body, blk0=blk0, nblk=nblk),
        grid_spec=pltpu.PrefetchScalarGridSpec(
            num_scalar_prefetch=4, grid=(nblk,),
            in_specs=[pl.BlockSpec(memory_space=pl.ANY),
                      pl.BlockSpec((SORT_TOKENS, D_MODEL), lambda i, *_: (i, 0)),
                      g2_spec,
                      pl.BlockSpec((SORT_TOKENS, SUBLANES), lambda i, *_: (blk0 + i, 0)),
                      pl.BlockSpec((SORT_TOKENS, SUBLANES), lambda i, *_: (blk0 + i, 0))],
            out_specs=pl.BlockSpec((SORT_TOKENS, D_MODEL), lambda i, *_: (i, 0)),
            scratch_shapes=[pltpu.VMEM((2, LOCAL_ROWS, HALF_D), jnp.uint32),
                            pltpu.SemaphoreType.DMA((2,))]),
        out_shape=jax.ShapeDtypeStruct(x1.shape, F32),
        compiler_params=pltpu.CompilerParams(
            dimension_semantics=("arbitrary",), vmem_limit_bytes=VMEM_LIMIT),
        name="moe_unsort",
    )(plan["lstart"], plan["goff"], plan["cpad"], plan["ltot"],
      ys, x1, g2, pos_cols, gate_cols)


def _rope_table(pos):
    half = ROT_DIM // 2
    inv_freq = ROPE_THETA ** (-jnp.arange(0, ROT_DIM, 2, dtype=F32) / ROT_DIM)
    ang = pos.astype(F32)[:, None] * inv_freq[None, :]
    cos, sin = jnp.cos(ang), jnp.sin(ang)
    n = pos.shape[0]
    rest = HEAD_DIM - ROT_DIM
    c = jnp.concatenate([cos, cos, jnp.ones((n, rest), F32)], axis=1)
    s_next = jnp.concatenate([-sin, jnp.zeros((n, half + rest), F32)], axis=1)
    s_prev = jnp.concatenate([jnp.zeros((n, half), F32), sin, jnp.zeros((n, rest), F32)], axis=1)
    rep = LANES // HEAD_DIM
    return jnp.concatenate([jnp.tile(c, (1, rep)), jnp.tile(s_next, (1, rep)), jnp.tile(s_prev, (1, rep))], axis=1)


def _num_expert_tiles(ntok):
    nblk = ntok // SORT_TOKENS
    worst = ntok * TOP_K + nblk * N_EXPERTS * RUN_ALIGN + N_EXPERTS * (EXPERT_TILE - 1)
    return -(-worst // EXPERT_TILE)


def _routing_plan(topi_all):
    ntok = topi_all.shape[1]
    nblk = ntok // SORT_TOKENS
    n_tiles = _num_expert_tiles(ntok)
    choice = topi_all[:TOP_K].reshape(TOP_K, nblk, SORT_TOKENS)
    experts = jnp.arange(N_EXPERTS, dtype=jnp.int32)
    cnt = (choice[:, :, :, None] == experts).astype(jnp.int32).sum(axis=(0, 2))
    cpad = jnp.maximum(-(-cnt // RUN_ALIGN) * RUN_ALIGN, RUN_ALIGN)
    lstart = jnp.cumsum(cpad, axis=1) - cpad
    tot = cpad.sum(axis=0)
    region = -(-tot // EXPERT_TILE) * EXPERT_TILE
    region_end = jnp.cumsum(region)
    gstart = region_end - region
    goff = gstart[None, :] + jnp.cumsum(cpad, axis=0) - cpad
    tile_start = jnp.arange(n_tiles, dtype=jnp.int32) * EXPERT_TILE
    tile_exp = jnp.minimum((tile_start[:, None] >= region_end[None, :]).astype(jnp.int32).sum(axis=1), N_EXPERTS - 1)
    tile_rows = jnp.clip((gstart + tot)[tile_exp] - tile_start, 0, EXPERT_TILE)
    i32 = lambda a: a.astype(jnp.int32)
    return dict(
        tile_rows=i32(tile_rows),
        lstart=i32(lstart).reshape(-1), goff=i32(goff).reshape(-1), cpad=i32(cpad).reshape(-1),
        ltot=i32(cpad.sum(axis=1)), fill_start=i32(gstart + tot), fill_len=i32(region - tot),
        lstart_col=lstart.astype(F32).reshape(nblk, N_EXPERTS, 1),
        tile_exp=i32(tile_exp), nact=i32(region_end[-1] // EXPERT_TILE).reshape(1))


def kernel(x_prompt, x_sample, cache_k_win, cache_v_win, state_conv, c_prompt, c_sample, w_ada, b_ada, norm1_g, norm2_g, w_in, q_norm_g, k_norm_g, attn_sinks, w_attn_out, conv_w, conv_b, w_conv_out, w_o, w_router, b_router, w_gate, b_gate, w_up, b_up, w_down, b_down):
    nb, seq, _ = x_prompt.shape
    ns, dseq, _ = x_sample.shape
    wbuf = cache_k_win.shape[2]
    assert w_ada.shape[0] == 1, "single-layer step"
    l = 0

    mod = _adaln(jnp.concatenate([c_prompt, c_sample], axis=0), w_ada[l], b_ada[l])
    mod_p = mod[:nb].reshape(nb, 6, D_MODEL)
    mod_s = jnp.repeat(mod[nb:].reshape(ns, 6, D_MODEL).transpose(1, 0, 2), dseq, axis=1)

    bd = jnp.asarray(np.kron(np.eye(N_HEADS), np.ones((HEAD_DIM, HEAD_DIM))), BF16)
    rep = LANES // HEAD_DIM
    weights = (norm1_g[l].reshape(1, D_MODEL), norm2_g[l].reshape(1, D_MODEL), w_in[l].astype(BF16), bd,
               jnp.tile(q_norm_g[l], N_HEADS).reshape(1, Q_W), jnp.tile(k_norm_g[l], N_KV).reshape(1, KV_W),
               w_attn_out[l].astype(BF16), conv_w[l], conv_b[l].reshape(1, CONV_CH), w_conv_out[l].astype(BF16),
               w_o[l].astype(BF16), w_router[l].T.astype(BF16), b_router[l].reshape(N_EXPERTS, 1))
    sinks = attn_sinks[l]

    tb = 512
    (x1_p, h2_p, topi_p, topg_p, kwin_p, vwin_p, zwin_p) = _prompt_mixer(
        x_prompt, mod_p, _rope_table(jnp.arange(seq)), sinks, weights, tb)
    pos_s = jnp.tile(PAST_LEN + jnp.arange(dseq), ns)
    state_rows = jnp.repeat(state_conv[l].transpose(1, 0, 2), dseq, axis=1)
    (x1_s, h2_s, topi_s, topg_s, knew, vnew, z_s) = _sample_mixer(
        x_sample.reshape(ns * dseq, D_MODEL), mod_s, _rope_table(pos_s),
        cache_k_win[l].reshape(ns, wbuf, KV_W), cache_v_win[l].reshape(ns, wbuf, KV_W),
        state_rows, sinks, weights, 16)

    ntok_p = nb * seq
    ntok = ntok_p + ns * dseq
    topi_all = jnp.concatenate([topi_p.transpose(1, 0, 2).reshape(SUBLANES, ntok_p), topi_s], axis=1)
    topg_all = jnp.concatenate([topg_p.transpose(1, 0, 2).reshape(SUBLANES, ntok_p), topg_s], axis=1)
    plan = _routing_plan(topi_all)
    tri = jnp.asarray(np.triu(np.ones((SORT_TOKENS, SORT_TOKENS)), k=1), BF16)
    xs, pos_all = _sort_tokens(plan, h2_p.reshape(ntok_p, D_MODEL), h2_s, topi_all, tri,
                               _num_expert_tiles(ntok) * EXPERT_TILE)
    pos_cols, gate_cols = pos_all.T, topg_all.T
    ys = _experts(plan["tile_exp"], plan["nact"], plan["tile_rows"], xs,
                  w_gate[l], b_gate[l].reshape(N_EXPERTS, 1, D_MODEL),
                  w_up[l], b_up[l].reshape(N_EXPERTS, 1, D_MODEL),
                  w_down[l], b_down[l].reshape(N_EXPERTS, 1, D_MODEL))
    blocks_per_seq = seq // SORT_TOKENS
    y_p = _unsort_tokens(plan, ys, x1_p.reshape(ntok_p, D_MODEL), mod_p[:, 5:6, :],
                         pl.BlockSpec((None, 1, D_MODEL), lambda i, *_: (i // blocks_per_seq, 0, 0)),
                         pos_cols, gate_cols, 0)
    y_s = _unsort_tokens(plan, ys, x1_s, mod_s[5],
                         pl.BlockSpec((SORT_TOKENS, D_MODEL), lambda i, *_: (i, 0)),
                         pos_cols, gate_cols, ntok_p // SORT_TOKENS)

    n_kv_shape = (N_KV, HEAD_DIM)
    k_win_p = kwin_p.reshape(1, nb, WINDOW, *n_kv_shape)
    v_win_p = vwin_p.reshape(1, nb, WINDOW, *n_kv_shape)
    conv_p = zwin_p[:, SUBLANES - (CONV_K - 1):, :][None]
    k_win_s = knew.reshape(1, ns, wbuf, *n_kv_shape)
    v_win_s = vnew.reshape(1, ns, wbuf, *n_kv_shape)
    conv_s = z_s.reshape(ns, dseq, CONV_CH)[:, dseq - (CONV_K - 1):, :][None]
    return (y_p.reshape(nb, seq, D_MODEL), y_s.reshape(ns, dseq, D_MODEL),
            k_win_p, v_win_p, conv_p, k_win_s, v_win_s, conv_s)
```

```python
import functools

import numpy as np
import jax
import jax.numpy as jnp
from jax import lax
from jax.experimental import pallas as pl
from jax.experimental.pallas import tpu as pltpu

D_MODEL = 1024
HEAD_DIM = 64
N_HEADS = 8
N_KV = 2
GROUP = N_HEADS // N_KV
Q_W = N_HEADS * HEAD_DIM
KV_W = N_KV * HEAD_DIM
WINDOW = 128
ROT_DIM = HEAD_DIM // 4
ROPE_THETA = 500000.0
ATTN_SCALE = HEAD_DIM ** -0.5
CONV_CH = D_MODEL // 2
CONV_K = 3
N_EXPERTS = 32
TOP_K = 4
SWIGLU_ALPHA = 1.702
SWIGLU_LIMIT = 7.0
MOE_BLOCK = 128
NORM_EPS = 1e-5
QK_EPS = 1e-6
NEG_INF = -1e30
PAST_LEN = 16384

LANES = 128
SUBLANES = 8
VMEM_LIMIT = 56 * 1024 * 1024

_OFF_Q = 0
_OFF_K = _OFF_Q + Q_W
_OFF_V = _OFF_K + KV_W
_OFF_GB = _OFF_V + KV_W
_OFF_GC = _OFF_GB + CONV_CH
_OFF_XC = _OFF_GC + CONV_CH
_OFF_BA = _OFF_XC + CONV_CH
_OFF_BC = _OFF_BA + D_MODEL
IN_COLS = _OFF_BC + D_MODEL

BF16 = jnp.bfloat16
F32 = jnp.float32
_NT = (((1,), (1,)), ((), ()))


def _dot(a, b):
    return jnp.dot(a, b, preferred_element_type=F32)


def _dot_nt(a, b):
    return lax.dot_general(a, b, _NT, preferred_element_type=F32)


def _sigmoid(x):
    return 1.0 / (1.0 + jnp.exp(-x))


def _split_bf16(x):
    hi = x.astype(BF16)
    lo = (x - hi.astype(F32)).astype(BF16)
    return hi, lo


def _const_spec(shape):
    nd = len(shape)
    return pl.BlockSpec(shape, lambda *_: (0,) * nd, pipeline_mode=pl.Buffered(1))


def _adaln_body(c_ref, w_ref, b_ref, o_ref):
    c = c_ref[...]
    s = c * _sigmoid(c)
    o_ref[...] = _dot(s.astype(BF16), w_ref[...].astype(BF16)) + b_ref[...]


def _adaln(c, w_ada, b_ada):
    n = c.shape[0]
    cols = w_ada.shape[1]
    bn = 1536
    return pl.pallas_call(
        _adaln_body,
        grid=(cols // bn,),
        in_specs=[pl.BlockSpec((n, D_MODEL), lambda i: (0, 0)),
                  pl.BlockSpec((D_MODEL, bn), lambda i: (0, i)),
                  pl.BlockSpec((1, bn), lambda i: (0, i))],
        out_specs=pl.BlockSpec((n, bn), lambda i: (0, i)),
        out_shape=jax.ShapeDtypeStruct((n, cols), F32),
        compiler_params=pltpu.CompilerParams(vmem_limit_bytes=VMEM_LIMIT),
        name="adaln",
    )(c, w_ada, b_ada.reshape(1, cols))


def _modulated_norm(x, g, shift, scale):
    y = x * lax.rsqrt(jnp.mean(x * x, axis=-1, keepdims=True) + NORM_EPS)
    return (y * g) * (1.0 + scale) + shift


def _head_norm(t, bd, g):
    w = t.shape[1]
    sq_hi, sq_lo = _split_bf16(t * t)
    blk = bd[:w, :w]
    ms = (_dot(sq_hi, blk) + _dot(sq_lo, blk)) * (1.0 / HEAD_DIM)
    return t * lax.rsqrt(ms + QK_EPS) * g[:, :w]


def _rope(y, rope):
    c = rope[:, 0:LANES]
    s_next = rope[:, LANES:2 * LANES]
    s_prev = rope[:, 2 * LANES:3 * LANES]
    half = ROT_DIM // 2
    return y * c + pltpu.roll(y, LANES - half, 1) * s_next + pltpu.roll(y, half, 1) * s_prev


def _pair_expand(t):
    lane = lax.broadcasted_iota(jnp.int32, t.shape, 1)
    lo = lane < HEAD_DIM
    r = pltpu.roll(t, HEAD_DIM, 1)
    zero = jnp.zeros_like(t)
    a0 = jnp.where(lo, t, zero).astype(BF16)
    b0 = jnp.where(lo, zero, r).astype(BF16)
    a1 = jnp.where(lo, r, zero).astype(BF16)
    b1 = jnp.where(lo, zero, t).astype(BF16)
    return ((a0, b0), (a1, b1))


def _softmax_pv(s_list, v_list, sink):
    m = jnp.full((s_list[0].shape[0], 1), sink, F32)
    for s in s_list:
        m = jnp.maximum(m, jnp.max(s, axis=-1, keepdims=True))
    den = jnp.exp(sink - m)
    acc = None
    for s, v in zip(s_list, v_list):
        e = jnp.exp(s - m)
        den = den + jnp.sum(e, axis=-1, keepdims=True)
        pv = _dot(e.astype(BF16), v)
        acc = pv if acc is None else acc + pv
    return acc * (1.0 / den)


def _route(h2, wr, br):
    logits = _dot_nt(wr, h2.astype(BF16)) + br
    eid = lax.broadcasted_iota(jnp.int32, logits.shape, 0).astype(F32)
    vals, idxs = [], []
    for _ in range(TOP_K):
        m = jnp.max(logits, axis=0, keepdims=True)
        idx = jnp.min(jnp.where(logits == m, eid, float(N_EXPERTS)), axis=0, keepdims=True)
        vals.append(m)
        idxs.append(idx)
        logits = jnp.where(eid == idx, -jnp.inf, logits)
    ex = [jnp.exp(v - vals[0]) for v in vals]
    tot = ex[0] + ex[1] + ex[2] + ex[3]
    inv = 1.0 / tot
    gates = [e * inv for e in ex]
    rowid = lax.broadcasted_iota(jnp.int32, (SUBLANES, h2.shape[0]), 0)
    topi = jnp.zeros((SUBLANES, h2.shape[0]), F32)
    topg = jnp.zeros((SUBLANES, h2.shape[0]), F32)
    for k in range(TOP_K):
        topi = jnp.where(rowid == k, idxs[k], topi)
        topg = jnp.where(rowid == k, gates[k], topg)
    return topi.astype(jnp.int32), topg


_QKV_COLS = (_OFF_Q, Q_W + 2 * KV_W)
_LATE_COLS = ((_OFF_GB, 2 * CONV_CH), (_OFF_XC, CONV_CH), (_OFF_BA, D_MODEL), (_OFF_BC, D_MODEL))


def _in_proj(proj, h_bf, w_in_ref, group):
    off, width = group
    proj[off] = _dot(h_bf, w_in_ref[:, off:off + width])


def _cols(proj, off, width):
    for start, piece in proj.items():
        if start <= off and off + width <= start + piece.shape[1]:
            return piece[:, off - start:off - start + width]
    raise KeyError(off)


def _merge_and_route(x, o_bf, y_conv, proj, mod, n2g_ref, w_ao_ref, w_co_ref, w_o_ref,
                     wr_ref, br_ref, before_tail=None):
    sh1, sc1, g1, sh2, sc2, g2 = mod
    a = _dot(o_bf, w_ao_ref[...])
    c = _dot(y_conv.astype(BF16), w_co_ref[...])
    m = _sigmoid(_cols(proj, _OFF_BA, D_MODEL)) * a + _sigmoid(_cols(proj, _OFF_BC, D_MODEL)) * c
    x1 = x + g1 * _dot(m.astype(BF16), w_o_ref[...])
    if before_tail is not None:
        before_tail()
    h2 = _modulated_norm(x1, n2g_ref[...], sh2, sc2)
    topi, topg = _route(h2, wr_ref[...], br_ref[...])
    return x1, h2, topi, topg


def _prompt_body(sinks_ref, x_ref, mod_ref, rope_ref, xn_ref, modn_ref,
                 n1g_ref, n2g_ref, w_in_ref, bd_ref, qg_ref, kg_ref,
                 w_ao_ref, cw_ref, cb_ref, w_co_ref, w_o_ref, wr_ref, br_ref,
                 x1_ref, h2_ref, topi_ref, topg_ref, kwin_ref, vwin_ref, zwin_ref,
                 kbuf, vbuf, zbuf, h_buf, qkv_buf, late_buf, *, tb):
    j = pl.program_id(1)
    nsub = tb // WINDOW

    @pl.when(j == 0)
    def _():
        kbuf[0:WINDOW, :] = jnp.zeros((WINDOW, KV_W), F32)
        vbuf[0:WINDOW, :] = jnp.zeros((WINDOW, KV_W), F32)
        zbuf[0:SUBLANES, :] = jnp.zeros((SUBLANES, CONV_CH), F32)

    def front(xr, mr):
        h = _modulated_norm(xr[0], n1g_ref[...], mr[0, 0:1, :], mr[0, 1:2, :]).astype(BF16)
        h_buf[...] = h
        for group, buf in ((_QKV_COLS, qkv_buf), (_LATE_COLS[0], late_buf)):
            buf[...] = _dot(h, w_in_ref[:, group[0]:group[0] + group[1]])

    @pl.when((pl.program_id(0) == 0) & (j == 0))
    def _():
        front(x_ref, mod_ref)

    x = x_ref[0]
    mod = tuple(mod_ref[0, i:i + 1, :] for i in range(6))
    h_bf = h_buf[...]
    proj = {_QKV_COLS[0]: qkv_buf[...], _LATE_COLS[0][0]: late_buf[...]}
    rope = rope_ref[...]
    bd = bd_ref[...]

    k = _rope(_head_norm(_cols(proj, _OFF_K, KV_W), bd, kg_ref[...]), rope)
    v = _cols(proj, _OFF_V, KV_W)
    kbuf[WINDOW:WINDOW + tb, :] = k
    vbuf[WINDOW:WINDOW + tb, :] = v
    kwin_ref[0] = k[tb - WINDOW:tb]
    vwin_ref[0] = v[tb - WINDOW:tb]
    k_exp = _pair_expand(kbuf[...])
    v_exp = _pair_expand(vbuf[...])

    _in_proj(proj, h_bf, w_in_ref, _LATE_COLS[1])
    qn = _head_norm(_cols(proj, _OFF_Q, Q_W), bd, qg_ref[...])
    q_tiles = []
    for t in range(Q_W // LANES):
        qt = _rope(qn[:, t * LANES:(t + 1) * LANES], rope)
        q_tiles.append((qt * ATTN_SCALE).astype(BF16))

    row = lax.broadcasted_iota(jnp.int32, (WINDOW, 2 * WINDOW), 0)
    col = lax.broadcasted_iota(jnp.int32, (WINDOW, 2 * WINDOW), 1)
    band = (col > row) & (col <= row + WINDOW)
    first_key = jnp.where(j > 0, 0, WINDOW)
    band_first = band & (col >= first_key)
    o_rows = []
    for i in range(nsub):
        for group in _LATE_COLS[2:][i::nsub]:
            _in_proj(proj, h_bf, w_in_ref, group)
        mask = band_first if i == 0 else band
        keys = slice(i * WINDOW, (i + 2) * WINDOW)
        o_tiles = []
        for t in range(Q_W // LANES):
            c = t // 2
            qt = q_tiles[t][i * WINDOW:(i + 1) * WINDOW]
            o_t = None
            for half in range(2):
                s = jnp.where(mask, _dot_nt(qt, k_exp[c][half][keys]), NEG_INF)
                sink = sinks_ref[c * GROUP + (t % 2) * 2 + half]
                part = _softmax_pv([s], [v_exp[c][half][keys]], sink)
                o_t = part if o_t is None else o_t + part
            o_tiles.append(o_t.astype(BF16))
        o_rows.append(jnp.concatenate(o_tiles, axis=1))
    o_bf = jnp.concatenate(o_rows, axis=0) if nsub > 1 else o_rows[0]
    kbuf[0:WINDOW, :] = kbuf[tb:tb + WINDOW, :]
    vbuf[0:WINDOW, :] = vbuf[tb:tb + WINDOW, :]

    z = _cols(proj, _OFF_GC, CONV_CH) * _cols(proj, _OFF_XC, CONV_CH)
    zbuf[SUBLANES:SUBLANES + tb, :] = z
    zwin_ref[0] = z[tb - SUBLANES:tb]
    z1 = zbuf[SUBLANES - 1:SUBLANES - 1 + tb, :]
    z2 = zbuf[SUBLANES - 2:SUBLANES - 2 + tb, :]
    conv = cb_ref[...] + cw_ref[0:1, :] * z2 + cw_ref[1:2, :] * z1 + cw_ref[2:3, :] * z
    y_conv = _cols(proj, _OFF_GB, CONV_CH) * conv
    zbuf[0:SUBLANES, :] = zbuf[tb:tb + SUBLANES, :]

    x1, h2, topi, topg = _merge_and_route(x, o_bf, y_conv, proj, mod, n2g_ref, w_ao_ref, w_co_ref,
                                          w_o_ref, wr_ref, br_ref,
                                          before_tail=lambda: front(xn_ref, modn_ref))
    x1_ref[0] = x1
    h2_ref[0] = h2.astype(BF16)
    topi_ref[0] = topi
    topg_ref[0] = topg


def _prompt_mixer(x, mod, rope, sinks, weights, tb):
    nb, seq, _ = x.shape
    nj = seq // tb
    consts = weights

    def nxt(b, j):
        s = jnp.minimum(b * nj + j + 1, nb * nj - 1)
        return s // nj, s % nj
    in_specs = [
        pl.BlockSpec((1, tb, D_MODEL), lambda b, j, s: (b, j, 0)),
        pl.BlockSpec((1, 6, D_MODEL), lambda b, j, s: (b, 0, 0)),
        pl.BlockSpec((tb, 3 * LANES), lambda b, j, s: (j, 0)),
        pl.BlockSpec((1, tb, D_MODEL), lambda b, j, s: (*nxt(b, j), 0)),
        pl.BlockSpec((1, 6, D_MODEL), lambda b, j, s: (nxt(b, j)[0], 0, 0)),
    ] + [_const_spec(w.shape) for w in consts]
    out_specs = [
        pl.BlockSpec((1, tb, D_MODEL), lambda b, j, s: (b, j, 0)),
        pl.BlockSpec((1, tb, D_MODEL), lambda b, j, s: (b, j, 0)),
        pl.BlockSpec((1, SUBLANES, tb), lambda b, j, s: (b, 0, j)),
        pl.BlockSpec((1, SUBLANES, tb), lambda b, j, s: (b, 0, j)),
        pl.BlockSpec((1, WINDOW, KV_W), lambda b, j, s: (b, 0, 0)),
        pl.BlockSpec((1, WINDOW, KV_W), lambda b, j, s: (b, 0, 0)),
        pl.BlockSpec((1, SUBLANES, CONV_CH), lambda b, j, s: (b, 0, 0)),
    ]
    out_shape = [
        jax.ShapeDtypeStruct((nb, seq, D_MODEL), F32),
        jax.ShapeDtypeStruct((nb, seq, D_MODEL), BF16),
        jax.ShapeDtypeStruct((nb, SUBLANES, seq), jnp.int32),
        jax.ShapeDtypeStruct((nb, SUBLANES, seq), F32),
        jax.ShapeDtypeStruct((nb, WINDOW, KV_W), F32),
        jax.ShapeDtypeStruct((nb, WINDOW, KV_W), F32),
        jax.ShapeDtypeStruct((nb, SUBLANES, CONV_CH), F32),
    ]
    return pl.pallas_call(
        functools.partial(_prompt_body, tb=tb),
        grid_spec=pltpu.PrefetchScalarGridSpec(
            num_scalar_prefetch=1, grid=(nb, seq // tb),
            in_specs=in_specs, out_specs=out_specs,
            scratch_shapes=[pltpu.VMEM((tb + WINDOW, KV_W), F32),
                            pltpu.VMEM((tb + WINDOW, KV_W), F32),
                            pltpu.VMEM((tb + SUBLANES, CONV_CH), F32),
                            pltpu.VMEM((tb, D_MODEL), BF16),
                            pltpu.VMEM((tb, _QKV_COLS[1]), F32),
                            pltpu.VMEM((tb, _LATE_COLS[0][1]), F32)]),
        out_shape=out_shape,
        compiler_params=pltpu.CompilerParams(
            dimension_semantics=("arbitrary", "arbitrary"), vmem_limit_bytes=VMEM_LIMIT),
        name="prompt_mixer",
    )(sinks, x, mod, rope, x, mod, *consts)


def _sample_body(sinks_ref, x_ref, mod_ref, rope_ref, ck_ref, cv_ref, st_ref,
                 n1g_ref, n2g_ref, w_in_ref, bd_ref, qg_ref, kg_ref,
                 w_ao_ref, cw_ref, cb_ref, w_co_ref, w_o_ref, wr_ref, br_ref,
                 x1_ref, h2_ref, topi_ref, topg_ref, knew_ref, vnew_ref, z_ref, *, nseq, dseq):
    rows = nseq * dseq
    wbuf = ck_ref.shape[1]
    x = x_ref[...]
    mod = tuple(mod_ref[i] for i in range(6))
    h_bf = _modulated_norm(x, n1g_ref[...], mod[0], mod[1]).astype(BF16)
    proj = {}
    for group in (_QKV_COLS,) + _LATE_COLS:
        _in_proj(proj, h_bf, w_in_ref, group)
    rope = rope_ref[...]
    bd = bd_ref[...]

    k = _rope(_head_norm(_cols(proj, _OFF_K, KV_W), bd, kg_ref[...]), rope)
    v = _cols(proj, _OFF_V, KV_W)
    for win_ref, cache_ref, new in ((knew_ref, ck_ref, k), (vnew_ref, cv_ref, v)):
        win_ref[:, 0:wbuf - dseq, :] = cache_ref[:, dseq:, :]
        win_ref[:, wbuf - dseq:, :] = new.reshape(nseq, dseq, KV_W)
    kn_exp = _pair_expand(k)
    vn_exp = _pair_expand(v)
    kc_exp = _pair_expand(ck_ref[...].reshape(nseq * wbuf, KV_W))
    vc_exp = _pair_expand(cv_ref[...].reshape(nseq * wbuf, KV_W))

    qrow = lax.broadcasted_iota(jnp.int32, (rows, nseq * wbuf), 0)
    ccol = lax.broadcasted_iota(jnp.int32, (rows, nseq * wbuf), 1)
    mask_c = ((ccol // wbuf) == (qrow // dseq)) & ((ccol % wbuf) > (qrow % dseq) + (wbuf - WINDOW))
    qrow_n = lax.broadcasted_iota(jnp.int32, (rows, rows), 0)
    ncol = lax.broadcasted_iota(jnp.int32, (rows, rows), 1)
    mask_n = ((ncol // dseq) == (qrow_n // dseq)) & ((ncol % dseq) <= (qrow_n % dseq))

    qn = _head_norm(_cols(proj, _OFF_Q, Q_W), bd, qg_ref[...])
    o_tiles = []
    for t in range(Q_W // LANES):
        c = t // 2
        qt = (_rope(qn[:, t * LANES:(t + 1) * LANES], rope) * ATTN_SCALE).astype(BF16)
        o_t = None
        for half in range(2):
            s_c = jnp.where(mask_c, _dot_nt(qt, kc_exp[c][half]), NEG_INF)
            s_n = jnp.where(mask_n, _dot_nt(qt, kn_exp[c][half]), NEG_INF)
            sink = sinks_ref[c * GROUP + (t % 2) * 2 + half]
            part = _softmax_pv([s_c, s_n], [vc_exp[c][half], vn_exp[c][half]], sink)
            o_t = part if o_t is None else o_t + part
        o_tiles.append(o_t.astype(BF16))
    o_bf = jnp.concatenate(o_tiles, axis=1)

    z = _cols(proj, _OFF_GC, CONV_CH) * _cols(proj, _OFF_XC, CONV_CH)
    z_ref[...] = z
    r = lax.broadcasted_iota(jnp.int32, z.shape, 0) % dseq
    st0 = st_ref[0]
    st1 = st_ref[1]
    z1 = jnp.where(r == 0, st1, pltpu.roll(z, 1, 0))
    z2 = jnp.where(r == 0, st0, jnp.where(r == 1, st1, pltpu.roll(z, 2, 0)))
    conv = cb_ref[...] + cw_ref[0:1, :] * z2 + cw_ref[1:2, :] * z1 + cw_ref[2:3, :] * z
    y_conv = _cols(proj, _OFF_GB, CONV_CH) * conv

    x1, h2, topi, topg = _merge_and_route(x, o_bf, y_conv, proj, mod, n2g_ref, w_ao_ref, w_co_ref,
                                          w_o_ref, wr_ref, br_ref)
    x1_ref[...] = x1
    h2_ref[...] = h2.astype(BF16)
    topi_ref[...] = topi
    topg_ref[...] = topg


def _sample_mixer(x, mod, rope, cache_k, cache_v, state, sinks, weights, nseq):
    ntok = x.shape[0]
    nall, wbuf, _ = cache_k.shape
    dseq = ntok // nall
    rows = nseq * dseq
    consts = weights
    in_specs = [
        pl.BlockSpec((rows, D_MODEL), lambda i, s: (i, 0)),
        pl.BlockSpec((6, rows, D_MODEL), lambda i, s: (0, i, 0)),
        pl.BlockSpec((rows, 3 * LANES), lambda i, s: (i, 0)),
        pl.BlockSpec((nseq, wbuf, KV_W), lambda i, s: (i, 0, 0)),
        pl.BlockSpec((nseq, wbuf, KV_W), lambda i, s: (i, 0, 0)),
        pl.BlockSpec((2, rows, CONV_CH), lambda i, s: (0, i, 0)),
    ] + [_const_spec(w.shape) for w in consts]
    out_specs = [
        pl.BlockSpec((rows, D_MODEL), lambda i, s: (i, 0)),
        pl.BlockSpec((rows, D_MODEL), lambda i, s: (i, 0)),
        pl.BlockSpec((SUBLANES, rows), lambda i, s: (0, i)),
        pl.BlockSpec((SUBLANES, rows), lambda i, s: (0, i)),
        pl.BlockSpec((nseq, wbuf, KV_W), lambda i, s: (i, 0, 0)),
        pl.BlockSpec((nseq, wbuf, KV_W), lambda i, s: (i, 0, 0)),
        pl.BlockSpec((rows, CONV_CH), lambda i, s: (i, 0)),
    ]
    out_shape = [
        jax.ShapeDtypeStruct((ntok, D_MODEL), F32),
        jax.ShapeDtypeStruct((ntok, D_MODEL), BF16),
        jax.ShapeDtypeStruct((SUBLANES, ntok), jnp.int32),
        jax.ShapeDtypeStruct((SUBLANES, ntok), F32),
        jax.ShapeDtypeStruct((nall, wbuf, KV_W), F32),
        jax.ShapeDtypeStruct((nall, wbuf, KV_W), F32),
        jax.ShapeDtypeStruct((ntok, CONV_CH), F32),
    ]
    return pl.pallas_call(
        functools.partial(_sample_body, nseq=nseq, dseq=dseq),
        grid_spec=pltpu.PrefetchScalarGridSpec(
            num_scalar_prefetch=1, grid=(ntok // rows,),
            in_specs=in_specs, out_specs=out_specs),
        out_shape=out_shape,
        compiler_params=pltpu.CompilerParams(
            dimension_semantics=("arbitrary",), vmem_limit_bytes=VMEM_LIMIT),
        name="sample_mixer",
    )(sinks, x, mod, rope, cache_k, cache_v, state, *consts)


SORT_TOKENS = 512
RUN_ALIGN = SUBLANES
EXPERT_TILE = 1024
HALF_D = D_MODEL // 2
LOCAL_ROWS = SORT_TOKENS * TOP_K + N_EXPERTS * RUN_ALIGN
UNSORT_CHUNK = 256
assert LOCAL_ROWS % UNSORT_CHUNK == 0
_HI_MASK = 0xFFFF0000


def _pack_halves(x):
    lo = pltpu.bitcast(x[:, :HALF_D], jnp.uint32)
    hi = pltpu.bitcast(x[:, HALF_D:], jnp.uint32)
    return hi | (lo >> 16)


def _unpack_halves(w):
    lo = pltpu.bitcast(w << 16, F32).astype(BF16)
    hi = pltpu.bitcast(w & jnp.uint32(_HI_MASK), F32).astype(BF16)
    return lo, hi


def _local_positions(topi, lstart_col, tri):
    ntok = topi.shape[1]
    eid = lax.broadcasted_iota(jnp.int32, (N_EXPERTS, ntok), 0)
    hits = [eid == topi[k:k + 1, :] for k in range(TOP_K)]
    chosen = jnp.zeros((N_EXPERTS, ntok), F32)
    for h in hits:
        chosen = jnp.where(h, 1.0, chosen)
    base = _dot(chosen.astype(BF16), tri) + lstart_col
    return [jnp.sum(jnp.where(h, base, 0.0), axis=0, keepdims=True) for h in hits]


def _run_copy(loc, glob, sem, lofs, gofs, n, to_global):
    lo = loc.at[pl.ds(pl.multiple_of(lofs, RUN_ALIGN), n)]
    gl = glob.at[pl.ds(pl.multiple_of(gofs, RUN_ALIGN), n)]
    return pltpu.make_async_copy(lo, gl, sem) if to_global else pltpu.make_async_copy(gl, lo, sem)


def _start_runs(meta, blk, loc, glob, sem, to_global):
    lstart_ref, goff_ref, cpad_ref = meta
    for e in range(N_EXPERTS):
        n = pl.multiple_of(cpad_ref[blk * N_EXPERTS + e], RUN_ALIGN)
        _run_copy(loc, glob, sem, lstart_ref[blk * N_EXPERTS + e], goff_ref[blk * N_EXPERTS + e], n,
                  to_global).start()


def _wait_runs(nrows, loc, glob, sem, to_global):
    _run_copy(loc, glob, sem, 0, 0, pl.multiple_of(nrows, RUN_ALIGN), to_global).wait()


def _sort_body(lstart_ref, goff_ref, cpad_ref, ltot_ref, fstart_ref, flen_ref, nact_ref,
               hp_ref, hs_ref, topi_ref, lcol_ref, tri_ref, xs_hbm, pos_ref, buf, zbuf, sems, fsem, *, nblk_p, nblk):
    i = pl.program_id(0)
    slot = i % 2
    meta = (lstart_ref, goff_ref, cpad_ref)

    @pl.when(i == 0)
    def _():
        zbuf[...] = jnp.zeros(zbuf.shape, jnp.uint32)

        def fill(e, tot):
            n = pl.multiple_of(flen_ref[e], RUN_ALIGN)

            @pl.when(n > 0)
            def _():
                _run_copy(zbuf, xs_hbm, fsem, 0, fstart_ref[e], n, True).start()
            return tot + n
        total = lax.fori_loop(0, N_EXPERTS, fill, 0)

        @pl.when(total > 0)
        def _():
            _wait_runs(total, zbuf, xs_hbm, fsem, True)

        def fill_tile(t, carry):
            cp = _run_copy(zbuf, xs_hbm, fsem, 0, t * EXPERT_TILE, EXPERT_TILE, True)
            cp.start()
            cp.wait()
            return carry
        lax.fori_loop(nact_ref[0], xs_hbm.shape[0] // EXPERT_TILE, fill_tile, 0)

    def run(h_ref):
        pos = [p.astype(jnp.int32) for p in _local_positions(topi_ref[...], lcol_ref[...], tri_ref[...])]
        h = h_ref[...]
        for r0 in range(0, LOCAL_ROWS, UNSORT_CHUNK):
            rowid = lax.broadcasted_iota(jnp.int32, (UNSORT_CHUNK, SORT_TOKENS), 0) + r0
            onehot = jnp.zeros((UNSORT_CHUNK, SORT_TOKENS), F32)
            for p in pos:
                onehot = jnp.where(rowid == p, 1.0, onehot)
            buf[slot, r0:r0 + UNSORT_CHUNK, :] = _pack_halves(_dot(onehot.astype(BF16), h))
        krow = lax.broadcasted_iota(jnp.int32, (SUBLANES, SORT_TOKENS), 0)
        pos_rows = jnp.zeros((SUBLANES, SORT_TOKENS), jnp.int32)
        for k in range(TOP_K):
            pos_rows = jnp.where(krow == k, pos[k], pos_rows)
        pos_ref[...] = pos_rows

    @pl.when(i < nblk_p)
    def _():
        run(hp_ref)

    @pl.when(i >= nblk_p)
    def _():
        run(hs_ref)

    _start_runs(meta, i, buf.at[slot], xs_hbm, sems.at[slot], True)

    @pl.when(i > 0)
    def _():
        _wait_runs(ltot_ref[jnp.maximum(i - 1, 0)], buf.at[1 - slot], xs_hbm, sems.at[1 - slot], True)

    @pl.when(i == nblk - 1)
    def _():
        _wait_runs(ltot_ref[i], buf.at[slot], xs_hbm, sems.at[slot], True)


def _sort_tokens(plan, h_p, h_s, topi_all, tri, n_slots):
    nblk_p = h_p.shape[0] // SORT_TOKENS
    nblk = nblk_p + h_s.shape[0] // SORT_TOKENS
    return pl.pallas_call(
        functools.partial(_sort_body, nblk_p=nblk_p, nblk=nblk),
        grid_spec=pltpu.PrefetchScalarGridSpec(
            num_scalar_prefetch=7, grid=(nblk,),
            in_specs=[pl.BlockSpec((SORT_TOKENS, D_MODEL), lambda i, *_: (jnp.minimum(i, nblk_p - 1), 0)),
                      pl.BlockSpec((SORT_TOKENS, D_MODEL), lambda i, *_: (jnp.maximum(i - nblk_p, 0), 0)),
                      pl.BlockSpec((SUBLANES, SORT_TOKENS), lambda i, *_: (0, i)),
                      pl.BlockSpec((None, N_EXPERTS, 1), lambda i, *_: (i, 0, 0)),
                      _const_spec(tri.shape)],
            out_specs=[pl.BlockSpec(memory_space=pl.ANY),
                       pl.BlockSpec((SUBLANES, SORT_TOKENS), lambda i, *_: (0, i))],
            scratch_shapes=[pltpu.VMEM((2, LOCAL_ROWS, HALF_D), jnp.uint32),
                            pltpu.VMEM((EXPERT_TILE, HALF_D), jnp.uint32),
                            pltpu.SemaphoreType.DMA((2,)),
                            pltpu.SemaphoreType.DMA]),
        out_shape=[jax.ShapeDtypeStruct((n_slots, HALF_D), jnp.uint32),
                   jax.ShapeDtypeStruct(topi_all.shape, jnp.int32)],
        compiler_params=pltpu.CompilerParams(
            dimension_semantics=("arbitrary",), vmem_limit_bytes=VMEM_LIMIT),
        name="moe_sort",
    )(plan["lstart"], plan["goff"], plan["cpad"], plan["ltot"], plan["fill_start"], plan["fill_len"], plan["nact"],
      h_p, h_s, topi_all, plan["lstart_col"], tri)


def _experts_body(tile_exp_ref, nact_ref, rows_ref, xs_ref, wg_hbm, bg_ref, wu_hbm, bu_ref, wd_hbm, bd_ref,
                  ys_ref, w_f32, w_bf, sems):
    t = pl.program_id(0)
    nact = nact_ref[0]
    w_hbm = (wg_hbm, wu_hbm, wd_hbm)

    def weight_copies(e, slot):
        return [pltpu.make_async_copy(w.at[e], w_f32.at[slot, j], sems.at[slot, j]) for j, w in enumerate(w_hbm)]

    @pl.when(t < nact)
    def _():
        e = tile_exp_ref[t]
        slot = e % 2

        @pl.when(t == 0)
        def _():
            for cp in weight_copies(e, slot):
                cp.start()

        @pl.when((t == 0) | (e != tile_exp_ref[jnp.maximum(t - 1, 0)]))
        def _():
            @pl.when(e + 1 < N_EXPERTS)
            def _():
                for cp in weight_copies(e + 1, 1 - slot):
                    cp.start()
            for j, cp in enumerate(weight_copies(e, slot)):
                cp.wait()
                w_bf[j] = w_f32[slot, j].astype(BF16)

        def swiglu(rows):
            x = jnp.concatenate(_unpack_halves(xs_ref[0:rows, :]), axis=1)
            g = _dot(x, w_bf[0]) + bg_ref[0]
            u = _dot(x, w_bf[1]) + bu_ref[0]
            g = jnp.minimum(g, SWIGLU_LIMIT)
            u = jnp.clip(u, -SWIGLU_LIMIT, SWIGLU_LIMIT)
            a = g * _sigmoid(SWIGLU_ALPHA * g) * (u + 1.0)
            out = _dot(a.astype(BF16), w_bf[2]) + bd_ref[0]
            ys_ref[0:rows, :] = _pack_halves(out.astype(BF16).astype(F32))

        quarter = EXPERT_TILE // 4
        quarters = (rows_ref[t] + quarter - 1) // quarter
        for q in range(1, 5):
            @pl.when(quarters == q)
            def _(q=q):
                swiglu(q * quarter)
                if q < 4:
                    ys_ref[q * quarter:, :] = jnp.zeros((EXPERT_TILE - q * quarter, HALF_D), jnp.uint32)

    @pl.when(t >= nact)
    def _():
        ys_ref[...] = jnp.zeros(ys_ref.shape, jnp.uint32)


def _experts(tile_exp, nact, tile_rows, xs, wg, bg, wu, bu, wd, bd):
    n_tiles = tile_exp.shape[0]

    def active(t, na):
        return jnp.minimum(t, na[0] - 1)
    wspec = pl.BlockSpec(memory_space=pl.ANY)
    bspec = pl.BlockSpec((1, 1, D_MODEL), lambda t, te, na, tr: (te[active(t, na)], 0, 0))
    xspec = pl.BlockSpec((EXPERT_TILE, HALF_D), lambda t, te, na, tr: (active(t, na), 0))
    return pl.pallas_call(
        _experts_body,
        grid_spec=pltpu.PrefetchScalarGridSpec(
            num_scalar_prefetch=3, grid=(n_tiles,),
            in_specs=[xspec, wspec, bspec, wspec, bspec, wspec, bspec],
            out_specs=pl.BlockSpec((EXPERT_TILE, HALF_D), lambda t, te, na, tr: (t, 0)),
            scratch_shapes=[pltpu.VMEM((2, 3, D_MODEL, D_MODEL), F32),
                            pltpu.VMEM((3, D_MODEL, D_MODEL), BF16),
                            pltpu.SemaphoreType.DMA((2, 3))]),
        out_shape=jax.ShapeDtypeStruct(xs.shape, jnp.uint32),
        compiler_params=pltpu.CompilerParams(
            dimension_semantics=("arbitrary",), vmem_limit_bytes=VMEM_LIMIT),
        name="experts",
    )(tile_exp, nact, tile_rows, xs, wg, bg, wu, bu, wd, bd)


def _unsort_body(lstart_ref, goff_ref, cpad_ref, ltot_ref,
                 ys_hbm, x1_ref, g2_ref, pos_ref, gate_ref, y_ref, buf, sems, *, blk0, nblk):
    i = pl.program_id(0)
    b = blk0 + i
    slot = i % 2
    meta = (lstart_ref, goff_ref, cpad_ref)

    @pl.when(i == 0)
    def _():
        buf[...] = jnp.zeros(buf.shape, jnp.uint32)
        _start_runs(meta, b, buf.at[0], ys_hbm, sems.at[0], False)

    @pl.when(i + 1 < nblk)
    def _():
        _start_runs(meta, b + 1, buf.at[1 - slot], ys_hbm, sems.at[1 - slot], False)

    _wait_runs(ltot_ref[b], buf.at[slot], ys_hbm, sems.at[slot], False)
    pos_cols = pos_ref[...]
    gate_cols = gate_ref[...]
    acc_lo = acc_hi = None
    for c0 in range(0, LOCAL_ROWS, UNSORT_CHUNK):
        colid = lax.broadcasted_iota(jnp.int32, (SORT_TOKENS, UNSORT_CHUNK), 1) + c0
        weights = jnp.zeros((SORT_TOKENS, UNSORT_CHUNK), F32)
        for k in range(TOP_K):
            weights = jnp.where(colid == pos_cols[:, k:k + 1], gate_cols[:, k:k + 1], weights)
        weights = weights.astype(BF16)
        y_lo, y_hi = _unpack_halves(buf[slot, c0:c0 + UNSORT_CHUNK, :])
        acc_lo = _dot(weights, y_lo) if acc_lo is None else acc_lo + _dot(weights, y_lo)
        acc_hi = _dot(weights, y_hi) if acc_hi is None else acc_hi + _dot(weights, y_hi)
    g2 = g2_ref[...]
    y_ref[:, :HALF_D] = x1_ref[:, :HALF_D] + g2[:, :HALF_D] * acc_lo
    y_ref[:, HALF_D:] = x1_ref[:, HALF_D:] + g2[:, HALF_D:] * acc_hi


def _unsort_tokens(plan, ys, x1, g2, g2_spec, pos_cols, gate_cols, blk0):
    nblk = x1.shape[0] // SORT_TOKENS
    return pl.pallas_call(
        functools.partial(_unsort_body, blk0=blk0, nblk=nblk),
        grid_spec=pltpu.PrefetchScalarGridSpec(
            num_scalar_prefetch=4, grid=(nblk,),
            in_specs=[pl.BlockSpec(memory_space=pl.ANY),
                      pl.BlockSpec((SORT_TOKENS, D_MODEL), lambda i, *_: (i, 0)),
                      g2_spec,
                      pl.BlockSpec((SORT_TOKENS, SUBLANES), lambda i, *_: (blk0 + i, 0)),
                      pl.BlockSpec((SORT_TOKENS, SUBLANES), lambda i, *_: (blk0 + i, 0))],
            out_specs=pl.BlockSpec((SORT_TOKENS, D_MODEL), lambda i, *_: (i, 0)),
            scratch_shapes=[pltpu.VMEM((2, LOCAL_ROWS, HALF_D), jnp.uint32),
                            pltpu.SemaphoreType.DMA((2,))]),
        out_shape=jax.ShapeDtypeStruct(x1.shape, F32),
        compiler_params=pltpu.CompilerParams(
            dimension_semantics=("arbitrary",), vmem_limit_bytes=VMEM_LIMIT),
        name="moe_unsort",
    )(plan["lstart"], plan["goff"], plan["cpad"], plan["ltot"],
      ys, x1, g2, pos_cols, gate_cols)


def _rope_table(pos):
    half = ROT_DIM // 2
    inv_freq = ROPE_THETA ** (-jnp.arange(0, ROT_DIM, 2, dtype=F32) / ROT_DIM)
    ang = pos.astype(F32)[:, None] * inv_freq[None, :]
    cos, sin = jnp.cos(ang), jnp.sin(ang)
    n = pos.shape[0]
    rest = HEAD_DIM - ROT_DIM
    c = jnp.concatenate([cos, cos, jnp.ones((n, rest), F32)], axis=1)
    s_next = jnp.concatenate([-sin, jnp.zeros((n, half + rest), F32)], axis=1)
    s_prev = jnp.concatenate([jnp.zeros((n, half), F32), sin, jnp.zeros((n, rest), F32)], axis=1)
    rep = LANES // HEAD_DIM
    return jnp.concatenate([jnp.tile(c, (1, rep)), jnp.tile(s_next, (1, rep)), jnp.tile(s_prev, (1, rep))], axis=1)


def _num_expert_tiles(ntok):
    nblk = ntok // SORT_TOKENS
    worst = ntok * TOP_K + nblk * N_EXPERTS * RUN_ALIGN + N_EXPERTS * (EXPERT_TILE - 1)
    return -(-worst // EXPERT_TILE)


def _routing_plan(topi_all):
    ntok = topi_all.shape[1]
    nblk = ntok // SORT_TOKENS
    n_tiles = _num_expert_tiles(ntok)
    choice = topi_all[:TOP_K].reshape(TOP_K, nblk, SORT_TOKENS)
    experts = jnp.arange(N_EXPERTS, dtype=jnp.int32)
    cnt = (choice[:, :, :, None] == experts).astype(jnp.int32).sum(axis=(0, 2))
    cpad = jnp.maximum(-(-cnt // RUN_ALIGN) * RUN_ALIGN, RUN_ALIGN)
    lstart = jnp.cumsum(cpad, axis=1) - cpad
    tot = cpad.sum(axis=0)
    region = -(-tot // EXPERT_TILE) * EXPERT_TILE
    region_end = jnp.cumsum(region)
    gstart = region_end - region
    goff = gstart[None, :] + jnp.cumsum(cpad, axis=0) - cpad
    tile_start = jnp.arange(n_tiles, dtype=jnp.int32) * EXPERT_TILE
    tile_exp = jnp.minimum((tile_start[:, None] >= region_end[None, :]).astype(jnp.int32).sum(axis=1), N_EXPERTS - 1)
    tile_rows = jnp.clip((gstart + tot)[tile_exp] - tile_start, 0, EXPERT_TILE)
    i32 = lambda a: a.astype(jnp.int32)
    return dict(
        tile_rows=i32(tile_rows),
        lstart=i32(lstart).reshape(-1), goff=i32(goff).reshape(-1), cpad=i32(cpad).reshape(-1),
        ltot=i32(cpad.sum(axis=1)), fill_start=i32(gstart + tot), fill_len=i32(region - tot),
        lstart_col=lstart.astype(F32).reshape(nblk, N_EXPERTS, 1),
        tile_exp=i32(tile_exp), nact=i32(region_end[-1] // EXPERT_TILE).reshape(1))


def kernel(x_prompt, x_sample, cache_k_win, cache_v_win, state_conv, c_prompt, c_sample, w_ada, b_ada, norm1_g, norm2_g, w_in, q_norm_g, k_norm_g, attn_sinks, w_attn_out, conv_w, conv_b, w_conv_out, w_o, w_router, b_router, w_gate, b_gate, w_up, b_up, w_down, b_down):
    nb, seq, _ = x_prompt.shape
    ns, dseq, _ = x_sample.shape
    wbuf = cache_k_win.shape[2]
    assert w_ada.shape[0] == 1, "single-layer step"
    l = 0

    mod = _adaln(jnp.concatenate([c_prompt, c_sample], axis=0), w_ada[l], b_ada[l])
    mod_p = mod[:nb].reshape(nb, 6, D_MODEL)
    mod_s = jnp.repeat(mod[nb:].reshape(ns, 6, D_MODEL).transpose(1, 0, 2), dseq, axis=1)

    bd = jnp.asarray(np.kron(np.eye(N_HEADS), np.ones((HEAD_DIM, HEAD_DIM))), BF16)
    rep = LANES // HEAD_DIM
    weights = (norm1_g[l].reshape(1, D_MODEL), norm2_g[l].reshape(1, D_MODEL), w_in[l].astype(BF16), bd,
               jnp.tile(q_norm_g[l], N_HEADS).reshape(1, Q_W), jnp.tile(k_norm_g[l], N_KV).reshape(1, KV_W),
               w_attn_out[l].astype(BF16), conv_w[l], conv_b[l].reshape(1, CONV_CH), w_conv_out[l].astype(BF16),
               w_o[l].astype(BF16), w_router[l].T.astype(BF16), b_router[l].reshape(N_EXPERTS, 1))
    sinks = attn_sinks[l]

    tb = 512
    (x1_p, h2_p, topi_p, topg_p, kwin_p, vwin_p, zwin_p) = _prompt_mixer(
        x_prompt, mod_p, _rope_table(jnp.arange(seq)), sinks, weights, tb)
    pos_s = jnp.tile(PAST_LEN + jnp.arange(dseq), ns)
    state_rows = jnp.repeat(state_conv[l].transpose(1, 0, 2), dseq, axis=1)
    (x1_s, h2_s, topi_s, topg_s, knew, vnew, z_s) = _sample_mixer(
        x_sample.reshape(ns * dseq, D_MODEL), mod_s, _rope_table(pos_s),
        cache_k_win[l].reshape(ns, wbuf, KV_W), cache_v_win[l].reshape(ns, wbuf, KV_W),
        state_rows, sinks, weights, 16)

    ntok_p = nb * seq
    ntok = ntok_p + ns * dseq
    topi_all = jnp.concatenate([topi_p.transpose(1, 0, 2).reshape(SUBLANES, ntok_p), topi_s], axis=1)
    topg_all = jnp.concatenate([topg_p.transpose(1, 0, 2).reshape(SUBLANES, ntok_p), topg_s], axis=1)
    plan = _routing_plan(topi_all)
    tri = jnp.asarray(np.triu(np.ones((SORT_TOKENS, SORT_TOKENS)), k=1), BF16)
    xs, pos_all = _sort_tokens(plan, h2_p.reshape(ntok_p, D_MODEL), h2_s, topi_all, tri,
                               _num_expert_tiles(ntok) * EXPERT_TILE)
    pos_cols, gate_cols = pos_all.T, topg_all.T
    ys = _experts(plan["tile_exp"], plan["nact"], plan["tile_rows"], xs,
                  w_gate[l], b_gate[l].reshape(N_EXPERTS, 1, D_MODEL),
                  w_up[l], b_up[l].reshape(N_EXPERTS, 1, D_MODEL),
                  w_down[l], b_down[l].reshape(N_EXPERTS, 1, D_MODEL))
    blocks_per_seq = seq // SORT_TOKENS
    y_p = _unsort_tokens(plan, ys, x1_p.reshape(ntok_p, D_MODEL), mod_p[:, 5:6, :],
                         pl.BlockSpec((None, 1, D_MODEL), lambda i, *_: (i // blocks_per_seq, 0, 0)),
                         pos_cols, gate_cols, 0)
    y_s = _unsort_tokens(plan, ys, x1_s, mod_s[5],
                         pl.BlockSpec((SORT_TOKENS, D_MODEL), lambda i, *_: (i, 0)),
                         pos_cols, gate_cols, ntok_p // SORT_TOKENS)

    n_kv_shape = (N_KV, HEAD_DIM)
    k_win_p = kwin_p.reshape(1, nb, WINDOW, *n_kv_shape)
    v_win_p = vwin_p.reshape(1, nb, WINDOW, *n_kv_shape)
    conv_p = zwin_p[:, SUBLANES - (CONV_K - 1):, :][None]
    k_win_s = knew.reshape(1, ns, wbuf, *n_kv_shape)
    v_win_s = vnew.reshape(1, ns, wbuf, *n_kv_shape)
    conv_s = z_s.reshape(ns, dseq, CONV_CH)[:, dseq - (CONV_K - 1):, :][None]
    return (y_p.reshape(nb, seq, D_MODEL), y_s.reshape(ns, dseq, D_MODEL),
            k_win_p, v_win_p, conv_p, k_win_s, v_win_s, conv_s)
```

```python
import functools

import numpy as np
import jax
import jax.numpy as jnp
from jax import lax
from jax.experimental import pallas as pl
from jax.experimental.pallas import tpu as pltpu

D_MODEL = 1024
HEAD_DIM = 64
N_HEADS = 8
N_KV = 2
GROUP = N_HEADS // N_KV
Q_W = N_HEADS * HEAD_DIM
KV_W = N_KV * HEAD_DIM
WINDOW = 128
ROT_DIM = HEAD_DIM // 4
ROPE_THETA = 500000.0
ATTN_SCALE = HEAD_DIM ** -0.5
CONV_CH = D_MODEL // 2
CONV_K = 3
N_EXPERTS = 32
TOP_K = 4
SWIGLU_ALPHA = 1.702
SWIGLU_LIMIT = 7.0
MOE_BLOCK = 128
NORM_EPS = 1e-5
QK_EPS = 1e-6
NEG_INF = -1e30
PAST_LEN = 16384

LANES = 128
SUBLANES = 8
VMEM_LIMIT = 56 * 1024 * 1024

_OFF_Q = 0
_OFF_K = _OFF_Q + Q_W
_OFF_V = _OFF_K + KV_W
_OFF_GB = _OFF_V + KV_W
_OFF_GC = _OFF_GB + CONV_CH
_OFF_XC = _OFF_GC + CONV_CH
_OFF_BA = _OFF_XC + CONV_CH
_OFF_BC = _OFF_BA + D_MODEL
IN_COLS = _OFF_BC + D_MODEL

BF16 = jnp.bfloat16
F32 = jnp.float32
_NT = (((1,), (1,)), ((), ()))


def _dot(a, b):
    return jnp.dot(a, b, preferred_element_type=F32)


def _dot_nt(a, b):
    return lax.dot_general(a, b, _NT, preferred_element_type=F32)


def _sigmoid(x):
    return 1.0 / (1.0 + jnp.exp(-x))


def _split_bf16(x):
    hi = x.astype(BF16)
    lo = (x - hi.astype(F32)).astype(BF16)
    return hi, lo


def _const_spec(shape):
    nd = len(shape)
    return pl.BlockSpec(shape, lambda *_: (0,) * nd, pipeline_mode=pl.Buffered(1))


def _adaln_body(c_ref, w_ref, b_ref, o_ref):
    c = c_ref[...]
    s = c * _sigmoid(c)
    o_ref[...] = _dot(s.astype(BF16), w_ref[...].astype(BF16)) + b_ref[...]


def _adaln(c, w_ada, b_ada):
    n = c.shape[0]
    cols = w_ada.shape[1]
    bn = 1536
    return pl.pallas_call(
        _adaln_body,
        grid=(cols // bn,),
        in_specs=[pl.BlockSpec((n, D_MODEL), lambda i: (0, 0)),
                  pl.BlockSpec((D_MODEL, bn), lambda i: (0, i)),
                  pl.BlockSpec((1, bn), lambda i: (0, i))],
        out_specs=pl.BlockSpec((n, bn), lambda i: (0, i)),
        out_shape=jax.ShapeDtypeStruct((n, cols), F32),
        compiler_params=pltpu.CompilerParams(vmem_limit_bytes=VMEM_LIMIT),
        name="adaln",
    )(c, w_ada, b_ada.reshape(1, cols))


def _modulated_norm(x, g, shift, scale):
    y = x * lax.rsqrt(jnp.mean(x * x, axis=-1, keepdims=True) + NORM_EPS)
    return (y * g) * (1.0 + scale) + shift


def _head_norm(t, bd, g):
    w = t.shape[1]
    sq_hi, sq_lo = _split_bf16(t * t)
    blk = bd[:w, :w]
    ms = (_dot(sq_hi, blk) + _dot(sq_lo, blk)) * (1.0 / HEAD_DIM)
    return t * lax.rsqrt(ms + QK_EPS) * g[:, :w]


def _rope(y, rope):
    c = rope[:, 0:LANES]
    s_next = rope[:, LANES:2 * LANES]
    s_prev = rope[:, 2 * LANES:3 * LANES]
    half = ROT_DIM // 2
    return y * c + pltpu.roll(y, LANES - half, 1) * s_next + pltpu.roll(y, half, 1) * s_prev


def _pair_expand(t):
    lane = lax.broadcasted_iota(jnp.int32, t.shape, 1)
    lo = lane < HEAD_DIM
    r = pltpu.roll(t, HEAD_DIM, 1)
    zero = jnp.zeros_like(t)
    a0 = jnp.where(lo, t, zero).astype(BF16)
    b0 = jnp.where(lo, zero, r).astype(BF16)
    a1 = jnp.where(lo, r, zero).astype(BF16)
    b1 = jnp.where(lo, zero, t).astype(BF16)
    return ((a0, b0), (a1, b1))


def _softmax_pv(s_list, v_list, sink):
    m = jnp.full((s_list[0].shape[0], 1), sink, F32)
    for s in s_list:
        m = jnp.maximum(m, jnp.max(s, axis=-1, keepdims=True))
    den = jnp.exp(sink - m)
    acc = None
    for s, v in zip(s_list, v_list):
        e = jnp.exp(s - m)
        den = den + jnp.sum(e, axis=-1, keepdims=True)
        pv = _dot(e.astype(BF16), v)
        acc = pv if acc is None else acc + pv
    return acc * (1.0 / den)


def _route(h2, wr, br):
    logits = _dot_nt(wr, h2.astype(BF16)) + br
    eid = lax.broadcasted_iota(jnp.int32, logits.shape, 0).astype(F32)
    vals, idxs = [], []
    for _ in range(TOP_K):
        m = jnp.max(logits, axis=0, keepdims=True)
        idx = jnp.min(jnp.where(logits == m, eid, float(N_EXPERTS)), axis=0, keepdims=True)
        vals.append(m)
        idxs.append(idx)
        logits = jnp.where(eid == idx, -jnp.inf, logits)
    ex = [jnp.exp(v - vals[0]) for v in vals]
    tot = ex[0] + ex[1] + ex[2] + ex[3]
    inv = 1.0 / tot
    gates = [e * inv for e in ex]
    rowid = lax.broadcasted_iota(jnp.int32, (SUBLANES, h2.shape[0]), 0)
    topi = jnp.zeros((SUBLANES, h2.shape[0]), F32)
    topg = jnp.zeros((SUBLANES, h2.shape[0]), F32)
    for k in range(TOP_K):
        topi = jnp.where(rowid == k, idxs[k], topi)
        topg = jnp.where(rowid == k, gates[k], topg)
    return topi.astype(jnp.int32), topg


_QKV_COLS = (_OFF_Q, Q_W + 2 * KV_W)
_LATE_COLS = ((_OFF_GB, 2 * CONV_CH), (_OFF_XC, CONV_CH), (_OFF_BA, D_MODEL), (_OFF_BC, D_MODEL))


def _in_proj(proj, h_bf, w_in_ref, group):
    off, width = group
    proj[off] = _dot(h_bf, w_in_ref[:, off:off + width])


def _cols(proj, off, width):
    for start, piece in proj.items():
        if start <= off and off + width <= start + piece.shape[1]:
            return piece[:, off - start:off - start + width]
    raise KeyError(off)


def _merge_and_route(x, o_bf, y_conv, proj, mod, n2g_ref, w_ao_ref, w_co_ref, w_o_ref,
                     wr_ref, br_ref, before_tail=None):
    sh1, sc1, g1, sh2, sc2, g2 = mod
    a = _dot(o_bf, w_ao_ref[...])
    c = _dot(y_conv.astype(BF16), w_co_ref[...])
    m = _sigmoid(_cols(proj, _OFF_BA, D_MODEL)) * a + _sigmoid(_cols(proj, _OFF_BC, D_MODEL)) * c
    x1 = x + g1 * _dot(m.astype(BF16), w_o_ref[...])
    if before_tail is not None:
        before_tail()
    h2 = _modulated_norm(x1, n2g_ref[...], sh2, sc2)
    topi, topg = _route(h2, wr_ref[...], br_ref[...])
    return x1, h2, topi, topg


def _prompt_body(sinks_ref, x_ref, mod_ref, rope_ref, xn_ref, modn_ref,
                 n1g_ref, n2g_ref, w_in_ref, bd_ref, qg_ref, kg_ref,
                 w_ao_ref, cw_ref, cb_ref, w_co_ref, w_o_ref, wr_ref, br_ref,
                 x1_ref, h2_ref, topi_ref, topg_ref, kwin_ref, vwin_ref, zwin_ref,
                 kbuf, vbuf, zbuf, h_buf, qkv_buf, late_buf, *, tb):
    j = pl.program_id(1)
    nsub = tb // WINDOW

    @pl.when(j == 0)
    def _():
        kbuf[0:WINDOW, :] = jnp.zeros((WINDOW, KV_W), F32)
        vbuf[0:WINDOW, :] = jnp.zeros((WINDOW, KV_W), F32)
        zbuf[0:SUBLANES, :] = jnp.zeros((SUBLANES, CONV_CH), F32)

    def front(xr, mr):
        h = _modulated_norm(xr[0], n1g_ref[...], mr[0, 0:1, :], mr[0, 1:2, :]).astype(BF16)
        h_buf[...] = h
        for group, buf in ((_QKV_COLS, qkv_buf), (_LATE_COLS[0], late_buf)):
            buf[...] = _dot(h, w_in_ref[:, group[0]:group[0] + group[1]])

    @pl.when((pl.program_id(0) == 0) & (j == 0))
    def _():
        front(x_ref, mod_ref)

    x = x_ref[0]
    mod = tuple(mod_ref[0, i:i + 1, :] for i in range(6))
    h_bf = h_buf[...]
    proj = {_QKV_COLS[0]: qkv_buf[...], _LATE_COLS[0][0]: late_buf[...]}
    rope = rope_ref[...]
    bd = bd_ref[...]

    k = _rope(_head_norm(_cols(proj, _OFF_K, KV_W), bd, kg_ref[...]), rope)
    v = _cols(proj, _OFF_V, KV_W)
    kbuf[WINDOW:WINDOW + tb, :] = k
    vbuf[WINDOW:WINDOW + tb, :] = v
    kwin_ref[0] = k[tb - WINDOW:tb]
    vwin_ref[0] = v[tb - WINDOW:tb]
    k_exp = _pair_expand(kbuf[...])
    v_exp = _pair_expand(vbuf[...])

    _in_proj(proj, h_bf, w_in_ref, _LATE_COLS[1])
    qn = _head_norm(_cols(proj, _OFF_Q, Q_W), bd, qg_ref[...])
    q_tiles = []
    for t in range(Q_W // LANES):
        qt = _rope(qn[:, t * LANES:(t + 1) * LANES], rope)
        q_tiles.append((qt * ATTN_SCALE).astype(BF16))

    row = lax.broadcasted_iota(jnp.int32, (WINDOW, 2 * WINDOW), 0)
    col = lax.broadcasted_iota(jnp.int32, (WINDOW, 2 * WINDOW), 1)
    band = (col > row) & (col <= row + WINDOW)
    first_key = jnp.where(j > 0, 0, WINDOW)
    band_first = band & (col >= first_key)
    o_rows = []
    for i in range(nsub):
        for group in _LATE_COLS[2:][i::nsub]:
            _in_proj(proj, h_bf, w_in_ref, group)
        mask = band_first if i == 0 else band
        keys = slice(i * WINDOW, (i + 2) * WINDOW)
        o_tiles = []
        for t in range(Q_W // LANES):
            c = t // 2
            qt = q_tiles[t][i * WINDOW:(i + 1) * WINDOW]
            o_t = None
            for half in range(2):
                s = jnp.where(mask, _dot_nt(qt, k_exp[c][half][keys]), NEG_INF)
                sink = sinks_ref[c * GROUP + (t % 2) * 2 + half]
                part = _softmax_pv([s], [v_exp[c][half][keys]], sink)
                o_t = part if o_t is None else o_t + part
            o_tiles.append(o_t.astype(BF16))
        o_rows.append(jnp.concatenate(o_tiles, axis=1))
    o_bf = jnp.concatenate(o_rows, axis=0) if nsub > 1 else o_rows[0]
    kbuf[0:WINDOW, :] = kbuf[tb:tb + WINDOW, :]
    vbuf[0:WINDOW, :] = vbuf[tb:tb + WINDOW, :]

    z = _cols(proj, _OFF_GC, CONV_CH) * _cols(proj, _OFF_XC, CONV_CH)
    zbuf[SUBLANES:SUBLANES + tb, :] = z
    zwin_ref[0] = z[tb - SUBLANES:tb]
    z1 = zbuf[SUBLANES - 1:SUBLANES - 1 + tb, :]
    z2 = zbuf[SUBLANES - 2:SUBLANES - 2 + tb, :]
    conv = cb_ref[...] + cw_ref[0:1, :] * z2 + cw_ref[1:2, :] * z1 + cw_ref[2:3, :] * z
    y_conv = _cols(proj, _OFF_GB, CONV_CH) * conv
    zbuf[0:SUBLANES, :] = zbuf[tb:tb + SUBLANES, :]

    x1, h2, topi, topg = _merge_and_route(x, o_bf, y_conv, proj, mod, n2g_ref, w_ao_ref, w_co_ref,
                                          w_o_ref, wr_ref, br_ref,
                                          before_tail=lambda: front(xn_ref, modn_ref))
    x1_ref[0] = x1
    h2_ref[0] = h2.astype(BF16)
    topi_ref[0] = topi
    topg_ref[0] = topg


def _prompt_mixer(x, mod, rope, sinks, weights, tb):
    nb, seq, _ = x.shape
    nj = seq // tb
    consts = weights

    def nxt(b, j):
        s = jnp.minimum(b * nj + j + 1, nb * nj - 1)
        return s // nj, s % nj
    in_specs = [
        pl.BlockSpec((1, tb, D_MODEL), lambda b, j, s: (b, j, 0)),
        pl.BlockSpec((1, 6, D_MODEL), lambda b, j, s: (b, 0, 0)),
        pl.BlockSpec((tb, 3 * LANES), lambda b, j, s: (j, 0)),
        pl.BlockSpec((1, tb, D_MODEL), lambda b, j, s: (*nxt(b, j), 0)),
        pl.BlockSpec((1, 6, D_MODEL), lambda b, j, s: (nxt(b, j)[0], 0, 0)),
    ] + [_const_spec(w.shape) for w in consts]
    out_specs = [
        pl.BlockSpec((1, tb, D_MODEL), lambda b, j, s: (b, j, 0)),
        pl.BlockSpec((1, tb, D_MODEL), lambda b, j, s: (b, j, 0)),
        pl.BlockSpec((1, SUBLANES, tb), lambda b, j, s: (b, 0, j)),
        pl.BlockSpec((1, SUBLANES, tb), lambda b, j, s: (b, 0, j)),
        pl.BlockSpec((1, WINDOW, KV_W), lambda b, j, s: (b, 0, 0)),
        pl.BlockSpec((1, WINDOW, KV_W), lambda b, j, s: (b, 0, 0)),
        pl.BlockSpec((1, SUBLANES, CONV_CH), lambda b, j, s: (b, 0, 0)),
    ]
    out_shape = [
        jax.ShapeDtypeStruct((nb, seq, D_MODEL), F32),
        jax.ShapeDtypeStruct((nb, seq, D_MODEL), BF16),
        jax.ShapeDtypeStruct((nb, SUBLANES, seq), jnp.int32),
        jax.ShapeDtypeStruct((nb, SUBLANES, seq), F32),
        jax.ShapeDtypeStruct((nb, WINDOW, KV_W), F32),
        jax.ShapeDtypeStruct((nb, WINDOW, KV_W), F32),
        jax.ShapeDtypeStruct((nb, SUBLANES, CONV_CH), F32),
    ]
    return pl.pallas_call(
        functools.partial(_prompt_body, tb=tb),
        grid_spec=pltpu.PrefetchScalarGridSpec(
            num_scalar_prefetch=1, grid=(nb, seq // tb),
            in_specs=in_specs, out_specs=out_specs,
            scratch_shapes=[pltpu.VMEM((tb + WINDOW, KV_W), F32),
                            pltpu.VMEM((tb + WINDOW, KV_W), F32),
                            pltpu.VMEM((tb + SUBLANES, CONV_CH), F32),
                            pltpu.VMEM((tb, D_MODEL), BF16),
                            pltpu.VMEM((tb, _QKV_COLS[1]), F32),
                            pltpu.VMEM((tb, _LATE_COLS[0][1]), F32)]),
        out_shape=out_shape,
        compiler_params=pltpu.CompilerParams(
            dimension_semantics=("arbitrary", "arbitrary"), vmem_limit_bytes=VMEM_LIMIT),
        name="prompt_mixer",
    )(sinks, x, mod, rope, x, mod, *consts)


def _sample_body(sinks_ref, x_ref, mod_ref, rope_ref, ck_ref, cv_ref, st_ref,
                 n1g_ref, n2g_ref, w_in_ref, bd_ref, qg_ref, kg_ref,
                 w_ao_ref, cw_ref, cb_ref, w_co_ref, w_o_ref, wr_ref, br_ref,
                 x1_ref, h2_ref, topi_ref, topg_ref, knew_ref, vnew_ref, z_ref, *, nseq, dseq):
    rows = nseq * dseq
    wbuf = ck_ref.shape[1]
    x = x_ref[...]
    mod = tuple(mod_ref[i] for i in range(6))
    h_bf = _modulated_norm(x, n1g_ref[...], mod[0], mod[1]).astype(BF16)
    proj = {}
    for group in (_QKV_COLS,) + _LATE_COLS:
        _in_proj(proj, h_bf, w_in_ref, group)
    rope = rope_ref[...]
    bd = bd_ref[...]

    k = _rope(_head_norm(_cols(proj, _OFF_K, KV_W), bd, kg_ref[...]), rope)
    v = _cols(proj, _OFF_V, KV_W)
    for win_ref, cache_ref, new in ((knew_ref, ck_ref, k), (vnew_ref, cv_ref, v)):
        win_ref[:, 0:wbuf - dseq, :] = cache_ref[:, dseq:, :]
        win_ref[:, wbuf - dseq:, :] = new.reshape(nseq, dseq, KV_W)
    kn_exp = _pair_expand(k)
    vn_exp = _pair_expand(v)
    kc_exp = _pair_expand(ck_ref[...].reshape(nseq * wbuf, KV_W))
    vc_exp = _pair_expand(cv_ref[...].reshape(nseq * wbuf, KV_W))

    qrow = lax.broadcasted_iota(jnp.int32, (rows, nseq * wbuf), 0)
    ccol = lax.broadcasted_iota(jnp.int32, (rows, nseq * wbuf), 1)
    mask_c = ((ccol // wbuf) == (qrow // dseq)) & ((ccol % wbuf) > (qrow % dseq) + (wbuf - WINDOW))
    qrow_n = lax.broadcasted_iota(jnp.int32, (rows, rows), 0)
    ncol = lax.broadcasted_iota(jnp.int32, (rows, rows), 1)
    mask_n = ((ncol // dseq) == (qrow_n // dseq)) & ((ncol % dseq) <= (qrow_n % dseq))

    qn = _head_norm(_cols(proj, _OFF_Q, Q_W), bd, qg_ref[...])
    o_tiles = []
    for t in range(Q_W // LANES):
        c = t // 2
        qt = (_rope(qn[:, t * LANES:(t + 1) * LANES], rope) * ATTN_SCALE).astype(BF16)
        o_t = None
        for half in range(2):
            s_c = jnp.where(mask_c, _dot_nt(qt, kc_exp[c][half]), NEG_INF)
            s_n = jnp.where(mask_n, _dot_nt(qt, kn_exp[c][half]), NEG_INF)
            sink = sinks_ref[c * GROUP + (t % 2) * 2 + half]
            part = _softmax_pv([s_c, s_n], [vc_exp[c][half], vn_exp[c][half]], sink)
            o_t = part if o_t is None else o_t + part
        o_tiles.append(o_t.astype(BF16))
    o_bf = jnp.concatenate(o_tiles, axis=1)

    z = _cols(proj, _OFF_GC, CONV_CH) * _cols(proj, _OFF_XC, CONV_CH)
    z_ref[...] = z
    r = lax.broadcasted_iota(jnp.int32, z.shape, 0) % dseq
    st0 = st_ref[0]
    st1 = st_ref[1]
    z1 = jnp.where(r == 0, st1, pltpu.roll(z, 1, 0))
    z2 = jnp.where(r == 0, st0, jnp.where(r == 1, st1, pltpu.roll(z, 2, 0)))
    conv = cb_ref[...] + cw_ref[0:1, :] * z2 + cw_ref[1:2, :] * z1 + cw_ref[2:3, :] * z
    y_conv = _cols(proj, _OFF_GB, CONV_CH) * conv

    x1, h2, topi, topg = _merge_and_route(x, o_bf, y_conv, proj, mod, n2g_ref, w_ao_ref, w_co_ref,
                                          w_o_ref, wr_ref, br_ref)
    x1_ref[...] = x1
    h2_ref[...] = h2.astype(BF16)
    topi_ref[...] = topi
    topg_ref[...] = topg


def _sample_mixer(x, mod, rope, cache_k, cache_v, state, sinks, weights, nseq):
    ntok = x.shape[0]
    nall, wbuf, _ = cache_k.shape
    dseq = ntok // nall
    rows = nseq * dseq
    consts = weights
    in_specs = [
        pl.BlockSpec((rows, D_MODEL), lambda i, s: (i, 0)),
        pl.BlockSpec((6, rows, D_MODEL), lambda i, s: (0, i, 0)),
        pl.BlockSpec((rows, 3 * LANES), lambda i, s: (i, 0)),
        pl.BlockSpec((nseq, wbuf, KV_W), lambda i, s: (i, 0, 0)),
        pl.BlockSpec((nseq, wbuf, KV_W), lambda i, s: (i, 0, 0)),
        pl.BlockSpec((2, rows, CONV_CH), lambda i, s: (0, i, 0)),
    ] + [_const_spec(w.shape) for w in consts]
    out_specs = [
        pl.BlockSpec((rows, D_MODEL), lambda i, s: (i, 0)),
        pl.BlockSpec((rows, D_MODEL), lambda i, s: (i, 0)),
        pl.BlockSpec((SUBLANES, rows), lambda i, s: (0, i)),
        pl.BlockSpec((SUBLANES, rows), lambda i, s: (0, i)),
        pl.BlockSpec((nseq, wbuf, KV_W), lambda i, s: (i, 0, 0)),
        pl.BlockSpec((nseq, wbuf, KV_W), lambda i, s: (i, 0, 0)),
        pl.BlockSpec((rows, CONV_CH), lambda i, s: (i, 0)),
    ]
    out_shape = [
        jax.ShapeDtypeStruct((ntok, D_MODEL), F32),
        jax.ShapeDtypeStruct((ntok, D_MODEL), BF16),
        jax.ShapeDtypeStruct((SUBLANES, ntok), jnp.int32),
        jax.ShapeDtypeStruct((SUBLANES, ntok), F32),
        jax.ShapeDtypeStruct((nall, wbuf, KV_W), F32),
        jax.ShapeDtypeStruct((nall, wbuf, KV_W), F32),
        jax.ShapeDtypeStruct((ntok, CONV_CH), F32),
    ]
    return pl.pallas_call(
        functools.partial(_sample_body, nseq=nseq, dseq=dseq),
        grid_spec=pltpu.PrefetchScalarGridSpec(
            num_scalar_prefetch=1, grid=(ntok // rows,),
            in_specs=in_specs, out_specs=out_specs),
        out_shape=out_shape,
        compiler_params=pltpu.CompilerParams(
            dimension_semantics=("arbitrary",), vmem_limit_bytes=VMEM_LIMIT),
        name="sample_mixer",
    )(sinks, x, mod, rope, cache_k, cache_v, state, *consts)


SORT_TOKENS = 512
RUN_ALIGN = SUBLANES
EXPERT_TILE = 1024
HALF_D = D_MODEL // 2
LOCAL_ROWS = SORT_TOKENS * TOP_K + N_EXPERTS * RUN_ALIGN
UNSORT_CHUNK = 256
assert LOCAL_ROWS % UNSORT_CHUNK == 0
_HI_MASK = 0xFFFF0000


def _pack_halves(x):
    lo = pltpu.bitcast(x[:, :HALF_D], jnp.uint32)
    hi = pltpu.bitcast(x[:, HALF_D:], jnp.uint32)
    return hi | (lo >> 16)


def _unpack_halves(w):
    lo = pltpu.bitcast(w << 16, F32).astype(BF16)
    hi = pltpu.bitcast(w & jnp.uint32(_HI_MASK), F32).astype(BF16)
    return lo, hi


def _local_positions(topi, lstart_col, tri):
    ntok = topi.shape[1]
    eid = lax.broadcasted_iota(jnp.int32, (N_EXPERTS, ntok), 0)
    hits = [eid == topi[k:k + 1, :] for k in range(TOP_K)]
    chosen = jnp.zeros((N_EXPERTS, ntok), F32)
    for h in hits:
        chosen = jnp.where(h, 1.0, chosen)
    base = _dot(chosen.astype(BF16), tri) + lstart_col
    return [jnp.sum(jnp.where(h, base, 0.0), axis=0, keepdims=True) for h in hits]


def _run_copy(loc, glob, sem, lofs, gofs, n, to_global):
    lo = loc.at[pl.ds(pl.multiple_of(lofs, RUN_ALIGN), n)]
    gl = glob.at[pl.ds(pl.multiple_of(gofs, RUN_ALIGN), n)]
    return pltpu.make_async_copy(lo, gl, sem) if to_global else pltpu.make_async_copy(gl, lo, sem)


def _start_runs(meta, blk, loc, glob, sem, to_global):
    lstart_ref, goff_ref, cpad_ref = meta
    for e in range(N_EXPERTS):
        n = pl.multiple_of(cpad_ref[blk * N_EXPERTS + e], RUN_ALIGN)
        _run_copy(loc, glob, sem, lstart_ref[blk * N_EXPERTS + e], goff_ref[blk * N_EXPERTS + e], n,
                  to_global).start()


def _wait_runs(nrows, loc, glob, sem, to_global):
    _run_copy(loc, glob, sem, 0, 0, pl.multiple_of(nrows, RUN_ALIGN), to_global).wait()


def _sort_body(lstart_ref, goff_ref, cpad_ref, ltot_ref, fstart_ref, flen_ref, nact_ref,
               hp_ref, hs_ref, topi_ref, lcol_ref, tri_ref, xs_hbm, pos_ref, buf, zbuf, sems, fsem, *, nblk_p, nblk):
    i = pl.program_id(0)
    slot = i % 2
    meta = (lstart_ref, goff_ref, cpad_ref)

    @pl.when(i == 0)
    def _():
        zbuf[...] = jnp.zeros(zbuf.shape, jnp.uint32)

        def fill(e, tot):
            n = pl.multiple_of(flen_ref[e], RUN_ALIGN)

            @pl.when(n > 0)
            def _():
                _run_copy(zbuf, xs_hbm, fsem, 0, fstart_ref[e], n, True).start()
            return tot + n
        total = lax.fori_loop(0, N_EXPERTS, fill, 0)

        @pl.when(total > 0)
        def _():
            _wait_runs(total, zbuf, xs_hbm, fsem, True)

        def fill_tile(t, carry):
            cp = _run_copy(zbuf, xs_hbm, fsem, 0, t * EXPERT_TILE, EXPERT_TILE, True)
            cp.start()
            cp.wait()
            return carry
        lax.fori_loop(nact_ref[0], xs_hbm.shape[0] // EXPERT_TILE, fill_tile, 0)

    def run(h_ref):
        pos = [p.astype(jnp.int32) for p in _local_positions(topi_ref[...], lcol_ref[...], tri_ref[...])]
        h = h_ref[...]
        for r0 in range(0, LOCAL_ROWS, UNSORT_CHUNK):
            rowid = lax.broadcasted_iota(jnp.int32, (UNSORT_CHUNK, SORT_TOKENS), 0) + r0
            onehot = jnp.zeros((UNSORT_CHUNK, SORT_TOKENS), F32)
            for p in pos:
                onehot = jnp.where(rowid == p, 1.0, onehot)
            buf[slot, r0:r0 + UNSORT_CHUNK, :] = _pack_halves(_dot(onehot.astype(BF16), h))
        krow = lax.broadcasted_iota(jnp.int32, (SUBLANES, SORT_TOKENS), 0)
        pos_rows = jnp.zeros((SUBLANES, SORT_TOKENS), jnp.int32)
        for k in range(TOP_K):
            pos_rows = jnp.where(krow == k, pos[k], pos_rows)
        pos_ref[...] = pos_rows

    @pl.when(i < nblk_p)
    def _():
        run(hp_ref)

    @pl.when(i >= nblk_p)
    def _():
        run(hs_ref)

    _start_runs(meta, i, buf.at[slot], xs_hbm, sems.at[slot], True)

    @pl.when(i > 0)
    def _():
        _wait_runs(ltot_ref[jnp.maximum(i - 1, 0)], buf.at[1 - slot], xs_hbm, sems.at[1 - slot], True)

    @pl.when(i == nblk - 1)
    def _():
        _wait_runs(ltot_ref[i], buf.at[slot], xs_hbm, sems.at[slot], True)


def _sort_tokens(plan, h_p, h_s, topi_all, tri, n_slots):
    nblk_p = h_p.shape[0] // SORT_TOKENS
    nblk = nblk_p + h_s.shape[0] // SORT_TOKENS
    return pl.pallas_call(
        functools.partial(_sort_body, nblk_p=nblk_p, nblk=nblk),
        grid_spec=pltpu.PrefetchScalarGridSpec(
            num_scalar_prefetch=7, grid=(nblk,),
            in_specs=[pl.BlockSpec((SORT_TOKENS, D_MODEL), lambda i, *_: (jnp.minimum(i, nblk_p - 1), 0)),
                      pl.BlockSpec((SORT_TOKENS, D_MODEL), lambda i, *_: (jnp.maximum(i - nblk_p, 0), 0)),
                      pl.BlockSpec((SUBLANES, SORT_TOKENS), lambda i, *_: (0, i)),
                      pl.BlockSpec((None, N_EXPERTS, 1), lambda i, *_: (i, 0, 0)),
                      _const_spec(tri.shape)],
            out_specs=[pl.BlockSpec(memory_space=pl.ANY),
                       pl.BlockSpec((SUBLANES, SORT_TOKENS), lambda i, *_: (0, i))],
            scratch_shapes=[pltpu.VMEM((2, LOCAL_ROWS, HALF_D), jnp.uint32),
                            pltpu.VMEM((EXPERT_TILE, HALF_D), jnp.uint32),
                            pltpu.SemaphoreType.DMA((2,)),
                            pltpu.SemaphoreType.DMA]),
        out_shape=[jax.ShapeDtypeStruct((n_slots, HALF_D), jnp.uint32),
                   jax.ShapeDtypeStruct(topi_all.shape, jnp.int32)],
        compiler_params=pltpu.CompilerParams(
            dimension_semantics=("arbitrary",), vmem_limit_bytes=VMEM_LIMIT),
        name="moe_sort",
    )(plan["lstart"], plan["goff"], plan["cpad"], plan["ltot"], plan["fill_start"], plan["fill_len"], plan["nact"],
      h_p, h_s, topi_all, plan["lstart_col"], tri)


def _experts_body(tile_exp_ref, nact_ref, rows_ref, xs_ref, wg_hbm, bg_ref, wu_hbm, bu_ref, wd_hbm, bd_ref,
                  ys_ref, w_f32, w_bf, sems):
    t = pl.program_id(0)
    nact = nact_ref[0]
    w_hbm = (wg_hbm, wu_hbm, wd_hbm)

    def weight_copies(e, slot):
        return [pltpu.make_async_copy(w.at[e], w_f32.at[slot, j], sems.at[slot, j]) for j, w in enumerate(w_hbm)]

    @pl.when(t < nact)
    def _():
        e = tile_exp_ref[t]
        slot = e % 2

        @pl.when(t == 0)
        def _():
            for cp in weight_copies(e, slot):
                cp.start()

        @pl.when((t == 0) | (e != tile_exp_ref[jnp.maximum(t - 1, 0)]))
        def _():
            @pl.when(e + 1 < N_EXPERTS)
            def _():
                for cp in weight_copies(e + 1, 1 - slot):
                    cp.start()
            for j, cp in enumerate(weight_copies(e, slot)):
                cp.wait()
                w_bf[j] = w_f32[slot, j].astype(BF16)

        def swiglu(rows):
            x = jnp.concatenate(_unpack_halves(xs_ref[0:rows, :]), axis=1)
            g = _dot(x, w_bf[0]) + bg_ref[0]
            u = _dot(x, w_bf[1]) + bu_ref[0]
            g = jnp.minimum(g, SWIGLU_LIMIT)
            u = jnp.clip(u, -SWIGLU_LIMIT, SWIGLU_LIMIT)
            a = g * _sigmoid(SWIGLU_ALPHA * g) * (u + 1.0)
            out = _dot(a.astype(BF16), w_bf[2]) + bd_ref[0]
            ys_ref[0:rows, :] = _pack_halves(out.astype(BF16).astype(F32))

        quarter = EXPERT_TILE // 4
        quarters = (rows_ref[t] + quarter - 1) // quarter
        for q in range(1, 5):
            @pl.when(quarters == q)
            def _(q=q):
                swiglu(q * quarter)
                if q < 4:
                    ys_ref[q * quarter:, :] = jnp.zeros((EXPERT_TILE - q * quarter, HALF_D), jnp.uint32)

    @pl.when(t >= nact)
    def _():
        ys_ref[...] = jnp.zeros(ys_ref.shape, jnp.uint32)


def _experts(tile_exp, nact, tile_rows, xs, wg, bg, wu, bu, wd, bd):
    n_tiles = tile_exp.shape[0]

    def active(t, na):
        return jnp.minimum(t, na[0] - 1)
    wspec = pl.BlockSpec(memory_space=pl.ANY)
    bspec = pl.BlockSpec((1, 1, D_MODEL), lambda t, te, na, tr: (te[active(t, na)], 0, 0))
    xspec = pl.BlockSpec((EXPERT_TILE, HALF_D), lambda t, te, na, tr: (active(t, na), 0))
    return pl.pallas_call(
        _experts_body,
        grid_spec=pltpu.PrefetchScalarGridSpec(
            num_scalar_prefetch=3, grid=(n_tiles,),
            in_specs=[xspec, wspec, bspec, wspec, bspec, wspec, bspec],
            out_specs=pl.BlockSpec((EXPERT_TILE, HALF_D), lambda t, te, na, tr: (t, 0)),
            scratch_shapes=[pltpu.VMEM((2, 3, D_MODEL, D_MODEL), F32),
                            pltpu.VMEM((3, D_MODEL, D_MODEL), BF16),
                            pltpu.SemaphoreType.DMA((2, 3))]),
        out_shape=jax.ShapeDtypeStruct(xs.shape, jnp.uint32),
        compiler_params=pltpu.CompilerParams(
            dimension_semantics=("arbitrary",), vmem_limit_bytes=VMEM_LIMIT),
        name="experts",
    )(tile_exp, nact, tile_rows, xs, wg, bg, wu, bu, wd, bd)


def _unsort_body(lstart_ref, goff_ref, cpad_ref, ltot_ref,
                 ys_hbm, x1_ref, g2_ref, pos_ref, gate_ref, y_ref, buf, sems, *, blk0, nblk):
    i = pl.program_id(0)
    b = blk0 + i
    slot = i % 2
    meta = (lstart_ref, goff_ref, cpad_ref)

    @pl.when(i == 0)
    def _():
        buf[...] = jnp.zeros(buf.shape, jnp.uint32)
        _start_runs(meta, b, buf.at[0], ys_hbm, sems.at[0], False)

    @pl.when(i + 1 < nblk)
    def _():
        _start_runs(meta, b + 1, buf.at[1 - slot], ys_hbm, sems.at[1 - slot], False)

    _wait_runs(ltot_ref[b], buf.at[slot], ys_hbm, sems.at[slot], False)
    pos_rows = pos_ref[...]
    gate_rows = gate_ref[...]
    tn = (((0,), (0,)), ((), ()))
    acc_lo = acc_hi = None
    for c0 in range(0, LOCAL_ROWS, UNSORT_CHUNK):
        rowid = lax.broadcasted_iota(jnp.int32, (UNSORT_CHUNK, SORT_TOKENS), 0) + c0
        weights = jnp.zeros((UNSORT_CHUNK, SORT_TOKENS), F32)
        for k in range(TOP_K):
            weights = jnp.where(rowid == pos_rows[k:k + 1, :], gate_rows[k:k + 1, :], weights)
        weights = weights.astype(BF16)
        y_lo, y_hi = _unpack_halves(buf[slot, c0:c0 + UNSORT_CHUNK, :])
        d_lo = lax.dot_general(weights, y_lo, tn, preferred_element_type=F32)
        d_hi = lax.dot_general(weights, y_hi, tn, preferred_element_type=F32)
        acc_lo = d_lo if acc_lo is None else acc_lo + d_lo
        acc_hi = d_hi if acc_hi is None else acc_hi + d_hi
    g2 = g2_ref[...]
    y_ref[:, :HALF_D] = x1_ref[:, :HALF_D] + g2[:, :HALF_D] * acc_lo
    y_ref[:, HALF_D:] = x1_ref[:, HALF_D:] + g2[:, HALF_D:] * acc_hi


def _unsort_tokens(plan, ys, x1, g2, g2_spec, pos_cols, gate_cols, blk0):
    nblk = x1.shape[0] // SORT_TOKENS
    return pl.pallas_call(
        functools.partial(_unsort_body, blk0=blk0, nblk=nblk),
        grid_spec=pltpu.PrefetchScalarGridSpec(
            num_scalar_prefetch=4, grid=(nblk,),
            in_specs=[pl.BlockSpec(memory_space=pl.ANY),
                      pl.BlockSpec((SORT_TOKENS, D_MODEL), lambda i, *_: (i, 0)),
                      g2_spec,
                      pl.BlockSpec((SUBLANES, SORT_TOKENS), lambda i, *_: (0, blk0 + i)),
                      pl.BlockSpec((SUBLANES, SORT_TOKENS), lambda i, *_: (0, blk0 + i))],
            out_specs=pl.BlockSpec((SORT_TOKENS, D_MODEL), lambda i, *_: (i, 0)),
            scratch_shapes=[pltpu.VMEM((2, LOCAL_ROWS, HALF_D), jnp.uint32),
                            pltpu.SemaphoreType.DMA((2,))]),
        out_shape=jax.ShapeDtypeStruct(x1.shape, F32),
        compiler_params=pltpu.CompilerParams(
            dimension_semantics=("arbitrary",), vmem_limit_bytes=VMEM_LIMIT),
        name="moe_unsort",
    )(plan["lstart"], plan["goff"], plan["cpad"], plan["ltot"],
      ys, x1, g2, pos_cols, gate_cols)


def _rope_table(pos):
    half = ROT_DIM // 2
    inv_freq = ROPE_THETA ** (-jnp.arange(0, ROT_DIM, 2, dtype=F32) / ROT_DIM)
    ang = pos.astype(F32)[:, None] * inv_freq[None, :]
    cos, sin = jnp.cos(ang), jnp.sin(ang)
    n = pos.shape[0]
    rest = HEAD_DIM - ROT_DIM
    c = jnp.concatenate([cos, cos, jnp.ones((n, rest), F32)], axis=1)
    s_next = jnp.concatenate([-sin, jnp.zeros((n, half + rest), F32)], axis=1)
    s_prev = jnp.concatenate([jnp.zeros((n, half), F32), sin, jnp.zeros((n, rest), F32)], axis=1)
    rep = LANES // HEAD_DIM
    return jnp.concatenate([jnp.tile(c, (1, rep)), jnp.tile(s_next, (1, rep)), jnp.tile(s_prev, (1, rep))], axis=1)


def _num_expert_tiles(ntok):
    nblk = ntok // SORT_TOKENS
    worst = ntok * TOP_K + nblk * N_EXPERTS * RUN_ALIGN + N_EXPERTS * (EXPERT_TILE - 1)
    return -(-worst // EXPERT_TILE)


def _routing_plan(topi_all):
    ntok = topi_all.shape[1]
    nblk = ntok // SORT_TOKENS
    n_tiles = _num_expert_tiles(ntok)
    choice = topi_all[:TOP_K].reshape(TOP_K, nblk, SORT_TOKENS)
    experts = jnp.arange(N_EXPERTS, dtype=jnp.int32)
    cnt = (choice[:, :, :, None] == experts).astype(jnp.int32).sum(axis=(0, 2))
    cpad = jnp.maximum(-(-cnt // RUN_ALIGN) * RUN_ALIGN, RUN_ALIGN)
    lstart = jnp.cumsum(cpad, axis=1) - cpad
    tot = cpad.sum(axis=0)
    region = -(-tot // EXPERT_TILE) * EXPERT_TILE
    region_end = jnp.cumsum(region)
    gstart = region_end - region
    goff = gstart[None, :] + jnp.cumsum(cpad, axis=0) - cpad
    tile_start = jnp.arange(n_tiles, dtype=jnp.int32) * EXPERT_TILE
    tile_exp = jnp.minimum((tile_start[:, None] >= region_end[None, :]).astype(jnp.int32).sum(axis=1), N_EXPERTS - 1)
    tile_rows = jnp.clip((gstart + tot)[tile_exp] - tile_start, 0, EXPERT_TILE)
    i32 = lambda a: a.astype(jnp.int32)
    return dict(
        tile_rows=i32(tile_rows),
        lstart=i32(lstart).reshape(-1), goff=i32(goff).reshape(-1), cpad=i32(cpad).reshape(-1),
        ltot=i32(cpad.sum(axis=1)), fill_start=i32(gstart + tot), fill_len=i32(region - tot),
        lstart_col=lstart.astype(F32).reshape(nblk, N_EXPERTS, 1),
        tile_exp=i32(tile_exp), nact=i32(region_end[-1] // EXPERT_TILE).reshape(1))


def kernel(x_prompt, x_sample, cache_k_win, cache_v_win, state_conv, c_prompt, c_sample, w_ada, b_ada, norm1_g, norm2_g, w_in, q_norm_g, k_norm_g, attn_sinks, w_attn_out, conv_w, conv_b, w_conv_out, w_o, w_router, b_router, w_gate, b_gate, w_up, b_up, w_down, b_down):
    nb, seq, _ = x_prompt.shape
    ns, dseq, _ = x_sample.shape
    wbuf = cache_k_win.shape[2]
    assert w_ada.shape[0] == 1, "single-layer step"
    l = 0

    mod = _adaln(jnp.concatenate([c_prompt, c_sample], axis=0), w_ada[l], b_ada[l])
    mod_p = mod[:nb].reshape(nb, 6, D_MODEL)
    mod_s = jnp.repeat(mod[nb:].reshape(ns, 6, D_MODEL).transpose(1, 0, 2), dseq, axis=1)

    bd = jnp.asarray(np.kron(np.eye(N_HEADS), np.ones((HEAD_DIM, HEAD_DIM))), BF16)
    rep = LANES // HEAD_DIM
    weights = (norm1_g[l].reshape(1, D_MODEL), norm2_g[l].reshape(1, D_MODEL), w_in[l].astype(BF16), bd,
               jnp.tile(q_norm_g[l], N_HEADS).reshape(1, Q_W), jnp.tile(k_norm_g[l], N_KV).reshape(1, KV_W),
               w_attn_out[l].astype(BF16), conv_w[l], conv_b[l].reshape(1, CONV_CH), w_conv_out[l].astype(BF16),
               w_o[l].astype(BF16), w_router[l].T.astype(BF16), b_router[l].reshape(N_EXPERTS, 1))
    sinks = attn_sinks[l]

    tb = 512
    (x1_p, h2_p, topi_p, topg_p, kwin_p, vwin_p, zwin_p) = _prompt_mixer(
        x_prompt, mod_p, _rope_table(jnp.arange(seq)), sinks, weights, tb)
    pos_s = jnp.tile(PAST_LEN + jnp.arange(dseq), ns)
    state_rows = jnp.repeat(state_conv[l].transpose(1, 0, 2), dseq, axis=1)
    (x1_s, h2_s, topi_s, topg_s, knew, vnew, z_s) = _sample_mixer(
        x_sample.reshape(ns * dseq, D_MODEL), mod_s, _rope_table(pos_s),
        cache_k_win[l].reshape(ns, wbuf, KV_W), cache_v_win[l].reshape(ns, wbuf, KV_W),
        state_rows, sinks, weights, 16)

    ntok_p = nb * seq
    ntok = ntok_p + ns * dseq
    topi_all = jnp.concatenate([topi_p.transpose(1, 0, 2).reshape(SUBLANES, ntok_p), topi_s], axis=1)
    topg_all = jnp.concatenate([topg_p.transpose(1, 0, 2).reshape(SUBLANES, ntok_p), topg_s], axis=1)
    plan = _routing_plan(topi_all)
    tri = jnp.asarray(np.triu(np.ones((SORT_TOKENS, SORT_TOKENS)), k=1), BF16)
    xs, pos_all = _sort_tokens(plan, h2_p.reshape(ntok_p, D_MODEL), h2_s, topi_all, tri,
                               _num_expert_tiles(ntok) * EXPERT_TILE)
    pos_cols, gate_cols = pos_all, topg_all
    ys = _experts(plan["tile_exp"], plan["nact"], plan["tile_rows"], xs,
                  w_gate[l], b_gate[l].reshape(N_EXPERTS, 1, D_MODEL),
                  w_up[l], b_up[l].reshape(N_EXPERTS, 1, D_MODEL),
                  w_down[l], b_down[l].reshape(N_EXPERTS, 1, D_MODEL))
    blocks_per_seq = seq // SORT_TOKENS
    y_p = _unsort_tokens(plan, ys, x1_p.reshape(ntok_p, D_MODEL), mod_p[:, 5:6, :],
                         pl.BlockSpec((None, 1, D_MODEL), lambda i, *_: (i // blocks_per_seq, 0, 0)),
                         pos_cols, gate_cols, 0)
    y_s = _unsort_tokens(plan, ys, x1_s, mod_s[5],
                         pl.BlockSpec((SORT_TOKENS, D_MODEL), lambda i, *_: (i, 0)),
                         pos_cols, gate_cols, ntok_p // SORT_TOKENS)

    n_kv_shape = (N_KV, HEAD_DIM)
    k_win_p = kwin_p.reshape(1, nb, WINDOW, *n_kv_shape)
    v_win_p = vwin_p.reshape(1, nb, WINDOW, *n_kv_shape)
    conv_p = zwin_p[:, SUBLANES - (CONV_K - 1):, :][None]
    k_win_s = knew.reshape(1, ns, wbuf, *n_kv_shape)
    v_win_s = vnew.reshape(1, ns, wbuf, *n_kv_shape)
    conv_s = z_s.reshape(ns, dseq, CONV_CH)[:, dseq - (CONV_K - 1):, :][None]
    return (y_p.reshape(nb, seq, D_MODEL), y_s.reshape(ns, dseq, D_MODEL),
            k_win_p, v_win_p, conv_p, k_win_s, v_win_s, conv_s)
```

```python
import functools

import numpy as np
import jax
import jax.numpy as jnp
from jax import lax
from jax.experimental import pallas as pl
from jax.experimental.pallas import tpu as pltpu

D_MODEL = 1024
HEAD_DIM = 64
N_HEADS = 8
N_KV = 2
GROUP = N_HEADS // N_KV
Q_W = N_HEADS * HEAD_DIM
KV_W = N_KV * HEAD_DIM
WINDOW = 128
ROT_DIM = HEAD_DIM // 4
ROPE_THETA = 500000.0
ATTN_SCALE = HEAD_DIM ** -0.5
CONV_CH = D_MODEL // 2
CONV_K = 3
N_EXPERTS = 32
TOP_K = 4
SWIGLU_ALPHA = 1.702
SWIGLU_LIMIT = 7.0
MOE_BLOCK = 128
NORM_EPS = 1e-5
QK_EPS = 1e-6
NEG_INF = -1e30
PAST_LEN = 16384

LANES = 128
SUBLANES = 8
VMEM_LIMIT = 56 * 1024 * 1024

_OFF_Q = 0
_OFF_K = _OFF_Q + Q_W
_OFF_V = _OFF_K + KV_W
_OFF_GB = _OFF_V + KV_W
_OFF_GC = _OFF_GB + CONV_CH
_OFF_XC = _OFF_GC + CONV_CH
_OFF_BA = _OFF_XC + CONV_CH
_OFF_BC = _OFF_BA + D_MODEL
IN_COLS = _OFF_BC + D_MODEL

BF16 = jnp.bfloat16
F32 = jnp.float32
_NT = (((1,), (1,)), ((), ()))


def _dot(a, b):
    return jnp.dot(a, b, preferred_element_type=F32)


def _dot_nt(a, b):
    return lax.dot_general(a, b, _NT, preferred_element_type=F32)


def _sigmoid(x):
    return 1.0 / (1.0 + jnp.exp(-x))


def _split_bf16(x):
    hi = x.astype(BF16)
    lo = (x - hi.astype(F32)).astype(BF16)
    return hi, lo


def _const_spec(shape):
    nd = len(shape)
    return pl.BlockSpec(shape, lambda *_: (0,) * nd, pipeline_mode=pl.Buffered(1))


def _adaln_body(c_ref, w_ref, b_ref, o_ref):
    c = c_ref[...]
    s = c * _sigmoid(c)
    o_ref[...] = _dot(s.astype(BF16), w_ref[...].astype(BF16)) + b_ref[...]


def _adaln(c, w_ada, b_ada):
    n = c.shape[0]
    cols = w_ada.shape[1]
    bn = 1536
    return pl.pallas_call(
        _adaln_body,
        grid=(cols // bn,),
        in_specs=[pl.BlockSpec((n, D_MODEL), lambda i: (0, 0)),
                  pl.BlockSpec((D_MODEL, bn), lambda i: (0, i)),
                  pl.BlockSpec((1, bn), lambda i: (0, i))],
        out_specs=pl.BlockSpec((n, bn), lambda i: (0, i)),
        out_shape=jax.ShapeDtypeStruct((n, cols), F32),
        compiler_params=pltpu.CompilerParams(vmem_limit_bytes=VMEM_LIMIT),
        name="adaln",
    )(c, w_ada, b_ada.reshape(1, cols))


def _expand_rows(v, rep):
    n, _, w = v.shape
    return jnp.broadcast_to(v, (n, rep, w)).reshape(n * rep, w)


def _modulated_norm(x, g, shift, scale):
    y = x * lax.rsqrt(jnp.mean(x * x, axis=-1, keepdims=True) + NORM_EPS)
    return (y * g) * (1.0 + scale) + shift


def _head_norm(t, bd, g):
    w = t.shape[1]
    sq_hi, sq_lo = _split_bf16(t * t)
    blk = bd[:w, :w]
    ms = (_dot(sq_hi, blk) + _dot(sq_lo, blk)) * (1.0 / HEAD_DIM)
    return t * lax.rsqrt(ms + QK_EPS) * g[:, :w]


def _rope(y, rope):
    c = rope[:, 0:LANES]
    s_next = rope[:, LANES:2 * LANES]
    s_prev = rope[:, 2 * LANES:3 * LANES]
    half = ROT_DIM // 2
    return y * c + pltpu.roll(y, LANES - half, 1) * s_next + pltpu.roll(y, half, 1) * s_prev


def _pair_expand(t):
    lane = lax.broadcasted_iota(jnp.int32, t.shape, 1)
    lo = lane < HEAD_DIM
    r = pltpu.roll(t, HEAD_DIM, 1)
    zero = jnp.zeros_like(t)
    a0 = jnp.where(lo, t, zero).astype(BF16)
    b0 = jnp.where(lo, zero, r).astype(BF16)
    a1 = jnp.where(lo, r, zero).astype(BF16)
    b1 = jnp.where(lo, zero, t).astype(BF16)
    return ((a0, b0), (a1, b1))


def _softmax_pv(s_list, v_list, sink):
    m = jnp.full((s_list[0].shape[0], 1), sink, F32)
    for s in s_list:
        m = jnp.maximum(m, jnp.max(s, axis=-1, keepdims=True))
    den = jnp.exp(sink - m)
    acc = None
    for s, v in zip(s_list, v_list):
        e = jnp.exp(s - m)
        den = den + jnp.sum(e, axis=-1, keepdims=True)
        pv = _dot(e.astype(BF16), v)
        acc = pv if acc is None else acc + pv
    return acc * (1.0 / den)


def _route(h2, wr, br):
    logits = _dot_nt(wr, h2.astype(BF16)) + br
    eid = lax.broadcasted_iota(jnp.int32, logits.shape, 0).astype(F32)
    vals, idxs = [], []
    chosen = jnp.zeros(logits.shape, F32)
    for _ in range(TOP_K):
        m = jnp.max(logits, axis=0, keepdims=True)
        idx = jnp.min(jnp.where(logits == m, eid, float(N_EXPERTS)), axis=0, keepdims=True)
        vals.append(m)
        idxs.append(idx)
        hit = eid == idx
        chosen = jnp.where(hit, 1.0, chosen)
        logits = jnp.where(hit, -jnp.inf, logits)
    counts = jnp.sum(chosen, axis=1, keepdims=True).astype(jnp.int32)
    ex = [jnp.exp(v - vals[0]) for v in vals]
    tot = ex[0] + ex[1] + ex[2] + ex[3]
    inv = 1.0 / tot
    gates = [e * inv for e in ex]
    rowid = lax.broadcasted_iota(jnp.int32, (SUBLANES, h2.shape[0]), 0)
    topi = jnp.zeros((SUBLANES, h2.shape[0]), F32)
    topg = jnp.zeros((SUBLANES, h2.shape[0]), F32)
    for k in range(TOP_K):
        topi = jnp.where(rowid == k, idxs[k], topi)
        topg = jnp.where(rowid == k, gates[k], topg)
    return topi.astype(jnp.int32), topg, counts


SAMPLE_STEP_SEQS = 32
SAMPLE_ATTN_SEQS = 16
_QKV_COLS = (_OFF_Q, Q_W + 2 * KV_W)
_LATE_COLS = ((_OFF_GB, 2 * CONV_CH), (_OFF_XC, CONV_CH), (_OFF_BA, D_MODEL), (_OFF_BC, D_MODEL))


def _in_proj(proj, h_bf, w_in_ref, group):
    off, width = group
    proj[off] = _dot(h_bf, w_in_ref[:, off:off + width])


def _cols(proj, off, width):
    for start, piece in proj.items():
        if start <= off and off + width <= start + piece.shape[1]:
            return piece[:, off - start:off - start + width]
    raise KeyError(off)


def _merge_and_route(x, o_bf, y_conv, proj, mod, n2g_ref, w_ao_ref, w_co_ref, w_o_ref,
                     wr_ref, br_ref, before_tail=None):
    sh1, sc1, g1, sh2, sc2, g2 = mod
    a = _dot(o_bf, w_ao_ref[...])
    c = _dot(y_conv.astype(BF16), w_co_ref[...])
    m = _sigmoid(_cols(proj, _OFF_BA, D_MODEL)) * a + _sigmoid(_cols(proj, _OFF_BC, D_MODEL)) * c
    x1 = x + g1 * _dot(m.astype(BF16), w_o_ref[...])
    if before_tail is not None:
        before_tail()
    h2 = _modulated_norm(x1, n2g_ref[...], sh2, sc2)
    return (x1, h2) + _route(h2, wr_ref[...], br_ref[...])


def _prompt_body(sinks_ref, x_ref, mod_ref, rope_ref, xn_ref, modn_ref,
                 n1g_ref, n2g_ref, w_in_ref, bd_ref, qg_ref, kg_ref,
                 w_ao_ref, cw_ref, cb_ref, w_co_ref, w_o_ref, wr_ref, br_ref,
                 x1_ref, h2_ref, topi_ref, topg_ref, kwin_ref, vwin_ref, zwin_ref, cnt_ref,
                 kbuf, vbuf, zbuf, h_buf, qkv_buf, late_buf, *, tb):
    j = pl.program_id(1)
    nsub = tb // WINDOW

    @pl.when(j == 0)
    def _():
        kbuf[0:WINDOW, :] = jnp.zeros((WINDOW, KV_W), F32)
        vbuf[0:WINDOW, :] = jnp.zeros((WINDOW, KV_W), F32)
        zbuf[0:SUBLANES, :] = jnp.zeros((SUBLANES, CONV_CH), F32)

    def front(xr, mr):
        h = _modulated_norm(xr[0], n1g_ref[...], mr[0, 0:1, :], mr[0, 1:2, :]).astype(BF16)
        h_buf[...] = h
        for group, buf in ((_QKV_COLS, qkv_buf), (_LATE_COLS[0], late_buf)):
            buf[...] = _dot(h, w_in_ref[:, group[0]:group[0] + group[1]])

    @pl.when((pl.program_id(0) == 0) & (j == 0))
    def _():
        front(x_ref, mod_ref)

    x = x_ref[0]
    mod = tuple(mod_ref[0, i:i + 1, :] for i in range(6))
    h_bf = h_buf[...]
    proj = {_QKV_COLS[0]: qkv_buf[...], _LATE_COLS[0][0]: late_buf[...]}
    rope = rope_ref[...]
    bd = bd_ref[...]

    k = _rope(_head_norm(_cols(proj, _OFF_K, KV_W), bd, kg_ref[...]), rope)
    v = _cols(proj, _OFF_V, KV_W)
    kbuf[WINDOW:WINDOW + tb, :] = k
    vbuf[WINDOW:WINDOW + tb, :] = v
    kwin_ref[0] = k[tb - WINDOW:tb]
    vwin_ref[0] = v[tb - WINDOW:tb]
    k_exp = _pair_expand(kbuf[...])
    v_exp = _pair_expand(vbuf[...])

    _in_proj(proj, h_bf, w_in_ref, _LATE_COLS[1])
    _in_proj(proj, h_bf, w_in_ref, _LATE_COLS[2])
    qn = _head_norm(_cols(proj, _OFF_Q, Q_W), bd, qg_ref[...])
    q_tiles = []
    for t in range(Q_W // LANES):
        qt = _rope(qn[:, t * LANES:(t + 1) * LANES], rope)
        q_tiles.append((qt * ATTN_SCALE).astype(BF16))

    row = lax.broadcasted_iota(jnp.int32, (WINDOW, 2 * WINDOW), 0)
    col = lax.broadcasted_iota(jnp.int32, (WINDOW, 2 * WINDOW), 1)
    band = (col > row) & (col <= row + WINDOW)
    first_key = jnp.where(j > 0, 0, WINDOW)
    band_first = band & (col >= first_key)
    o_rows = []
    for i in range(nsub):
        for group in _LATE_COLS[3:][i::nsub]:
            _in_proj(proj, h_bf, w_in_ref, group)
        mask = band_first if i == 0 else band
        keys = slice(i * WINDOW, (i + 2) * WINDOW)
        o_tiles = []
        for t in range(Q_W // LANES):
            c = t // 2
            qt = q_tiles[t][i * WINDOW:(i + 1) * WINDOW]
            o_t = None
            for half in range(2):
                s = jnp.where(mask, _dot_nt(qt, k_exp[c][half][keys]), NEG_INF)
                sink = sinks_ref[c * GROUP + (t % 2) * 2 + half]
                part = _softmax_pv([s], [v_exp[c][half][keys]], sink)
                o_t = part if o_t is None else o_t + part
            o_tiles.append(o_t.astype(BF16))
        o_rows.append(jnp.concatenate(o_tiles, axis=1))
    o_bf = jnp.concatenate(o_rows, axis=0) if nsub > 1 else o_rows[0]
    kbuf[0:WINDOW, :] = kbuf[tb:tb + WINDOW, :]
    vbuf[0:WINDOW, :] = vbuf[tb:tb + WINDOW, :]

    z = _cols(proj, _OFF_GC, CONV_CH) * _cols(proj, _OFF_XC, CONV_CH)
    zbuf[SUBLANES:SUBLANES + tb, :] = z
    zwin_ref[0] = z[tb - SUBLANES:tb]
    z1 = zbuf[SUBLANES - 1:SUBLANES - 1 + tb, :]
    z2 = zbuf[SUBLANES - 2:SUBLANES - 2 + tb, :]
    conv = cb_ref[...] + cw_ref[0:1, :] * z2 + cw_ref[1:2, :] * z1 + cw_ref[2:3, :] * z
    y_conv = _cols(proj, _OFF_GB, CONV_CH) * conv
    zbuf[0:SUBLANES, :] = zbuf[tb:tb + SUBLANES, :]

    x1, h2, topi, topg, counts = _merge_and_route(x, o_bf, y_conv, proj, mod, n2g_ref, w_ao_ref, w_co_ref,
                                                  w_o_ref, wr_ref, br_ref,
                                                  before_tail=lambda: front(xn_ref, modn_ref))
    x1_ref[0] = x1
    h2_ref[0] = h2.astype(BF16)
    topi_ref[0] = topi
    topg_ref[0] = topg
    cnt_ref[...] = counts


def _prompt_mixer(x, mod, rope, sinks, weights, tb):
    nb, seq, _ = x.shape
    nj = seq // tb
    consts = weights

    def nxt(b, j):
        s = jnp.minimum(b * nj + j + 1, nb * nj - 1)
        return s // nj, s % nj
    in_specs = [
        pl.BlockSpec((1, tb, D_MODEL), lambda b, j, s: (b, j, 0)),
        pl.BlockSpec((1, 6, D_MODEL), lambda b, j, s: (b, 0, 0)),
        pl.BlockSpec((tb, 3 * LANES), lambda b, j, s: (j, 0)),
        pl.BlockSpec((1, tb, D_MODEL), lambda b, j, s: (*nxt(b, j), 0)),
        pl.BlockSpec((1, 6, D_MODEL), lambda b, j, s: (nxt(b, j)[0], 0, 0)),
    ] + [_const_spec(w.shape) for w in consts]
    out_specs = [
        pl.BlockSpec((1, tb, D_MODEL), lambda b, j, s: (b, j, 0)),
        pl.BlockSpec((1, tb, D_MODEL), lambda b, j, s: (b, j, 0)),
        pl.BlockSpec((1, SUBLANES, tb), lambda b, j, s: (b, 0, j)),
        pl.BlockSpec((1, SUBLANES, tb), lambda b, j, s: (b, 0, j)),
        pl.BlockSpec((1, WINDOW, KV_W), lambda b, j, s: (b, 0, 0)),
        pl.BlockSpec((1, WINDOW, KV_W), lambda b, j, s: (b, 0, 0)),
        pl.BlockSpec((1, SUBLANES, CONV_CH), lambda b, j, s: (b, 0, 0)),
        pl.BlockSpec((None, None, N_EXPERTS, 1), lambda b, j, s: (b, j, 0, 0)),
    ]
    out_shape = [
        jax.ShapeDtypeStruct((nb, seq, D_MODEL), F32),
        jax.ShapeDtypeStruct((nb, seq, D_MODEL), BF16),
        jax.ShapeDtypeStruct((nb, SUBLANES, seq), jnp.int32),
        jax.ShapeDtypeStruct((nb, SUBLANES, seq), F32),
        jax.ShapeDtypeStruct((nb, WINDOW, KV_W), F32),
        jax.ShapeDtypeStruct((nb, WINDOW, KV_W), F32),
        jax.ShapeDtypeStruct((nb, SUBLANES, CONV_CH), F32),
        jax.ShapeDtypeStruct((nb, nj, N_EXPERTS, 1), jnp.int32),
    ]
    return pl.pallas_call(
        functools.partial(_prompt_body, tb=tb),
        grid_spec=pltpu.PrefetchScalarGridSpec(
            num_scalar_prefetch=1, grid=(nb, seq // tb),
            in_specs=in_specs, out_specs=out_specs,
            scratch_shapes=[pltpu.VMEM((tb + WINDOW, KV_W), F32),
                            pltpu.VMEM((tb + WINDOW, KV_W), F32),
                            pltpu.VMEM((tb + SUBLANES, CONV_CH), F32),
                            pltpu.VMEM((tb, D_MODEL), BF16),
                            pltpu.VMEM((tb, _QKV_COLS[1]), F32),
                            pltpu.VMEM((tb, _LATE_COLS[0][1]), F32)]),
        out_shape=out_shape,
        compiler_params=pltpu.CompilerParams(
            dimension_semantics=("arbitrary", "arbitrary"), vmem_limit_bytes=VMEM_LIMIT),
        name="prompt_mixer",
    )(sinks, x, mod, rope, x, mod, *consts)


def _sample_body(sinks_ref, x_ref, mod_ref, rope_ref, ck_ref, cv_ref, st_ref,
                 n1g_ref, n2g_ref, w_in_ref, bd_ref, qg_ref, kg_ref,
                 w_ao_ref, cw_ref, cb_ref, w_co_ref, w_o_ref, wr_ref, br_ref,
                 x1_ref, h2_ref, topi_ref, topg_ref, knew_ref, vnew_ref, z_ref, cnt_ref, *, nseq, dseq):
    rows = nseq * dseq
    wbuf = ck_ref.shape[1]
    x = x_ref[...]
    mod = tuple(_expand_rows(mod_ref[i], dseq) for i in range(6))
    h_bf = _modulated_norm(x, n1g_ref[...], mod[0], mod[1]).astype(BF16)
    proj = {}
    for group in (_QKV_COLS,) + _LATE_COLS:
        _in_proj(proj, h_bf, w_in_ref, group)
    rope = rope_ref[...]
    bd = bd_ref[...]

    k = _rope(_head_norm(_cols(proj, _OFF_K, KV_W), bd, kg_ref[...]), rope)
    v = _cols(proj, _OFF_V, KV_W)
    for win_ref, cache_ref, new in ((knew_ref, ck_ref, k), (vnew_ref, cv_ref, v)):
        win_ref[:, 0:wbuf - dseq, :] = cache_ref[:, dseq:, :]
        win_ref[:, wbuf - dseq:, :] = new.reshape(nseq, dseq, KV_W)
    gseq = SAMPLE_ATTN_SEQS
    grows = gseq * dseq
    qrow = lax.broadcasted_iota(jnp.int32, (grows, gseq * wbuf), 0)
    ccol = lax.broadcasted_iota(jnp.int32, (grows, gseq * wbuf), 1)
    mask_c = ((ccol // wbuf) == (qrow // dseq)) & ((ccol % wbuf) > (qrow % dseq) + (wbuf - WINDOW))
    qrow_n = lax.broadcasted_iota(jnp.int32, (grows, grows), 0)
    ncol = lax.broadcasted_iota(jnp.int32, (grows, grows), 1)
    mask_n = ((ncol // dseq) == (qrow_n // dseq)) & ((ncol % dseq) <= (qrow_n % dseq))

    qn = _head_norm(_cols(proj, _OFF_Q, Q_W), bd, qg_ref[...])
    q_tiles = [(_rope(qn[:, t * LANES:(t + 1) * LANES], rope) * ATTN_SCALE).astype(BF16)
               for t in range(Q_W // LANES)]
    o_rows = []
    for g0 in range(0, nseq, gseq):
        rs = slice(g0 * dseq, g0 * dseq + grows)
        kn_exp = _pair_expand(k[rs])
        vn_exp = _pair_expand(v[rs])
        kc_exp = _pair_expand(ck_ref[g0:g0 + gseq].reshape(gseq * wbuf, KV_W))
        vc_exp = _pair_expand(cv_ref[g0:g0 + gseq].reshape(gseq * wbuf, KV_W))
        o_tiles = []
        for t in range(Q_W // LANES):
            c = t // 2
            o_t = None
            for half in range(2):
                s_c = jnp.where(mask_c, _dot_nt(q_tiles[t][rs], kc_exp[c][half]), NEG_INF)
                s_n = jnp.where(mask_n, _dot_nt(q_tiles[t][rs], kn_exp[c][half]), NEG_INF)
                sink = sinks_ref[c * GROUP + (t % 2) * 2 + half]
                part = _softmax_pv([s_c, s_n], [vc_exp[c][half], vn_exp[c][half]], sink)
                o_t = part if o_t is None else o_t + part
            o_tiles.append(o_t.astype(BF16))
        o_rows.append(jnp.concatenate(o_tiles, axis=1))
    o_bf = jnp.concatenate(o_rows, axis=0) if len(o_rows) > 1 else o_rows[0]

    z = _cols(proj, _OFF_GC, CONV_CH) * _cols(proj, _OFF_XC, CONV_CH)
    z_ref[...] = z
    r = lax.broadcasted_iota(jnp.int32, z.shape, 0) % dseq
    st0 = _expand_rows(st_ref[0], dseq)
    st1 = _expand_rows(st_ref[1], dseq)
    z1 = jnp.where(r == 0, st1, pltpu.roll(z, 1, 0))
    z2 = jnp.where(r == 0, st0, jnp.where(r == 1, st1, pltpu.roll(z, 2, 0)))
    conv = cb_ref[...] + cw_ref[0:1, :] * z2 + cw_ref[1:2, :] * z1 + cw_ref[2:3, :] * z
    y_conv = _cols(proj, _OFF_GB, CONV_CH) * conv

    x1, h2, topi, topg, counts = _merge_and_route(x, o_bf, y_conv, proj, mod, n2g_ref, w_ao_ref, w_co_ref,
                                                  w_o_ref, wr_ref, br_ref)
    x1_ref[...] = x1
    h2_ref[...] = h2.astype(BF16)
    topi_ref[...] = topi
    topg_ref[...] = topg
    cnt_ref[...] = counts


def _sample_mixer(x, mod, rope, cache_k, cache_v, state, sinks, weights, nseq):
    ntok = x.shape[0]
    nall, wbuf, _ = cache_k.shape
    dseq = ntok // nall
    rows = nseq * dseq
    consts = weights
    in_specs = [
        pl.BlockSpec((rows, D_MODEL), lambda i, s: (i, 0)),
        pl.BlockSpec((6, nseq, 1, D_MODEL), lambda i, s: (0, i, 0, 0)),
        pl.BlockSpec((rows, 3 * LANES), lambda i, s: (i, 0)),
        pl.BlockSpec((nseq, wbuf, KV_W), lambda i, s: (i, 0, 0)),
        pl.BlockSpec((nseq, wbuf, KV_W), lambda i, s: (i, 0, 0)),
        pl.BlockSpec((2, nseq, 1, CONV_CH), lambda i, s: (0, i, 0, 0)),
    ] + [_const_spec(w.shape) for w in consts]
    out_specs = [
        pl.BlockSpec((rows, D_MODEL), lambda i, s: (i, 0)),
        pl.BlockSpec((rows, D_MODEL), lambda i, s: (i, 0)),
        pl.BlockSpec((SUBLANES, rows), lambda i, s: (0, i)),
        pl.BlockSpec((SUBLANES, rows), lambda i, s: (0, i)),
        pl.BlockSpec((nseq, wbuf, KV_W), lambda i, s: (i, 0, 0)),
        pl.BlockSpec((nseq, wbuf, KV_W), lambda i, s: (i, 0, 0)),
        pl.BlockSpec((rows, CONV_CH), lambda i, s: (i, 0)),
        pl.BlockSpec((None, N_EXPERTS, 1), lambda i, s: (i, 0, 0)),
    ]
    out_shape = [
        jax.ShapeDtypeStruct((ntok, D_MODEL), F32),
        jax.ShapeDtypeStruct((ntok, D_MODEL), BF16),
        jax.ShapeDtypeStruct((SUBLANES, ntok), jnp.int32),
        jax.ShapeDtypeStruct((SUBLANES, ntok), F32),
        jax.ShapeDtypeStruct((nall, wbuf, KV_W), F32),
        jax.ShapeDtypeStruct((nall, wbuf, KV_W), F32),
        jax.ShapeDtypeStruct((ntok, CONV_CH), F32),
        jax.ShapeDtypeStruct((ntok // rows, N_EXPERTS, 1), jnp.int32),
    ]
    return pl.pallas_call(
        functools.partial(_sample_body, nseq=nseq, dseq=dseq),
        grid_spec=pltpu.PrefetchScalarGridSpec(
            num_scalar_prefetch=1, grid=(ntok // rows,),
            in_specs=in_specs, out_specs=out_specs),
        out_shape=out_shape,
        compiler_params=pltpu.CompilerParams(
            dimension_semantics=("arbitrary",), vmem_limit_bytes=VMEM_LIMIT),
        name="sample_mixer",
    )(sinks, x, mod, rope, cache_k, cache_v, state, *consts)


SORT_TOKENS = 512
RUN_ALIGN = SUBLANES
EXPERT_TILE = 1024
HALF_D = D_MODEL // 2
LOCAL_ROWS = SORT_TOKENS * TOP_K + N_EXPERTS * RUN_ALIGN
UNSORT_CHUNK = 256
assert LOCAL_ROWS % UNSORT_CHUNK == 0
_HI_MASK = 0xFFFF0000


def _pack_halves(x):
    lo = pltpu.bitcast(x[:, :HALF_D], jnp.uint32)
    hi = pltpu.bitcast(x[:, HALF_D:], jnp.uint32)
    return hi | (lo >> 16)


def _unpack_halves(w):
    lo = pltpu.bitcast(w << 16, F32).astype(BF16)
    hi = pltpu.bitcast(w & jnp.uint32(_HI_MASK), F32).astype(BF16)
    return lo, hi


def _local_positions(topi, lstart_col, tri):
    ntok = topi.shape[1]
    eid = lax.broadcasted_iota(jnp.int32, (N_EXPERTS, ntok), 0)
    hits = [eid == topi[k:k + 1, :] for k in range(TOP_K)]
    chosen = jnp.zeros((N_EXPERTS, ntok), F32)
    for h in hits:
        chosen = jnp.where(h, 1.0, chosen)
    base = _dot(chosen.astype(BF16), tri) + lstart_col
    return [jnp.sum(jnp.where(h, base, 0.0), axis=0, keepdims=True) for h in hits]


def _run_copy(loc, glob, sem, lofs, gofs, n, to_global):
    lo = loc.at[pl.ds(pl.multiple_of(lofs, RUN_ALIGN), n)]
    gl = glob.at[pl.ds(pl.multiple_of(gofs, RUN_ALIGN), n)]
    return pltpu.make_async_copy(lo, gl, sem) if to_global else pltpu.make_async_copy(gl, lo, sem)


def _start_runs(meta, blk, loc, glob, sem, to_global):
    lstart_ref, goff_ref, cpad_ref = meta
    for e in range(N_EXPERTS):
        n = pl.multiple_of(cpad_ref[blk * N_EXPERTS + e], RUN_ALIGN)
        _run_copy(loc, glob, sem, lstart_ref[blk * N_EXPERTS + e], goff_ref[blk * N_EXPERTS + e], n,
                  to_global).start()


def _wait_runs(nrows, loc, glob, sem, to_global):
    _run_copy(loc, glob, sem, 0, 0, pl.multiple_of(nrows, RUN_ALIGN), to_global).wait()


def _sort_body(lstart_ref, goff_ref, cpad_ref, ltot_ref, fstart_ref, flen_ref, nact_ref,
               hp_ref, hs_ref, topi_ref, lcol_ref, tri_ref, xs_hbm, pos_ref, buf, zbuf, sems, fsem, *, nblk_p, nblk):
    i = pl.program_id(0)
    slot = i % 2
    meta = (lstart_ref, goff_ref, cpad_ref)

    @pl.when(i == 0)
    def _():
        zbuf[...] = jnp.zeros(zbuf.shape, jnp.uint32)

        def fill(e, tot):
            n = pl.multiple_of(flen_ref[e], RUN_ALIGN)

            @pl.when(n > 0)
            def _():
                _run_copy(zbuf, xs_hbm, fsem, 0, fstart_ref[e], n, True).start()
            return tot + n
        total = lax.fori_loop(0, N_EXPERTS, fill, 0)

        @pl.when(total > 0)
        def _():
            _wait_runs(total, zbuf, xs_hbm, fsem, True)

        def fill_tile(t, carry):
            cp = _run_copy(zbuf, xs_hbm, fsem, 0, t * EXPERT_TILE, EXPERT_TILE, True)
            cp.start()
            cp.wait()
            return carry
        lax.fori_loop(nact_ref[0], xs_hbm.shape[0] // EXPERT_TILE, fill_tile, 0)

    def run(h_ref):
        pos = [p.astype(jnp.int32) for p in _local_positions(topi_ref[...], lcol_ref[...], tri_ref[...])]
        h = h_ref[...]
        for r0 in range(0, LOCAL_ROWS, UNSORT_CHUNK):
            rowid = lax.broadcasted_iota(jnp.int32, (UNSORT_CHUNK, SORT_TOKENS), 0) + r0
            onehot = jnp.zeros((UNSORT_CHUNK, SORT_TOKENS), F32)
            for p in pos:
                onehot = jnp.where(rowid == p, 1.0, onehot)
            buf[slot, r0:r0 + UNSORT_CHUNK, :] = _pack_halves(_dot(onehot.astype(BF16), h))
        krow = lax.broadcasted_iota(jnp.int32, (SUBLANES, SORT_TOKENS), 0)
        pos_rows = jnp.zeros((SUBLANES, SORT_TOKENS), jnp.int32)
        for k in range(TOP_K):
            pos_rows = jnp.where(krow == k, pos[k], pos_rows)
        pos_ref[...] = pos_rows

    @pl.when(i < nblk_p)
    def _():
        run(hp_ref)

    @pl.when(i >= nblk_p)
    def _():
        run(hs_ref)

    _start_runs(meta, i, buf.at[slot], xs_hbm, sems.at[slot], True)

    @pl.when(i > 0)
    def _():
        _wait_runs(ltot_ref[jnp.maximum(i - 1, 0)], buf.at[1 - slot], xs_hbm, sems.at[1 - slot], True)

    @pl.when(i == nblk - 1)
    def _():
        _wait_runs(ltot_ref[i], buf.at[slot], xs_hbm, sems.at[slot], True)


def _sort_tokens(plan, h_p, h_s, topi_all, tri, n_slots):
    nblk_p = h_p.shape[0] // SORT_TOKENS
    nblk = nblk_p + h_s.shape[0] // SORT_TOKENS
    return pl.pallas_call(
        functools.partial(_sort_body, nblk_p=nblk_p, nblk=nblk),
        grid_spec=pltpu.PrefetchScalarGridSpec(
            num_scalar_prefetch=7, grid=(nblk,),
            in_specs=[pl.BlockSpec((SORT_TOKENS, D_MODEL), lambda i, *_: (jnp.minimum(i, nblk_p - 1), 0)),
                      pl.BlockSpec((SORT_TOKENS, D_MODEL), lambda i, *_: (jnp.maximum(i - nblk_p, 0), 0)),
                      pl.BlockSpec((SUBLANES, SORT_TOKENS), lambda i, *_: (0, i)),
                      pl.BlockSpec((None, N_EXPERTS, 1), lambda i, *_: (i, 0, 0)),
                      _const_spec(tri.shape)],
            out_specs=[pl.BlockSpec(memory_space=pl.ANY),
                       pl.BlockSpec((SUBLANES, SORT_TOKENS), lambda i, *_: (0, i))],
            scratch_shapes=[pltpu.VMEM((2, LOCAL_ROWS, HALF_D), jnp.uint32),
                            pltpu.VMEM((EXPERT_TILE, HALF_D), jnp.uint32),
                            pltpu.SemaphoreType.DMA((2,)),
                            pltpu.SemaphoreType.DMA]),
        out_shape=[jax.ShapeDtypeStruct((n_slots, HALF_D), jnp.uint32),
                   jax.ShapeDtypeStruct(topi_all.shape, jnp.int32)],
        compiler_params=pltpu.CompilerParams(
            dimension_semantics=("arbitrary",), vmem_limit_bytes=VMEM_LIMIT),
        name="moe_sort",
    )(plan["lstart"], plan["goff"], plan["cpad"], plan["ltot"], plan["fill_start"], plan["fill_len"], plan["nact"],
      h_p, h_s, topi_all, plan["lstart_col"], tri)


def _experts_body(tile_exp_ref, nact_ref, rows_ref, xs_ref, wg_hbm, bg_ref, wu_hbm, bu_ref, wd_hbm, bd_ref,
                  ys_ref, w_f32, w_bf, sems):
    t = pl.program_id(0)
    nact = nact_ref[0]
    w_hbm = (wg_hbm, wu_hbm, wd_hbm)

    def weight_copies(e, slot):
        return [pltpu.make_async_copy(w.at[e], w_f32.at[slot, j], sems.at[slot, j]) for j, w in enumerate(w_hbm)]

    @pl.when(t < nact)
    def _():
        e = tile_exp_ref[t]
        slot = e % 2

        @pl.when(t == 0)
        def _():
            for cp in weight_copies(e, slot):
                cp.start()

        @pl.when((t == 0) | (e != tile_exp_ref[jnp.maximum(t - 1, 0)]))
        def _():
            @pl.when(e + 1 < N_EXPERTS)
            def _():
                for cp in weight_copies(e + 1, 1 - slot):
                    cp.start()
            for j, cp in enumerate(weight_copies(e, slot)):
                cp.wait()
                w_bf[j] = w_f32[slot, j].astype(BF16)

        def swiglu(rows):
            x = jnp.concatenate(_unpack_halves(xs_ref[0:rows, :]), axis=1)
            g = _dot(x, w_bf[0]) + bg_ref[0]
            u = _dot(x, w_bf[1]) + bu_ref[0]
            g = jnp.minimum(g, SWIGLU_LIMIT)
            u = jnp.clip(u, -SWIGLU_LIMIT, SWIGLU_LIMIT)
            a = g * _sigmoid(SWIGLU_ALPHA * g) * (u + 1.0)
            out = _dot(a.astype(BF16), w_bf[2]) + bd_ref[0]
            ys_ref[0:rows, :] = _pack_halves(out.astype(BF16).astype(F32))

        quarter = EXPERT_TILE // 4
        quarters = (rows_ref[t] + quarter - 1) // quarter
        for q in range(1, 5):
            @pl.when(quarters == q)
            def _(q=q):
                swiglu(q * quarter)
                if q < 4:
                    ys_ref[q * quarter:, :] = jnp.zeros((EXPERT_TILE - q * quarter, HALF_D), jnp.uint32)

    @pl.when(t >= nact)
    def _():
        ys_ref[...] = jnp.zeros(ys_ref.shape, jnp.uint32)


def _experts(tile_exp, nact, tile_rows, xs, wg, bg, wu, bu, wd, bd):
    n_tiles = tile_exp.shape[0]

    def active(t, na):
        return jnp.minimum(t, na[0] - 1)
    wspec = pl.BlockSpec(memory_space=pl.ANY)
    bspec = pl.BlockSpec((1, 1, D_MODEL), lambda t, te, na, tr: (te[active(t, na)], 0, 0))
    xspec = pl.BlockSpec((EXPERT_TILE, HALF_D), lambda t, te, na, tr: (active(t, na), 0))
    return pl.pallas_call(
        _experts_body,
        grid_spec=pltpu.PrefetchScalarGridSpec(
            num_scalar_prefetch=3, grid=(n_tiles,),
            in_specs=[xspec, wspec, bspec, wspec, bspec, wspec, bspec],
            out_specs=pl.BlockSpec((EXPERT_TILE, HALF_D), lambda t, te, na, tr: (t, 0)),
            scratch_shapes=[pltpu.VMEM((2, 3, D_MODEL, D_MODEL), F32),
                            pltpu.VMEM((3, D_MODEL, D_MODEL), BF16),
                            pltpu.SemaphoreType.DMA((2, 3))]),
        out_shape=jax.ShapeDtypeStruct(xs.shape, jnp.uint32),
        compiler_params=pltpu.CompilerParams(
            dimension_semantics=("arbitrary",), vmem_limit_bytes=VMEM_LIMIT),
        name="experts",
    )(tile_exp, nact, tile_rows, xs, wg, bg, wu, bu, wd, bd)


def _unsort_body(lstart_ref, goff_ref, cpad_ref, ltot_ref,
                 ys_hbm, x1_ref, g2_ref, pos_ref, gate_ref, y_ref, buf, sems, *, blk0, nblk):
    i = pl.program_id(0)
    b = blk0 + i
    slot = i % 2
    meta = (lstart_ref, goff_ref, cpad_ref)

    @pl.when(i == 0)
    def _():
        buf[...] = jnp.zeros(buf.shape, jnp.uint32)
        _start_runs(meta, b, buf.at[0], ys_hbm, sems.at[0], False)

    @pl.when(i + 1 < nblk)
    def _():
        _start_runs(meta, b + 1, buf.at[1 - slot], ys_hbm, sems.at[1 - slot], False)

    _wait_runs(ltot_ref[b], buf.at[slot], ys_hbm, sems.at[slot], False)
    pos_rows = pos_ref[...]
    gate_rows = gate_ref[...]
    tn = (((0,), (0,)), ((), ()))
    acc_lo = acc_hi = None
    for c0 in range(0, LOCAL_ROWS, UNSORT_CHUNK):
        rowid = lax.broadcasted_iota(jnp.int32, (UNSORT_CHUNK, SORT_TOKENS), 0) + c0
        weights = jnp.zeros((UNSORT_CHUNK, SORT_TOKENS), F32)
        for k in range(TOP_K):
            weights = jnp.where(rowid == pos_rows[k:k + 1, :], gate_rows[k:k + 1, :], weights)
        weights = weights.astype(BF16)
        y_lo, y_hi = _unpack_halves(buf[slot, c0:c0 + UNSORT_CHUNK, :])
        d_lo = lax.dot_general(weights, y_lo, tn, preferred_element_type=F32)
        d_hi = lax.dot_general(weights, y_hi, tn, preferred_element_type=F32)
        acc_lo = d_lo if acc_lo is None else acc_lo + d_lo
        acc_hi = d_hi if acc_hi is None else acc_hi + d_hi
    g2 = g2_ref[...]
    if g2.ndim == 3:
        g2 = _expand_rows(g2, SORT_TOKENS // g2.shape[0])
    y_ref[:, :HALF_D] = x1_ref[:, :HALF_D] + g2[:, :HALF_D] * acc_lo
    y_ref[:, HALF_D:] = x1_ref[:, HALF_D:] + g2[:, HALF_D:] * acc_hi


def _unsort_tokens(plan, ys, x1, g2, g2_spec, pos_cols, gate_cols, blk0):
    nblk = x1.shape[0] // SORT_TOKENS
    return pl.pallas_call(
        functools.partial(_unsort_body, blk0=blk0, nblk=nblk),
        grid_spec=pltpu.PrefetchScalarGridSpec(
            num_scalar_prefetch=4, grid=(nblk,),
            in_specs=[pl.BlockSpec(memory_space=pl.ANY),
                      pl.BlockSpec((SORT_TOKENS, D_MODEL), lambda i, *_: (i, 0)),
                      g2_spec,
                      pl.BlockSpec((SUBLANES, SORT_TOKENS), lambda i, *_: (0, blk0 + i)),
                      pl.BlockSpec((SUBLANES, SORT_TOKENS), lambda i, *_: (0, blk0 + i))],
            out_specs=pl.BlockSpec((SORT_TOKENS, D_MODEL), lambda i, *_: (i, 0)),
            scratch_shapes=[pltpu.VMEM((2, LOCAL_ROWS, HALF_D), jnp.uint32),
                            pltpu.SemaphoreType.DMA((2,))]),
        out_shape=jax.ShapeDtypeStruct(x1.shape, F32),
        compiler_params=pltpu.CompilerParams(
            dimension_semantics=("arbitrary",), vmem_limit_bytes=VMEM_LIMIT),
        name="moe_unsort",
    )(plan["lstart"], plan["goff"], plan["cpad"], plan["ltot"],
      ys, x1, g2, pos_cols, gate_cols)


def _rope_table(pos):
    half = ROT_DIM // 2
    inv_freq = ROPE_THETA ** (-jnp.arange(0, ROT_DIM, 2, dtype=F32) / ROT_DIM)
    ang = pos.astype(F32)[:, None] * inv_freq[None, :]
    cos, sin = jnp.cos(ang), jnp.sin(ang)
    n = pos.shape[0]
    rest = HEAD_DIM - ROT_DIM
    c = jnp.concatenate([cos, cos, jnp.ones((n, rest), F32)], axis=1)
    s_next = jnp.concatenate([-sin, jnp.zeros((n, half + rest), F32)], axis=1)
    s_prev = jnp.concatenate([jnp.zeros((n, half), F32), sin, jnp.zeros((n, rest), F32)], axis=1)
    rep = LANES // HEAD_DIM
    return jnp.concatenate([jnp.tile(c, (1, rep)), jnp.tile(s_next, (1, rep)), jnp.tile(s_prev, (1, rep))], axis=1)


def _num_expert_tiles(ntok):
    nblk = ntok // SORT_TOKENS
    worst = ntok * TOP_K + nblk * N_EXPERTS * RUN_ALIGN + N_EXPERTS * (EXPERT_TILE - 1)
    return -(-worst // EXPERT_TILE)


def _routing_plan(cnt):
    nblk = cnt.shape[0]
    n_tiles = _num_expert_tiles(nblk * SORT_TOKENS)
    cpad = jnp.maximum(-(-cnt // RUN_ALIGN) * RUN_ALIGN, RUN_ALIGN)
    lstart = jnp.cumsum(cpad, axis=1) - cpad
    tot = cpad.sum(axis=0)
    region = -(-tot // EXPERT_TILE) * EXPERT_TILE
    region_end = jnp.cumsum(region)
    gstart = region_end - region
    goff = gstart[None, :] + jnp.cumsum(cpad, axis=0) - cpad
    tile_start = jnp.arange(n_tiles, dtype=jnp.int32) * EXPERT_TILE
    tile_exp = jnp.minimum((tile_start[:, None] >= region_end[None, :]).astype(jnp.int32).sum(axis=1), N_EXPERTS - 1)
    tile_rows = jnp.clip((gstart + tot)[tile_exp] - tile_start, 0, EXPERT_TILE)
    i32 = lambda a: a.astype(jnp.int32)
    return dict(
        tile_rows=i32(tile_rows),
        lstart=i32(lstart).reshape(-1), goff=i32(goff).reshape(-1), cpad=i32(cpad).reshape(-1),
        ltot=i32(cpad.sum(axis=1)), fill_start=i32(gstart + tot), fill_len=i32(region - tot),
        lstart_col=lstart.astype(F32).reshape(nblk, N_EXPERTS, 1),
        tile_exp=i32(tile_exp), nact=i32(region_end[-1] // EXPERT_TILE).reshape(1))


def kernel(x_prompt, x_sample, cache_k_win, cache_v_win, state_conv, c_prompt, c_sample, w_ada, b_ada, norm1_g, norm2_g, w_in, q_norm_g, k_norm_g, attn_sinks, w_attn_out, conv_w, conv_b, w_conv_out, w_o, w_router, b_router, w_gate, b_gate, w_up, b_up, w_down, b_down):
    nb, seq, _ = x_prompt.shape
    ns, dseq, _ = x_sample.shape
    wbuf = cache_k_win.shape[2]
    assert w_ada.shape[0] == 1, "single-layer step"
    l = 0

    mod = _adaln(jnp.concatenate([c_prompt, c_sample], axis=0), w_ada[l], b_ada[l])
    mod_p = mod[:nb].reshape(nb, 6, D_MODEL)
    mod_s = mod[nb:].reshape(ns, 6, D_MODEL).transpose(1, 0, 2).reshape(6, ns, 1, D_MODEL)

    bd = jnp.asarray(np.kron(np.eye(N_HEADS), np.ones((HEAD_DIM, HEAD_DIM))), BF16)
    rep = LANES // HEAD_DIM
    weights = (norm1_g[l].reshape(1, D_MODEL), norm2_g[l].reshape(1, D_MODEL), w_in[l].astype(BF16), bd,
               jnp.tile(q_norm_g[l], N_HEADS).reshape(1, Q_W), jnp.tile(k_norm_g[l], N_KV).reshape(1, KV_W),
               w_attn_out[l].astype(BF16), conv_w[l], conv_b[l].reshape(1, CONV_CH), w_conv_out[l].astype(BF16),
               w_o[l].astype(BF16), w_router[l].T.astype(BF16), b_router[l].reshape(N_EXPERTS, 1))
    sinks = attn_sinks[l]

    tb = 512
    assert tb == SORT_TOKENS and SORT_TOKENS % (SAMPLE_STEP_SEQS * dseq) == 0
    (x1_p, h2_p, topi_p, topg_p, kwin_p, vwin_p, zwin_p, cnt_p) = _prompt_mixer(
        x_prompt, mod_p, _rope_table(jnp.arange(seq)), sinks, weights, tb)
    pos_s = jnp.tile(PAST_LEN + jnp.arange(dseq), ns)
    state_rows = state_conv[l].transpose(1, 0, 2).reshape(CONV_K - 1, ns, 1, CONV_CH)
    (x1_s, h2_s, topi_s, topg_s, knew, vnew, z_s, cnt_s) = _sample_mixer(
        x_sample.reshape(ns * dseq, D_MODEL), mod_s, _rope_table(pos_s),
        cache_k_win[l].reshape(ns, wbuf, KV_W), cache_v_win[l].reshape(ns, wbuf, KV_W),
        state_rows, sinks, weights, SAMPLE_STEP_SEQS)

    ntok_p = nb * seq
    ntok = ntok_p + ns * dseq
    topi_all = jnp.concatenate([topi_p.transpose(1, 0, 2).reshape(SUBLANES, ntok_p), topi_s], axis=1)
    topg_all = jnp.concatenate([topg_p.transpose(1, 0, 2).reshape(SUBLANES, ntok_p), topg_s], axis=1)
    cnt_s = cnt_s.reshape(-1, SORT_TOKENS // (SAMPLE_STEP_SEQS * dseq), N_EXPERTS).sum(axis=1)
    plan = _routing_plan(jnp.concatenate([cnt_p.reshape(-1, N_EXPERTS), cnt_s], axis=0))
    tri = jnp.asarray(np.triu(np.ones((SORT_TOKENS, SORT_TOKENS)), k=1), BF16)
    xs, pos_all = _sort_tokens(plan, h2_p.reshape(ntok_p, D_MODEL), h2_s, topi_all, tri,
                               _num_expert_tiles(ntok) * EXPERT_TILE)
    pos_cols, gate_cols = pos_all, topg_all
    ys = _experts(plan["tile_exp"], plan["nact"], plan["tile_rows"], xs,
                  w_gate[l], b_gate[l].reshape(N_EXPERTS, 1, D_MODEL),
                  w_up[l], b_up[l].reshape(N_EXPERTS, 1, D_MODEL),
                  w_down[l], b_down[l].reshape(N_EXPERTS, 1, D_MODEL))
    blocks_per_seq = seq // SORT_TOKENS
    y_p = _unsort_tokens(plan, ys, x1_p.reshape(ntok_p, D_MODEL), mod_p[:, 5:6, :],
                         pl.BlockSpec((None, 1, D_MODEL), lambda i, *_: (i // blocks_per_seq, 0, 0)),
                         pos_cols, gate_cols, 0)
    y_s = _unsort_tokens(plan, ys, x1_s, mod_s[5],
                         pl.BlockSpec((SORT_TOKENS // dseq, 1, D_MODEL), lambda i, *_: (i, 0, 0)),
                         pos_cols, gate_cols, ntok_p // SORT_TOKENS)

    n_kv_shape = (N_KV, HEAD_DIM)
    k_win_p = kwin_p.reshape(1, nb, WINDOW, *n_kv_shape)
    v_win_p = vwin_p.reshape(1, nb, WINDOW, *n_kv_shape)
    conv_p = zwin_p[:, SUBLANES - (CONV_K - 1):, :][None]
    k_win_s = knew.reshape(1, ns, wbuf, *n_kv_shape)
    v_win_s = vnew.reshape(1, ns, wbuf, *n_kv_shape)
    conv_s = z_s.reshape(ns, dseq, CONV_CH)[:, dseq - (CONV_K - 1):, :][None]
    return (y_p.reshape(nb, seq, D_MODEL), y_s.reshape(ns, dseq, D_MODEL),
            k_win_p, v_win_p, conv_p, k_win_s, v_win_s, conv_s)
```

```python
import functools

import numpy as np
import jax
import jax.numpy as jnp
from jax import lax
from jax.experimental import pallas as pl
from jax.experimental.pallas import tpu as pltpu

D_MODEL = 1024
HEAD_DIM = 64
N_HEADS = 8
N_KV = 2
GROUP = N_HEADS // N_KV
Q_W = N_HEADS * HEAD_DIM
KV_W = N_KV * HEAD_DIM
WINDOW = 128
ROT_DIM = HEAD_DIM // 4
ROPE_THETA = 500000.0
ATTN_SCALE = HEAD_DIM ** -0.5
CONV_CH = D_MODEL // 2
CONV_K = 3
N_EXPERTS = 32
TOP_K = 4
SWIGLU_ALPHA = 1.702
SWIGLU_LIMIT = 7.0
MOE_BLOCK = 128
NORM_EPS = 1e-5
QK_EPS = 1e-6
NEG_INF = -1e30
PAST_LEN = 16384

LANES = 128
SUBLANES = 8
VMEM_LIMIT = 56 * 1024 * 1024

_OFF_Q = 0
_OFF_K = _OFF_Q + Q_W
_OFF_V = _OFF_K + KV_W
_OFF_GB = _OFF_V + KV_W
_OFF_GC = _OFF_GB + CONV_CH
_OFF_XC = _OFF_GC + CONV_CH
_OFF_BA = _OFF_XC + CONV_CH
_OFF_BC = _OFF_BA + D_MODEL
IN_COLS = _OFF_BC + D_MODEL

BF16 = jnp.bfloat16
F32 = jnp.float32
_NT = (((1,), (1,)), ((), ()))


def _dot(a, b):
    return jnp.dot(a, b, preferred_element_type=F32)


def _dot_nt(a, b):
    return lax.dot_general(a, b, _NT, preferred_element_type=F32)


def _sigmoid(x):
    return 1.0 / (1.0 + jnp.exp(-x))


def _split_bf16(x):
    hi = x.astype(BF16)
    lo = (x - hi.astype(F32)).astype(BF16)
    return hi, lo


def _const_spec(shape):
    nd = len(shape)
    return pl.BlockSpec(shape, lambda *_: (0,) * nd, pipeline_mode=pl.Buffered(1))


def _adaln_body(c_ref, w_ref, b_ref, o_ref):
    c = c_ref[...]
    s = c * _sigmoid(c)
    o_ref[...] = _dot(s.astype(BF16), w_ref[...].astype(BF16)) + b_ref[...]


def _adaln(c, w_ada, b_ada):
    n = c.shape[0]
    cols = w_ada.shape[1]
    bn = 1536
    return pl.pallas_call(
        _adaln_body,
        grid=(cols // bn,),
        in_specs=[pl.BlockSpec((n, D_MODEL), lambda i: (0, 0)),
                  pl.BlockSpec((D_MODEL, bn), lambda i: (0, i)),
                  pl.BlockSpec((1, bn), lambda i: (0, i))],
        out_specs=pl.BlockSpec((n, bn), lambda i: (0, i)),
        out_shape=jax.ShapeDtypeStruct((n, cols), F32),
        compiler_params=pltpu.CompilerParams(vmem_limit_bytes=VMEM_LIMIT),
        name="adaln",
    )(c, w_ada, b_ada.reshape(1, cols))


def _expand_rows(v, rep):
    n, _, w = v.shape
    return jnp.broadcast_to(v, (n, rep, w)).reshape(n * rep, w)


def _modulated_norm(x, g, shift, scale):
    y = x * lax.rsqrt(jnp.mean(x * x, axis=-1, keepdims=True) + NORM_EPS)
    return (y * g) * (1.0 + scale) + shift


def _head_norm(t, bd, g):
    w = t.shape[1]
    sq_hi, sq_lo = _split_bf16(t * t)
    blk = bd[:w, :w]
    ms = (_dot(sq_hi, blk) + _dot(sq_lo, blk)) * (1.0 / HEAD_DIM)
    return t * lax.rsqrt(ms + QK_EPS) * g[:, :w]


def _rope(y, rope):
    c = rope[:, 0:LANES]
    s_next = rope[:, LANES:2 * LANES]
    s_prev = rope[:, 2 * LANES:3 * LANES]
    half = ROT_DIM // 2
    return y * c + pltpu.roll(y, LANES - half, 1) * s_next + pltpu.roll(y, half, 1) * s_prev


def _pair_expand(t):
    lane = lax.broadcasted_iota(jnp.int32, t.shape, 1)
    lo = lane < HEAD_DIM
    r = pltpu.roll(t, HEAD_DIM, 1)
    zero = jnp.zeros_like(t)
    a0 = jnp.where(lo, t, zero).astype(BF16)
    b0 = jnp.where(lo, zero, r).astype(BF16)
    a1 = jnp.where(lo, r, zero).astype(BF16)
    b1 = jnp.where(lo, zero, t).astype(BF16)
    return ((a0, b0), (a1, b1))


def _softmax_pv(s_list, v_list, sink):
    m = jnp.full((s_list[0].shape[0], 1), sink, F32)
    for s in s_list:
        m = jnp.maximum(m, jnp.max(s, axis=-1, keepdims=True))
    den = jnp.exp(sink - m)
    acc = None
    for s, v in zip(s_list, v_list):
        e = jnp.exp(s - m)
        den = den + jnp.sum(e, axis=-1, keepdims=True)
        pv = _dot(e.astype(BF16), v)
        acc = pv if acc is None else acc + pv
    return acc * (1.0 / den)


def _route(h2, wr, br):
    logits = _dot_nt(wr, h2.astype(BF16)) + br
    eid = lax.broadcasted_iota(jnp.int32, logits.shape, 0).astype(F32)
    vals, idxs = [], []
    chosen = jnp.zeros(logits.shape, F32)
    for _ in range(TOP_K):
        m = jnp.max(logits, axis=0, keepdims=True)
        idx = jnp.min(jnp.where(logits == m, eid, float(N_EXPERTS)), axis=0, keepdims=True)
        vals.append(m)
        idxs.append(idx)
        hit = eid == idx
        chosen = jnp.where(hit, 1.0, chosen)
        logits = jnp.where(hit, -jnp.inf, logits)
    counts = jnp.sum(chosen, axis=1, keepdims=True).astype(jnp.int32)
    ex = [jnp.exp(v - vals[0]) for v in vals]
    tot = ex[0] + ex[1] + ex[2] + ex[3]
    inv = 1.0 / tot
    gates = [e * inv for e in ex]
    rowid = lax.broadcasted_iota(jnp.int32, (SUBLANES, h2.shape[0]), 0)
    topi = jnp.zeros((SUBLANES, h2.shape[0]), F32)
    topg = jnp.zeros((SUBLANES, h2.shape[0]), F32)
    for k in range(TOP_K):
        topi = jnp.where(rowid == k, idxs[k], topi)
        topg = jnp.where(rowid == k, gates[k], topg)
    return topi.astype(jnp.int32), topg, counts


SAMPLE_STEP_SEQS = 32
SAMPLE_ATTN_SEQS = 16
_QKV_COLS = (_OFF_Q, Q_W + 2 * KV_W)
_LATE_COLS = ((_OFF_GB, 2 * CONV_CH), (_OFF_XC, CONV_CH), (_OFF_BA, D_MODEL), (_OFF_BC, D_MODEL))


def _in_proj(proj, h_bf, w_in_ref, group):
    off, width = group
    proj[off] = _dot(h_bf, w_in_ref[:, off:off + width])


def _cols(proj, off, width):
    for start, piece in proj.items():
        if start <= off and off + width <= start + piece.shape[1]:
            return piece[:, off - start:off - start + width]
    raise KeyError(off)


def _merge_and_route(x, o_bf, y_conv, proj, mod, n2g_ref, w_ao_ref, w_co_ref, w_o_ref,
                     wr_ref, br_ref, before_tail=None):
    sh1, sc1, g1, sh2, sc2, g2 = mod
    a = _dot(o_bf, w_ao_ref[...])
    c = _dot(y_conv.astype(BF16), w_co_ref[...])
    m = _sigmoid(_cols(proj, _OFF_BA, D_MODEL)) * a + _sigmoid(_cols(proj, _OFF_BC, D_MODEL)) * c
    x1 = x + g1 * _dot(m.astype(BF16), w_o_ref[...])
    if before_tail is not None:
        before_tail()
    h2 = _modulated_norm(x1, n2g_ref[...], sh2, sc2)
    return (x1, h2) + _route(h2, wr_ref[...], br_ref[...])


def _prompt_body(sinks_ref, x_ref, mod_ref, rope_ref, xn_ref, modn_ref,
                 n1g_ref, n2g_ref, w_in_ref, bd_ref, qg_ref, kg_ref,
                 w_ao_ref, cw_ref, cb_ref, w_co_ref, w_o_ref, wr_ref, br_ref,
                 x1_ref, h2_ref, topi_ref, topg_ref, kwin_ref, vwin_ref, zwin_ref, cnt_ref,
                 kbuf, vbuf, zbuf, h_buf, qkv_buf, late_buf, *, tb):
    j = pl.program_id(1)
    nsub = tb // WINDOW

    @pl.when(j == 0)
    def _():
        kbuf[0:WINDOW, :] = jnp.zeros((WINDOW, KV_W), F32)
        vbuf[0:WINDOW, :] = jnp.zeros((WINDOW, KV_W), F32)
        zbuf[0:SUBLANES, :] = jnp.zeros((SUBLANES, CONV_CH), F32)

    def front(xr, mr):
        h = _modulated_norm(xr[0], n1g_ref[...], mr[0, 0:1, :], mr[0, 1:2, :]).astype(BF16)
        h_buf[...] = h
        for group, buf in ((_QKV_COLS, qkv_buf), (_LATE_COLS[0], late_buf)):
            buf[...] = _dot(h, w_in_ref[:, group[0]:group[0] + group[1]])

    @pl.when((pl.program_id(0) == 0) & (j == 0))
    def _():
        front(x_ref, mod_ref)

    x = x_ref[0]
    mod = tuple(mod_ref[0, i:i + 1, :] for i in range(6))
    h_bf = h_buf[...]
    proj = {_QKV_COLS[0]: qkv_buf[...], _LATE_COLS[0][0]: late_buf[...]}
    rope = rope_ref[...]
    bd = bd_ref[...]

    k = _rope(_head_norm(_cols(proj, _OFF_K, KV_W), bd, kg_ref[...]), rope)
    v = _cols(proj, _OFF_V, KV_W)
    kbuf[WINDOW:WINDOW + tb, :] = k
    vbuf[WINDOW:WINDOW + tb, :] = v
    kwin_ref[0] = k[tb - WINDOW:tb]
    vwin_ref[0] = v[tb - WINDOW:tb]
    k_exp = _pair_expand(kbuf[...])
    v_exp = _pair_expand(vbuf[...])

    _in_proj(proj, h_bf, w_in_ref, _LATE_COLS[1])
    _in_proj(proj, h_bf, w_in_ref, _LATE_COLS[2])
    qn = _head_norm(_cols(proj, _OFF_Q, Q_W), bd, qg_ref[...])
    q_tiles = []
    for t in range(Q_W // LANES):
        qt = _rope(qn[:, t * LANES:(t + 1) * LANES], rope)
        q_tiles.append((qt * ATTN_SCALE).astype(BF16))

    row = lax.broadcasted_iota(jnp.int32, (WINDOW, 2 * WINDOW), 0)
    col = lax.broadcasted_iota(jnp.int32, (WINDOW, 2 * WINDOW), 1)
    band = (col > row) & (col <= row + WINDOW)
    first_key = jnp.where(j > 0, 0, WINDOW)
    band_first = band & (col >= first_key)
    o_rows = []
    for i in range(nsub):
        for group in _LATE_COLS[3:][i::nsub]:
            _in_proj(proj, h_bf, w_in_ref, group)
        mask = band_first if i == 0 else band
        keys = slice(i * WINDOW, (i + 2) * WINDOW)
        o_tiles = []
        for t in range(Q_W // LANES):
            c = t // 2
            qt = q_tiles[t][i * WINDOW:(i + 1) * WINDOW]
            o_t = None
            for half in range(2):
                s = jnp.where(mask, _dot_nt(qt, k_exp[c][half][keys]), NEG_INF)
                sink = sinks_ref[c * GROUP + (t % 2) * 2 + half]
                part = _softmax_pv([s], [v_exp[c][half][keys]], sink)
                o_t = part if o_t is None else o_t + part
            o_tiles.append(o_t.astype(BF16))
        o_rows.append(jnp.concatenate(o_tiles, axis=1))
    o_bf = jnp.concatenate(o_rows, axis=0) if nsub > 1 else o_rows[0]
    kbuf[0:WINDOW, :] = kbuf[tb:tb + WINDOW, :]
    vbuf[0:WINDOW, :] = vbuf[tb:tb + WINDOW, :]

    z = _cols(proj, _OFF_GC, CONV_CH) * _cols(proj, _OFF_XC, CONV_CH)
    zbuf[SUBLANES:SUBLANES + tb, :] = z
    zwin_ref[0] = z[tb - SUBLANES:tb]
    z1 = zbuf[SUBLANES - 1:SUBLANES - 1 + tb, :]
    z2 = zbuf[SUBLANES - 2:SUBLANES - 2 + tb, :]
    conv = cb_ref[...] + cw_ref[0:1, :] * z2 + cw_ref[1:2, :] * z1 + cw_ref[2:3, :] * z
    y_conv = _cols(proj, _OFF_GB, CONV_CH) * conv
    zbuf[0:SUBLANES, :] = zbuf[tb:tb + SUBLANES, :]

    x1, h2, topi, topg, counts = _merge_and_route(x, o_bf, y_conv, proj, mod, n2g_ref, w_ao_ref, w_co_ref,
                                                  w_o_ref, wr_ref, br_ref,
                                                  before_tail=lambda: front(xn_ref, modn_ref))
    x1_ref[0] = x1
    h2_ref[0] = h2.astype(BF16)
    topi_ref[0] = topi
    topg_ref[0] = topg
    cnt_ref[...] = counts


def _prompt_mixer(x, mod, rope, sinks, weights, tb):
    nb, seq, _ = x.shape
    nj = seq // tb
    consts = weights

    def nxt(b, j):
        s = jnp.minimum(b * nj + j + 1, nb * nj - 1)
        return s // nj, s % nj
    in_specs = [
        pl.BlockSpec((1, tb, D_MODEL), lambda b, j, s: (b, j, 0)),
        pl.BlockSpec((1, 6, D_MODEL), lambda b, j, s: (b, 0, 0)),
        pl.BlockSpec((tb, 3 * LANES), lambda b, j, s: (j, 0)),
        pl.BlockSpec((1, tb, D_MODEL), lambda b, j, s: (*nxt(b, j), 0)),
        pl.BlockSpec((1, 6, D_MODEL), lambda b, j, s: (nxt(b, j)[0], 0, 0)),
    ] + [_const_spec(w.shape) for w in consts]
    out_specs = [
        pl.BlockSpec((1, tb, D_MODEL), lambda b, j, s: (b, j, 0)),
        pl.BlockSpec((1, tb, D_MODEL), lambda b, j, s: (b, j, 0)),
        pl.BlockSpec((1, SUBLANES, tb), lambda b, j, s: (b, 0, j)),
        pl.BlockSpec((1, SUBLANES, tb), lambda b, j, s: (b, 0, j)),
        pl.BlockSpec((1, WINDOW, KV_W), lambda b, j, s: (b, 0, 0)),
        pl.BlockSpec((1, WINDOW, KV_W), lambda b, j, s: (b, 0, 0)),
        pl.BlockSpec((1, SUBLANES, CONV_CH), lambda b, j, s: (b, 0, 0)),
        pl.BlockSpec((None, None, N_EXPERTS, 1), lambda b, j, s: (b, j, 0, 0)),
    ]
    out_shape = [
        jax.ShapeDtypeStruct((nb, seq, D_MODEL), F32),
        jax.ShapeDtypeStruct((nb, seq, D_MODEL), BF16),
        jax.ShapeDtypeStruct((nb, SUBLANES, seq), jnp.int32),
        jax.ShapeDtypeStruct((nb, SUBLANES, seq), F32),
        jax.ShapeDtypeStruct((nb, WINDOW, KV_W), F32),
        jax.ShapeDtypeStruct((nb, WINDOW, KV_W), F32),
        jax.ShapeDtypeStruct((nb, SUBLANES, CONV_CH), F32),
        jax.ShapeDtypeStruct((nb, nj, N_EXPERTS, 1), jnp.int32),
    ]
    return pl.pallas_call(
        functools.partial(_prompt_body, tb=tb),
        grid_spec=pltpu.PrefetchScalarGridSpec(
            num_scalar_prefetch=1, grid=(nb, seq // tb),
            in_specs=in_specs, out_specs=out_specs,
            scratch_shapes=[pltpu.VMEM((tb + WINDOW, KV_W), F32),
                            pltpu.VMEM((tb + WINDOW, KV_W), F32),
                            pltpu.VMEM((tb + SUBLANES, CONV_CH), F32),
                            pltpu.VMEM((tb, D_MODEL), BF16),
                            pltpu.VMEM((tb, _QKV_COLS[1]), F32),
                            pltpu.VMEM((tb, _LATE_COLS[0][1]), F32)]),
        out_shape=out_shape,
        compiler_params=pltpu.CompilerParams(
            dimension_semantics=("arbitrary", "arbitrary"), vmem_limit_bytes=VMEM_LIMIT),
        name="prompt_mixer",
    )(sinks, x, mod, rope, x, mod, *consts)


def _sample_body(sinks_ref, x_ref, mod_ref, rope_ref, ck_ref, cv_ref, st_ref,
                 n1g_ref, n2g_ref, w_in_ref, bd_ref, qg_ref, kg_ref,
                 w_ao_ref, cw_ref, cb_ref, w_co_ref, w_o_ref, wr_ref, br_ref,
                 x1_ref, h2_ref, topi_ref, topg_ref, knew_ref, vnew_ref, z_ref, cnt_ref, *, nseq, dseq):
    rows = nseq * dseq
    wbuf = ck_ref.shape[1]
    x = x_ref[...]
    mod = tuple(_expand_rows(mod_ref[i], dseq) for i in range(6))
    h_bf = _modulated_norm(x, n1g_ref[...], mod[0], mod[1]).astype(BF16)
    proj = {}
    for group in (_QKV_COLS,) + _LATE_COLS:
        _in_proj(proj, h_bf, w_in_ref, group)
    rope = rope_ref[...]
    bd = bd_ref[...]

    k = _rope(_head_norm(_cols(proj, _OFF_K, KV_W), bd, kg_ref[...]), rope)
    v = _cols(proj, _OFF_V, KV_W)
    for win_ref, cache_ref, new in ((knew_ref, ck_ref, k), (vnew_ref, cv_ref, v)):
        win_ref[:, 0:wbuf - dseq, :] = cache_ref[:, dseq:, :]
        win_ref[:, wbuf - dseq:, :] = new.reshape(nseq, dseq, KV_W)
    gseq = SAMPLE_ATTN_SEQS
    grows = gseq * dseq
    qrow = lax.broadcasted_iota(jnp.int32, (grows, gseq * wbuf), 0)
    ccol = lax.broadcasted_iota(jnp.int32, (grows, gseq * wbuf), 1)
    mask_c = ((ccol // wbuf) == (qrow // dseq)) & ((ccol % wbuf) > (qrow % dseq) + (wbuf - WINDOW))
    qrow_n = lax.broadcasted_iota(jnp.int32, (grows, grows), 0)
    ncol = lax.broadcasted_iota(jnp.int32, (grows, grows), 1)
    mask_n = ((ncol // dseq) == (qrow_n // dseq)) & ((ncol % dseq) <= (qrow_n % dseq))

    qn = _head_norm(_cols(proj, _OFF_Q, Q_W), bd, qg_ref[...])
    q_tiles = [(_rope(qn[:, t * LANES:(t + 1) * LANES], rope) * ATTN_SCALE).astype(BF16)
               for t in range(Q_W // LANES)]
    o_rows = []
    for g0 in range(0, nseq, gseq):
        rs = slice(g0 * dseq, g0 * dseq + grows)
        kn_exp = _pair_expand(k[rs])
        vn_exp = _pair_expand(v[rs])
        kc_exp = _pair_expand(ck_ref[g0:g0 + gseq].reshape(gseq * wbuf, KV_W))
        vc_exp = _pair_expand(cv_ref[g0:g0 + gseq].reshape(gseq * wbuf, KV_W))
        o_tiles = []
        for t in range(Q_W // LANES):
            c = t // 2
            o_t = None
            for half in range(2):
                s_c = jnp.where(mask_c, _dot_nt(q_tiles[t][rs], kc_exp[c][half]), NEG_INF)
                s_n = jnp.where(mask_n, _dot_nt(q_tiles[t][rs], kn_exp[c][half]), NEG_INF)
                sink = sinks_ref[c * GROUP + (t % 2) * 2 + half]
                part = _softmax_pv([s_c, s_n], [vc_exp[c][half], vn_exp[c][half]], sink)
                o_t = part if o_t is None else o_t + part
            o_tiles.append(o_t.astype(BF16))
        o_rows.append(jnp.concatenate(o_tiles, axis=1))
    o_bf = jnp.concatenate(o_rows, axis=0) if len(o_rows) > 1 else o_rows[0]

    z = _cols(proj, _OFF_GC, CONV_CH) * _cols(proj, _OFF_XC, CONV_CH)
    z_ref[...] = z
    r = lax.broadcasted_iota(jnp.int32, z.shape, 0) % dseq
    st0 = _expand_rows(st_ref[0], dseq)
    st1 = _expand_rows(st_ref[1], dseq)
    z1 = jnp.where(r == 0, st1, pltpu.roll(z, 1, 0))
    z2 = jnp.where(r == 0, st0, jnp.where(r == 1, st1, pltpu.roll(z, 2, 0)))
    conv = cb_ref[...] + cw_ref[0:1, :] * z2 + cw_ref[1:2, :] * z1 + cw_ref[2:3, :] * z
    y_conv = _cols(proj, _OFF_GB, CONV_CH) * conv

    x1, h2, topi, topg, counts = _merge_and_route(x, o_bf, y_conv, proj, mod, n2g_ref, w_ao_ref, w_co_ref,
                                                  w_o_ref, wr_ref, br_ref)
    x1_ref[...] = x1
    h2_ref[...] = h2.astype(BF16)
    topi_ref[...] = topi
    topg_ref[...] = topg
    cnt_ref[...] = counts


def _sample_mixer(x, mod, rope, cache_k, cache_v, state, sinks, weights, nseq):
    ntok = x.shape[0]
    nall, wbuf, _ = cache_k.shape
    dseq = ntok // nall
    rows = nseq * dseq
    consts = weights
    in_specs = [
        pl.BlockSpec((rows, D_MODEL), lambda i, s: (i, 0)),
        pl.BlockSpec((6, nseq, 1, D_MODEL), lambda i, s: (0, i, 0, 0)),
        pl.BlockSpec((rows, 3 * LANES), lambda i, s: (i, 0)),
        pl.BlockSpec((nseq, wbuf, KV_W), lambda i, s: (i, 0, 0)),
        pl.BlockSpec((nseq, wbuf, KV_W), lambda i, s: (i, 0, 0)),
        pl.BlockSpec((2, nseq, 1, CONV_CH), lambda i, s: (0, i, 0, 0)),
    ] + [_const_spec(w.shape) for w in consts]
    out_specs = [
        pl.BlockSpec((rows, D_MODEL), lambda i, s: (i, 0)),
        pl.BlockSpec((rows, D_MODEL), lambda i, s: (i, 0)),
        pl.BlockSpec((SUBLANES, rows), lambda i, s: (0, i)),
        pl.BlockSpec((SUBLANES, rows), lambda i, s: (0, i)),
        pl.BlockSpec((nseq, wbuf, KV_W), lambda i, s: (i, 0, 0)),
        pl.BlockSpec((nseq, wbuf, KV_W), lambda i, s: (i, 0, 0)),
        pl.BlockSpec((rows, CONV_CH), lambda i, s: (i, 0)),
        pl.BlockSpec((None, N_EXPERTS, 1), lambda i, s: (i, 0, 0)),
    ]
    out_shape = [
        jax.ShapeDtypeStruct((ntok, D_MODEL), F32),
        jax.ShapeDtypeStruct((ntok, D_MODEL), BF16),
        jax.ShapeDtypeStruct((SUBLANES, ntok), jnp.int32),
        jax.ShapeDtypeStruct((SUBLANES, ntok), F32),
        jax.ShapeDtypeStruct((nall, wbuf, KV_W), F32),
        jax.ShapeDtypeStruct((nall, wbuf, KV_W), F32),
        jax.ShapeDtypeStruct((ntok, CONV_CH), F32),
        jax.ShapeDtypeStruct((ntok // rows, N_EXPERTS, 1), jnp.int32),
    ]
    return pl.pallas_call(
        functools.partial(_sample_body, nseq=nseq, dseq=dseq),
        grid_spec=pltpu.PrefetchScalarGridSpec(
            num_scalar_prefetch=1, grid=(ntok // rows,),
            in_specs=in_specs, out_specs=out_specs),
        out_shape=out_shape,
        compiler_params=pltpu.CompilerParams(
            dimension_semantics=("arbitrary",), vmem_limit_bytes=VMEM_LIMIT),
        name="sample_mixer",
    )(sinks, x, mod, rope, cache_k, cache_v, state, *consts)


SORT_TOKENS = 512
RUN_ALIGN = SUBLANES
EXPERT_TILE = 1024
HALF_D = D_MODEL // 2
LOCAL_ROWS = SORT_TOKENS * TOP_K + N_EXPERTS * RUN_ALIGN
UNSORT_CHUNK = 256
assert LOCAL_ROWS % UNSORT_CHUNK == 0
_HI_MASK = 0xFFFF0000


def _pack_halves(x):
    lo = pltpu.bitcast(x[:, :HALF_D], jnp.uint32)
    hi = pltpu.bitcast(x[:, HALF_D:], jnp.uint32)
    return hi | (lo >> 16)


def _unpack_halves(w):
    lo = pltpu.bitcast(w << 16, F32).astype(BF16)
    hi = pltpu.bitcast(w & jnp.uint32(_HI_MASK), F32).astype(BF16)
    return lo, hi


def _local_positions(topi, lstart_col, tri):
    ntok = topi.shape[1]
    eid = lax.broadcasted_iota(jnp.int32, (N_EXPERTS, ntok), 0)
    hits = [eid == topi[k:k + 1, :] for k in range(TOP_K)]
    chosen = jnp.zeros((N_EXPERTS, ntok), F32)
    for h in hits:
        chosen = jnp.where(h, 1.0, chosen)
    base = _dot(chosen.astype(BF16), tri) + lstart_col
    return [jnp.sum(jnp.where(h, base, 0.0), axis=0, keepdims=True) for h in hits]


def _run_copy(loc, glob, sem, lofs, gofs, n, to_global):
    lo = loc.at[pl.ds(pl.multiple_of(lofs, RUN_ALIGN), n)]
    gl = glob.at[pl.ds(pl.multiple_of(gofs, RUN_ALIGN), n)]
    return pltpu.make_async_copy(lo, gl, sem) if to_global else pltpu.make_async_copy(gl, lo, sem)


def _start_runs(meta, blk, loc, glob, sem, to_global):
    lstart_ref, goff_ref, cpad_ref = meta
    for e in range(N_EXPERTS):
        n = pl.multiple_of(cpad_ref[blk * N_EXPERTS + e], RUN_ALIGN)
        _run_copy(loc, glob, sem, lstart_ref[blk * N_EXPERTS + e], goff_ref[blk * N_EXPERTS + e], n,
                  to_global).start()


def _wait_runs(nrows, loc, glob, sem, to_global):
    _run_copy(loc, glob, sem, 0, 0, pl.multiple_of(nrows, RUN_ALIGN), to_global).wait()


def _sort_body(lstart_ref, goff_ref, cpad_ref, ltot_ref, fstart_ref, flen_ref, nact_ref,
               hp_ref, hs_ref, topi_ref, lcol_ref, tri_ref, xs_hbm, pos_ref, buf, zbuf, sems, fsem, *, nblk_p, nblk):
    i = pl.program_id(0)
    slot = i % 2
    meta = (lstart_ref, goff_ref, cpad_ref)

    @pl.when(i == 0)
    def _():
        zbuf[...] = jnp.zeros(zbuf.shape, jnp.uint32)

        def fill(e, tot):
            n = pl.multiple_of(flen_ref[e], RUN_ALIGN)

            @pl.when(n > 0)
            def _():
                _run_copy(zbuf, xs_hbm, fsem, 0, fstart_ref[e], n, True).start()
            return tot + n
        total = lax.fori_loop(0, N_EXPERTS, fill, 0)

        @pl.when(total > 0)
        def _():
            _wait_runs(total, zbuf, xs_hbm, fsem, True)

        def fill_tile(t, carry):
            _run_copy(zbuf, xs_hbm, fsem, 0, t * EXPERT_TILE, EXPERT_TILE, True).start()
            return carry
        lax.fori_loop(nact_ref[0], xs_hbm.shape[0] // EXPERT_TILE, fill_tile, 0)

    def run(h_ref):
        pos = [p.astype(jnp.int32) for p in _local_positions(topi_ref[...], lcol_ref[...], tri_ref[...])]
        h = h_ref[...]
        for r0 in range(0, LOCAL_ROWS, UNSORT_CHUNK):
            rowid = lax.broadcasted_iota(jnp.int32, (UNSORT_CHUNK, SORT_TOKENS), 0) + r0
            onehot = jnp.zeros((UNSORT_CHUNK, SORT_TOKENS), F32)
            for p in pos:
                onehot = jnp.where(rowid == p, 1.0, onehot)
            buf[slot, r0:r0 + UNSORT_CHUNK, :] = _pack_halves(_dot(onehot.astype(BF16), h))
        krow = lax.broadcasted_iota(jnp.int32, (SUBLANES, SORT_TOKENS), 0)
        pos_rows = jnp.zeros((SUBLANES, SORT_TOKENS), jnp.int32)
        for k in range(TOP_K):
            pos_rows = jnp.where(krow == k, pos[k], pos_rows)
        pos_ref[...] = pos_rows

    @pl.when(i < nblk_p)
    def _():
        run(hp_ref)

    @pl.when(i >= nblk_p)
    def _():
        run(hs_ref)

    _start_runs(meta, i, buf.at[slot], xs_hbm, sems.at[slot], True)

    @pl.when(i > 0)
    def _():
        _wait_runs(ltot_ref[jnp.maximum(i - 1, 0)], buf.at[1 - slot], xs_hbm, sems.at[1 - slot], True)

    @pl.when(i == nblk - 1)
    def _():
        _wait_runs(ltot_ref[i], buf.at[slot], xs_hbm, sems.at[slot], True)

        def wait_tile(t, carry):
            _run_copy(zbuf, xs_hbm, fsem, 0, t * EXPERT_TILE, EXPERT_TILE, True).wait()
            return carry
        lax.fori_loop(nact_ref[0], xs_hbm.shape[0] // EXPERT_TILE, wait_tile, 0)


def _sort_tokens(plan, h_p, h_s, topi_all, tri, n_slots):
    nblk_p = h_p.shape[0] // SORT_TOKENS
    nblk = nblk_p + h_s.shape[0] // SORT_TOKENS
    return pl.pallas_call(
        functools.partial(_sort_body, nblk_p=nblk_p, nblk=nblk),
        grid_spec=pltpu.PrefetchScalarGridSpec(
            num_scalar_prefetch=7, grid=(nblk,),
            in_specs=[pl.BlockSpec((SORT_TOKENS, D_MODEL), lambda i, *_: (jnp.minimum(i, nblk_p - 1), 0)),
                      pl.BlockSpec((SORT_TOKENS, D_MODEL), lambda i, *_: (jnp.maximum(i - nblk_p, 0), 0)),
                      pl.BlockSpec((SUBLANES, SORT_TOKENS), lambda i, *_: (0, i)),
                      pl.BlockSpec((None, N_EXPERTS, 1), lambda i, *_: (i, 0, 0)),
                      _const_spec(tri.shape)],
            out_specs=[pl.BlockSpec(memory_space=pl.ANY),
                       pl.BlockSpec((SUBLANES, SORT_TOKENS), lambda i, *_: (0, i))],
            scratch_shapes=[pltpu.VMEM((2, LOCAL_ROWS, HALF_D), jnp.uint32),
                            pltpu.VMEM((EXPERT_TILE, HALF_D), jnp.uint32),
                            pltpu.SemaphoreType.DMA((2,)),
                            pltpu.SemaphoreType.DMA]),
        out_shape=[jax.ShapeDtypeStruct((n_slots, HALF_D), jnp.uint32),
                   jax.ShapeDtypeStruct(topi_all.shape, jnp.int32)],
        compiler_params=pltpu.CompilerParams(
            dimension_semantics=("arbitrary",), vmem_limit_bytes=VMEM_LIMIT),
        name="moe_sort",
    )(plan["lstart"], plan["goff"], plan["cpad"], plan["ltot"], plan["fill_start"], plan["fill_len"], plan["nact"],
      h_p, h_s, topi_all, plan["lstart_col"], tri)


def _experts_body(tile_exp_ref, nact_ref, rows_ref, xs_ref, wg_hbm, bg_ref, wu_hbm, bu_ref, wd_hbm, bd_ref,
                  ys_ref, w_f32, w_bf, sems):
    t = pl.program_id(0)
    nact = nact_ref[0]
    w_hbm = (wg_hbm, wu_hbm, wd_hbm)

    def weight_copies(e, slot):
        return [pltpu.make_async_copy(w.at[e], w_f32.at[slot, j], sems.at[slot, j]) for j, w in enumerate(w_hbm)]

    @pl.when(t < nact)
    def _():
        e = tile_exp_ref[t]
        slot = e % 2

        @pl.when(t == 0)
        def _():
            for cp in weight_copies(e, slot):
                cp.start()

        @pl.when((t == 0) | (e != tile_exp_ref[jnp.maximum(t - 1, 0)]))
        def _():
            @pl.when(e + 1 < N_EXPERTS)
            def _():
                for cp in weight_copies(e + 1, 1 - slot):
                    cp.start()
            for j, cp in enumerate(weight_copies(e, slot)):
                cp.wait()
                w_bf[j] = w_f32[slot, j].astype(BF16)

        def swiglu(rows):
            x = jnp.concatenate(_unpack_halves(xs_ref[0:rows, :]), axis=1)
            g = _dot(x, w_bf[0]) + bg_ref[0]
            u = _dot(x, w_bf[1]) + bu_ref[0]
            g = jnp.minimum(g, SWIGLU_LIMIT)
            u = jnp.clip(u, -SWIGLU_LIMIT, SWIGLU_LIMIT)
            a = g * _sigmoid(SWIGLU_ALPHA * g) * (u + 1.0)
            out = _dot(a.astype(BF16), w_bf[2]) + bd_ref[0]
            ys_ref[0:rows, :] = _pack_halves(out.astype(BF16).astype(F32))

        quarter = EXPERT_TILE // 4
        quarters = (rows_ref[t] + quarter - 1) // quarter
        for q in range(1, 5):
            @pl.when(quarters == q)
            def _(q=q):
                swiglu(q * quarter)
                if q < 4:
                    ys_ref[q * quarter:, :] = jnp.zeros((EXPERT_TILE - q * quarter, HALF_D), jnp.uint32)

    @pl.when(t >= nact)
    def _():
        ys_ref[...] = jnp.zeros(ys_ref.shape, jnp.uint32)


def _experts(tile_exp, nact, tile_rows, xs, wg, bg, wu, bu, wd, bd):
    n_tiles = tile_exp.shape[0]

    def active(t, na):
        return jnp.minimum(t, na[0] - 1)
    wspec = pl.BlockSpec(memory_space=pl.ANY)
    bspec = pl.BlockSpec((1, 1, D_MODEL), lambda t, te, na, tr: (te[active(t, na)], 0, 0))
    xspec = pl.BlockSpec((EXPERT_TILE, HALF_D), lambda t, te, na, tr: (active(t, na), 0))
    return pl.pallas_call(
        _experts_body,
        grid_spec=pltpu.PrefetchScalarGridSpec(
            num_scalar_prefetch=3, grid=(n_tiles,),
            in_specs=[xspec, wspec, bspec, wspec, bspec, wspec, bspec],
            out_specs=pl.BlockSpec((EXPERT_TILE, HALF_D), lambda t, te, na, tr: (t, 0)),
            scratch_shapes=[pltpu.VMEM((2, 3, D_MODEL, D_MODEL), F32),
                            pltpu.VMEM((3, D_MODEL, D_MODEL), BF16),
                            pltpu.SemaphoreType.DMA((2, 3))]),
        out_shape=jax.ShapeDtypeStruct(xs.shape, jnp.uint32),
        compiler_params=pltpu.CompilerParams(
            dimension_semantics=("arbitrary",), vmem_limit_bytes=VMEM_LIMIT),
        name="experts",
    )(tile_exp, nact, tile_rows, xs, wg, bg, wu, bu, wd, bd)


def _unsort_body(lstart_ref, goff_ref, cpad_ref, ltot_ref,
                 ys_hbm, x1_ref, g2_ref, pos_ref, gate_ref, y_ref, buf, sems, *, blk0, nblk):
    i = pl.program_id(0)
    b = blk0 + i
    slot = i % 2
    meta = (lstart_ref, goff_ref, cpad_ref)

    @pl.when(i == 0)
    def _():
        buf[...] = jnp.zeros(buf.shape, jnp.uint32)
        _start_runs(meta, b, buf.at[0], ys_hbm, sems.at[0], False)

    @pl.when(i + 1 < nblk)
    def _():
        _start_runs(meta, b + 1, buf.at[1 - slot], ys_hbm, sems.at[1 - slot], False)

    _wait_runs(ltot_ref[b], buf.at[slot], ys_hbm, sems.at[slot], False)
    pos_rows = pos_ref[...]
    gate_rows = gate_ref[...]
    tn = (((0,), (0,)), ((), ()))
    acc_lo = acc_hi = None
    for c0 in range(0, LOCAL_ROWS, UNSORT_CHUNK):
        rowid = lax.broadcasted_iota(jnp.int32, (UNSORT_CHUNK, SORT_TOKENS), 0) + c0
        weights = jnp.zeros((UNSORT_CHUNK, SORT_TOKENS), F32)
        for k in range(TOP_K):
            weights = jnp.where(rowid == pos_rows[k:k + 1, :], gate_rows[k:k + 1, :], weights)
        weights = weights.astype(BF16)
        y_lo, y_hi = _unpack_halves(buf[slot, c0:c0 + UNSORT_CHUNK, :])
        d_lo = lax.dot_general(weights, y_lo, tn, preferred_element_type=F32)
        d_hi = lax.dot_general(weights, y_hi, tn, preferred_element_type=F32)
        acc_lo = d_lo if acc_lo is None else acc_lo + d_lo
        acc_hi = d_hi if acc_hi is None else acc_hi + d_hi
    g2 = g2_ref[...]
    if g2.ndim == 3:
        g2 = _expand_rows(g2, SORT_TOKENS // g2.shape[0])
    y_ref[:, :HALF_D] = x1_ref[:, :HALF_D] + g2[:, :HALF_D] * acc_lo
    y_ref[:, HALF_D:] = x1_ref[:, HALF_D:] + g2[:, HALF_D:] * acc_hi


def _unsort_tokens(plan, ys, x1, g2, g2_spec, pos_cols, gate_cols, blk0):
    nblk = x1.shape[0] // SORT_TOKENS
    return pl.pallas_call(
        functools.partial(_unsort_body, blk0=blk0, nblk=nblk),
        grid_spec=pltpu.PrefetchScalarGridSpec(
            num_scalar_prefetch=4, grid=(nblk,),
            in_specs=[pl.BlockSpec(memory_space=pl.ANY),
                      pl.BlockSpec((SORT_TOKENS, D_MODEL), lambda i, *_: (i, 0)),
                      g2_spec,
                      pl.BlockSpec((SUBLANES, SORT_TOKENS), lambda i, *_: (0, blk0 + i)),
                      pl.BlockSpec((SUBLANES, SORT_TOKENS), lambda i, *_: (0, blk0 + i))],
            out_specs=pl.BlockSpec((SORT_TOKENS, D_MODEL), lambda i, *_: (i, 0)),
            scratch_shapes=[pltpu.VMEM((2, LOCAL_ROWS, HALF_D), jnp.uint32),
                            pltpu.SemaphoreType.DMA((2,))]),
        out_shape=jax.ShapeDtypeStruct(x1.shape, F32),
        compiler_params=pltpu.CompilerParams(
            dimension_semantics=("arbitrary",), vmem_limit_bytes=VMEM_LIMIT),
        name="moe_unsort",
    )(plan["lstart"], plan["goff"], plan["cpad"], plan["ltot"],
      ys, x1, g2, pos_cols, gate_cols)


def _rope_table(pos):
    half = ROT_DIM // 2
    inv_freq = ROPE_THETA ** (-jnp.arange(0, ROT_DIM, 2, dtype=F32) / ROT_DIM)
    ang = pos.astype(F32)[:, None] * inv_freq[None, :]
    cos, sin = jnp.cos(ang), jnp.sin(ang)
    n = pos.shape[0]
    rest = HEAD_DIM - ROT_DIM
    c = jnp.concatenate([cos, cos, jnp.ones((n, rest), F32)], axis=1)
    s_next = jnp.concatenate([-sin, jnp.zeros((n, half + rest), F32)], axis=1)
    s_prev = jnp.concatenate([jnp.zeros((n, half), F32), sin, jnp.zeros((n, rest), F32)], axis=1)
    rep = LANES // HEAD_DIM
    return jnp.concatenate([jnp.tile(c, (1, rep)), jnp.tile(s_next, (1, rep)), jnp.tile(s_prev, (1, rep))], axis=1)


def _num_expert_tiles(ntok):
    nblk = ntok // SORT_TOKENS
    worst = ntok * TOP_K + nblk * N_EXPERTS * RUN_ALIGN + N_EXPERTS * (EXPERT_TILE - 1)
    return -(-worst // EXPERT_TILE)


def _routing_plan(cnt):
    nblk = cnt.shape[0]
    n_tiles = _num_expert_tiles(nblk * SORT_TOKENS)
    cpad = jnp.maximum(-(-cnt // RUN_ALIGN) * RUN_ALIGN, RUN_ALIGN)
    lstart = jnp.cumsum(cpad, axis=1) - cpad
    tot = cpad.sum(axis=0)
    region = -(-tot // EXPERT_TILE) * EXPERT_TILE
    region_end = jnp.cumsum(region)
    gstart = region_end - region
    goff = gstart[None, :] + jnp.cumsum(cpad, axis=0) - cpad
    tile_start = jnp.arange(n_tiles, dtype=jnp.int32) * EXPERT_TILE
    tile_exp = jnp.minimum((tile_start[:, None] >= region_end[None, :]).astype(jnp.int32).sum(axis=1), N_EXPERTS - 1)
    tile_rows = jnp.clip((gstart + tot)[tile_exp] - tile_start, 0, EXPERT_TILE)
    i32 = lambda a: a.astype(jnp.int32)
    return dict(
        tile_rows=i32(tile_rows),
        lstart=i32(lstart).reshape(-1), goff=i32(goff).reshape(-1), cpad=i32(cpad).reshape(-1),
        ltot=i32(cpad.sum(axis=1)), fill_start=i32(gstart + tot), fill_len=i32(region - tot),
        lstart_col=lstart.astype(F32).reshape(nblk, N_EXPERTS, 1),
        tile_exp=i32(tile_exp), nact=i32(region_end[-1] // EXPERT_TILE).reshape(1))


def kernel(x_prompt, x_sample, cache_k_win, cache_v_win, state_conv, c_prompt, c_sample, w_ada, b_ada, norm1_g, norm2_g, w_in, q_norm_g, k_norm_g, attn_sinks, w_attn_out, conv_w, conv_b, w_conv_out, w_o, w_router, b_router, w_gate, b_gate, w_up, b_up, w_down, b_down):
    nb, seq, _ = x_prompt.shape
    ns, dseq, _ = x_sample.shape
    wbuf = cache_k_win.shape[2]
    assert w_ada.shape[0] == 1, "single-layer step"
    l = 0

    mod = _adaln(jnp.concatenate([c_prompt, c_sample], axis=0), w_ada[l], b_ada[l])
    mod_p = mod[:nb].reshape(nb, 6, D_MODEL)
    mod_s = mod[nb:].reshape(ns, 6, D_MODEL).transpose(1, 0, 2).reshape(6, ns, 1, D_MODEL)

    bd = jnp.asarray(np.kron(np.eye(N_HEADS), np.ones((HEAD_DIM, HEAD_DIM))), BF16)
    rep = LANES // HEAD_DIM
    weights = (norm1_g[l].reshape(1, D_MODEL), norm2_g[l].reshape(1, D_MODEL), w_in[l].astype(BF16), bd,
               jnp.tile(q_norm_g[l], N_HEADS).reshape(1, Q_W), jnp.tile(k_norm_g[l], N_KV).reshape(1, KV_W),
               w_attn_out[l].astype(BF16), conv_w[l], conv_b[l].reshape(1, CONV_CH), w_conv_out[l].astype(BF16),
               w_o[l].astype(BF16), w_router[l].T.astype(BF16), b_router[l].reshape(N_EXPERTS, 1))
    sinks = attn_sinks[l]

    tb = 512
    assert tb == SORT_TOKENS and SORT_TOKENS % (SAMPLE_STEP_SEQS * dseq) == 0
    (x1_p, h2_p, topi_p, topg_p, kwin_p, vwin_p, zwin_p, cnt_p) = _prompt_mixer(
        x_prompt, mod_p, _rope_table(jnp.arange(seq)), sinks, weights, tb)
    pos_s = jnp.tile(PAST_LEN + jnp.arange(dseq), ns)
    state_rows = state_conv[l].transpose(1, 0, 2).reshape(CONV_K - 1, ns, 1, CONV_CH)
    (x1_s, h2_s, topi_s, topg_s, knew, vnew, z_s, cnt_s) = _sample_mixer(
        x_sample.reshape(ns * dseq, D_MODEL), mod_s, _rope_table(pos_s),
        cache_k_win[l].reshape(ns, wbuf, KV_W), cache_v_win[l].reshape(ns, wbuf, KV_W),
        state_rows, sinks, weights, SAMPLE_STEP_SEQS)

    ntok_p = nb * seq
    ntok = ntok_p + ns * dseq
    topi_all = jnp.concatenate([topi_p.transpose(1, 0, 2).reshape(SUBLANES, ntok_p), topi_s], axis=1)
    topg_all = jnp.concatenate([topg_p.transpose(1, 0, 2).reshape(SUBLANES, ntok_p), topg_s], axis=1)
    cnt_s = cnt_s.reshape(-1, SORT_TOKENS // (SAMPLE_STEP_SEQS * dseq), N_EXPERTS).sum(axis=1)
    plan = _routing_plan(jnp.concatenate([cnt_p.reshape(-1, N_EXPERTS), cnt_s], axis=0))
    tri = jnp.asarray(np.triu(np.ones((SORT_TOKENS, SORT_TOKENS)), k=1), BF16)
    xs, pos_all = _sort_tokens(plan, h2_p.reshape(ntok_p, D_MODEL), h2_s, topi_all, tri,
                               _num_expert_tiles(ntok) * EXPERT_TILE)
    pos_cols, gate_cols = pos_all, topg_all
    ys = _experts(plan["tile_exp"], plan["nact"], plan["tile_rows"], xs,
                  w_gate[l], b_gate[l].reshape(N_EXPERTS, 1, D_MODEL),
                  w_up[l], b_up[l].reshape(N_EXPERTS, 1, D_MODEL),
                  w_down[l], b_down[l].reshape(N_EXPERTS, 1, D_MODEL))
    blocks_per_seq = seq // SORT_TOKENS
    y_p = _unsort_tokens(plan, ys, x1_p.reshape(ntok_p, D_MODEL), mod_p[:, 5:6, :],
                         pl.BlockSpec((None, 1, D_MODEL), lambda i, *_: (i // blocks_per_seq, 0, 0)),
                         pos_cols, gate_cols, 0)
    y_s = _unsort_tokens(plan, ys, x1_s, mod_s[5],
                         pl.BlockSpec((SORT_TOKENS // dseq, 1, D_MODEL), lambda i, *_: (i, 0, 0)),
                         pos_cols, gate_cols, ntok_p // SORT_TOKENS)

    n_kv_shape = (N_KV, HEAD_DIM)
    k_win_p = kwin_p.reshape(1, nb, WINDOW, *n_kv_shape)
    v_win_p = vwin_p.reshape(1, nb, WINDOW, *n_kv_shape)
    conv_p = zwin_p[:, SUBLANES - (CONV_K - 1):, :][None]
    k_win_s = knew.reshape(1, ns, wbuf, *n_kv_shape)
    v_win_s = vnew.reshape(1, ns, wbuf, *n_kv_shape)
    conv_s = z_s.reshape(ns, dseq, CONV_CH)[:, dseq - (CONV_K - 1):, :][None]
    return (y_p.reshape(nb, seq, D_MODEL), y_s.reshape(ns, dseq, D_MODEL),
            k_win_p, v_win_p, conv_p, k_win_s, v_win_s, conv_s)
```

```python
import functools

import numpy as np
import jax
import jax.numpy as jnp
from jax import lax
from jax.experimental import pallas as pl
from jax.experimental.pallas import tpu as pltpu

D_MODEL = 1024
HEAD_DIM = 64
N_HEADS = 8
N_KV = 2
GROUP = N_HEADS // N_KV
Q_W = N_HEADS * HEAD_DIM
KV_W = N_KV * HEAD_DIM
WINDOW = 128
ROT_DIM = HEAD_DIM // 4
ROPE_THETA = 500000.0
ATTN_SCALE = HEAD_DIM ** -0.5
CONV_CH = D_MODEL // 2
CONV_K = 3
N_EXPERTS = 32
TOP_K = 4
SWIGLU_ALPHA = 1.702
SWIGLU_LIMIT = 7.0
MOE_BLOCK = 128
NORM_EPS = 1e-5
QK_EPS = 1e-6
NEG_INF = -1e30
PAST_LEN = 16384

LANES = 128
SUBLANES = 8
VMEM_LIMIT = 56 * 1024 * 1024

_OFF_Q = 0
_OFF_K = _OFF_Q + Q_W
_OFF_V = _OFF_K + KV_W
_OFF_GB = _OFF_V + KV_W
_OFF_GC = _OFF_GB + CONV_CH
_OFF_XC = _OFF_GC + CONV_CH
_OFF_BA = _OFF_XC + CONV_CH
_OFF_BC = _OFF_BA + D_MODEL
IN_COLS = _OFF_BC + D_MODEL

BF16 = jnp.bfloat16
F32 = jnp.float32
_NT = (((1,), (1,)), ((), ()))


def _dot(a, b):
    return jnp.dot(a, b, preferred_element_type=F32)


def _dot_nt(a, b):
    return lax.dot_general(a, b, _NT, preferred_element_type=F32)


def _sigmoid(x):
    return 1.0 / (1.0 + jnp.exp(-x))


def _split_bf16(x):
    hi = x.astype(BF16)
    lo = (x - hi.astype(F32)).astype(BF16)
    return hi, lo


def _const_spec(shape):
    nd = len(shape)
    return pl.BlockSpec(shape, lambda *_: (0,) * nd, pipeline_mode=pl.Buffered(1))


def _adaln_body(c_ref, w_ref, b_ref, o_ref):
    c = c_ref[...]
    s = c * _sigmoid(c)
    o_ref[...] = _dot(s.astype(BF16), w_ref[...].astype(BF16)) + b_ref[...]


def _adaln(c, w_ada, b_ada):
    n = c.shape[0]
    cols = w_ada.shape[1]
    bn = 1536
    return pl.pallas_call(
        _adaln_body,
        grid=(cols // bn,),
        in_specs=[pl.BlockSpec((n, D_MODEL), lambda i: (0, 0)),
                  pl.BlockSpec((D_MODEL, bn), lambda i: (0, i)),
                  pl.BlockSpec((1, bn), lambda i: (0, i))],
        out_specs=pl.BlockSpec((n, bn), lambda i: (0, i)),
        out_shape=jax.ShapeDtypeStruct((n, cols), F32),
        compiler_params=pltpu.CompilerParams(vmem_limit_bytes=VMEM_LIMIT),
        name="adaln",
    )(c, w_ada, b_ada.reshape(1, cols))


def _expand_rows(v, rep):
    n, _, w = v.shape
    return jnp.broadcast_to(v, (n, rep, w)).reshape(n * rep, w)


def _modulated_norm(x, g, shift, scale):
    y = x * lax.rsqrt(jnp.mean(x * x, axis=-1, keepdims=True) + NORM_EPS)
    return (y * g) * (1.0 + scale) + shift


def _head_norm(t, bd, g):
    w = t.shape[1]
    sq_hi, sq_lo = _split_bf16(t * t)
    blk = bd[:w, :w]
    ms = (_dot(sq_hi, blk) + _dot(sq_lo, blk)) * (1.0 / HEAD_DIM)
    return t * lax.rsqrt(ms + QK_EPS) * g[:, :w]


def _rope(y, rope):
    c = rope[:, 0:LANES]
    s_next = rope[:, LANES:2 * LANES]
    s_prev = rope[:, 2 * LANES:3 * LANES]
    half = ROT_DIM // 2
    return y * c + pltpu.roll(y, LANES - half, 1) * s_next + pltpu.roll(y, half, 1) * s_prev


def _pair_expand(t):
    lane = lax.broadcasted_iota(jnp.int32, t.shape, 1)
    lo = lane < HEAD_DIM
    r = pltpu.roll(t, HEAD_DIM, 1)
    zero = jnp.zeros_like(t)
    a0 = jnp.where(lo, t, zero).astype(BF16)
    b0 = jnp.where(lo, zero, r).astype(BF16)
    a1 = jnp.where(lo, r, zero).astype(BF16)
    b1 = jnp.where(lo, zero, t).astype(BF16)
    return ((a0, b0), (a1, b1))


def _softmax_pv(s_list, v_list, sink):
    m = jnp.full((s_list[0].shape[0], 1), sink, F32)
    for s in s_list:
        m = jnp.maximum(m, jnp.max(s, axis=-1, keepdims=True))
    den = jnp.exp(sink - m)
    acc = None
    for s, v in zip(s_list, v_list):
        e = jnp.exp(s - m)
        den = den + jnp.sum(e, axis=-1, keepdims=True)
        pv = _dot(e.astype(BF16), v)
        acc = pv if acc is None else acc + pv
    return acc * (1.0 / den)


def _route(h2, wr, br):
    logits = _dot_nt(wr, h2.astype(BF16)) + br
    eid = lax.broadcasted_iota(jnp.int32, logits.shape, 0).astype(F32)
    vals, idxs = [], []
    chosen = jnp.zeros(logits.shape, F32)
    for _ in range(TOP_K):
        m = jnp.max(logits, axis=0, keepdims=True)
        idx = jnp.min(jnp.where(logits == m, eid, float(N_EXPERTS)), axis=0, keepdims=True)
        vals.append(m)
        idxs.append(idx)
        hit = eid == idx
        chosen = jnp.where(hit, 1.0, chosen)
        logits = jnp.where(hit, -jnp.inf, logits)
    counts = jnp.sum(chosen, axis=1, keepdims=True).astype(jnp.int32)
    ex = [jnp.exp(v - vals[0]) for v in vals]
    tot = ex[0] + ex[1] + ex[2] + ex[3]
    inv = 1.0 / tot
    gates = [e * inv for e in ex]
    rowid = lax.broadcasted_iota(jnp.int32, (SUBLANES, h2.shape[0]), 0)
    topi = jnp.zeros((SUBLANES, h2.shape[0]), F32)
    topg = jnp.zeros((SUBLANES, h2.shape[0]), F32)
    for k in range(TOP_K):
        topi = jnp.where(rowid == k, idxs[k], topi)
        topg = jnp.where(rowid == k, gates[k], topg)
    return topi.astype(jnp.int32), topg, counts


SAMPLE_STEP_SEQS = 32
SAMPLE_ATTN_SEQS = 16
_QKV_COLS = (_OFF_Q, Q_W + 2 * KV_W)
_LATE_COLS = ((_OFF_GB, 2 * CONV_CH), (_OFF_XC, CONV_CH), (_OFF_BA, D_MODEL), (_OFF_BC, D_MODEL))


def _in_proj(proj, h_bf, w_in_ref, group):
    off, width = group
    proj[off] = _dot(h_bf, w_in_ref[:, off:off + width])


def _cols(proj, off, width):
    for start, piece in proj.items():
        if start <= off and off + width <= start + piece.shape[1]:
            return piece[:, off - start:off - start + width]
    raise KeyError(off)


def _merge_and_route(x, o_bf, y_conv, proj, mod, n2g_ref, w_ao_ref, w_co_ref, w_o_ref,
                     wr_ref, br_ref, before_tail=None):
    sh1, sc1, g1, sh2, sc2, g2 = mod
    a = _dot(o_bf, w_ao_ref[...])
    c = _dot(y_conv.astype(BF16), w_co_ref[...])
    m = _sigmoid(_cols(proj, _OFF_BA, D_MODEL)) * a + _sigmoid(_cols(proj, _OFF_BC, D_MODEL)) * c
    x1 = x + g1 * _dot(m.astype(BF16), w_o_ref[...])
    if before_tail is not None:
        before_tail()
    h2 = _modulated_norm(x1, n2g_ref[...], sh2, sc2)
    return (x1, h2) + _route(h2, wr_ref[...], br_ref[...])


def _prompt_body(sinks_ref, x_ref, mod_ref, rope_ref, xn_ref, modn_ref,
                 n1g_ref, n2g_ref, w_in_ref, bd_ref, qg_ref, kg_ref,
                 w_ao_ref, cw_ref, cb_ref, w_co_ref, w_o_ref, wr_ref, br_ref,
                 x1_ref, h2_ref, topi_ref, topg_ref, kwin_ref, vwin_ref, zwin_ref, cnt_ref,
                 kbuf, vbuf, zbuf, h_buf, qkv_buf, late_buf, *, tb):
    j = pl.program_id(1)
    nsub = tb // WINDOW

    @pl.when(j == 0)
    def _():
        kbuf[0:WINDOW, :] = jnp.zeros((WINDOW, KV_W), F32)
        vbuf[0:WINDOW, :] = jnp.zeros((WINDOW, KV_W), F32)
        zbuf[0:SUBLANES, :] = jnp.zeros((SUBLANES, CONV_CH), F32)

    def front(xr, mr):
        h = _modulated_norm(xr[0], n1g_ref[...], mr[0, 0:1, :], mr[0, 1:2, :]).astype(BF16)
        h_buf[...] = h
        for group, buf in ((_QKV_COLS, qkv_buf), (_LATE_COLS[0], late_buf)):
            buf[...] = _dot(h, w_in_ref[:, group[0]:group[0] + group[1]])

    @pl.when((pl.program_id(0) == 0) & (j == 0))
    def _():
        front(x_ref, mod_ref)

    x = x_ref[0]
    mod = tuple(mod_ref[0, i:i + 1, :] for i in range(6))
    h_bf = h_buf[...]
    proj = {_QKV_COLS[0]: qkv_buf[...], _LATE_COLS[0][0]: late_buf[...]}
    rope = rope_ref[...]
    bd = bd_ref[...]

    k = _rope(_head_norm(_cols(proj, _OFF_K, KV_W), bd, kg_ref[...]), rope)
    v = _cols(proj, _OFF_V, KV_W)
    kbuf[WINDOW:WINDOW + tb, :] = k
    vbuf[WINDOW:WINDOW + tb, :] = v
    kwin_ref[0] = k[tb - WINDOW:tb]
    vwin_ref[0] = v[tb - WINDOW:tb]
    k_exp = _pair_expand(kbuf[...])
    v_exp = _pair_expand(vbuf[...])

    _in_proj(proj, h_bf, w_in_ref, _LATE_COLS[1])
    _in_proj(proj, h_bf, w_in_ref, _LATE_COLS[2])
    qn = _head_norm(_cols(proj, _OFF_Q, Q_W), bd, qg_ref[...])
    q_tiles = []
    for t in range(Q_W // LANES):
        qt = _rope(qn[:, t * LANES:(t + 1) * LANES], rope)
        q_tiles.append((qt * ATTN_SCALE).astype(BF16))

    row = lax.broadcasted_iota(jnp.int32, (WINDOW, 2 * WINDOW), 0)
    col = lax.broadcasted_iota(jnp.int32, (WINDOW, 2 * WINDOW), 1)
    band = (col > row) & (col <= row + WINDOW)
    first_key = jnp.where(j > 0, 0, WINDOW)
    band_first = band & (col >= first_key)
    o_rows = []
    for i in range(nsub):
        for group in _LATE_COLS[3:][i::nsub]:
            _in_proj(proj, h_bf, w_in_ref, group)
        mask = band_first if i == 0 else band
        keys = slice(i * WINDOW, (i + 2) * WINDOW)
        o_tiles = []
        for t in range(Q_W // LANES):
            c = t // 2
            qt = q_tiles[t][i * WINDOW:(i + 1) * WINDOW]
            o_t = None
            for half in range(2):
                s = jnp.where(mask, _dot_nt(qt, k_exp[c][half][keys]), NEG_INF)
                sink = sinks_ref[c * GROUP + (t % 2) * 2 + half]
                part = _softmax_pv([s], [v_exp[c][half][keys]], sink)
                o_t = part if o_t is None else o_t + part
            o_tiles.append(o_t.astype(BF16))
        o_rows.append(jnp.concatenate(o_tiles, axis=1))
    o_bf = jnp.concatenate(o_rows, axis=0) if nsub > 1 else o_rows[0]
    kbuf[0:WINDOW, :] = kbuf[tb:tb + WINDOW, :]
    vbuf[0:WINDOW, :] = vbuf[tb:tb + WINDOW, :]

    z = _cols(proj, _OFF_GC, CONV_CH) * _cols(proj, _OFF_XC, CONV_CH)
    zbuf[SUBLANES:SUBLANES + tb, :] = z
    zwin_ref[0] = z[tb - SUBLANES:tb]
    z1 = zbuf[SUBLANES - 1:SUBLANES - 1 + tb, :]
    z2 = zbuf[SUBLANES - 2:SUBLANES - 2 + tb, :]
    conv = cb_ref[...] + cw_ref[0:1, :] * z2 + cw_ref[1:2, :] * z1 + cw_ref[2:3, :] * z
    y_conv = _cols(proj, _OFF_GB, CONV_CH) * conv
    zbuf[0:SUBLANES, :] = zbuf[tb:tb + SUBLANES, :]

    x1, h2, topi, topg, counts = _merge_and_route(x, o_bf, y_conv, proj, mod, n2g_ref, w_ao_ref, w_co_ref,
                                                  w_o_ref, wr_ref, br_ref,
                                                  before_tail=lambda: front(xn_ref, modn_ref))
    x1_ref[0] = x1
    h2_ref[0] = h2.astype(BF16)
    topi_ref[0] = topi
    topg_ref[0] = topg
    cnt_ref[...] = counts


def _prompt_mixer(x, mod, rope, sinks, weights, tb):
    nb, seq, _ = x.shape
    nj = seq // tb
    consts = weights

    def nxt(b, j):
        s = jnp.minimum(b * nj + j + 1, nb * nj - 1)
        return s // nj, s % nj
    in_specs = [
        pl.BlockSpec((1, tb, D_MODEL), lambda b, j, s: (b, j, 0)),
        pl.BlockSpec((1, 6, D_MODEL), lambda b, j, s: (b, 0, 0)),
        pl.BlockSpec((tb, 3 * LANES), lambda b, j, s: (j, 0)),
        pl.BlockSpec((1, tb, D_MODEL), lambda b, j, s: (*nxt(b, j), 0)),
        pl.BlockSpec((1, 6, D_MODEL), lambda b, j, s: (nxt(b, j)[0], 0, 0)),
    ] + [_const_spec(w.shape) for w in consts]
    out_specs = [
        pl.BlockSpec((1, tb, D_MODEL), lambda b, j, s: (b, j, 0)),
        pl.BlockSpec((1, tb, D_MODEL), lambda b, j, s: (b, j, 0)),
        pl.BlockSpec((1, SUBLANES, tb), lambda b, j, s: (b, 0, j)),
        pl.BlockSpec((1, SUBLANES, tb), lambda b, j, s: (b, 0, j)),
        pl.BlockSpec((1, WINDOW, KV_W), lambda b, j, s: (b, 0, 0)),
        pl.BlockSpec((1, WINDOW, KV_W), lambda b, j, s: (b, 0, 0)),
        pl.BlockSpec((1, SUBLANES, CONV_CH), lambda b, j, s: (b, 0, 0)),
        pl.BlockSpec((None, None, N_EXPERTS, 1), lambda b, j, s: (b, j, 0, 0)),
    ]
    out_shape = [
        jax.ShapeDtypeStruct((nb, seq, D_MODEL), F32),
        jax.ShapeDtypeStruct((nb, seq, D_MODEL), BF16),
        jax.ShapeDtypeStruct((nb, SUBLANES, seq), jnp.int32),
        jax.ShapeDtypeStruct((nb, SUBLANES, seq), F32),
        jax.ShapeDtypeStruct((nb, WINDOW, KV_W), F32),
        jax.ShapeDtypeStruct((nb, WINDOW, KV_W), F32),
        jax.ShapeDtypeStruct((nb, SUBLANES, CONV_CH), F32),
        jax.ShapeDtypeStruct((nb, nj, N_EXPERTS, 1), jnp.int32),
    ]
    return pl.pallas_call(
        functools.partial(_prompt_body, tb=tb),
        grid_spec=pltpu.PrefetchScalarGridSpec(
            num_scalar_prefetch=1, grid=(nb, seq // tb),
            in_specs=in_specs, out_specs=out_specs,
            scratch_shapes=[pltpu.VMEM((tb + WINDOW, KV_W), F32),
                            pltpu.VMEM((tb + WINDOW, KV_W), F32),
                            pltpu.VMEM((tb + SUBLANES, CONV_CH), F32),
                            pltpu.VMEM((tb, D_MODEL), BF16),
                            pltpu.VMEM((tb, _QKV_COLS[1]), F32),
                            pltpu.VMEM((tb, _LATE_COLS[0][1]), F32)]),
        out_shape=out_shape,
        compiler_params=pltpu.CompilerParams(
            dimension_semantics=("arbitrary", "arbitrary"), vmem_limit_bytes=VMEM_LIMIT),
        name="prompt_mixer",
    )(sinks, x, mod, rope, x, mod, *consts)


def _sample_body(sinks_ref, x_ref, mod_ref, rope_ref, ck_ref, cv_ref, st_ref,
                 n1g_ref, n2g_ref, w_in_ref, bd_ref, qg_ref, kg_ref,
                 w_ao_ref, cw_ref, cb_ref, w_co_ref, w_o_ref, wr_ref, br_ref,
                 x1_ref, h2_ref, topi_ref, topg_ref, knew_ref, vnew_ref, z_ref, cnt_ref, *, nseq, dseq):
    rows = nseq * dseq
    wbuf = ck_ref.shape[1]
    x = x_ref[...]
    mod = tuple(_expand_rows(mod_ref[i], dseq) for i in range(6))
    h_bf = _modulated_norm(x, n1g_ref[...], mod[0], mod[1]).astype(BF16)
    proj = {}
    for group in (_QKV_COLS,) + _LATE_COLS:
        _in_proj(proj, h_bf, w_in_ref, group)
    rope = rope_ref[...]
    bd = bd_ref[...]

    k = _rope(_head_norm(_cols(proj, _OFF_K, KV_W), bd, kg_ref[...]), rope)
    v = _cols(proj, _OFF_V, KV_W)
    for win_ref, cache_ref, new in ((knew_ref, ck_ref, k), (vnew_ref, cv_ref, v)):
        win_ref[:, 0:wbuf - dseq, :] = cache_ref[:, dseq:, :]
        win_ref[:, wbuf - dseq:, :] = new.reshape(nseq, dseq, KV_W)
    gseq = SAMPLE_ATTN_SEQS
    grows = gseq * dseq
    qrow = lax.broadcasted_iota(jnp.int32, (grows, gseq * wbuf), 0)
    ccol = lax.broadcasted_iota(jnp.int32, (grows, gseq * wbuf), 1)
    mask_c = ((ccol // wbuf) == (qrow // dseq)) & ((ccol % wbuf) > (qrow % dseq) + (wbuf - WINDOW))
    qrow_n = lax.broadcasted_iota(jnp.int32, (grows, grows), 0)
    ncol = lax.broadcasted_iota(jnp.int32, (grows, grows), 1)
    mask_n = ((ncol // dseq) == (qrow_n // dseq)) & ((ncol % dseq) <= (qrow_n % dseq))

    qn = _head_norm(_cols(proj, _OFF_Q, Q_W), bd, qg_ref[...])
    q_tiles = [(_rope(qn[:, t * LANES:(t + 1) * LANES], rope) * ATTN_SCALE).astype(BF16)
               for t in range(Q_W // LANES)]
    o_rows = []
    for g0 in range(0, nseq, gseq):
        rs = slice(g0 * dseq, g0 * dseq + grows)
        kn_exp = _pair_expand(k[rs])
        vn_exp = _pair_expand(v[rs])
        kc_exp = _pair_expand(ck_ref[g0:g0 + gseq].reshape(gseq * wbuf, KV_W))
        vc_exp = _pair_expand(cv_ref[g0:g0 + gseq].reshape(gseq * wbuf, KV_W))
        o_tiles = []
        for t in range(Q_W // LANES):
            c = t // 2
            o_t = None
            for half in range(2):
                s_c = jnp.where(mask_c, _dot_nt(q_tiles[t][rs], kc_exp[c][half]), NEG_INF)
                s_n = jnp.where(mask_n, _dot_nt(q_tiles[t][rs], kn_exp[c][half]), NEG_INF)
                sink = sinks_ref[c * GROUP + (t % 2) * 2 + half]
                part = _softmax_pv([s_c, s_n], [vc_exp[c][half], vn_exp[c][half]], sink)
                o_t = part if o_t is None else o_t + part
            o_tiles.append(o_t.astype(BF16))
        o_rows.append(jnp.concatenate(o_tiles, axis=1))
    o_bf = jnp.concatenate(o_rows, axis=0) if len(o_rows) > 1 else o_rows[0]

    z = _cols(proj, _OFF_GC, CONV_CH) * _cols(proj, _OFF_XC, CONV_CH)
    z_ref[...] = z
    r = lax.broadcasted_iota(jnp.int32, z.shape, 0) % dseq
    st0 = _expand_rows(st_ref[0], dseq)
    st1 = _expand_rows(st_ref[1], dseq)
    z1 = jnp.where(r == 0, st1, pltpu.roll(z, 1, 0))
    z2 = jnp.where(r == 0, st0, jnp.where(r == 1, st1, pltpu.roll(z, 2, 0)))
    conv = cb_ref[...] + cw_ref[0:1, :] * z2 + cw_ref[1:2, :] * z1 + cw_ref[2:3, :] * z
    y_conv = _cols(proj, _OFF_GB, CONV_CH) * conv

    x1, h2, topi, topg, counts = _merge_and_route(x, o_bf, y_conv, proj, mod, n2g_ref, w_ao_ref, w_co_ref,
                                                  w_o_ref, wr_ref, br_ref)
    x1_ref[...] = x1
    h2_ref[...] = h2.astype(BF16)
    topi_ref[...] = topi
    topg_ref[...] = topg
    cnt_ref[...] = counts


def _sample_mixer(x, mod, rope, cache_k, cache_v, state, sinks, weights, nseq):
    ntok = x.shape[0]
    nall, wbuf, _ = cache_k.shape
    dseq = ntok // nall
    rows = nseq * dseq
    consts = weights
    in_specs = [
        pl.BlockSpec((rows, D_MODEL), lambda i, s: (i, 0)),
        pl.BlockSpec((6, nseq, 1, D_MODEL), lambda i, s: (0, i, 0, 0)),
        pl.BlockSpec((rows, 3 * LANES), lambda i, s: (i, 0)),
        pl.BlockSpec((nseq, wbuf, KV_W), lambda i, s: (i, 0, 0)),
        pl.BlockSpec((nseq, wbuf, KV_W), lambda i, s: (i, 0, 0)),
        pl.BlockSpec((2, nseq, 1, CONV_CH), lambda i, s: (0, i, 0, 0)),
    ] + [_const_spec(w.shape) for w in consts]
    out_specs = [
        pl.BlockSpec((rows, D_MODEL), lambda i, s: (i, 0)),
        pl.BlockSpec((rows, D_MODEL), lambda i, s: (i, 0)),
        pl.BlockSpec((SUBLANES, rows), lambda i, s: (0, i)),
        pl.BlockSpec((SUBLANES, rows), lambda i, s: (0, i)),
        pl.BlockSpec((nseq, wbuf, KV_W), lambda i, s: (i, 0, 0)),
        pl.BlockSpec((nseq, wbuf, KV_W), lambda i, s: (i, 0, 0)),
        pl.BlockSpec((rows, CONV_CH), lambda i, s: (i, 0)),
        pl.BlockSpec((None, N_EXPERTS, 1), lambda i, s: (i, 0, 0)),
    ]
    out_shape = [
        jax.ShapeDtypeStruct((ntok, D_MODEL), F32),
        jax.ShapeDtypeStruct((ntok, D_MODEL), BF16),
        jax.ShapeDtypeStruct((SUBLANES, ntok), jnp.int32),
        jax.ShapeDtypeStruct((SUBLANES, ntok), F32),
        jax.ShapeDtypeStruct((nall, wbuf, KV_W), F32),
        jax.ShapeDtypeStruct((nall, wbuf, KV_W), F32),
        jax.ShapeDtypeStruct((ntok, CONV_CH), F32),
        jax.ShapeDtypeStruct((ntok // rows, N_EXPERTS, 1), jnp.int32),
    ]
    return pl.pallas_call(
        functools.partial(_sample_body, nseq=nseq, dseq=dseq),
        grid_spec=pltpu.PrefetchScalarGridSpec(
            num_scalar_prefetch=1, grid=(ntok // rows,),
            in_specs=in_specs, out_specs=out_specs),
        out_shape=out_shape,
        compiler_params=pltpu.CompilerParams(
            dimension_semantics=("arbitrary",), vmem_limit_bytes=VMEM_LIMIT),
        name="sample_mixer",
    )(sinks, x, mod, rope, cache_k, cache_v, state, *consts)


SORT_TOKENS = 512
RUN_ALIGN = SUBLANES
EXPERT_TILE = 1024
HALF_D = D_MODEL // 2
LOCAL_ROWS = SORT_TOKENS * TOP_K + N_EXPERTS * RUN_ALIGN
UNSORT_CHUNK = 256
assert LOCAL_ROWS % UNSORT_CHUNK == 0
_HI_MASK = 0xFFFF0000


def _pack_halves(x):
    lo = pltpu.bitcast(x[:, :HALF_D], jnp.uint32)
    hi = pltpu.bitcast(x[:, HALF_D:], jnp.uint32)
    return hi | (lo >> 16)


def _unpack_halves(w):
    lo = pltpu.bitcast(w << 16, F32).astype(BF16)
    hi = pltpu.bitcast(w & jnp.uint32(_HI_MASK), F32).astype(BF16)
    return lo, hi


def _local_positions(topi, lstart_col, tri):
    ntok = topi.shape[1]
    eid = lax.broadcasted_iota(jnp.int32, (N_EXPERTS, ntok), 0)
    hits = [eid == topi[k:k + 1, :] for k in range(TOP_K)]
    chosen = jnp.zeros((N_EXPERTS, ntok), F32)
    for h in hits:
        chosen = jnp.where(h, 1.0, chosen)
    base = _dot(chosen.astype(BF16), tri) + lstart_col
    return [jnp.sum(jnp.where(h, base, 0.0), axis=0, keepdims=True) for h in hits]


def _run_copy(loc, glob, sem, lofs, gofs, n, to_global):
    lo = loc.at[pl.ds(pl.multiple_of(lofs, RUN_ALIGN), n)]
    gl = glob.at[pl.ds(pl.multiple_of(gofs, RUN_ALIGN), n)]
    return pltpu.make_async_copy(lo, gl, sem) if to_global else pltpu.make_async_copy(gl, lo, sem)


def _start_runs(meta, blk, loc, glob, sem, to_global):
    lstart_ref, goff_ref, cpad_ref = meta
    for e in range(N_EXPERTS):
        n = pl.multiple_of(cpad_ref[blk * N_EXPERTS + e], RUN_ALIGN)
        _run_copy(loc, glob, sem, lstart_ref[blk * N_EXPERTS + e], goff_ref[blk * N_EXPERTS + e], n,
                  to_global).start()


def _wait_runs(nrows, loc, glob, sem, to_global):
    _run_copy(loc, glob, sem, 0, 0, pl.multiple_of(nrows, RUN_ALIGN), to_global).wait()


def _sort_body(lstart_ref, goff_ref, cpad_ref, ltot_ref, fstart_ref, flen_ref, nact_ref,
               hp_ref, hs_ref, topi_ref, lcol_ref, tri_ref, xs_hbm, pos_ref, buf, zbuf, sems, fsem, *, nblk_p, nblk):
    i = pl.program_id(0)
    slot = i % 2
    meta = (lstart_ref, goff_ref, cpad_ref)

    @pl.when(i == 0)
    def _():
        zbuf[...] = jnp.zeros(zbuf.shape, jnp.uint32)

        def fill(e, tot):
            n = pl.multiple_of(flen_ref[e], RUN_ALIGN)

            @pl.when(n > 0)
            def _():
                _run_copy(zbuf, xs_hbm, fsem, 0, fstart_ref[e], n, True).start()
            return tot + n
        total = lax.fori_loop(0, N_EXPERTS, fill, 0)

        @pl.when(total > 0)
        def _():
            _wait_runs(total, zbuf, xs_hbm, fsem, True)

        def fill_tile(t, carry):
            _run_copy(zbuf, xs_hbm, fsem, 0, t * EXPERT_TILE, EXPERT_TILE, True).start()
            return carry
        lax.fori_loop(nact_ref[0], xs_hbm.shape[0] // EXPERT_TILE, fill_tile, 0)

    def run(h_ref):
        pos = [p.astype(jnp.int32) for p in _local_positions(topi_ref[...], lcol_ref[...], tri_ref[...])]
        h = h_ref[...]
        for r0 in range(0, LOCAL_ROWS, UNSORT_CHUNK):
            rowid = lax.broadcasted_iota(jnp.int32, (UNSORT_CHUNK, SORT_TOKENS), 0) + r0
            onehot = jnp.zeros((UNSORT_CHUNK, SORT_TOKENS), F32)
            for p in pos:
                onehot = jnp.where(rowid == p, 1.0, onehot)
            buf[slot, r0:r0 + UNSORT_CHUNK, :] = _pack_halves(_dot(onehot.astype(BF16), h))
        krow = lax.broadcasted_iota(jnp.int32, (SUBLANES, SORT_TOKENS), 0)
        pos_rows = jnp.zeros((SUBLANES, SORT_TOKENS), jnp.int32)
        for k in range(TOP_K):
            pos_rows = jnp.where(krow == k, pos[k], pos_rows)
        pos_ref[...] = pos_rows

    @pl.when(i < nblk_p)
    def _():
        run(hp_ref)

    @pl.when(i >= nblk_p)
    def _():
        run(hs_ref)

    _start_runs(meta, i, buf.at[slot], xs_hbm, sems.at[slot], True)

    @pl.when(i > 0)
    def _():
        _wait_runs(ltot_ref[jnp.maximum(i - 1, 0)], buf.at[1 - slot], xs_hbm, sems.at[1 - slot], True)

    @pl.when(i == nblk - 1)
    def _():
        _wait_runs(ltot_ref[i], buf.at[slot], xs_hbm, sems.at[slot], True)

        def wait_tile(t, carry):
            _run_copy(zbuf, xs_hbm, fsem, 0, t * EXPERT_TILE, EXPERT_TILE, True).wait()
            return carry
        lax.fori_loop(nact_ref[0], xs_hbm.shape[0] // EXPERT_TILE, wait_tile, 0)


def _sort_tokens(plan, h_p, h_s, topi_all, tri, n_slots):
    nblk_p = h_p.shape[0] // SORT_TOKENS
    nblk = nblk_p + h_s.shape[0] // SORT_TOKENS
    return pl.pallas_call(
        functools.partial(_sort_body, nblk_p=nblk_p, nblk=nblk),
        grid_spec=pltpu.PrefetchScalarGridSpec(
            num_scalar_prefetch=7, grid=(nblk,),
            in_specs=[pl.BlockSpec((SORT_TOKENS, D_MODEL), lambda i, *_: (jnp.minimum(i, nblk_p - 1), 0)),
                      pl.BlockSpec((SORT_TOKENS, D_MODEL), lambda i, *_: (jnp.maximum(i - nblk_p, 0), 0)),
                      pl.BlockSpec((SUBLANES, SORT_TOKENS), lambda i, *_: (0, i)),
                      pl.BlockSpec((None, N_EXPERTS, 1), lambda i, *_: (i, 0, 0)),
                      _const_spec(tri.shape)],
            out_specs=[pl.BlockSpec(memory_space=pl.ANY),
                       pl.BlockSpec((SUBLANES, SORT_TOKENS), lambda i, *_: (0, i))],
            scratch_shapes=[pltpu.VMEM((2, LOCAL_ROWS, HALF_D), jnp.uint32),
                            pltpu.VMEM((EXPERT_TILE, HALF_D), jnp.uint32),
                            pltpu.SemaphoreType.DMA((2,)),
                            pltpu.SemaphoreType.DMA]),
        out_shape=[jax.ShapeDtypeStruct((n_slots, HALF_D), jnp.uint32),
                   jax.ShapeDtypeStruct(topi_all.shape, jnp.int32)],
        compiler_params=pltpu.CompilerParams(
            dimension_semantics=("arbitrary",), vmem_limit_bytes=VMEM_LIMIT),
        name="moe_sort",
    )(plan["lstart"], plan["goff"], plan["cpad"], plan["ltot"], plan["fill_start"], plan["fill_len"], plan["nact"],
      h_p, h_s, topi_all, plan["lstart_col"], tri)


def _experts_body(tile_exp_ref, nact_ref, rows_ref, xs_ref, wg_hbm, bg_ref, wu_hbm, bu_ref, wd_hbm, bd_ref,
                  ys_ref, w_f32, w_bf, sems):
    t = pl.program_id(0)
    nact = nact_ref[0]
    w_hbm = (wg_hbm, wu_hbm, wd_hbm)

    def weight_copies(e, slot):
        return [pltpu.make_async_copy(w.at[e], w_f32.at[slot, j], sems.at[slot, j]) for j, w in enumerate(w_hbm)]

    @pl.when(t < nact)
    def _():
        e = tile_exp_ref[t]
        slot = e % 2

        @pl.when(t == 0)
        def _():
            for cp in weight_copies(e, slot):
                cp.start()

        @pl.when((t == 0) | (e != tile_exp_ref[jnp.maximum(t - 1, 0)]))
        def _():
            @pl.when(e + 1 < N_EXPERTS)
            def _():
                for cp in weight_copies(e + 1, 1 - slot):
                    cp.start()
            for j, cp in enumerate(weight_copies(e, slot)):
                cp.wait()
                w_bf[j] = w_f32[slot, j].astype(BF16)

        def swiglu(rows):
            x = jnp.concatenate(_unpack_halves(xs_ref[0:rows, :]), axis=1)
            g = _dot(x, w_bf[0]) + bg_ref[0]
            u = _dot(x, w_bf[1]) + bu_ref[0]
            g = jnp.minimum(g, SWIGLU_LIMIT)
            u = jnp.clip(u, -SWIGLU_LIMIT, SWIGLU_LIMIT)
            a = g * _sigmoid(SWIGLU_ALPHA * g) * (u + 1.0)
            out = _dot(a.astype(BF16), w_bf[2]) + bd_ref[0]
            ys_ref[0:rows, :] = _pack_halves(out.astype(BF16).astype(F32))

        quarter = EXPERT_TILE // 4
        quarters = (rows_ref[t] + quarter - 1) // quarter
        for q in range(1, 5):
            @pl.when(quarters == q)
            def _(q=q):
                swiglu(q * quarter)
                if q < 4:
                    ys_ref[q * quarter:, :] = jnp.zeros((EXPERT_TILE - q * quarter, HALF_D), jnp.uint32)


def _experts(tile_exp, nact, tile_rows, xs, wg, bg, wu, bu, wd, bd):
    n_tiles = tile_exp.shape[0]

    def active(t, na):
        return jnp.minimum(t, na[0] - 1)
    wspec = pl.BlockSpec(memory_space=pl.ANY)
    bspec = pl.BlockSpec((1, 1, D_MODEL), lambda t, te, na, tr: (te[active(t, na)], 0, 0))
    xspec = pl.BlockSpec((EXPERT_TILE, HALF_D), lambda t, te, na, tr: (active(t, na), 0))
    return pl.pallas_call(
        _experts_body,
        grid_spec=pltpu.PrefetchScalarGridSpec(
            num_scalar_prefetch=3, grid=(n_tiles,),
            in_specs=[xspec, wspec, bspec, wspec, bspec, wspec, bspec],
            out_specs=xspec,
            scratch_shapes=[pltpu.VMEM((2, 3, D_MODEL, D_MODEL), F32),
                            pltpu.VMEM((3, D_MODEL, D_MODEL), BF16),
                            pltpu.SemaphoreType.DMA((2, 3))]),
        out_shape=jax.ShapeDtypeStruct(xs.shape, jnp.uint32),
        input_output_aliases={3: 0},
        compiler_params=pltpu.CompilerParams(
            dimension_semantics=("arbitrary",), vmem_limit_bytes=VMEM_LIMIT),
        name="experts",
    )(tile_exp, nact, tile_rows, xs, wg, bg, wu, bu, wd, bd)


def _unsort_body(lstart_ref, goff_ref, cpad_ref, ltot_ref,
                 ys_hbm, x1_ref, g2_ref, pos_ref, gate_ref, y_ref, buf, sems, *, blk0, nblk):
    i = pl.program_id(0)
    b = blk0 + i
    slot = i % 2
    meta = (lstart_ref, goff_ref, cpad_ref)

    @pl.when(i == 0)
    def _():
        buf[...] = jnp.zeros(buf.shape, jnp.uint32)
        _start_runs(meta, b, buf.at[0], ys_hbm, sems.at[0], False)

    @pl.when(i + 1 < nblk)
    def _():
        _start_runs(meta, b + 1, buf.at[1 - slot], ys_hbm, sems.at[1 - slot], False)

    _wait_runs(ltot_ref[b], buf.at[slot], ys_hbm, sems.at[slot], False)
    pos_rows = pos_ref[...]
    gate_rows = gate_ref[...]
    tn = (((0,), (0,)), ((), ()))
    acc_lo = acc_hi = None
    for c0 in range(0, LOCAL_ROWS, UNSORT_CHUNK):
        rowid = lax.broadcasted_iota(jnp.int32, (UNSORT_CHUNK, SORT_TOKENS), 0) + c0
        weights = jnp.zeros((UNSORT_CHUNK, SORT_TOKENS), F32)
        for k in range(TOP_K):
            weights = jnp.where(rowid == pos_rows[k:k + 1, :], gate_rows[k:k + 1, :], weights)
        weights = weights.astype(BF16)
        y_lo, y_hi = _unpack_halves(buf[slot, c0:c0 + UNSORT_CHUNK, :])
        d_lo = lax.dot_general(weights, y_lo, tn, preferred_element_type=F32)
        d_hi = lax.dot_general(weights, y_hi, tn, preferred_element_type=F32)
        acc_lo = d_lo if acc_lo is None else acc_lo + d_lo
        acc_hi = d_hi if acc_hi is None else acc_hi + d_hi
    g2 = g2_ref[...]
    if g2.ndim == 3:
        g2 = _expand_rows(g2, SORT_TOKENS // g2.shape[0])
    y_ref[:, :HALF_D] = x1_ref[:, :HALF_D] + g2[:, :HALF_D] * acc_lo
    y_ref[:, HALF_D:] = x1_ref[:, HALF_D:] + g2[:, HALF_D:] * acc_hi


def _unsort_tokens(plan, ys, x1, g2, g2_spec, pos_cols, gate_cols, blk0):
    nblk = x1.shape[0] // SORT_TOKENS
    return pl.pallas_call(
        functools.partial(_unsort_body, blk0=blk0, nblk=nblk),
        grid_spec=pltpu.PrefetchScalarGridSpec(
            num_scalar_prefetch=4, grid=(nblk,),
            in_specs=[pl.BlockSpec(memory_space=pl.ANY),
                      pl.BlockSpec((SORT_TOKENS, D_MODEL), lambda i, *_: (i, 0)),
                      g2_spec,
                      pl.BlockSpec((SUBLANES, SORT_TOKENS), lambda i, *_: (0, blk0 + i)),
                      pl.BlockSpec((SUBLANES, SORT_TOKENS), lambda i, *_: (0, blk0 + i))],
            out_specs=pl.BlockSpec((SORT_TOKENS, D_MODEL), lambda i, *_: (i, 0)),
            scratch_shapes=[pltpu.VMEM((2, LOCAL_ROWS, HALF_D), jnp.uint32),
                            pltpu.SemaphoreType.DMA((2,))]),
        out_shape=jax.ShapeDtypeStruct(x1.shape, F32),
        compiler_params=pltpu.CompilerParams(
            dimension_semantics=("arbitrary",), vmem_limit_bytes=VMEM_LIMIT),
        name="moe_unsort",
    )(plan["lstart"], plan["goff"], plan["cpad"], plan["ltot"],
      ys, x1, g2, pos_cols, gate_cols)


def _rope_table(pos):
    half = ROT_DIM // 2
    inv_freq = ROPE_THETA ** (-jnp.arange(0, ROT_DIM, 2, dtype=F32) / ROT_DIM)
    ang = pos.astype(F32)[:, None] * inv_freq[None, :]
    cos, sin = jnp.cos(ang), jnp.sin(ang)
    n = pos.shape[0]
    rest = HEAD_DIM - ROT_DIM
    c = jnp.concatenate([cos, cos, jnp.ones((n, rest), F32)], axis=1)
    s_next = jnp.concatenate([-sin, jnp.zeros((n, half + rest), F32)], axis=1)
    s_prev = jnp.concatenate([jnp.zeros((n, half), F32), sin, jnp.zeros((n, rest), F32)], axis=1)
    rep = LANES // HEAD_DIM
    return jnp.concatenate([jnp.tile(c, (1, rep)), jnp.tile(s_next, (1, rep)), jnp.tile(s_prev, (1, rep))], axis=1)


def _num_expert_tiles(ntok):
    nblk = ntok // SORT_TOKENS
    worst = ntok * TOP_K + nblk * N_EXPERTS * RUN_ALIGN + N_EXPERTS * (EXPERT_TILE - 1)
    return -(-worst // EXPERT_TILE)


def _routing_plan(cnt):
    nblk = cnt.shape[0]
    n_tiles = _num_expert_tiles(nblk * SORT_TOKENS)
    cpad = jnp.maximum(-(-cnt // RUN_ALIGN) * RUN_ALIGN, RUN_ALIGN)
    lstart = jnp.cumsum(cpad, axis=1) - cpad
    tot = cpad.sum(axis=0)
    region = -(-tot // EXPERT_TILE) * EXPERT_TILE
    region_end = jnp.cumsum(region)
    gstart = region_end - region
    goff = gstart[None, :] + jnp.cumsum(cpad, axis=0) - cpad
    tile_start = jnp.arange(n_tiles, dtype=jnp.int32) * EXPERT_TILE
    tile_exp = jnp.minimum((tile_start[:, None] >= region_end[None, :]).astype(jnp.int32).sum(axis=1), N_EXPERTS - 1)
    tile_rows = jnp.clip((gstart + tot)[tile_exp] - tile_start, 0, EXPERT_TILE)
    i32 = lambda a: a.astype(jnp.int32)
    return dict(
        tile_rows=i32(tile_rows),
        lstart=i32(lstart).reshape(-1), goff=i32(goff).reshape(-1), cpad=i32(cpad).reshape(-1),
        ltot=i32(cpad.sum(axis=1)), fill_start=i32(gstart + tot), fill_len=i32(region - tot),
        lstart_col=lstart.astype(F32).reshape(nblk, N_EXPERTS, 1),
        tile_exp=i32(tile_exp), nact=i32(region_end[-1] // EXPERT_TILE).reshape(1))


def kernel(x_prompt, x_sample, cache_k_win, cache_v_win, state_conv, c_prompt, c_sample, w_ada, b_ada, norm1_g, norm2_g, w_in, q_norm_g, k_norm_g, attn_sinks, w_attn_out, conv_w, conv_b, w_conv_out, w_o, w_router, b_router, w_gate, b_gate, w_up, b_up, w_down, b_down):
    nb, seq, _ = x_prompt.shape
    ns, dseq, _ = x_sample.shape
    wbuf = cache_k_win.shape[2]
    assert w_ada.shape[0] == 1, "single-layer step"
    l = 0

    mod = _adaln(jnp.concatenate([c_prompt, c_sample], axis=0), w_ada[l], b_ada[l])
    mod_p = mod[:nb].reshape(nb, 6, D_MODEL)
    mod_s = mod[nb:].reshape(ns, 6, D_MODEL).transpose(1, 0, 2).reshape(6, ns, 1, D_MODEL)

    bd = jnp.asarray(np.kron(np.eye(N_HEADS), np.ones((HEAD_DIM, HEAD_DIM))), BF16)
    rep = LANES // HEAD_DIM
    weights = (norm1_g[l].reshape(1, D_MODEL), norm2_g[l].reshape(1, D_MODEL), w_in[l].astype(BF16), bd,
               jnp.tile(q_norm_g[l], N_HEADS).reshape(1, Q_W), jnp.tile(k_norm_g[l], N_KV).reshape(1, KV_W),
               w_attn_out[l].astype(BF16), conv_w[l], conv_b[l].reshape(1, CONV_CH), w_conv_out[l].astype(BF16),
               w_o[l].astype(BF16), w_router[l].T.astype(BF16), b_router[l].reshape(N_EXPERTS, 1))
    sinks = attn_sinks[l]

    tb = 512
    assert tb == SORT_TOKENS and SORT_TOKENS % (SAMPLE_STEP_SEQS * dseq) == 0
    (x1_p, h2_p, topi_p, topg_p, kwin_p, vwin_p, zwin_p, cnt_p) = _prompt_mixer(
        x_prompt, mod_p, _rope_table(jnp.arange(seq)), sinks, weights, tb)
    pos_s = jnp.tile(PAST_LEN + jnp.arange(dseq), ns)
    state_rows = state_conv[l].transpose(1, 0, 2).reshape(CONV_K - 1, ns, 1, CONV_CH)
    (x1_s, h2_s, topi_s, topg_s, knew, vnew, z_s, cnt_s) = _sample_mixer(
        x_sample.reshape(ns * dseq, D_MODEL), mod_s, _rope_table(pos_s),
        cache_k_win[l].reshape(ns, wbuf, KV_W), cache_v_win[l].reshape(ns, wbuf, KV_W),
        state_rows, sinks, weights, SAMPLE_STEP_SEQS)

    ntok_p = nb * seq
    ntok = ntok_p + ns * dseq
    topi_all = jnp.concatenate([topi_p.transpose(1, 0, 2).reshape(SUBLANES, ntok_p), topi_s], axis=1)
    topg_all = jnp.concatenate([topg_p.transpose(1, 0, 2).reshape(SUBLANES, ntok_p), topg_s], axis=1)
    cnt_s = cnt_s.reshape(-1, SORT_TOKENS // (SAMPLE_STEP_SEQS * dseq), N_EXPERTS).sum(axis=1)
    plan = _routing_plan(jnp.concatenate([cnt_p.reshape(-1, N_EXPERTS), cnt_s], axis=0))
    tri = jnp.asarray(np.triu(np.ones((SORT_TOKENS, SORT_TOKENS)), k=1), BF16)
    xs, pos_all = _sort_tokens(plan, h2_p.reshape(ntok_p, D_MODEL), h2_s, topi_all, tri,
                               _num_expert_tiles(ntok) * EXPERT_TILE)
    pos_cols, gate_cols = pos_all, topg_all
    ys = _experts(plan["tile_exp"], plan["nact"], plan["tile_rows"], xs,
                  w_gate[l], b_gate[l].reshape(N_EXPERTS, 1, D_MODEL),
                  w_up[l], b_up[l].reshape(N_EXPERTS, 1, D_MODEL),
                  w_down[l], b_down[l].reshape(N_EXPERTS, 1, D_MODEL))
    blocks_per_seq = seq // SORT_TOKENS
    y_p = _unsort_tokens(plan, ys, x1_p.reshape(ntok_p, D_MODEL), mod_p[:, 5:6, :],
                         pl.BlockSpec((None, 1, D_MODEL), lambda i, *_: (i // blocks_per_seq, 0, 0)),
                         pos_cols, gate_cols, 0)
    y_s = _unsort_tokens(plan, ys, x1_s, mod_s[5],
                         pl.BlockSpec((SORT_TOKENS // dseq, 1, D_MODEL), lambda i, *_: (i, 0, 0)),
                         pos_cols, gate_cols, ntok_p // SORT_TOKENS)

    n_kv_shape = (N_KV, HEAD_DIM)
    k_win_p = kwin_p.reshape(1, nb, WINDOW, *n_kv_shape)
    v_win_p = vwin_p.reshape(1, nb, WINDOW, *n_kv_shape)
    conv_p = zwin_p[:, SUBLANES - (CONV_K - 1):, :][None]
    k_win_s = knew.reshape(1, ns, wbuf, *n_kv_shape)
    v_win_s = vnew.reshape(1, ns, wbuf, *n_kv_shape)
    conv_s = z_s.reshape(ns, dseq, CONV_CH)[:, dseq - (CONV_K - 1):, :][None]
    return (y_p.reshape(nb, seq, D_MODEL), y_s.reshape(ns, dseq, D_MODEL),
            k_win_p, v_win_p, conv_p, k_win_s, v_win_s, conv_s)
```

```python
import functools

import numpy as np
import jax
import jax.numpy as jnp
from jax import lax
from jax.experimental import pallas as pl
from jax.experimental.pallas import tpu as pltpu

D_MODEL = 1024
HEAD_DIM = 64
N_HEADS = 8
N_KV = 2
GROUP = N_HEADS // N_KV
Q_W = N_HEADS * HEAD_DIM
KV_W = N_KV * HEAD_DIM
WINDOW = 128
ROT_DIM = HEAD_DIM // 4
ROPE_THETA = 500000.0
ATTN_SCALE = HEAD_DIM ** -0.5
CONV_CH = D_MODEL // 2
CONV_K = 3
N_EXPERTS = 32
TOP_K = 4
SWIGLU_ALPHA = 1.702
SWIGLU_LIMIT = 7.0
NORM_EPS = 1e-5
QK_EPS = 1e-6
NEG_INF = -1e30
PAST_LEN = 16384

LANES = 128
SUBLANES = 8
VMEM_LIMIT = 56 * 1024 * 1024
ADALN_COLS = 1536
MIXER_TOKENS = 512

_OFF_Q = 0
_OFF_K = _OFF_Q + Q_W
_OFF_V = _OFF_K + KV_W
_OFF_GB = _OFF_V + KV_W
_OFF_GC = _OFF_GB + CONV_CH
_OFF_XC = _OFF_GC + CONV_CH
_OFF_BA = _OFF_XC + CONV_CH
_OFF_BC = _OFF_BA + D_MODEL

BF16 = jnp.bfloat16
F32 = jnp.float32
_NT = (((1,), (1,)), ((), ()))


def _dot(a, b):
    return jnp.dot(a, b, preferred_element_type=F32)


def _dot_nt(a, b):
    return lax.dot_general(a, b, _NT, preferred_element_type=F32)


def _sigmoid(x):
    return 1.0 / (1.0 + jnp.exp(-x))


def _split_bf16(x):
    hi = x.astype(BF16)
    lo = (x - hi.astype(F32)).astype(BF16)
    return hi, lo


def _const_spec(shape):
    nd = len(shape)
    return pl.BlockSpec(shape, lambda *_: (0,) * nd, pipeline_mode=pl.Buffered(1))


def _adaln_body(c_ref, w_ref, b_ref, o_ref):
    c = c_ref[...]
    s = c * _sigmoid(c)
    o_ref[...] = _dot(s.astype(BF16), w_ref[...].astype(BF16)) + b_ref[...]


def _adaln(c, w_ada, b_ada):
    n = c.shape[0]
    cols = w_ada.shape[1]
    bn = ADALN_COLS
    return pl.pallas_call(
        _adaln_body,
        grid=(cols // bn,),
        in_specs=[pl.BlockSpec((n, D_MODEL), lambda i: (0, 0)),
                  pl.BlockSpec((D_MODEL, bn), lambda i: (0, i)),
                  pl.BlockSpec((1, bn), lambda i: (0, i))],
        out_specs=pl.BlockSpec((n, bn), lambda i: (0, i)),
        out_shape=jax.ShapeDtypeStruct((n, cols), F32),
        compiler_params=pltpu.CompilerParams(vmem_limit_bytes=VMEM_LIMIT),
        name="adaln",
    )(c, w_ada, b_ada.reshape(1, cols))


def _expand_rows(v, rep):
    n, _, w = v.shape
    return jnp.broadcast_to(v, (n, rep, w)).reshape(n * rep, w)


def _modulated_norm(x, g, shift, scale):
    y = x * lax.rsqrt(jnp.mean(x * x, axis=-1, keepdims=True) + NORM_EPS)
    return (y * g) * (1.0 + scale) + shift


def _head_norm(t, bd, g):
    w = t.shape[1]
    sq_hi, sq_lo = _split_bf16(t * t)
    blk = bd[:w, :w]
    ms = (_dot(sq_hi, blk) + _dot(sq_lo, blk)) * (1.0 / HEAD_DIM)
    return t * lax.rsqrt(ms + QK_EPS) * g[:, :w]


def _rope(y, rope):
    c = rope[:, 0:LANES]
    s_next = rope[:, LANES:2 * LANES]
    s_prev = rope[:, 2 * LANES:3 * LANES]
    half = ROT_DIM // 2
    return y * c + pltpu.roll(y, LANES - half, 1) * s_next + pltpu.roll(y, half, 1) * s_prev


def _pair_expand(t):
    lane = lax.broadcasted_iota(jnp.int32, t.shape, 1)
    lo = lane < HEAD_DIM
    r = pltpu.roll(t, HEAD_DIM, 1)
    zero = jnp.zeros_like(t)
    a0 = jnp.where(lo, t, zero).astype(BF16)
    b0 = jnp.where(lo, zero, r).astype(BF16)
    a1 = jnp.where(lo, r, zero).astype(BF16)
    b1 = jnp.where(lo, zero, t).astype(BF16)
    return ((a0, b0), (a1, b1))


def _softmax_pv(s_list, v_list, sink):
    m = jnp.full((s_list[0].shape[0], 1), sink, F32)
    for s in s_list:
        m = jnp.maximum(m, jnp.max(s, axis=-1, keepdims=True))
    den = jnp.exp(sink - m)
    acc = None
    for s, v in zip(s_list, v_list):
        e = jnp.exp(s - m)
        den = den + jnp.sum(e, axis=-1, keepdims=True)
        pv = _dot(e.astype(BF16), v)
        acc = pv if acc is None else acc + pv
    return acc * (1.0 / den)


def _route(h2, wr, br):
    logits = _dot_nt(wr, h2.astype(BF16)) + br
    eid = lax.broadcasted_iota(jnp.int32, logits.shape, 0).astype(F32)
    vals, idxs = [], []
    chosen = jnp.zeros(logits.shape, F32)
    for _ in range(TOP_K):
        m = jnp.max(logits, axis=0, keepdims=True)
        idx = jnp.min(jnp.where(logits == m, eid, float(N_EXPERTS)), axis=0, keepdims=True)
        vals.append(m)
        idxs.append(idx)
        hit = eid == idx
        chosen = jnp.where(hit, 1.0, chosen)
        logits = jnp.where(hit, -jnp.inf, logits)
    counts = jnp.sum(chosen, axis=1, keepdims=True).astype(jnp.int32)
    ex = [jnp.exp(v - vals[0]) for v in vals]
    tot = ex[0] + ex[1] + ex[2] + ex[3]
    inv = 1.0 / tot
    gates = [e * inv for e in ex]
    rowid = lax.broadcasted_iota(jnp.int32, (SUBLANES, h2.shape[0]), 0)
    topi = jnp.zeros((SUBLANES, h2.shape[0]), F32)
    topg = jnp.zeros((SUBLANES, h2.shape[0]), F32)
    for k in range(TOP_K):
        topi = jnp.where(rowid == k, idxs[k], topi)
        topg = jnp.where(rowid == k, gates[k], topg)
    return topi.astype(jnp.int32), topg, counts


SAMPLE_STEP_SEQS = 32
SAMPLE_ATTN_SEQS = 16
_QKV_COLS = (_OFF_Q, Q_W + 2 * KV_W)
_LATE_COLS = ((_OFF_GB, 2 * CONV_CH), (_OFF_XC, CONV_CH), (_OFF_BA, D_MODEL), (_OFF_BC, D_MODEL))


def _in_proj(proj, h_bf, w_in_ref, group):
    off, width = group
    proj[off] = _dot(h_bf, w_in_ref[:, off:off + width])


def _cols(proj, off, width):
    for start, piece in proj.items():
        if start <= off and off + width <= start + piece.shape[1]:
            return piece[:, off - start:off - start + width]
    raise KeyError(off)


def _merge_and_route(x, o_bf, y_conv, proj, mod, n2g_ref, w_ao_ref, w_co_ref, w_o_ref,
                     wr_ref, br_ref, before_tail=None):
    sh1, sc1, g1, sh2, sc2, g2 = mod
    a = _dot(o_bf, w_ao_ref[...])
    c = _dot(y_conv.astype(BF16), w_co_ref[...])
    m = _sigmoid(_cols(proj, _OFF_BA, D_MODEL)) * a + _sigmoid(_cols(proj, _OFF_BC, D_MODEL)) * c
    x1 = x + g1 * _dot(m.astype(BF16), w_o_ref[...])
    if before_tail is not None:
        before_tail()
    h2 = _modulated_norm(x1, n2g_ref[...], sh2, sc2)
    return (x1, h2) + _route(h2, wr_ref[...], br_ref[...])


def _prompt_body(sinks_ref, x_ref, mod_ref, rope_ref, xn_ref, modn_ref,
                 n1g_ref, n2g_ref, w_in_ref, bd_ref, qg_ref, kg_ref,
                 w_ao_ref, cw_ref, cb_ref, w_co_ref, w_o_ref, wr_ref, br_ref,
                 x1_ref, h2_ref, topi_ref, topg_ref, kwin_ref, vwin_ref, zwin_ref, cnt_ref,
                 kbuf, vbuf, zbuf, h_buf, qkv_buf, late_buf, *, tb):
    j = pl.program_id(1)
    nsub = tb // WINDOW

    @pl.when(j == 0)
    def _():
        kbuf[0:WINDOW, :] = jnp.zeros((WINDOW, KV_W), F32)
        vbuf[0:WINDOW, :] = jnp.zeros((WINDOW, KV_W), F32)
        zbuf[0:SUBLANES, :] = jnp.zeros((SUBLANES, CONV_CH), F32)

    def front(xr, mr):
        h = _modulated_norm(xr[0], n1g_ref[...], mr[0, 0:1, :], mr[0, 1:2, :]).astype(BF16)
        h_buf[...] = h
        for group, buf in ((_QKV_COLS, qkv_buf), (_LATE_COLS[0], late_buf)):
            buf[...] = _dot(h, w_in_ref[:, group[0]:group[0] + group[1]])

    @pl.when((pl.program_id(0) == 0) & (j == 0))
    def _():
        front(x_ref, mod_ref)

    x = x_ref[0]
    mod = tuple(mod_ref[0, i:i + 1, :] for i in range(6))
    h_bf = h_buf[...]
    proj = {_QKV_COLS[0]: qkv_buf[...], _LATE_COLS[0][0]: late_buf[...]}
    rope = rope_ref[...]
    bd = bd_ref[...]

    k = _rope(_head_norm(_cols(proj, _OFF_K, KV_W), bd, kg_ref[...]), rope)
    v = _cols(proj, _OFF_V, KV_W)
    kbuf[WINDOW:WINDOW + tb, :] = k
    vbuf[WINDOW:WINDOW + tb, :] = v
    kwin_ref[0] = k[tb - WINDOW:tb]
    vwin_ref[0] = v[tb - WINDOW:tb]
    k_exp = _pair_expand(kbuf[...])
    v_exp = _pair_expand(vbuf[...])

    _in_proj(proj, h_bf, w_in_ref, _LATE_COLS[1])
    _in_proj(proj, h_bf, w_in_ref, _LATE_COLS[2])
    qn = _head_norm(_cols(proj, _OFF_Q, Q_W), bd, qg_ref[...])
    q_tiles = []
    for t in range(Q_W // LANES):
        qt = _rope(qn[:, t * LANES:(t + 1) * LANES], rope)
        q_tiles.append((qt * ATTN_SCALE).astype(BF16))

    row = lax.broadcasted_iota(jnp.int32, (WINDOW, 2 * WINDOW), 0)
    col = lax.broadcasted_iota(jnp.int32, (WINDOW, 2 * WINDOW), 1)
    band = (col > row) & (col <= row + WINDOW)
    first_key = jnp.where(j > 0, 0, WINDOW)
    band_first = band & (col >= first_key)
    o_rows = []
    for i in range(nsub):
        for group in _LATE_COLS[3:][i::nsub]:
            _in_proj(proj, h_bf, w_in_ref, group)
        mask = band_first if i == 0 else band
        keys = slice(i * WINDOW, (i + 2) * WINDOW)
        o_tiles = []
        for t in range(Q_W // LANES):
            c = t // 2
            qt = q_tiles[t][i * WINDOW:(i + 1) * WINDOW]
            o_t = None
            for half in range(2):
                s = jnp.where(mask, _dot_nt(qt, k_exp[c][half][keys]), NEG_INF)
                sink = sinks_ref[c * GROUP + (t % 2) * 2 + half]
                part = _softmax_pv([s], [v_exp[c][half][keys]], sink)
                o_t = part if o_t is None else o_t + part
            o_tiles.append(o_t.astype(BF16))
        o_rows.append(jnp.concatenate(o_tiles, axis=1))
    o_bf = jnp.concatenate(o_rows, axis=0) if nsub > 1 else o_rows[0]
    kbuf[0:WINDOW, :] = kbuf[tb:tb + WINDOW, :]
    vbuf[0:WINDOW, :] = vbuf[tb:tb + WINDOW, :]

    z = _cols(proj, _OFF_GC, CONV_CH) * _cols(proj, _OFF_XC, CONV_CH)
    zbuf[SUBLANES:SUBLANES + tb, :] = z
    zwin_ref[0] = z[tb - SUBLANES:tb]
    z1 = zbuf[SUBLANES - 1:SUBLANES - 1 + tb, :]
    z2 = zbuf[SUBLANES - 2:SUBLANES - 2 + tb, :]
    conv = cb_ref[...] + cw_ref[0:1, :] * z2 + cw_ref[1:2, :] * z1 + cw_ref[2:3, :] * z
    y_conv = _cols(proj, _OFF_GB, CONV_CH) * conv
    zbuf[0:SUBLANES, :] = zbuf[tb:tb + SUBLANES, :]

    x1, h2, topi, topg, counts = _merge_and_route(x, o_bf, y_conv, proj, mod, n2g_ref, w_ao_ref, w_co_ref,
                                                  w_o_ref, wr_ref, br_ref,
                                                  before_tail=lambda: front(xn_ref, modn_ref))
    x1_ref[0] = x1
    h2_ref[0] = h2.astype(BF16)
    topi_ref[0] = topi
    topg_ref[0] = topg
    cnt_ref[...] = counts


def _prompt_mixer(x, mod, rope, sinks, weights, tb):
    nb, seq, _ = x.shape
    nj = seq // tb
    consts = weights

    def nxt(b, j):
        s = jnp.minimum(b * nj + j + 1, nb * nj - 1)
        return s // nj, s % nj
    in_specs = [
        pl.BlockSpec((1, tb, D_MODEL), lambda b, j, s: (b, j, 0)),
        pl.BlockSpec((1, 6, D_MODEL), lambda b, j, s: (b, 0, 0)),
        pl.BlockSpec((tb, 3 * LANES), lambda b, j, s: (j, 0)),
        pl.BlockSpec((1, tb, D_MODEL), lambda b, j, s: (*nxt(b, j), 0)),
        pl.BlockSpec((1, 6, D_MODEL), lambda b, j, s: (nxt(b, j)[0], 0, 0)),
    ] + [_const_spec(w.shape) for w in consts]
    out_specs = [
        pl.BlockSpec((1, tb, D_MODEL), lambda b, j, s: (b, j, 0)),
        pl.BlockSpec((1, tb, D_MODEL), lambda b, j, s: (b, j, 0)),
        pl.BlockSpec((1, SUBLANES, tb), lambda b, j, s: (b, 0, j)),
        pl.BlockSpec((1, SUBLANES, tb), lambda b, j, s: (b, 0, j)),
        pl.BlockSpec((1, WINDOW, KV_W), lambda b, j, s: (b, 0, 0)),
        pl.BlockSpec((1, WINDOW, KV_W), lambda b, j, s: (b, 0, 0)),
        pl.BlockSpec((1, SUBLANES, CONV_CH), lambda b, j, s: (b, 0, 0)),
        pl.BlockSpec((None, None, N_EXPERTS, 1), lambda b, j, s: (b, j, 0, 0)),
    ]
    out_shape = [
        jax.ShapeDtypeStruct((nb, seq, D_MODEL), F32),
        jax.ShapeDtypeStruct((nb, seq, D_MODEL), BF16),
        jax.ShapeDtypeStruct((nb, SUBLANES, seq), jnp.int32),
        jax.ShapeDtypeStruct((nb, SUBLANES, seq), F32),
        jax.ShapeDtypeStruct((nb, WINDOW, KV_W), F32),
        jax.ShapeDtypeStruct((nb, WINDOW, KV_W), F32),
        jax.ShapeDtypeStruct((nb, SUBLANES, CONV_CH), F32),
        jax.ShapeDtypeStruct((nb, nj, N_EXPERTS, 1), jnp.int32),
    ]
    return pl.pallas_call(
        functools.partial(_prompt_body, tb=tb),
        grid_spec=pltpu.PrefetchScalarGridSpec(
            num_scalar_prefetch=1, grid=(nb, seq // tb),
            in_specs=in_specs, out_specs=out_specs,
            scratch_shapes=[pltpu.VMEM((tb + WINDOW, KV_W), F32),
                            pltpu.VMEM((tb + WINDOW, KV_W), F32),
                            pltpu.VMEM((tb + SUBLANES, CONV_CH), F32),
                            pltpu.VMEM((tb, D_MODEL), BF16),
                            pltpu.VMEM((tb, _QKV_COLS[1]), F32),
                            pltpu.VMEM((tb, _LATE_COLS[0][1]), F32)]),
        out_shape=out_shape,
        compiler_params=pltpu.CompilerParams(
            dimension_semantics=("arbitrary", "arbitrary"), vmem_limit_bytes=VMEM_LIMIT),
        name="prompt_mixer",
    )(sinks, x, mod, rope, x, mod, *consts)


def _sample_body(sinks_ref, x_ref, mod_ref, rope_ref, ck_ref, cv_ref, st_ref,
                 n1g_ref, n2g_ref, w_in_ref, bd_ref, qg_ref, kg_ref,
                 w_ao_ref, cw_ref, cb_ref, w_co_ref, w_o_ref, wr_ref, br_ref,
                 x1_ref, h2_ref, topi_ref, topg_ref, knew_ref, vnew_ref, z_ref, cnt_ref, *, nseq, dseq):
    rows = nseq * dseq
    wbuf = ck_ref.shape[1]
    x = x_ref[...]
    mod = tuple(_expand_rows(mod_ref[i], dseq) for i in range(6))
    h_bf = _modulated_norm(x, n1g_ref[...], mod[0], mod[1]).astype(BF16)
    proj = {}
    for group in (_QKV_COLS,) + _LATE_COLS:
        _in_proj(proj, h_bf, w_in_ref, group)
    rope = rope_ref[...]
    bd = bd_ref[...]

    k = _rope(_head_norm(_cols(proj, _OFF_K, KV_W), bd, kg_ref[...]), rope)
    v = _cols(proj, _OFF_V, KV_W)
    for win_ref, cache_ref, new in ((knew_ref, ck_ref, k), (vnew_ref, cv_ref, v)):
        win_ref[:, 0:wbuf - dseq, :] = cache_ref[:, dseq:, :]
        win_ref[:, wbuf - dseq:, :] = new.reshape(nseq, dseq, KV_W)
    gseq = SAMPLE_ATTN_SEQS
    grows = gseq * dseq
    qrow = lax.broadcasted_iota(jnp.int32, (grows, gseq * wbuf), 0)
    ccol = lax.broadcasted_iota(jnp.int32, (grows, gseq * wbuf), 1)
    mask_c = ((ccol // wbuf) == (qrow // dseq)) & ((ccol % wbuf) > (qrow % dseq) + (wbuf - WINDOW))
    qrow_n = lax.broadcasted_iota(jnp.int32, (grows, grows), 0)
    ncol = lax.broadcasted_iota(jnp.int32, (grows, grows), 1)
    mask_n = ((ncol // dseq) == (qrow_n // dseq)) & ((ncol % dseq) <= (qrow_n % dseq))

    qn = _head_norm(_cols(proj, _OFF_Q, Q_W), bd, qg_ref[...])
    q_tiles = [(_rope(qn[:, t * LANES:(t + 1) * LANES], rope) * ATTN_SCALE).astype(BF16)
               for t in range(Q_W // LANES)]
    o_rows = []
    for g0 in range(0, nseq, gseq):
        rs = slice(g0 * dseq, g0 * dseq + grows)
        kn_exp = _pair_expand(k[rs])
        vn_exp = _pair_expand(v[rs])
        kc_exp = _pair_expand(ck_ref[g0:g0 + gseq].reshape(gseq * wbuf, KV_W))
        vc_exp = _pair_expand(cv_ref[g0:g0 + gseq].reshape(gseq * wbuf, KV_W))
        o_tiles = []
        for t in range(Q_W // LANES):
            c = t // 2
            o_t = None
            for half in range(2):
                s_c = jnp.where(mask_c, _dot_nt(q_tiles[t][rs], kc_exp[c][half]), NEG_INF)
                s_n = jnp.where(mask_n, _dot_nt(q_tiles[t][rs], kn_exp[c][half]), NEG_INF)
                sink = sinks_ref[c * GROUP + (t % 2) * 2 + half]
                part = _softmax_pv([s_c, s_n], [vc_exp[c][half], vn_exp[c][half]], sink)
                o_t = part if o_t is None else o_t + part
            o_tiles.append(o_t.astype(BF16))
        o_rows.append(jnp.concatenate(o_tiles, axis=1))
    o_bf = jnp.concatenate(o_rows, axis=0) if len(o_rows) > 1 else o_rows[0]

    z = _cols(proj, _OFF_GC, CONV_CH) * _cols(proj, _OFF_XC, CONV_CH)
    z_ref[...] = z
    r = lax.broadcasted_iota(jnp.int32, z.shape, 0) % dseq
    st0 = _expand_rows(st_ref[0], dseq)
    st1 = _expand_rows(st_ref[1], dseq)
    z1 = jnp.where(r == 0, st1, pltpu.roll(z, 1, 0))
    z2 = jnp.where(r == 0, st0, jnp.where(r == 1, st1, pltpu.roll(z, 2, 0)))
    conv = cb_ref[...] + cw_ref[0:1, :] * z2 + cw_ref[1:2, :] * z1 + cw_ref[2:3, :] * z
    y_conv = _cols(proj, _OFF_GB, CONV_CH) * conv

    x1, h2, topi, topg, counts = _merge_and_route(x, o_bf, y_conv, proj, mod, n2g_ref, w_ao_ref, w_co_ref,
                                                  w_o_ref, wr_ref, br_ref)
    x1_ref[...] = x1
    h2_ref[...] = h2.astype(BF16)
    topi_ref[...] = topi
    topg_ref[...] = topg
    cnt_ref[...] = counts


def _sample_mixer(x, mod, rope, cache_k, cache_v, state, sinks, weights, nseq):
    ntok = x.shape[0]
    nall, wbuf, _ = cache_k.shape
    dseq = ntok // nall
    rows = nseq * dseq
    consts = weights
    in_specs = [
        pl.BlockSpec((rows, D_MODEL), lambda i, s: (i, 0)),
        pl.BlockSpec((6, nseq, 1, D_MODEL), lambda i, s: (0, i, 0, 0)),
        pl.BlockSpec((rows, 3 * LANES), lambda i, s: (i, 0)),
        pl.BlockSpec((nseq, wbuf, KV_W), lambda i, s: (i, 0, 0)),
        pl.BlockSpec((nseq, wbuf, KV_W), lambda i, s: (i, 0, 0)),
        pl.BlockSpec((2, nseq, 1, CONV_CH), lambda i, s: (0, i, 0, 0)),
    ] + [_const_spec(w.shape) for w in consts]
    out_specs = [
        pl.BlockSpec((rows, D_MODEL), lambda i, s: (i, 0)),
        pl.BlockSpec((rows, D_MODEL), lambda i, s: (i, 0)),
        pl.BlockSpec((SUBLANES, rows), lambda i, s: (0, i)),
        pl.BlockSpec((SUBLANES, rows), lambda i, s: (0, i)),
        pl.BlockSpec((nseq, wbuf, KV_W), lambda i, s: (i, 0, 0)),
        pl.BlockSpec((nseq, wbuf, KV_W), lambda i, s: (i, 0, 0)),
        pl.BlockSpec((rows, CONV_CH), lambda i, s: (i, 0)),
        pl.BlockSpec((None, N_EXPERTS, 1), lambda i, s: (i, 0, 0)),
    ]
    out_shape = [
        jax.ShapeDtypeStruct((ntok, D_MODEL), F32),
        jax.ShapeDtypeStruct((ntok, D_MODEL), BF16),
        jax.ShapeDtypeStruct((SUBLANES, ntok), jnp.int32),
        jax.ShapeDtypeStruct((SUBLANES, ntok), F32),
        jax.ShapeDtypeStruct((nall, wbuf, KV_W), F32),
        jax.ShapeDtypeStruct((nall, wbuf, KV_W), F32),
        jax.ShapeDtypeStruct((ntok, CONV_CH), F32),
        jax.ShapeDtypeStruct((ntok // rows, N_EXPERTS, 1), jnp.int32),
    ]
    return pl.pallas_call(
        functools.partial(_sample_body, nseq=nseq, dseq=dseq),
        grid_spec=pltpu.PrefetchScalarGridSpec(
            num_scalar_prefetch=1, grid=(ntok // rows,),
            in_specs=in_specs, out_specs=out_specs),
        out_shape=out_shape,
        compiler_params=pltpu.CompilerParams(
            dimension_semantics=("arbitrary",), vmem_limit_bytes=VMEM_LIMIT),
        name="sample_mixer",
    )(sinks, x, mod, rope, cache_k, cache_v, state, *consts)


SORT_TOKENS = 512
RUN_ALIGN = SUBLANES
EXPERT_TILE = 1024
HALF_D = D_MODEL // 2
LOCAL_ROWS = SORT_TOKENS * TOP_K + N_EXPERTS * RUN_ALIGN
LOCAL_CHUNK = 256
assert LOCAL_ROWS % LOCAL_CHUNK == 0
_HI_MASK = 0xFFFF0000


def _pack_halves(x):
    lo = pltpu.bitcast(x[:, :HALF_D], jnp.uint32)
    hi = pltpu.bitcast(x[:, HALF_D:], jnp.uint32)
    return hi | (lo >> 16)


def _unpack_halves(w):
    lo = pltpu.bitcast(w << 16, F32).astype(BF16)
    hi = pltpu.bitcast(w & jnp.uint32(_HI_MASK), F32).astype(BF16)
    return lo, hi


def _local_positions(topi, lstart_col, tri):
    ntok = topi.shape[1]
    eid = lax.broadcasted_iota(jnp.int32, (N_EXPERTS, ntok), 0)
    hits = [eid == topi[k:k + 1, :] for k in range(TOP_K)]
    chosen = jnp.zeros((N_EXPERTS, ntok), F32)
    for h in hits:
        chosen = jnp.where(h, 1.0, chosen)
    base = _dot(chosen.astype(BF16), tri) + lstart_col
    return [jnp.sum(jnp.where(h, base, 0.0), axis=0, keepdims=True) for h in hits]


def _run_copy(loc, glob, sem, lofs, gofs, n, to_global):
    lo = loc.at[pl.ds(pl.multiple_of(lofs, RUN_ALIGN), n)]
    gl = glob.at[pl.ds(pl.multiple_of(gofs, RUN_ALIGN), n)]
    return pltpu.make_async_copy(lo, gl, sem) if to_global else pltpu.make_async_copy(gl, lo, sem)


def _start_runs(meta, blk, loc, glob, sem, to_global):
    lstart_ref, goff_ref, cpad_ref = meta
    for e in range(N_EXPERTS):
        n = pl.multiple_of(cpad_ref[blk * N_EXPERTS + e], RUN_ALIGN)
        _run_copy(loc, glob, sem, lstart_ref[blk * N_EXPERTS + e], goff_ref[blk * N_EXPERTS + e], n,
                  to_global).start()


def _wait_runs(nrows, loc, glob, sem, to_global):
    _run_copy(loc, glob, sem, 0, 0, pl.multiple_of(nrows, RUN_ALIGN), to_global).wait()


def _sort_body(lstart_ref, goff_ref, cpad_ref, ltot_ref, fstart_ref, flen_ref, nact_ref,
               hp_ref, hs_ref, topi_ref, lcol_ref, tri_ref, xs_hbm, pos_ref, buf, zbuf, sems, fsem, *, nblk_p, nblk):
    i = pl.program_id(0)
    slot = i % 2
    meta = (lstart_ref, goff_ref, cpad_ref)

    @pl.when(i == 0)
    def _():
        zbuf[...] = jnp.zeros(zbuf.shape, jnp.uint32)

        def fill(e, tot):
            n = pl.multiple_of(flen_ref[e], RUN_ALIGN)

            @pl.when(n > 0)
            def _():
                _run_copy(zbuf, xs_hbm, fsem, 0, fstart_ref[e], n, True).start()
            return tot + n
        total = lax.fori_loop(0, N_EXPERTS, fill, 0)

        @pl.when(total > 0)
        def _():
            _wait_runs(total, zbuf, xs_hbm, fsem, True)

        def fill_tile(t, carry):
            _run_copy(zbuf, xs_hbm, fsem, 0, t * EXPERT_TILE, EXPERT_TILE, True).start()
            return carry
        lax.fori_loop(nact_ref[0], xs_hbm.shape[0] // EXPERT_TILE, fill_tile, 0)

    def run(h_ref):
        pos = [p.astype(jnp.int32) for p in _local_positions(topi_ref[...], lcol_ref[...], tri_ref[...])]
        h = h_ref[...]
        for r0 in range(0, LOCAL_ROWS, LOCAL_CHUNK):
            rowid = lax.broadcasted_iota(jnp.int32, (LOCAL_CHUNK, SORT_TOKENS), 0) + r0
            onehot = jnp.zeros((LOCAL_CHUNK, SORT_TOKENS), F32)
            for p in pos:
                onehot = jnp.where(rowid == p, 1.0, onehot)
            buf[slot, r0:r0 + LOCAL_CHUNK, :] = _pack_halves(_dot(onehot.astype(BF16), h))
        krow = lax.broadcasted_iota(jnp.int32, (SUBLANES, SORT_TOKENS), 0)
        pos_rows = jnp.zeros((SUBLANES, SORT_TOKENS), jnp.int32)
        for k in range(TOP_K):
            pos_rows = jnp.where(krow == k, pos[k], pos_rows)
        pos_ref[...] = pos_rows

    @pl.when(i < nblk_p)
    def _():
        run(hp_ref)

    @pl.when(i >= nblk_p)
    def _():
        run(hs_ref)

    _start_runs(meta, i, buf.at[slot], xs_hbm, sems.at[slot], True)

    @pl.when(i > 0)
    def _():
        _wait_runs(ltot_ref[jnp.maximum(i - 1, 0)], buf.at[1 - slot], xs_hbm, sems.at[1 - slot], True)

    @pl.when(i == nblk - 1)
    def _():
        _wait_runs(ltot_ref[i], buf.at[slot], xs_hbm, sems.at[slot], True)

        def wait_tile(t, carry):
            _run_copy(zbuf, xs_hbm, fsem, 0, t * EXPERT_TILE, EXPERT_TILE, True).wait()
            return carry
        lax.fori_loop(nact_ref[0], xs_hbm.shape[0] // EXPERT_TILE, wait_tile, 0)


def _sort_tokens(plan, h_p, h_s, topi_all, tri, n_slots):
    nblk_p = h_p.shape[0] // SORT_TOKENS
    nblk = nblk_p + h_s.shape[0] // SORT_TOKENS
    return pl.pallas_call(
        functools.partial(_sort_body, nblk_p=nblk_p, nblk=nblk),
        grid_spec=pltpu.PrefetchScalarGridSpec(
            num_scalar_prefetch=7, grid=(nblk,),
            in_specs=[pl.BlockSpec((SORT_TOKENS, D_MODEL), lambda i, *_: (jnp.minimum(i, nblk_p - 1), 0)),
                      pl.BlockSpec((SORT_TOKENS, D_MODEL), lambda i, *_: (jnp.maximum(i - nblk_p, 0), 0)),
                      pl.BlockSpec((SUBLANES, SORT_TOKENS), lambda i, *_: (0, i)),
                      pl.BlockSpec((None, N_EXPERTS, 1), lambda i, *_: (i, 0, 0)),
                      _const_spec(tri.shape)],
            out_specs=[pl.BlockSpec(memory_space=pl.ANY),
                       pl.BlockSpec((SUBLANES, SORT_TOKENS), lambda i, *_: (0, i))],
            scratch_shapes=[pltpu.VMEM((2, LOCAL_ROWS, HALF_D), jnp.uint32),
                            pltpu.VMEM((EXPERT_TILE, HALF_D), jnp.uint32),
                            pltpu.SemaphoreType.DMA((2,)),
                            pltpu.SemaphoreType.DMA]),
        out_shape=[jax.ShapeDtypeStruct((n_slots, HALF_D), jnp.uint32),
                   jax.ShapeDtypeStruct(topi_all.shape, jnp.int32)],
        compiler_params=pltpu.CompilerParams(
            dimension_semantics=("arbitrary",), vmem_limit_bytes=VMEM_LIMIT),
        name="moe_sort",
    )(plan["lstart"], plan["goff"], plan["cpad"], plan["ltot"], plan["fill_start"], plan["fill_len"], plan["nact"],
      h_p, h_s, topi_all, plan["lstart_col"], tri)


def _experts_body(tile_exp_ref, nact_ref, rows_ref, xs_ref, wg_hbm, bg_ref, wu_hbm, bu_ref, wd_hbm, bd_ref,
                  ys_ref, w_f32, w_bf, sems):
    t = pl.program_id(0)
    nact = nact_ref[0]
    w_hbm = (wg_hbm, wu_hbm, wd_hbm)

    def weight_copies(e, slot):
        return [pltpu.make_async_copy(w.at[e], w_f32.at[slot, j], sems.at[slot, j]) for j, w in enumerate(w_hbm)]

    @pl.when(t < nact)
    def _():
        e = tile_exp_ref[t]
        slot = e % 2

        @pl.when(t == 0)
        def _():
            for cp in weight_copies(e, slot):
                cp.start()

        @pl.when((t == 0) | (e != tile_exp_ref[jnp.maximum(t - 1, 0)]))
        def _():
            @pl.when(e + 1 < N_EXPERTS)
            def _():
                for cp in weight_copies(e + 1, 1 - slot):
                    cp.start()
            for j, cp in enumerate(weight_copies(e, slot)):
                cp.wait()
                w_bf[j] = w_f32[slot, j].astype(BF16)

        def swiglu(rows):
            x = jnp.concatenate(_unpack_halves(xs_ref[0:rows, :]), axis=1)
            g = _dot(x, w_bf[0]) + bg_ref[0]
            u = _dot(x, w_bf[1]) + bu_ref[0]
            g = jnp.minimum(g, SWIGLU_LIMIT)
            u = jnp.clip(u, -SWIGLU_LIMIT, SWIGLU_LIMIT)
            a = g * _sigmoid(SWIGLU_ALPHA * g) * (u + 1.0)
            out = _dot(a.astype(BF16), w_bf[2]) + bd_ref[0]
            ys_ref[0:rows, :] = _pack_halves(out.astype(BF16).astype(F32))

        quarter = EXPERT_TILE // 4
        quarters = (rows_ref[t] + quarter - 1) // quarter
        for q in range(1, 5):
            @pl.when(quarters == q)
            def _(q=q):
                swiglu(q * quarter)
                if q < 4:
                    ys_ref[q * quarter:, :] = jnp.zeros((EXPERT_TILE - q * quarter, HALF_D), jnp.uint32)


def _experts(tile_exp, nact, tile_rows, xs, wg, bg, wu, bu, wd, bd):
    n_tiles = tile_exp.shape[0]

    def active(t, na):
        return jnp.minimum(t, na[0] - 1)
    wspec = pl.BlockSpec(memory_space=pl.ANY)
    bspec = pl.BlockSpec((1, 1, D_MODEL), lambda t, te, na, tr: (te[active(t, na)], 0, 0))
    xspec = pl.BlockSpec((EXPERT_TILE, HALF_D), lambda t, te, na, tr: (active(t, na), 0))
    return pl.pallas_call(
        _experts_body,
        grid_spec=pltpu.PrefetchScalarGridSpec(
            num_scalar_prefetch=3, grid=(n_tiles,),
            in_specs=[xspec, wspec, bspec, wspec, bspec, wspec, bspec],
            out_specs=xspec,
            scratch_shapes=[pltpu.VMEM((2, 3, D_MODEL, D_MODEL), F32),
                            pltpu.VMEM((3, D_MODEL, D_MODEL), BF16),
                            pltpu.SemaphoreType.DMA((2, 3))]),
        out_shape=jax.ShapeDtypeStruct(xs.shape, jnp.uint32),
        input_output_aliases={3: 0},
        compiler_params=pltpu.CompilerParams(
            dimension_semantics=("arbitrary",), vmem_limit_bytes=VMEM_LIMIT),
        name="experts",
    )(tile_exp, nact, tile_rows, xs, wg, bg, wu, bu, wd, bd)


def _unsort_body(lstart_ref, goff_ref, cpad_ref, ltot_ref,
                 ys_hbm, x1_ref, g2_ref, pos_ref, gate_ref, y_ref, buf, sems, *, blk0, nblk):
    i = pl.program_id(0)
    b = blk0 + i
    slot = i % 2
    meta = (lstart_ref, goff_ref, cpad_ref)

    @pl.when(i == 0)
    def _():
        buf[...] = jnp.zeros(buf.shape, jnp.uint32)
        _start_runs(meta, b, buf.at[0], ys_hbm, sems.at[0], False)

    @pl.when(i + 1 < nblk)
    def _():
        _start_runs(meta, b + 1, buf.at[1 - slot], ys_hbm, sems.at[1 - slot], False)

    _wait_runs(ltot_ref[b], buf.at[slot], ys_hbm, sems.at[slot], False)
    pos_rows = pos_ref[...]
    gate_rows = gate_ref[...]
    tn = (((0,), (0,)), ((), ()))
    acc_lo = acc_hi = None
    for c0 in range(0, LOCAL_ROWS, LOCAL_CHUNK):
        rowid = lax.broadcasted_iota(jnp.int32, (LOCAL_CHUNK, SORT_TOKENS), 0) + c0
        weights = jnp.zeros((LOCAL_CHUNK, SORT_TOKENS), F32)
        for k in range(TOP_K):
            weights = jnp.where(rowid == pos_rows[k:k + 1, :], gate_rows[k:k + 1, :], weights)
        weights = weights.astype(BF16)
        y_lo, y_hi = _unpack_halves(buf[slot, c0:c0 + LOCAL_CHUNK, :])
        d_lo = lax.dot_general(weights, y_lo, tn, preferred_element_type=F32)
        d_hi = lax.dot_general(weights, y_hi, tn, preferred_element_type=F32)
        acc_lo = d_lo if acc_lo is None else acc_lo + d_lo
        acc_hi = d_hi if acc_hi is None else acc_hi + d_hi
    g2 = g2_ref[...]
    if g2.ndim == 3:
        g2 = _expand_rows(g2, SORT_TOKENS // g2.shape[0])
    y_ref[:, :HALF_D] = x1_ref[:, :HALF_D] + g2[:, :HALF_D] * acc_lo
    y_ref[:, HALF_D:] = x1_ref[:, HALF_D:] + g2[:, HALF_D:] * acc_hi


def _unsort_tokens(plan, ys, x1, g2, g2_spec, pos_cols, gate_cols, blk0):
    nblk = x1.shape[0] // SORT_TOKENS
    return pl.pallas_call(
        functools.partial(_unsort_body, blk0=blk0, nblk=nblk),
        grid_spec=pltpu.PrefetchScalarGridSpec(
            num_scalar_prefetch=4, grid=(nblk,),
            in_specs=[pl.BlockSpec(memory_space=pl.ANY),
                      pl.BlockSpec((SORT_TOKENS, D_MODEL), lambda i, *_: (i, 0)),
                      g2_spec,
                      pl.BlockSpec((SUBLANES, SORT_TOKENS), lambda i, *_: (0, blk0 + i)),
                      pl.BlockSpec((SUBLANES, SORT_TOKENS), lambda i, *_: (0, blk0 + i))],
            out_specs=pl.BlockSpec((SORT_TOKENS, D_MODEL), lambda i, *_: (i, 0)),
            scratch_shapes=[pltpu.VMEM((2, LOCAL_ROWS, HALF_D), jnp.uint32),
                            pltpu.SemaphoreType.DMA((2,))]),
        out_shape=jax.ShapeDtypeStruct(x1.shape, F32),
        compiler_params=pltpu.CompilerParams(
            dimension_semantics=("arbitrary",), vmem_limit_bytes=VMEM_LIMIT),
        name="moe_unsort",
    )(plan["lstart"], plan["goff"], plan["cpad"], plan["ltot"],
      ys, x1, g2, pos_cols, gate_cols)


def _rope_table(pos):
    half = ROT_DIM // 2
    inv_freq = ROPE_THETA ** (-jnp.arange(0, ROT_DIM, 2, dtype=F32) / ROT_DIM)
    ang = pos.astype(F32)[:, None] * inv_freq[None, :]
    cos, sin = jnp.cos(ang), jnp.sin(ang)
    n = pos.shape[0]
    rest = HEAD_DIM - ROT_DIM
    c = jnp.concatenate([cos, cos, jnp.ones((n, rest), F32)], axis=1)
    s_next = jnp.concatenate([-sin, jnp.zeros((n, half + rest), F32)], axis=1)
    s_prev = jnp.concatenate([jnp.zeros((n, half), F32), sin, jnp.zeros((n, rest), F32)], axis=1)
    rep = LANES // HEAD_DIM
    return jnp.concatenate([jnp.tile(c, (1, rep)), jnp.tile(s_next, (1, rep)), jnp.tile(s_prev, (1, rep))], axis=1)


def _num_expert_tiles(ntok):
    nblk = ntok // SORT_TOKENS
    worst = ntok * TOP_K + nblk * N_EXPERTS * RUN_ALIGN + N_EXPERTS * (EXPERT_TILE - 1)
    return -(-worst // EXPERT_TILE)


def _routing_plan(cnt):
    nblk = cnt.shape[0]
    n_tiles = _num_expert_tiles(nblk * SORT_TOKENS)
    cpad = jnp.maximum(-(-cnt // RUN_ALIGN) * RUN_ALIGN, RUN_ALIGN)
    lstart = jnp.cumsum(cpad, axis=1) - cpad
    tot = cpad.sum(axis=0)
    region = -(-tot // EXPERT_TILE) * EXPERT_TILE
    region_end = jnp.cumsum(region)
    gstart = region_end - region
    goff = gstart[None, :] + jnp.cumsum(cpad, axis=0) - cpad
    tile_start = jnp.arange(n_tiles, dtype=jnp.int32) * EXPERT_TILE
    tile_exp = jnp.minimum((tile_start[:, None] >= region_end[None, :]).astype(jnp.int32).sum(axis=1), N_EXPERTS - 1)
    onehot = (tile_exp[:, None] == jnp.arange(N_EXPERTS, dtype=jnp.int32)[None, :]).astype(jnp.int32)
    runs_end = (onehot * (gstart + tot)[None, :]).sum(axis=1)
    tile_rows = jnp.clip(runs_end - tile_start, 0, EXPERT_TILE)
    i32 = lambda a: a.astype(jnp.int32)
    return dict(
        tile_rows=i32(tile_rows),
        lstart=i32(lstart).reshape(-1), goff=i32(goff).reshape(-1), cpad=i32(cpad).reshape(-1),
        ltot=i32(cpad.sum(axis=1)), fill_start=i32(gstart + tot), fill_len=i32(region - tot),
        lstart_col=lstart.astype(F32).reshape(nblk, N_EXPERTS, 1),
        tile_exp=i32(tile_exp), nact=i32(region_end[-1] // EXPERT_TILE).reshape(1))


def kernel(x_prompt, x_sample, cache_k_win, cache_v_win, state_conv, c_prompt, c_sample, w_ada, b_ada, norm1_g, norm2_g, w_in, q_norm_g, k_norm_g, attn_sinks, w_attn_out, conv_w, conv_b, w_conv_out, w_o, w_router, b_router, w_gate, b_gate, w_up, b_up, w_down, b_down):
    nb, seq, _ = x_prompt.shape
    ns, dseq, _ = x_sample.shape
    wbuf = cache_k_win.shape[2]
    assert w_ada.shape[0] == 1, "single-layer step"
    l = 0

    mod = _adaln(jnp.concatenate([c_prompt, c_sample], axis=0), w_ada[l], b_ada[l])
    mod_p = mod[:nb].reshape(nb, 6, D_MODEL)
    mod_s = mod[nb:].reshape(ns, 6, D_MODEL).transpose(1, 0, 2).reshape(6, ns, 1, D_MODEL)

    bd = jnp.asarray(np.kron(np.eye(N_HEADS), np.ones((HEAD_DIM, HEAD_DIM))), BF16)
    weights = (norm1_g[l].reshape(1, D_MODEL), norm2_g[l].reshape(1, D_MODEL), w_in[l].astype(BF16), bd,
               jnp.tile(q_norm_g[l], N_HEADS).reshape(1, Q_W), jnp.tile(k_norm_g[l], N_KV).reshape(1, KV_W),
               w_attn_out[l].astype(BF16), conv_w[l], conv_b[l].reshape(1, CONV_CH), w_conv_out[l].astype(BF16),
               w_o[l].astype(BF16), w_router[l].T.astype(BF16), b_router[l].reshape(N_EXPERTS, 1))
    sinks = attn_sinks[l]

    tb = MIXER_TOKENS
    assert tb == SORT_TOKENS and SORT_TOKENS % (SAMPLE_STEP_SEQS * dseq) == 0
    (x1_p, h2_p, topi_p, topg_p, kwin_p, vwin_p, zwin_p, cnt_p) = _prompt_mixer(
        x_prompt, mod_p, _rope_table(jnp.arange(seq)), sinks, weights, tb)
    pos_s = jnp.tile(PAST_LEN + jnp.arange(dseq), ns)
    state_rows = state_conv[l].transpose(1, 0, 2).reshape(CONV_K - 1, ns, 1, CONV_CH)
    (x1_s, h2_s, topi_s, topg_s, knew, vnew, z_s, cnt_s) = _sample_mixer(
        x_sample.reshape(ns * dseq, D_MODEL), mod_s, _rope_table(pos_s),
        cache_k_win[l].reshape(ns, wbuf, KV_W), cache_v_win[l].reshape(ns, wbuf, KV_W),
        state_rows, sinks, weights, SAMPLE_STEP_SEQS)

    ntok_p = nb * seq
    ntok = ntok_p + ns * dseq
    topi_all = jnp.concatenate([topi_p.transpose(1, 0, 2).reshape(SUBLANES, ntok_p), topi_s], axis=1)
    topg_all = jnp.concatenate([topg_p.transpose(1, 0, 2).reshape(SUBLANES, ntok_p), topg_s], axis=1)
    cnt_s = cnt_s.reshape(-1, SORT_TOKENS // (SAMPLE_STEP_SEQS * dseq), N_EXPERTS).sum(axis=1)
    plan = _routing_plan(jnp.concatenate([cnt_p.reshape(-1, N_EXPERTS), cnt_s], axis=0))
    tri = jnp.asarray(np.triu(np.ones((SORT_TOKENS, SORT_TOKENS)), k=1), BF16)
    xs, pos_all = _sort_tokens(plan, h2_p.reshape(ntok_p, D_MODEL), h2_s, topi_all, tri,
                               _num_expert_tiles(ntok) * EXPERT_TILE)
    pos_cols, gate_cols = pos_all, topg_all
    ys = _experts(plan["tile_exp"], plan["nact"], plan["tile_rows"], xs,
                  w_gate[l], b_gate[l].reshape(N_EXPERTS, 1, D_MODEL),
                  w_up[l], b_up[l].reshape(N_EXPERTS, 1, D_MODEL),
                  w_down[l], b_down[l].reshape(N_EXPERTS, 1, D_MODEL))
    blocks_per_seq = seq // SORT_TOKENS
    y_p = _unsort_tokens(plan, ys, x1_p.reshape(ntok_p, D_MODEL), mod_p[:, 5:6, :],
                         pl.BlockSpec((None, 1, D_MODEL), lambda i, *_: (i // blocks_per_seq, 0, 0)),
                         pos_cols, gate_cols, 0)
    y_s = _unsort_tokens(plan, ys, x1_s, mod_s[5],
                         pl.BlockSpec((SORT_TOKENS // dseq, 1, D_MODEL), lambda i, *_: (i, 0, 0)),
                         pos_cols, gate_cols, ntok_p // SORT_TOKENS)

    n_kv_shape = (N_KV, HEAD_DIM)
    k_win_p = kwin_p.reshape(1, nb, WINDOW, *n_kv_shape)
    v_win_p = vwin_p.reshape(1, nb, WINDOW, *n_kv_shape)
    conv_p = zwin_p[:, SUBLANES - (CONV_K - 1):, :][None]
    k_win_s = knew.reshape(1, ns, wbuf, *n_kv_shape)
    v_win_s = vnew.reshape(1, ns, wbuf, *n_kv_shape)
    conv_s = z_s.reshape(ns, dseq, CONV_CH)[:, dseq - (CONV_K - 1):, :][None]
    return (y_p.reshape(nb, seq, D_MODEL), y_s.reshape(ns, dseq, D_MODEL),
            k_win_p, v_win_p, conv_p, k_win_s, v_win_s, conv_s)
```

```python
import functools

import numpy as np
import jax
import jax.numpy as jnp
from jax import lax
from jax.experimental import pallas as pl
from jax.experimental.pallas import tpu as pltpu

D_MODEL = 1024
HEAD_DIM = 64
N_HEADS = 8
N_KV = 2
GROUP = N_HEADS // N_KV
Q_W = N_HEADS * HEAD_DIM
KV_W = N_KV * HEAD_DIM
WINDOW = 128
ROT_DIM = HEAD_DIM // 4
ROPE_THETA = 500000.0
ATTN_SCALE = HEAD_DIM ** -0.5
CONV_CH = D_MODEL // 2
CONV_K = 3
N_EXPERTS = 32
TOP_K = 4
SWIGLU_ALPHA = 1.702
SWIGLU_LIMIT = 7.0
NORM_EPS = 1e-5
QK_EPS = 1e-6
NEG_INF = -1e30
PAST_LEN = 16384

LANES = 128
SUBLANES = 8
VMEM_LIMIT = 56 * 1024 * 1024
ADALN_COLS = 1536
MIXER_TOKENS = 512

_OFF_Q = 0
_OFF_K = _OFF_Q + Q_W
_OFF_V = _OFF_K + KV_W
_OFF_GB = _OFF_V + KV_W
_OFF_GC = _OFF_GB + CONV_CH
_OFF_XC = _OFF_GC + CONV_CH
_OFF_BA = _OFF_XC + CONV_CH
_OFF_BC = _OFF_BA + D_MODEL

BF16 = jnp.bfloat16
F32 = jnp.float32
_NT = (((1,), (1,)), ((), ()))


def _dot(a, b):
    return jnp.dot(a, b, preferred_element_type=F32)


def _dot_nt(a, b):
    return lax.dot_general(a, b, _NT, preferred_element_type=F32)


def _sigmoid(x):
    return 1.0 / (1.0 + jnp.exp(-x))


def _split_bf16(x):
    hi = x.astype(BF16)
    lo = (x - hi.astype(F32)).astype(BF16)
    return hi, lo


def _const_spec(shape):
    nd = len(shape)
    return pl.BlockSpec(shape, lambda *_: (0,) * nd, pipeline_mode=pl.Buffered(1))


def _adaln_body(c_ref, w_ref, b_ref, o_ref):
    c = c_ref[...]
    s = c * _sigmoid(c)
    o_ref[...] = _dot(s.astype(BF16), w_ref[...].astype(BF16)) + b_ref[...]


def _adaln(c, w_ada, b_ada):
    n = c.shape[0]
    cols = w_ada.shape[1]
    bn = ADALN_COLS
    return pl.pallas_call(
        _adaln_body,
        grid=(cols // bn,),
        in_specs=[pl.BlockSpec((n, D_MODEL), lambda i: (0, 0)),
                  pl.BlockSpec((D_MODEL, bn), lambda i: (0, i)),
                  pl.BlockSpec((1, bn), lambda i: (0, i))],
        out_specs=pl.BlockSpec((n, bn), lambda i: (0, i)),
        out_shape=jax.ShapeDtypeStruct((n, cols), F32),
        compiler_params=pltpu.CompilerParams(vmem_limit_bytes=VMEM_LIMIT),
        name="adaln",
    )(c, w_ada, b_ada.reshape(1, cols))


def _expand_rows(v, rep):
    n, _, w = v.shape
    return jnp.broadcast_to(v, (n, rep, w)).reshape(n * rep, w)


def _modulated_norm(x, g, shift, scale):
    y = x * lax.rsqrt(jnp.mean(x * x, axis=-1, keepdims=True) + NORM_EPS)
    return (y * g) * (1.0 + scale) + shift


def _head_norm(t, bd, g):
    w = t.shape[1]
    sq_hi, sq_lo = _split_bf16(t * t)
    blk = bd[:w, :w]
    ms = (_dot(sq_hi, blk) + _dot(sq_lo, blk)) * (1.0 / HEAD_DIM)
    return t * lax.rsqrt(ms + QK_EPS) * g[:, :w]


def _rope(y, rope):
    c = rope[:, 0:LANES]
    s_next = rope[:, LANES:2 * LANES]
    s_prev = rope[:, 2 * LANES:3 * LANES]
    half = ROT_DIM // 2
    return y * c + pltpu.roll(y, LANES - half, 1) * s_next + pltpu.roll(y, half, 1) * s_prev


def _pair_expand(t):
    lane = lax.broadcasted_iota(jnp.int32, t.shape, 1)
    lo = lane < HEAD_DIM
    r = pltpu.roll(t, HEAD_DIM, 1)
    zero = jnp.zeros_like(t)
    a0 = jnp.where(lo, t, zero).astype(BF16)
    b0 = jnp.where(lo, zero, r).astype(BF16)
    a1 = jnp.where(lo, r, zero).astype(BF16)
    b1 = jnp.where(lo, zero, t).astype(BF16)
    return ((a0, b0), (a1, b1))


def _softmax_pv(s_list, v_list, sink):
    m = jnp.full((s_list[0].shape[0], 1), sink, F32)
    for s in s_list:
        m = jnp.maximum(m, jnp.max(s, axis=-1, keepdims=True))
    den = jnp.exp(sink - m)
    acc = None
    for s, v in zip(s_list, v_list):
        e = jnp.exp(s - m)
        den = den + jnp.sum(e, axis=-1, keepdims=True)
        pv = _dot(e.astype(BF16), v)
        acc = pv if acc is None else acc + pv
    return acc * (1.0 / den)


def _route(h2, wr, br):
    logits = _dot_nt(wr, h2.astype(BF16)) + br
    eid = lax.broadcasted_iota(jnp.int32, logits.shape, 0).astype(F32)
    vals, idxs = [], []
    chosen = jnp.zeros(logits.shape, F32)
    for _ in range(TOP_K):
        m = jnp.max(logits, axis=0, keepdims=True)
        idx = jnp.min(jnp.where(logits == m, eid, float(N_EXPERTS)), axis=0, keepdims=True)
        vals.append(m)
        idxs.append(idx)
        hit = eid == idx
        chosen = jnp.where(hit, 1.0, chosen)
        logits = jnp.where(hit, -jnp.inf, logits)
    counts = jnp.sum(chosen, axis=1, keepdims=True).astype(jnp.int32)
    ex = [jnp.exp(v - vals[0]) for v in vals]
    tot = ex[0] + ex[1] + ex[2] + ex[3]
    inv = 1.0 / tot
    gates = [e * inv for e in ex]
    rowid = lax.broadcasted_iota(jnp.int32, (SUBLANES, h2.shape[0]), 0)
    topi = jnp.zeros((SUBLANES, h2.shape[0]), F32)
    topg = jnp.zeros((SUBLANES, h2.shape[0]), F32)
    for k in range(TOP_K):
        topi = jnp.where(rowid == k, idxs[k], topi)
        topg = jnp.where(rowid == k, gates[k], topg)
    return topi.astype(jnp.int32), topg, counts


SAMPLE_STEP_SEQS = 32
SAMPLE_ATTN_SEQS = 16
_QKV_COLS = (_OFF_Q, Q_W + 2 * KV_W)
_LATE_COLS = ((_OFF_GB, 2 * CONV_CH), (_OFF_XC, CONV_CH), (_OFF_BA, D_MODEL), (_OFF_BC, D_MODEL))


def _in_proj(proj, h_bf, w_in_ref, group):
    off, width = group
    proj[off] = _dot(h_bf, w_in_ref[:, off:off + width])


def _cols(proj, off, width):
    for start, piece in proj.items():
        if start <= off and off + width <= start + piece.shape[1]:
            return piece[:, off - start:off - start + width]
    raise KeyError(off)


def _merge_and_route(x, o_bf, y_conv, proj, mod, n2g_ref, w_ao_ref, w_co_ref, w_o_ref,
                     wr_ref, br_ref, before_tail=None):
    sh1, sc1, g1, sh2, sc2, g2 = mod
    a = _dot(o_bf, w_ao_ref[...])
    c = _dot(y_conv.astype(BF16), w_co_ref[...])
    m = _sigmoid(_cols(proj, _OFF_BA, D_MODEL)) * a + _sigmoid(_cols(proj, _OFF_BC, D_MODEL)) * c
    x1 = x + g1 * _dot(m.astype(BF16), w_o_ref[...])
    if before_tail is not None:
        before_tail()
    h2 = _modulated_norm(x1, n2g_ref[...], sh2, sc2)
    return (x1, h2) + _route(h2, wr_ref[...], br_ref[...])


def _prompt_body(sinks_ref, x_ref, mod_ref, rope_ref, xn_ref, modn_ref,
                 n1g_ref, n2g_ref, w_in_ref, bd_ref, qg_ref, kg_ref,
                 w_ao_ref, cw_ref, cb_ref, w_co_ref, w_o_ref, wr_ref, br_ref,
                 x1_ref, h2_ref, topi_ref, topg_ref, kwin_ref, vwin_ref, zwin_ref, cnt_ref,
                 kbuf, vbuf, zbuf, h_buf, qkv_buf, late_buf, *, tb):
    j = pl.program_id(1)
    nsub = tb // WINDOW

    @pl.when(j == 0)
    def _():
        kbuf[0:WINDOW, :] = jnp.zeros((WINDOW, KV_W), F32)
        vbuf[0:WINDOW, :] = jnp.zeros((WINDOW, KV_W), F32)
        zbuf[0:SUBLANES, :] = jnp.zeros((SUBLANES, CONV_CH), F32)

    def front(xr, mr):
        h = _modulated_norm(xr[0], n1g_ref[...], mr[0, 0:1, :], mr[0, 1:2, :]).astype(BF16)
        h_buf[...] = h
        for group, buf in ((_QKV_COLS, qkv_buf), (_LATE_COLS[0], late_buf)):
            buf[...] = _dot(h, w_in_ref[:, group[0]:group[0] + group[1]])

    @pl.when((pl.program_id(0) == 0) & (j == 0))
    def _():
        front(x_ref, mod_ref)

    x = x_ref[0]
    mod = tuple(mod_ref[0, i:i + 1, :] for i in range(6))
    h_bf = h_buf[...]
    proj = {_QKV_COLS[0]: qkv_buf[...], _LATE_COLS[0][0]: late_buf[...]}
    rope = rope_ref[...]
    bd = bd_ref[...]

    k = _rope(_head_norm(_cols(proj, _OFF_K, KV_W), bd, kg_ref[...]), rope)
    v = _cols(proj, _OFF_V, KV_W)
    kbuf[WINDOW:WINDOW + tb, :] = k
    vbuf[WINDOW:WINDOW + tb, :] = v
    kwin_ref[0] = k[tb - WINDOW:tb]
    vwin_ref[0] = v[tb - WINDOW:tb]
    k_exp = _pair_expand(kbuf[...])
    v_exp = _pair_expand(vbuf[...])

    _in_proj(proj, h_bf, w_in_ref, _LATE_COLS[1])
    _in_proj(proj, h_bf, w_in_ref, _LATE_COLS[2])
    qn = _head_norm(_cols(proj, _OFF_Q, Q_W), bd, qg_ref[...])
    q_tiles = []
    for t in range(Q_W // LANES):
        qt = _rope(qn[:, t * LANES:(t + 1) * LANES], rope)
        q_tiles.append((qt * ATTN_SCALE).astype(BF16))

    row = lax.broadcasted_iota(jnp.int32, (WINDOW, 2 * WINDOW), 0)
    col = lax.broadcasted_iota(jnp.int32, (WINDOW, 2 * WINDOW), 1)
    band = (col > row) & (col <= row + WINDOW)
    first_key = jnp.where(j > 0, 0, WINDOW)
    band_first = band & (col >= first_key)
    o_rows = []
    for i in range(nsub):
        for group in _LATE_COLS[3:][i::nsub]:
            _in_proj(proj, h_bf, w_in_ref, group)
        mask = band_first if i == 0 else band
        keys = slice(i * WINDOW, (i + 2) * WINDOW)
        o_tiles = []
        for t in range(Q_W // LANES):
            c = t // 2
            qt = q_tiles[t][i * WINDOW:(i + 1) * WINDOW]
            o_t = None
            for half in range(2):
                s = jnp.where(mask, _dot_nt(qt, k_exp[c][half][keys]), NEG_INF)
                sink = sinks_ref[c * GROUP + (t % 2) * 2 + half]
                part = _softmax_pv([s], [v_exp[c][half][keys]], sink)
                o_t = part if o_t is None else o_t + part
            o_tiles.append(o_t.astype(BF16))
        o_rows.append(jnp.concatenate(o_tiles, axis=1))
    o_bf = jnp.concatenate(o_rows, axis=0) if nsub > 1 else o_rows[0]
    kbuf[0:WINDOW, :] = kbuf[tb:tb + WINDOW, :]
    vbuf[0:WINDOW, :] = vbuf[tb:tb + WINDOW, :]

    z = _cols(proj, _OFF_GC, CONV_CH) * _cols(proj, _OFF_XC, CONV_CH)
    zbuf[SUBLANES:SUBLANES + tb, :] = z
    zwin_ref[0] = z[tb - SUBLANES:tb]
    z1 = zbuf[SUBLANES - 1:SUBLANES - 1 + tb, :]
    z2 = zbuf[SUBLANES - 2:SUBLANES - 2 + tb, :]
    conv = cb_ref[...] + cw_ref[0:1, :] * z2 + cw_ref[1:2, :] * z1 + cw_ref[2:3, :] * z
    y_conv = _cols(proj, _OFF_GB, CONV_CH) * conv
    zbuf[0:SUBLANES, :] = zbuf[tb:tb + SUBLANES, :]

    x1, h2, topi, topg, counts = _merge_and_route(x, o_bf, y_conv, proj, mod, n2g_ref, w_ao_ref, w_co_ref,
                                                  w_o_ref, wr_ref, br_ref,
                                                  before_tail=lambda: front(xn_ref, modn_ref))
    x1_ref[0] = x1
    h2_ref[0] = h2.astype(BF16)
    topi_ref[0] = topi
    topg_ref[0] = topg
    cnt_ref[...] = counts


def _prompt_mixer(x, mod, rope, sinks, weights, tb):
    nb, seq, _ = x.shape
    nj = seq // tb
    consts = weights

    def nxt(b, j):
        s = jnp.minimum(b * nj + j + 1, nb * nj - 1)
        return s // nj, s % nj
    in_specs = [
        pl.BlockSpec((1, tb, D_MODEL), lambda b, j, s: (b, j, 0)),
        pl.BlockSpec((1, 6, D_MODEL), lambda b, j, s: (b, 0, 0)),
        pl.BlockSpec((tb, 3 * LANES), lambda b, j, s: (j, 0)),
        pl.BlockSpec((1, tb, D_MODEL), lambda b, j, s: (*nxt(b, j), 0)),
        pl.BlockSpec((1, 6, D_MODEL), lambda b, j, s: (nxt(b, j)[0], 0, 0)),
    ] + [_const_spec(w.shape) for w in consts]
    out_specs = [
        pl.BlockSpec((1, tb, D_MODEL), lambda b, j, s: (b, j, 0)),
        pl.BlockSpec((1, tb, D_MODEL), lambda b, j, s: (b, j, 0)),
        pl.BlockSpec((1, SUBLANES, tb), lambda b, j, s: (b, 0, j)),
        pl.BlockSpec((1, SUBLANES, tb), lambda b, j, s: (b, 0, j)),
        pl.BlockSpec((1, WINDOW, KV_W), lambda b, j, s: (b, 0, 0)),
        pl.BlockSpec((1, WINDOW, KV_W), lambda b, j, s: (b, 0, 0)),
        pl.BlockSpec((1, SUBLANES, CONV_CH), lambda b, j, s: (b, 0, 0)),
        pl.BlockSpec((None, None, N_EXPERTS, 1), lambda b, j, s: (b, j, 0, 0)),
    ]
    out_shape = [
        jax.ShapeDtypeStruct((nb, seq, D_MODEL), F32),
        jax.ShapeDtypeStruct((nb, seq, D_MODEL), BF16),
        jax.ShapeDtypeStruct((nb, SUBLANES, seq), jnp.int32),
        jax.ShapeDtypeStruct((nb, SUBLANES, seq), F32),
        jax.ShapeDtypeStruct((nb, WINDOW, KV_W), F32),
        jax.ShapeDtypeStruct((nb, WINDOW, KV_W), F32),
        jax.ShapeDtypeStruct((nb, SUBLANES, CONV_CH), F32),
        jax.ShapeDtypeStruct((nb, nj, N_EXPERTS, 1), jnp.int32),
    ]
    return pl.pallas_call(
        functools.partial(_prompt_body, tb=tb),
        grid_spec=pltpu.PrefetchScalarGridSpec(
            num_scalar_prefetch=1, grid=(nb, seq // tb),
            in_specs=in_specs, out_specs=out_specs,
            scratch_shapes=[pltpu.VMEM((tb + WINDOW, KV_W), F32),
                            pltpu.VMEM((tb + WINDOW, KV_W), F32),
                            pltpu.VMEM((tb + SUBLANES, CONV_CH), F32),
                            pltpu.VMEM((tb, D_MODEL), BF16),
                            pltpu.VMEM((tb, _QKV_COLS[1]), F32),
                            pltpu.VMEM((tb, _LATE_COLS[0][1]), F32)]),
        out_shape=out_shape,
        compiler_params=pltpu.CompilerParams(
            dimension_semantics=("arbitrary", "arbitrary"), vmem_limit_bytes=VMEM_LIMIT),
        name="prompt_mixer",
    )(sinks, x, mod, rope, x, mod, *consts)


def _sample_body(sinks_ref, x_ref, mod_ref, rope_ref, ck_ref, cv_ref, st_ref,
                 n1g_ref, n2g_ref, w_in_ref, bd_ref, qg_ref, kg_ref,
                 w_ao_ref, cw_ref, cb_ref, w_co_ref, w_o_ref, wr_ref, br_ref,
                 x1_ref, h2_ref, topi_ref, topg_ref, knew_ref, vnew_ref, z_ref, cnt_ref, *, nseq, dseq):
    rows = nseq * dseq
    wbuf = ck_ref.shape[1]
    x = x_ref[...]
    mod = tuple(_expand_rows(mod_ref[i], dseq) for i in range(6))
    h_bf = _modulated_norm(x, n1g_ref[...], mod[0], mod[1]).astype(BF16)
    proj = {}
    for group in (_QKV_COLS,) + _LATE_COLS:
        _in_proj(proj, h_bf, w_in_ref, group)
    rope = rope_ref[...]
    bd = bd_ref[...]

    k = _rope(_head_norm(_cols(proj, _OFF_K, KV_W), bd, kg_ref[...]), rope)
    v = _cols(proj, _OFF_V, KV_W)
    for win_ref, cache_ref, new in ((knew_ref, ck_ref, k), (vnew_ref, cv_ref, v)):
        win_ref[:, 0:wbuf - dseq, :] = cache_ref[:, dseq:, :]
        win_ref[:, wbuf - dseq:, :] = new.reshape(nseq, dseq, KV_W)
    gseq = SAMPLE_ATTN_SEQS
    grows = gseq * dseq
    qrow = lax.broadcasted_iota(jnp.int32, (grows, gseq * wbuf), 0)
    ccol = lax.broadcasted_iota(jnp.int32, (grows, gseq * wbuf), 1)
    mask_c = ((ccol // wbuf) == (qrow // dseq)) & ((ccol % wbuf) > (qrow % dseq) + (wbuf - WINDOW))
    qrow_n = lax.broadcasted_iota(jnp.int32, (grows, grows), 0)
    ncol = lax.broadcasted_iota(jnp.int32, (grows, grows), 1)
    mask_n = ((ncol // dseq) == (qrow_n // dseq)) & ((ncol % dseq) <= (qrow_n % dseq))

    qn = _head_norm(_cols(proj, _OFF_Q, Q_W), bd, qg_ref[...])
    q_tiles = [(_rope(qn[:, t * LANES:(t + 1) * LANES], rope) * ATTN_SCALE).astype(BF16)
               for t in range(Q_W // LANES)]
    o_rows = []
    for g0 in range(0, nseq, gseq):
        rs = slice(g0 * dseq, g0 * dseq + grows)
        kn_exp = _pair_expand(k[rs])
        vn_exp = _pair_expand(v[rs])
        kc_exp = _pair_expand(ck_ref[g0:g0 + gseq].reshape(gseq * wbuf, KV_W))
        vc_exp = _pair_expand(cv_ref[g0:g0 + gseq].reshape(gseq * wbuf, KV_W))
        o_tiles = []
        for t in range(Q_W // LANES):
            c = t // 2
            o_t = None
            for half in range(2):
                s_c = jnp.where(mask_c, _dot_nt(q_tiles[t][rs], kc_exp[c][half]), NEG_INF)
                s_n = jnp.where(mask_n, _dot_nt(q_tiles[t][rs], kn_exp[c][half]), NEG_INF)
                sink = sinks_ref[c * GROUP + (t % 2) * 2 + half]
                part = _softmax_pv([s_c, s_n], [vc_exp[c][half], vn_exp[c][half]], sink)
                o_t = part if o_t is None else o_t + part
            o_tiles.append(o_t.astype(BF16))
        o_rows.append(jnp.concatenate(o_tiles, axis=1))
    o_bf = jnp.concatenate(o_rows, axis=0) if len(o_rows) > 1 else o_rows[0]

    z = _cols(proj, _OFF_GC, CONV_CH) * _cols(proj, _OFF_XC, CONV_CH)
    z_ref[...] = z
    r = lax.broadcasted_iota(jnp.int32, z.shape, 0) % dseq
    st0 = _expand_rows(st_ref[0], dseq)
    st1 = _expand_rows(st_ref[1], dseq)
    z1 = jnp.where(r == 0, st1, pltpu.roll(z, 1, 0))
    z2 = jnp.where(r == 0, st0, jnp.where(r == 1, st1, pltpu.roll(z, 2, 0)))
    conv = cb_ref[...] + cw_ref[0:1, :] * z2 + cw_ref[1:2, :] * z1 + cw_ref[2:3, :] * z
    y_conv = _cols(proj, _OFF_GB, CONV_CH) * conv

    x1, h2, topi, topg, counts = _merge_and_route(x, o_bf, y_conv, proj, mod, n2g_ref, w_ao_ref, w_co_ref,
                                                  w_o_ref, wr_ref, br_ref)
    x1_ref[...] = x1
    h2_ref[...] = h2.astype(BF16)
    topi_ref[...] = topi
    topg_ref[...] = topg
    cnt_ref[...] = counts


def _sample_mixer(x, mod, rope, cache_k, cache_v, state, sinks, weights, nseq):
    ntok = x.shape[0]
    nall, wbuf, _ = cache_k.shape
    dseq = ntok // nall
    rows = nseq * dseq
    consts = weights
    in_specs = [
        pl.BlockSpec((rows, D_MODEL), lambda i, s: (i, 0)),
        pl.BlockSpec((6, nseq, 1, D_MODEL), lambda i, s: (0, i, 0, 0)),
        pl.BlockSpec((rows, 3 * LANES), lambda i, s: (i, 0)),
        pl.BlockSpec((nseq, wbuf, KV_W), lambda i, s: (i, 0, 0)),
        pl.BlockSpec((nseq, wbuf, KV_W), lambda i, s: (i, 0, 0)),
        pl.BlockSpec((2, nseq, 1, CONV_CH), lambda i, s: (0, i, 0, 0)),
    ] + [_const_spec(w.shape) for w in consts]
    out_specs = [
        pl.BlockSpec((rows, D_MODEL), lambda i, s: (i, 0)),
        pl.BlockSpec((rows, D_MODEL), lambda i, s: (i, 0)),
        pl.BlockSpec((SUBLANES, rows), lambda i, s: (0, i)),
        pl.BlockSpec((SUBLANES, rows), lambda i, s: (0, i)),
        pl.BlockSpec((nseq, wbuf, KV_W), lambda i, s: (i, 0, 0)),
        pl.BlockSpec((nseq, wbuf, KV_W), lambda i, s: (i, 0, 0)),
        pl.BlockSpec((rows, CONV_CH), lambda i, s: (i, 0)),
        pl.BlockSpec((None, N_EXPERTS, 1), lambda i, s: (i, 0, 0)),
    ]
    out_shape = [
        jax.ShapeDtypeStruct((ntok, D_MODEL), F32),
        jax.ShapeDtypeStruct((ntok, D_MODEL), BF16),
        jax.ShapeDtypeStruct((SUBLANES, ntok), jnp.int32),
        jax.ShapeDtypeStruct((SUBLANES, ntok), F32),
        jax.ShapeDtypeStruct((nall, wbuf, KV_W), F32),
        jax.ShapeDtypeStruct((nall, wbuf, KV_W), F32),
        jax.ShapeDtypeStruct((ntok, CONV_CH), F32),
        jax.ShapeDtypeStruct((ntok // rows, N_EXPERTS, 1), jnp.int32),
    ]
    return pl.pallas_call(
        functools.partial(_sample_body, nseq=nseq, dseq=dseq),
        grid_spec=pltpu.PrefetchScalarGridSpec(
            num_scalar_prefetch=1, grid=(ntok // rows,),
            in_specs=in_specs, out_specs=out_specs),
        out_shape=out_shape,
        compiler_params=pltpu.CompilerParams(
            dimension_semantics=("arbitrary",), vmem_limit_bytes=VMEM_LIMIT),
        name="sample_mixer",
    )(sinks, x, mod, rope, cache_k, cache_v, state, *consts)


SORT_TOKENS = 512
RUN_ALIGN = SUBLANES
EXPERT_TILE = 1024
HALF_D = D_MODEL // 2
LOCAL_ROWS = SORT_TOKENS * TOP_K + N_EXPERTS * RUN_ALIGN
LOCAL_CHUNK = 256
assert LOCAL_ROWS % LOCAL_CHUNK == 0
_HI_MASK = 0xFFFF0000


def _pack_halves(x):
    lo = pltpu.bitcast(x[:, :HALF_D], jnp.uint32)
    hi = pltpu.bitcast(x[:, HALF_D:], jnp.uint32)
    return hi | (lo >> 16)


def _unpack_halves(w):
    lo = pltpu.bitcast(w << 16, F32).astype(BF16)
    hi = pltpu.bitcast(w & jnp.uint32(_HI_MASK), F32).astype(BF16)
    return lo, hi


def _local_positions(topi, lstart_col, tri):
    ntok = topi.shape[1]
    eid = lax.broadcasted_iota(jnp.int32, (N_EXPERTS, ntok), 0)
    hits = [eid == topi[k:k + 1, :] for k in range(TOP_K)]
    chosen = jnp.zeros((N_EXPERTS, ntok), F32)
    for h in hits:
        chosen = jnp.where(h, 1.0, chosen)
    base = _dot(chosen.astype(BF16), tri) + lstart_col
    return [jnp.sum(jnp.where(h, base, 0.0), axis=0, keepdims=True) for h in hits]


def _run_copy(loc, glob, sem, lofs, gofs, n, to_global):
    lo = loc.at[pl.ds(pl.multiple_of(lofs, RUN_ALIGN), n)]
    gl = glob.at[pl.ds(pl.multiple_of(gofs, RUN_ALIGN), n)]
    return pltpu.make_async_copy(lo, gl, sem) if to_global else pltpu.make_async_copy(gl, lo, sem)


def _start_runs(meta, blk, loc, glob, sem, to_global):
    lstart_ref, goff_ref, cpad_ref = meta
    for e in range(N_EXPERTS):
        n = pl.multiple_of(cpad_ref[blk * N_EXPERTS + e], RUN_ALIGN)
        _run_copy(loc, glob, sem, lstart_ref[blk * N_EXPERTS + e], goff_ref[blk * N_EXPERTS + e], n,
                  to_global).start()


def _wait_runs(nrows, loc, glob, sem, to_global):
    _run_copy(loc, glob, sem, 0, 0, pl.multiple_of(nrows, RUN_ALIGN), to_global).wait()


def _sort_body(lstart_ref, goff_ref, cpad_ref, ltot_ref, fstart_ref, flen_ref, nact_ref,
               hp_ref, hs_ref, topi_ref, lcol_ref, tri_ref, xs_hbm, pos_ref, buf, zbuf, sems, fsem, *, nblk_p, nblk):
    i = pl.program_id(0)
    slot = i % 2
    meta = (lstart_ref, goff_ref, cpad_ref)

    @pl.when(i == 0)
    def _():
        zbuf[...] = jnp.zeros(zbuf.shape, jnp.uint32)

        def fill(e, tot):
            n = pl.multiple_of(flen_ref[e], RUN_ALIGN)

            @pl.when(n > 0)
            def _():
                _run_copy(zbuf, xs_hbm, fsem, 0, fstart_ref[e], n, True).start()
            return tot + n
        total = lax.fori_loop(0, N_EXPERTS, fill, 0)

        @pl.when(total > 0)
        def _():
            _wait_runs(total, zbuf, xs_hbm, fsem, True)

        def fill_tile(t, carry):
            _run_copy(zbuf, xs_hbm, fsem, 0, t * EXPERT_TILE, EXPERT_TILE, True).start()
            return carry
        lax.fori_loop(nact_ref[0], xs_hbm.shape[0] // EXPERT_TILE, fill_tile, 0)

    def run(h_ref):
        pos = [p.astype(jnp.int32) for p in _local_positions(topi_ref[...], lcol_ref[...], tri_ref[...])]
        h = h_ref[...]
        for r0 in range(0, LOCAL_ROWS, LOCAL_CHUNK):
            rowid = lax.broadcasted_iota(jnp.int32, (LOCAL_CHUNK, SORT_TOKENS), 0) + r0
            onehot = jnp.zeros((LOCAL_CHUNK, SORT_TOKENS), F32)
            for p in pos:
                onehot = jnp.where(rowid == p, 1.0, onehot)
            buf[slot, r0:r0 + LOCAL_CHUNK, :] = _pack_halves(_dot(onehot.astype(BF16), h))
        krow = lax.broadcasted_iota(jnp.int32, (SUBLANES, SORT_TOKENS), 0)
        pos_rows = jnp.zeros((SUBLANES, SORT_TOKENS), jnp.int32)
        for k in range(TOP_K):
            pos_rows = jnp.where(krow == k, pos[k], pos_rows)
        pos_ref[...] = pos_rows

    @pl.when(i < nblk_p)
    def _():
        run(hp_ref)

    @pl.when(i >= nblk_p)
    def _():
        run(hs_ref)

    _start_runs(meta, i, buf.at[slot], xs_hbm, sems.at[slot], True)

    @pl.when(i > 0)
    def _():
        _wait_runs(ltot_ref[jnp.maximum(i - 1, 0)], buf.at[1 - slot], xs_hbm, sems.at[1 - slot], True)

    @pl.when(i == nblk - 1)
    def _():
        _wait_runs(ltot_ref[i], buf.at[slot], xs_hbm, sems.at[slot], True)

        def wait_tile(t, carry):
            _run_copy(zbuf, xs_hbm, fsem, 0, t * EXPERT_TILE, EXPERT_TILE, True).wait()
            return carry
        lax.fori_loop(nact_ref[0], xs_hbm.shape[0] // EXPERT_TILE, wait_tile, 0)


def _sort_tokens(plan, h_p, h_s, topi_all, tri, n_slots):
    nblk_p = h_p.shape[0] // SORT_TOKENS
    nblk = nblk_p + h_s.shape[0] // SORT_TOKENS
    return pl.pallas_call(
        functools.partial(_sort_body, nblk_p=nblk_p, nblk=nblk),
        grid_spec=pltpu.PrefetchScalarGridSpec(
            num_scalar_prefetch=7, grid=(nblk,),
            in_specs=[pl.BlockSpec((SORT_TOKENS, D_MODEL), lambda i, *_: (jnp.minimum(i, nblk_p - 1), 0)),
                      pl.BlockSpec((SORT_TOKENS, D_MODEL), lambda i, *_: (jnp.maximum(i - nblk_p, 0), 0)),
                      pl.BlockSpec((SUBLANES, SORT_TOKENS), lambda i, *_: (0, i)),
                      pl.BlockSpec((None, N_EXPERTS, 1), lambda i, *_: (i, 0, 0)),
                      _const_spec(tri.shape)],
            out_specs=[pl.BlockSpec(memory_space=pl.ANY),
                       pl.BlockSpec((SUBLANES, SORT_TOKENS), lambda i, *_: (0, i))],
            scratch_shapes=[pltpu.VMEM((2, LOCAL_ROWS, HALF_D), jnp.uint32),
                            pltpu.VMEM((EXPERT_TILE, HALF_D), jnp.uint32),
                            pltpu.SemaphoreType.DMA((2,)),
                            pltpu.SemaphoreType.DMA]),
        out_shape=[jax.ShapeDtypeStruct((n_slots, HALF_D), jnp.uint32),
                   jax.ShapeDtypeStruct(topi_all.shape, jnp.int32)],
        compiler_params=pltpu.CompilerParams(
            dimension_semantics=("arbitrary",), vmem_limit_bytes=VMEM_LIMIT),
        name="moe_sort",
    )(plan["lstart"], plan["goff"], plan["cpad"], plan["ltot"], plan["fill_start"], plan["fill_len"], plan["nact"],
      h_p, h_s, topi_all, plan["lstart_col"], tri)


def _experts_body(tile_exp_ref, nact_ref, rows_ref, xs_ref, wg_hbm, bg_ref, wu_hbm, bu_ref, wd_hbm, bd_ref,
                  ys_ref, w_f32, w_bf, sems):
    t = pl.program_id(0)
    nact = nact_ref[0]
    w_hbm = (wg_hbm, wu_hbm, wd_hbm)

    def weight_copies(e, slot):
        return [pltpu.make_async_copy(w.at[e], w_f32.at[slot, j], sems.at[slot, j]) for j, w in enumerate(w_hbm)]

    @pl.when(t < nact)
    def _():
        e = tile_exp_ref[t]
        slot = e % 2

        @pl.when(t == 0)
        def _():
            for cp in weight_copies(e, slot):
                cp.start()

        quarter = EXPERT_TILE // 4
        quarters = (rows_ref[t] + quarter - 1) // quarter
        first = (t == 0) | (e != tile_exp_ref[jnp.maximum(t - 1, 0)])
        cast_inline = first & (quarters == 4)

        def cast_weights():
            for j in range(3):
                w_bf[j] = w_f32[slot, j].astype(BF16)

        @pl.when(first)
        def _():
            @pl.when(e + 1 < N_EXPERTS)
            def _():
                for cp in weight_copies(e + 1, 1 - slot):
                    cp.start()
            for cp in weight_copies(e, slot):
                cp.wait()

        @pl.when(first & (quarters < 4))
        def _():
            cast_weights()

        def swiglu(rows, with_cast=False):
            if with_cast:
                cast_weights()
            x = jnp.concatenate(_unpack_halves(xs_ref[0:rows, :]), axis=1)
            g = _dot(x, w_bf[0]) + bg_ref[0]
            u = _dot(x, w_bf[1]) + bu_ref[0]
            g = jnp.minimum(g, SWIGLU_LIMIT)
            u = jnp.clip(u, -SWIGLU_LIMIT, SWIGLU_LIMIT)
            a = g * _sigmoid(SWIGLU_ALPHA * g) * (u + 1.0)
            out = _dot(a.astype(BF16), w_bf[2]) + bd_ref[0]
            ys_ref[0:rows, :] = _pack_halves(out.astype(BF16).astype(F32))

        for q in range(1, 4):
            @pl.when(quarters == q)
            def _(q=q):
                swiglu(q * quarter)
                ys_ref[q * quarter:, :] = jnp.zeros((EXPERT_TILE - q * quarter, HALF_D), jnp.uint32)

        @pl.when((quarters == 4) & jnp.logical_not(cast_inline))
        def _():
            swiglu(EXPERT_TILE)

        @pl.when(cast_inline)
        def _():
            swiglu(EXPERT_TILE, with_cast=True)


def _experts(tile_exp, nact, tile_rows, xs, wg, bg, wu, bu, wd, bd):
    n_tiles = tile_exp.shape[0]

    def active(t, na):
        return jnp.minimum(t, na[0] - 1)
    wspec = pl.BlockSpec(memory_space=pl.ANY)
    bspec = pl.BlockSpec((1, 1, D_MODEL), lambda t, te, na, tr: (te[active(t, na)], 0, 0))
    xspec = pl.BlockSpec((EXPERT_TILE, HALF_D), lambda t, te, na, tr: (active(t, na), 0))
    return pl.pallas_call(
        _experts_body,
        grid_spec=pltpu.PrefetchScalarGridSpec(
            num_scalar_prefetch=3, grid=(n_tiles,),
            in_specs=[xspec, wspec, bspec, wspec, bspec, wspec, bspec],
            out_specs=xspec,
            scratch_shapes=[pltpu.VMEM((2, 3, D_MODEL, D_MODEL), F32),
                            pltpu.VMEM((3, D_MODEL, D_MODEL), BF16),
                            pltpu.SemaphoreType.DMA((2, 3))]),
        out_shape=jax.ShapeDtypeStruct(xs.shape, jnp.uint32),
        input_output_aliases={3: 0},
        compiler_params=pltpu.CompilerParams(
            dimension_semantics=("arbitrary",), vmem_limit_bytes=VMEM_LIMIT),
        name="experts",
    )(tile_exp, nact, tile_rows, xs, wg, bg, wu, bu, wd, bd)


def _unsort_body(lstart_ref, goff_ref, cpad_ref, ltot_ref,
                 ys_hbm, x1_ref, g2_ref, pos_ref, gate_ref, y_ref, buf, sems, *, blk0, nblk):
    i = pl.program_id(0)
    b = blk0 + i
    slot = i % 2
    meta = (lstart_ref, goff_ref, cpad_ref)

    @pl.when(i == 0)
    def _():
        buf[...] = jnp.zeros(buf.shape, jnp.uint32)
        _start_runs(meta, b, buf.at[0], ys_hbm, sems.at[0], False)

    @pl.when(i + 1 < nblk)
    def _():
        _start_runs(meta, b + 1, buf.at[1 - slot], ys_hbm, sems.at[1 - slot], False)

    _wait_runs(ltot_ref[b], buf.at[slot], ys_hbm, sems.at[slot], False)
    pos_rows = pos_ref[...]
    gate_rows = gate_ref[...]
    tn = (((0,), (0,)), ((), ()))
    acc_lo = acc_hi = None
    for c0 in range(0, LOCAL_ROWS, LOCAL_CHUNK):
        rowid = lax.broadcasted_iota(jnp.int32, (LOCAL_CHUNK, SORT_TOKENS), 0) + c0
        weights = jnp.zeros((LOCAL_CHUNK, SORT_TOKENS), F32)
        for k in range(TOP_K):
            weights = jnp.where(rowid == pos_rows[k:k + 1, :], gate_rows[k:k + 1, :], weights)
        weights = weights.astype(BF16)
        y_lo, y_hi = _unpack_halves(buf[slot, c0:c0 + LOCAL_CHUNK, :])
        d_lo = lax.dot_general(weights, y_lo, tn, preferred_element_type=F32)
        d_hi = lax.dot_general(weights, y_hi, tn, preferred_element_type=F32)
        acc_lo = d_lo if acc_lo is None else acc_lo + d_lo
        acc_hi = d_hi if acc_hi is None else acc_hi + d_hi
    g2 = g2_ref[...]
    if g2.ndim == 3:
        g2 = _expand_rows(g2, SORT_TOKENS // g2.shape[0])
    y_ref[:, :HALF_D] = x1_ref[:, :HALF_D] + g2[:, :HALF_D] * acc_lo
    y_ref[:, HALF_D:] = x1_ref[:, HALF_D:] + g2[:, HALF_D:] * acc_hi


def _unsort_tokens(plan, ys, x1, g2, g2_spec, pos_cols, gate_cols, blk0):
    nblk = x1.shape[0] // SORT_TOKENS
    return pl.pallas_call(
        functools.partial(_unsort_body, blk0=blk0, nblk=nblk),
        grid_spec=pltpu.PrefetchScalarGridSpec(
            num_scalar_prefetch=4, grid=(nblk,),
            in_specs=[pl.BlockSpec(memory_space=pl.ANY),
                      pl.BlockSpec((SORT_TOKENS, D_MODEL), lambda i, *_: (i, 0)),
                      g2_spec,
                      pl.BlockSpec((SUBLANES, SORT_TOKENS), lambda i, *_: (0, blk0 + i)),
                      pl.BlockSpec((SUBLANES, SORT_TOKENS), lambda i, *_: (0, blk0 + i))],
            out_specs=pl.BlockSpec((SORT_TOKENS, D_MODEL), lambda i, *_: (i, 0)),
            scratch_shapes=[pltpu.VMEM((2, LOCAL_ROWS, HALF_D), jnp.uint32),
                            pltpu.SemaphoreType.DMA((2,))]),
        out_shape=jax.ShapeDtypeStruct(x1.shape, F32),
        compiler_params=pltpu.CompilerParams(
            dimension_semantics=("arbitrary",), vmem_limit_bytes=VMEM_LIMIT),
        name="moe_unsort",
    )(plan["lstart"], plan["goff"], plan["cpad"], plan["ltot"],
      ys, x1, g2, pos_cols, gate_cols)


def _rope_table(pos):
    half = ROT_DIM // 2
    inv_freq = ROPE_THETA ** (-jnp.arange(0, ROT_DIM, 2, dtype=F32) / ROT_DIM)
    ang = pos.astype(F32)[:, None] * inv_freq[None, :]
    cos, sin = jnp.cos(ang), jnp.sin(ang)
    n = pos.shape[0]
    rest = HEAD_DIM - ROT_DIM
    c = jnp.concatenate([cos, cos, jnp.ones((n, rest), F32)], axis=1)
    s_next = jnp.concatenate([-sin, jnp.zeros((n, half + rest), F32)], axis=1)
    s_prev = jnp.concatenate([jnp.zeros((n, half), F32), sin, jnp.zeros((n, rest), F32)], axis=1)
    rep = LANES // HEAD_DIM
    return jnp.concatenate([jnp.tile(c, (1, rep)), jnp.tile(s_next, (1, rep)), jnp.tile(s_prev, (1, rep))], axis=1)


def _num_expert_tiles(ntok):
    nblk = ntok // SORT_TOKENS
    worst = ntok * TOP_K + nblk * N_EXPERTS * RUN_ALIGN + N_EXPERTS * (EXPERT_TILE - 1)
    return -(-worst // EXPERT_TILE)


def _routing_plan(cnt):
    nblk = cnt.shape[0]
    n_tiles = _num_expert_tiles(nblk * SORT_TOKENS)
    cpad = jnp.maximum(-(-cnt // RUN_ALIGN) * RUN_ALIGN, RUN_ALIGN)
    lstart = jnp.cumsum(cpad, axis=1) - cpad
    tot = cpad.sum(axis=0)
    region = -(-tot // EXPERT_TILE) * EXPERT_TILE
    region_end = jnp.cumsum(region)
    gstart = region_end - region
    goff = gstart[None, :] + jnp.cumsum(cpad, axis=0) - cpad
    tile_start = jnp.arange(n_tiles, dtype=jnp.int32) * EXPERT_TILE
    tile_exp = jnp.minimum((tile_start[:, None] >= region_end[None, :]).astype(jnp.int32).sum(axis=1), N_EXPERTS - 1)
    onehot = (tile_exp[:, None] == jnp.arange(N_EXPERTS, dtype=jnp.int32)[None, :]).astype(jnp.int32)
    runs_end = (onehot * (gstart + tot)[None, :]).sum(axis=1)
    tile_rows = jnp.clip(runs_end - tile_start, 0, EXPERT_TILE)
    i32 = lambda a: a.astype(jnp.int32)
    return dict(
        tile_rows=i32(tile_rows),
        lstart=i32(lstart).reshape(-1), goff=i32(goff).reshape(-1), cpad=i32(cpad).reshape(-1),
        ltot=i32(cpad.sum(axis=1)), fill_start=i32(gstart + tot), fill_len=i32(region - tot),
        lstart_col=lstart.astype(F32).reshape(nblk, N_EXPERTS, 1),
        tile_exp=i32(tile_exp), nact=i32(region_end[-1] // EXPERT_TILE).reshape(1))


def kernel(x_prompt, x_sample, cache_k_win, cache_v_win, state_conv, c_prompt, c_sample, w_ada, b_ada, norm1_g, norm2_g, w_in, q_norm_g, k_norm_g, attn_sinks, w_attn_out, conv_w, conv_b, w_conv_out, w_o, w_router, b_router, w_gate, b_gate, w_up, b_up, w_down, b_down):
    nb, seq, _ = x_prompt.shape
    ns, dseq, _ = x_sample.shape
    wbuf = cache_k_win.shape[2]
    assert w_ada.shape[0] == 1, "single-layer step"
    l = 0

    mod = _adaln(jnp.concatenate([c_prompt, c_sample], axis=0), w_ada[l], b_ada[l])
    mod_p = mod[:nb].reshape(nb, 6, D_MODEL)
    mod_s = mod[nb:].reshape(ns, 6, D_MODEL).transpose(1, 0, 2).reshape(6, ns, 1, D_MODEL)

    bd = jnp.asarray(np.kron(np.eye(N_HEADS), np.ones((HEAD_DIM, HEAD_DIM))), BF16)
    weights = (norm1_g[l].reshape(1, D_MODEL), norm2_g[l].reshape(1, D_MODEL), w_in[l].astype(BF16), bd,
               jnp.tile(q_norm_g[l], N_HEADS).reshape(1, Q_W), jnp.tile(k_norm_g[l], N_KV).reshape(1, KV_W),
               w_attn_out[l].astype(BF16), conv_w[l], conv_b[l].reshape(1, CONV_CH), w_conv_out[l].astype(BF16),
               w_o[l].astype(BF16), w_router[l].T.astype(BF16), b_router[l].reshape(N_EXPERTS, 1))
    sinks = attn_sinks[l]

    tb = MIXER_TOKENS
    assert tb == SORT_TOKENS and SORT_TOKENS % (SAMPLE_STEP_SEQS * dseq) == 0
    (x1_p, h2_p, topi_p, topg_p, kwin_p, vwin_p, zwin_p, cnt_p) = _prompt_mixer(
        x_prompt, mod_p, _rope_table(jnp.arange(seq)), sinks, weights, tb)
    pos_s = jnp.tile(PAST_LEN + jnp.arange(dseq), ns)
    state_rows = state_conv[l].transpose(1, 0, 2).reshape(CONV_K - 1, ns, 1, CONV_CH)
    (x1_s, h2_s, topi_s, topg_s, knew, vnew, z_s, cnt_s) = _sample_mixer(
        x_sample.reshape(ns * dseq, D_MODEL), mod_s, _rope_table(pos_s),
        cache_k_win[l].reshape(ns, wbuf, KV_W), cache_v_win[l].reshape(ns, wbuf, KV_W),
        state_rows, sinks, weights, SAMPLE_STEP_SEQS)

    ntok_p = nb * seq
    ntok = ntok_p + ns * dseq
    topi_all = jnp.concatenate([topi_p.transpose(1, 0, 2).reshape(SUBLANES, ntok_p), topi_s], axis=1)
    topg_all = jnp.concatenate([topg_p.transpose(1, 0, 2).reshape(SUBLANES, ntok_p), topg_s], axis=1)
    cnt_s = cnt_s.reshape(-1, SORT_TOKENS // (SAMPLE_STEP_SEQS * dseq), N_EXPERTS).sum(axis=1)
    plan = _routing_plan(jnp.concatenate([cnt_p.reshape(-1, N_EXPERTS), cnt_s], axis=0))
    tri = jnp.asarray(np.triu(np.ones((SORT_TOKENS, SORT_TOKENS)), k=1), BF16)
    xs, pos_all = _sort_tokens(plan, h2_p.reshape(ntok_p, D_MODEL), h2_s, topi_all, tri,
                               _num_expert_tiles(ntok) * EXPERT_TILE)
    pos_cols, gate_cols = pos_all, topg_all
    ys = _experts(plan["tile_exp"], plan["nact"], plan["tile_rows"], xs,
                  w_gate[l], b_gate[l].reshape(N_EXPERTS, 1, D_MODEL),
                  w_up[l], b_up[l].reshape(N_EXPERTS, 1, D_MODEL),
                  w_down[l], b_down[l].reshape(N_EXPERTS, 1, D_MODEL))
    blocks_per_seq = seq // SORT_TOKENS
    y_p = _unsort_tokens(plan, ys, x1_p.reshape(ntok_p, D_MODEL), mod_p[:, 5:6, :],
                         pl.BlockSpec((None, 1, D_MODEL), lambda i, *_: (i // blocks_per_seq, 0, 0)),
                         pos_cols, gate_cols, 0)
    y_s = _unsort_tokens(plan, ys, x1_s, mod_s[5],
                         pl.BlockSpec((SORT_TOKENS // dseq, 1, D_MODEL), lambda i, *_: (i, 0, 0)),
                         pos_cols, gate_cols, ntok_p // SORT_TOKENS)

    n_kv_shape = (N_KV, HEAD_DIM)
    k_win_p = kwin_p.reshape(1, nb, WINDOW, *n_kv_shape)
    v_win_p = vwin_p.reshape(1, nb, WINDOW, *n_kv_shape)
    conv_p = zwin_p[:, SUBLANES - (CONV_K - 1):, :][None]
    k_win_s = knew.reshape(1, ns, wbuf, *n_kv_shape)
    v_win_s = vnew.reshape(1, ns, wbuf, *n_kv_shape)
    conv_s = z_s.reshape(ns, dseq, CONV_CH)[:, dseq - (CONV_K - 1):, :][None]
    return (y_p.reshape(nb, seq, D_MODEL), y_s.reshape(ns, dseq, D_MODEL),
            k_win_p, v_win_p, conv_p, k_win_s, v_win_s, conv_s)
```

```python
import functools

import numpy as np
import jax
import jax.numpy as jnp
from jax import lax
from jax.experimental import pallas as pl
from jax.experimental.pallas import tpu as pltpu

D_MODEL = 1024
HEAD_DIM = 64
N_HEADS = 8
N_KV = 2
GROUP = N_HEADS // N_KV
Q_W = N_HEADS * HEAD_DIM
KV_W = N_KV * HEAD_DIM
WINDOW = 128
ROT_DIM = HEAD_DIM // 4
ROPE_THETA = 500000.0
ATTN_SCALE = HEAD_DIM ** -0.5
CONV_CH = D_MODEL // 2
CONV_K = 3
N_EXPERTS = 32
TOP_K = 4
SWIGLU_ALPHA = 1.702
SWIGLU_LIMIT = 7.0
NORM_EPS = 1e-5
QK_EPS = 1e-6
NEG_INF = -1e30
PAST_LEN = 16384

LANES = 128
SUBLANES = 8
VMEM_LIMIT = 56 * 1024 * 1024
ADALN_COLS = 1536
MIXER_TOKENS = 512

_OFF_Q = 0
_OFF_K = _OFF_Q + Q_W
_OFF_V = _OFF_K + KV_W
_OFF_GB = _OFF_V + KV_W
_OFF_GC = _OFF_GB + CONV_CH
_OFF_XC = _OFF_GC + CONV_CH
_OFF_BA = _OFF_XC + CONV_CH
_OFF_BC = _OFF_BA + D_MODEL

BF16 = jnp.bfloat16
F32 = jnp.float32
_NT = (((1,), (1,)), ((), ()))


def _dot(a, b):
    return jnp.dot(a, b, preferred_element_type=F32)


def _dot_nt(a, b):
    return lax.dot_general(a, b, _NT, preferred_element_type=F32)


def _sigmoid(x):
    return 1.0 / (1.0 + jnp.exp(-x))


def _split_bf16(x):
    hi = x.astype(BF16)
    lo = (x - hi.astype(F32)).astype(BF16)
    return hi, lo


def _const_spec(shape):
    nd = len(shape)
    return pl.BlockSpec(shape, lambda *_: (0,) * nd, pipeline_mode=pl.Buffered(1))


def _adaln_body(c_ref, w_ref, b_ref, o_ref):
    c = c_ref[...]
    s = c * _sigmoid(c)
    o_ref[...] = _dot(s.astype(BF16), w_ref[...].astype(BF16)) + b_ref[...]


def _adaln(c, w_ada, b_ada):
    n = c.shape[0]
    cols = w_ada.shape[1]
    bn = ADALN_COLS
    return pl.pallas_call(
        _adaln_body,
        grid=(cols // bn,),
        in_specs=[pl.BlockSpec((n, D_MODEL), lambda i: (0, 0)),
                  pl.BlockSpec((D_MODEL, bn), lambda i: (0, i)),
                  pl.BlockSpec((1, bn), lambda i: (0, i))],
        out_specs=pl.BlockSpec((n, bn), lambda i: (0, i)),
        out_shape=jax.ShapeDtypeStruct((n, cols), F32),
        compiler_params=pltpu.CompilerParams(vmem_limit_bytes=VMEM_LIMIT),
        name="adaln",
    )(c, w_ada, b_ada.reshape(1, cols))


def _expand_rows(v, rep):
    n, _, w = v.shape
    return jnp.broadcast_to(v, (n, rep, w)).reshape(n * rep, w)


def _modulated_norm(x, g, shift, scale):
    y = x * lax.rsqrt(jnp.mean(x * x, axis=-1, keepdims=True) + NORM_EPS)
    return (y * g) * (1.0 + scale) + shift


def _head_norm(t, bd, g):
    w = t.shape[1]
    sq_hi, sq_lo = _split_bf16(t * t)
    blk = bd[:w, :w]
    ms = (_dot(sq_hi, blk) + _dot(sq_lo, blk)) * (1.0 / HEAD_DIM)
    return t * lax.rsqrt(ms + QK_EPS) * g[:, :w]


def _rope(y, rope):
    c = rope[:, 0:LANES]
    s_next = rope[:, LANES:2 * LANES]
    s_prev = rope[:, 2 * LANES:3 * LANES]
    half = ROT_DIM // 2
    return y * c + pltpu.roll(y, LANES - half, 1) * s_next + pltpu.roll(y, half, 1) * s_prev


def _pair_expand(t):
    lane = lax.broadcasted_iota(jnp.int32, t.shape, 1)
    lo = lane < HEAD_DIM
    r = pltpu.roll(t, HEAD_DIM, 1)
    zero = jnp.zeros_like(t)
    a0 = jnp.where(lo, t, zero).astype(BF16)
    b0 = jnp.where(lo, zero, r).astype(BF16)
    a1 = jnp.where(lo, r, zero).astype(BF16)
    b1 = jnp.where(lo, zero, t).astype(BF16)
    return ((a0, b0), (a1, b1))


def _softmax_pv(s_list, v_list, sink):
    m = jnp.full((s_list[0].shape[0], 1), sink, F32)
    for s in s_list:
        m = jnp.maximum(m, jnp.max(s, axis=-1, keepdims=True))
    den = jnp.exp(sink - m)
    acc = None
    for s, v in zip(s_list, v_list):
        e = jnp.exp(s - m)
        den = den + jnp.sum(e, axis=-1, keepdims=True)
        pv = _dot(e.astype(BF16), v)
        acc = pv if acc is None else acc + pv
    return acc * (1.0 / den)


def _route(h2, wr, br):
    logits = _dot_nt(wr, h2.astype(BF16)) + br
    eid = lax.broadcasted_iota(jnp.int32, logits.shape, 0).astype(F32)
    vals, idxs = [], []
    chosen = jnp.zeros(logits.shape, F32)
    for _ in range(TOP_K):
        m = jnp.max(logits, axis=0, keepdims=True)
        idx = jnp.min(jnp.where(logits == m, eid, float(N_EXPERTS)), axis=0, keepdims=True)
        vals.append(m)
        idxs.append(idx)
        hit = eid == idx
        chosen = jnp.where(hit, 1.0, chosen)
        logits = jnp.where(hit, -jnp.inf, logits)
    counts = jnp.sum(chosen, axis=1, keepdims=True).astype(jnp.int32)
    ex = [jnp.exp(v - vals[0]) for v in vals]
    tot = ex[0] + ex[1] + ex[2] + ex[3]
    inv = 1.0 / tot
    gates = [e * inv for e in ex]
    rowid = lax.broadcasted_iota(jnp.int32, (SUBLANES, h2.shape[0]), 0)
    topi = jnp.zeros((SUBLANES, h2.shape[0]), F32)
    topg = jnp.zeros((SUBLANES, h2.shape[0]), F32)
    for k in range(TOP_K):
        topi = jnp.where(rowid == k, idxs[k], topi)
        topg = jnp.where(rowid == k, gates[k], topg)
    return topi.astype(jnp.int32), topg, counts


SAMPLE_STEP_SEQS = 32
SAMPLE_ATTN_SEQS = 16
_QKV_COLS = (_OFF_Q, Q_W + 2 * KV_W)
_LATE_COLS = ((_OFF_GB, 2 * CONV_CH), (_OFF_XC, CONV_CH), (_OFF_BA, D_MODEL), (_OFF_BC, D_MODEL))


def _in_proj(proj, h_bf, w_in_ref, group):
    off, width = group
    proj[off] = _dot(h_bf, w_in_ref[:, off:off + width])


def _cols(proj, off, width):
    for start, piece in proj.items():
        if start <= off and off + width <= start + piece.shape[1]:
            return piece[:, off - start:off - start + width]
    raise KeyError(off)


def _merge_and_route(x, o_bf, y_conv, proj, mod, n2g_ref, w_ao_ref, w_co_ref, w_o_ref,
                     wr_ref, br_ref, before_tail=None):
    sh1, sc1, g1, sh2, sc2, g2 = mod
    a = _dot(o_bf, w_ao_ref[...])
    c = _dot(y_conv.astype(BF16), w_co_ref[...])
    m = _sigmoid(_cols(proj, _OFF_BA, D_MODEL)) * a + _sigmoid(_cols(proj, _OFF_BC, D_MODEL)) * c
    x1 = x + g1 * _dot(m.astype(BF16), w_o_ref[...])
    if before_tail is not None:
        before_tail()
    h2 = _modulated_norm(x1, n2g_ref[...], sh2, sc2)
    return (x1, h2) + _route(h2, wr_ref[...], br_ref[...])


def _prompt_body(sinks_ref, x_ref, mod_ref, rope_ref, xn_ref, modn_ref,
                 n1g_ref, n2g_ref, w_in_ref, bd_ref, qg_ref, kg_ref,
                 w_ao_ref, cw_ref, cb_ref, w_co_ref, w_o_ref, wr_ref, br_ref,
                 x1_ref, h2_ref, topi_ref, topg_ref, kwin_ref, vwin_ref, zwin_ref, cnt_ref,
                 kbuf, vbuf, zbuf, h_buf, qkv_buf, late_buf, *, tb):
    j = pl.program_id(1)
    nsub = tb // WINDOW

    @pl.when(j == 0)
    def _():
        kbuf[0:WINDOW, :] = jnp.zeros((WINDOW, KV_W), F32)
        vbuf[0:WINDOW, :] = jnp.zeros((WINDOW, KV_W), F32)
        zbuf[0:SUBLANES, :] = jnp.zeros((SUBLANES, CONV_CH), F32)

    def front(xr, mr):
        h = _modulated_norm(xr[0], n1g_ref[...], mr[0, 0:1, :], mr[0, 1:2, :]).astype(BF16)
        h_buf[...] = h
        for group, buf in ((_QKV_COLS, qkv_buf), (_LATE_COLS[0], late_buf)):
            buf[...] = _dot(h, w_in_ref[:, group[0]:group[0] + group[1]])

    @pl.when((pl.program_id(0) == 0) & (j == 0))
    def _():
        front(x_ref, mod_ref)

    x = x_ref[0]
    mod = tuple(mod_ref[0, i:i + 1, :] for i in range(6))
    h_bf = h_buf[...]
    proj = {_QKV_COLS[0]: qkv_buf[...], _LATE_COLS[0][0]: late_buf[...]}
    rope = rope_ref[...]
    bd = bd_ref[...]

    k = _rope(_head_norm(_cols(proj, _OFF_K, KV_W), bd, kg_ref[...]), rope)
    v = _cols(proj, _OFF_V, KV_W)
    kbuf[WINDOW:WINDOW + tb, :] = k
    vbuf[WINDOW:WINDOW + tb, :] = v
    kwin_ref[0] = k[tb - WINDOW:tb]
    vwin_ref[0] = v[tb - WINDOW:tb]
    k_exp = _pair_expand(kbuf[...])
    v_exp = _pair_expand(vbuf[...])

    _in_proj(proj, h_bf, w_in_ref, _LATE_COLS[1])
    _in_proj(proj, h_bf, w_in_ref, _LATE_COLS[2])
    qn = _head_norm(_cols(proj, _OFF_Q, Q_W), bd, qg_ref[...])
    q_tiles = []
    for t in range(Q_W // LANES):
        qt = _rope(qn[:, t * LANES:(t + 1) * LANES], rope)
        q_tiles.append((qt * ATTN_SCALE).astype(BF16))

    row = lax.broadcasted_iota(jnp.int32, (WINDOW, 2 * WINDOW), 0)
    col = lax.broadcasted_iota(jnp.int32, (WINDOW, 2 * WINDOW), 1)
    band = (col > row) & (col <= row + WINDOW)
    first_key = jnp.where(j > 0, 0, WINDOW)
    band_first = band & (col >= first_key)
    o_rows = []
    for i in range(nsub):
        for group in _LATE_COLS[3:][i::nsub]:
            _in_proj(proj, h_bf, w_in_ref, group)
        mask = band_first if i == 0 else band
        keys = slice(i * WINDOW, (i + 2) * WINDOW)
        o_tiles = []
        for t in range(Q_W // LANES):
            c = t // 2
            qt = q_tiles[t][i * WINDOW:(i + 1) * WINDOW]
            o_t = None
            for half in range(2):
                s = jnp.where(mask, _dot_nt(qt, k_exp[c][half][keys]), NEG_INF)
                sink = sinks_ref[c * GROUP + (t % 2) * 2 + half]
                part = _softmax_pv([s], [v_exp[c][half][keys]], sink)
                o_t = part if o_t is None else o_t + part
            o_tiles.append(o_t.astype(BF16))
        o_rows.append(jnp.concatenate(o_tiles, axis=1))
    o_bf = jnp.concatenate(o_rows, axis=0) if nsub > 1 else o_rows[0]
    kbuf[0:WINDOW, :] = kbuf[tb:tb + WINDOW, :]
    vbuf[0:WINDOW, :] = vbuf[tb:tb + WINDOW, :]

    z = _cols(proj, _OFF_GC, CONV_CH) * _cols(proj, _OFF_XC, CONV_CH)
    zbuf[SUBLANES:SUBLANES + tb, :] = z
    zwin_ref[0] = z[tb - SUBLANES:tb]
    z1 = zbuf[SUBLANES - 1:SUBLANES - 1 + tb, :]
    z2 = zbuf[SUBLANES - 2:SUBLANES - 2 + tb, :]
    conv = cb_ref[...] + cw_ref[0:1, :] * z2 + cw_ref[1:2, :] * z1 + cw_ref[2:3, :] * z
    y_conv = _cols(proj, _OFF_GB, CONV_CH) * conv
    zbuf[0:SUBLANES, :] = zbuf[tb:tb + SUBLANES, :]

    x1, h2, topi, topg, counts = _merge_and_route(x, o_bf, y_conv, proj, mod, n2g_ref, w_ao_ref, w_co_ref,
                                                  w_o_ref, wr_ref, br_ref,
                                                  before_tail=lambda: front(xn_ref, modn_ref))
    x1_ref[0] = x1
    h2_ref[0] = h2.astype(BF16)
    topi_ref[0] = topi
    topg_ref[0] = topg
    cnt_ref[...] = counts


def _prompt_mixer(x, mod, rope, sinks, weights, tb):
    nb, seq, _ = x.shape
    nj = seq // tb
    consts = weights

    def nxt(b, j):
        s = jnp.minimum(b * nj + j + 1, nb * nj - 1)
        return s // nj, s % nj
    in_specs = [
        pl.BlockSpec((1, tb, D_MODEL), lambda b, j, s: (b, j, 0)),
        pl.BlockSpec((1, 6, D_MODEL), lambda b, j, s: (b, 0, 0)),
        pl.BlockSpec((tb, 3 * LANES), lambda b, j, s: (j, 0)),
        pl.BlockSpec((1, tb, D_MODEL), lambda b, j, s: (*nxt(b, j), 0)),
        pl.BlockSpec((1, 6, D_MODEL), lambda b, j, s: (nxt(b, j)[0], 0, 0)),
    ] + [_const_spec(w.shape) for w in consts]
    out_specs = [
        pl.BlockSpec((1, tb, D_MODEL), lambda b, j, s: (b, j, 0)),
        pl.BlockSpec((1, tb, D_MODEL), lambda b, j, s: (b, j, 0)),
        pl.BlockSpec((1, SUBLANES, tb), lambda b, j, s: (b, 0, j)),
        pl.BlockSpec((1, SUBLANES, tb), lambda b, j, s: (b, 0, j)),
        pl.BlockSpec((1, WINDOW, KV_W), lambda b, j, s: (b, 0, 0)),
        pl.BlockSpec((1, WINDOW, KV_W), lambda b, j, s: (b, 0, 0)),
        pl.BlockSpec((1, SUBLANES, CONV_CH), lambda b, j, s: (b, 0, 0)),
        pl.BlockSpec((None, None, N_EXPERTS, 1), lambda b, j, s: (b, j, 0, 0)),
    ]
    out_shape = [
        jax.ShapeDtypeStruct((nb, seq, D_MODEL), F32),
        jax.ShapeDtypeStruct((nb, seq, D_MODEL), BF16),
        jax.ShapeDtypeStruct((nb, SUBLANES, seq), jnp.int32),
        jax.ShapeDtypeStruct((nb, SUBLANES, seq), F32),
        jax.ShapeDtypeStruct((nb, WINDOW, KV_W), F32),
        jax.ShapeDtypeStruct((nb, WINDOW, KV_W), F32),
        jax.ShapeDtypeStruct((nb, SUBLANES, CONV_CH), F32),
        jax.ShapeDtypeStruct((nb, nj, N_EXPERTS, 1), jnp.int32),
    ]
    return pl.pallas_call(
        functools.partial(_prompt_body, tb=tb),
        grid_spec=pltpu.PrefetchScalarGridSpec(
            num_scalar_prefetch=1, grid=(nb, seq // tb),
            in_specs=in_specs, out_specs=out_specs,
            scratch_shapes=[pltpu.VMEM((tb + WINDOW, KV_W), F32),
                            pltpu.VMEM((tb + WINDOW, KV_W), F32),
                            pltpu.VMEM((tb + SUBLANES, CONV_CH), F32),
                            pltpu.VMEM((tb, D_MODEL), BF16),
                            pltpu.VMEM((tb, _QKV_COLS[1]), F32),
                            pltpu.VMEM((tb, _LATE_COLS[0][1]), F32)]),
        out_shape=out_shape,
        compiler_params=pltpu.CompilerParams(
            dimension_semantics=("arbitrary", "arbitrary"), vmem_limit_bytes=VMEM_LIMIT),
        name="prompt_mixer",
    )(sinks, x, mod, rope, x, mod, *consts)


def _sample_body(sinks_ref, x_ref, mod_ref, rope_ref, ck_ref, cv_ref, st_ref,
                 n1g_ref, n2g_ref, w_in_ref, bd_ref, qg_ref, kg_ref,
                 w_ao_ref, cw_ref, cb_ref, w_co_ref, w_o_ref, wr_ref, br_ref,
                 x1_ref, h2_ref, topi_ref, topg_ref, knew_ref, vnew_ref, z_ref, cnt_ref, *, nseq, dseq):
    rows = nseq * dseq
    wbuf = ck_ref.shape[1]
    x = x_ref[...]
    mod = tuple(_expand_rows(mod_ref[i], dseq) for i in range(6))
    h_bf = _modulated_norm(x, n1g_ref[...], mod[0], mod[1]).astype(BF16)
    proj = {}
    for group in (_QKV_COLS,) + _LATE_COLS:
        _in_proj(proj, h_bf, w_in_ref, group)
    rope = rope_ref[...]
    bd = bd_ref[...]

    k = _rope(_head_norm(_cols(proj, _OFF_K, KV_W), bd, kg_ref[...]), rope)
    v = _cols(proj, _OFF_V, KV_W)
    for win_ref, cache_ref, new in ((knew_ref, ck_ref, k), (vnew_ref, cv_ref, v)):
        win_ref[:, 0:wbuf - dseq, :] = cache_ref[:, dseq:, :]
        win_ref[:, wbuf - dseq:, :] = new.reshape(nseq, dseq, KV_W)
    gseq = SAMPLE_ATTN_SEQS
    grows = gseq * dseq
    qrow = lax.broadcasted_iota(jnp.int32, (grows, gseq * wbuf), 0)
    ccol = lax.broadcasted_iota(jnp.int32, (grows, gseq * wbuf), 1)
    mask_c = ((ccol // wbuf) == (qrow // dseq)) & ((ccol % wbuf) > (qrow % dseq) + (wbuf - WINDOW))
    qrow_n = lax.broadcasted_iota(jnp.int32, (grows, grows), 0)
    ncol = lax.broadcasted_iota(jnp.int32, (grows, grows), 1)
    mask_n = ((ncol // dseq) == (qrow_n // dseq)) & ((ncol % dseq) <= (qrow_n % dseq))

    qn = _head_norm(_cols(proj, _OFF_Q, Q_W), bd, qg_ref[...])
    q_tiles = [(_rope(qn[:, t * LANES:(t + 1) * LANES], rope) * ATTN_SCALE).astype(BF16)
               for t in range(Q_W // LANES)]
    o_rows = []
    for g0 in range(0, nseq, gseq):
        rs = slice(g0 * dseq, g0 * dseq + grows)
        kn_exp = _pair_expand(k[rs])
        vn_exp = _pair_expand(v[rs])
        kc_exp = _pair_expand(ck_ref[g0:g0 + gseq].reshape(gseq * wbuf, KV_W))
        vc_exp = _pair_expand(cv_ref[g0:g0 + gseq].reshape(gseq * wbuf, KV_W))
        o_tiles = []
        for t in range(Q_W // LANES):
            c = t // 2
            o_t = None
            for half in range(2):
                s_c = jnp.where(mask_c, _dot_nt(q_tiles[t][rs], kc_exp[c][half]), NEG_INF)
                s_n = jnp.where(mask_n, _dot_nt(q_tiles[t][rs], kn_exp[c][half]), NEG_INF)
                sink = sinks_ref[c * GROUP + (t % 2) * 2 + half]
                part = _softmax_pv([s_c, s_n], [vc_exp[c][half], vn_exp[c][half]], sink)
                o_t = part if o_t is None else o_t + part
            o_tiles.append(o_t.astype(BF16))
        o_rows.append(jnp.concatenate(o_tiles, axis=1))
    o_bf = jnp.concatenate(o_rows, axis=0) if len(o_rows) > 1 else o_rows[0]

    z = _cols(proj, _OFF_GC, CONV_CH) * _cols(proj, _OFF_XC, CONV_CH)
    z_ref[...] = z
    r = lax.broadcasted_iota(jnp.int32, z.shape, 0) % dseq
    st0 = _expand_rows(st_ref[0], dseq)
    st1 = _expand_rows(st_ref[1], dseq)
    z1 = jnp.where(r == 0, st1, pltpu.roll(z, 1, 0))
    z2 = jnp.where(r == 0, st0, jnp.where(r == 1, st1, pltpu.roll(z, 2, 0)))
    conv = cb_ref[...] + cw_ref[0:1, :] * z2 + cw_ref[1:2, :] * z1 + cw_ref[2:3, :] * z
    y_conv = _cols(proj, _OFF_GB, CONV_CH) * conv

    x1, h2, topi, topg, counts = _merge_and_route(x, o_bf, y_conv, proj, mod, n2g_ref, w_ao_ref, w_co_ref,
                                                  w_o_ref, wr_ref, br_ref)
    x1_ref[...] = x1
    h2_ref[...] = h2.astype(BF16)
    topi_ref[...] = topi
    topg_ref[...] = topg
    cnt_ref[...] = counts


def _sample_mixer(x, mod, rope, cache_k, cache_v, state, sinks, weights, nseq):
    ntok = x.shape[0]
    nall, wbuf, _ = cache_k.shape
    dseq = ntok // nall
    rows = nseq * dseq
    consts = weights
    in_specs = [
        pl.BlockSpec((rows, D_MODEL), lambda i, s: (i, 0)),
        pl.BlockSpec((6, nseq, 1, D_MODEL), lambda i, s: (0, i, 0, 0)),
        pl.BlockSpec((rows, 3 * LANES), lambda i, s: (i, 0)),
        pl.BlockSpec((nseq, wbuf, KV_W), lambda i, s: (i, 0, 0)),
        pl.BlockSpec((nseq, wbuf, KV_W), lambda i, s: (i, 0, 0)),
        pl.BlockSpec((2, nseq, 1, CONV_CH), lambda i, s: (0, i, 0, 0)),
    ] + [_const_spec(w.shape) for w in consts]
    out_specs = [
        pl.BlockSpec((rows, D_MODEL), lambda i, s: (i, 0)),
        pl.BlockSpec((rows, D_MODEL), lambda i, s: (i, 0)),
        pl.BlockSpec((SUBLANES, rows), lambda i, s: (0, i)),
        pl.BlockSpec((SUBLANES, rows), lambda i, s: (0, i)),
        pl.BlockSpec((nseq, wbuf, KV_W), lambda i, s: (i, 0, 0)),
        pl.BlockSpec((nseq, wbuf, KV_W), lambda i, s: (i, 0, 0)),
        pl.BlockSpec((rows, CONV_CH), lambda i, s: (i, 0)),
        pl.BlockSpec((None, N_EXPERTS, 1), lambda i, s: (i, 0, 0)),
    ]
    out_shape = [
        jax.ShapeDtypeStruct((ntok, D_MODEL), F32),
        jax.ShapeDtypeStruct((ntok, D_MODEL), BF16),
        jax.ShapeDtypeStruct((SUBLANES, ntok), jnp.int32),
        jax.ShapeDtypeStruct((SUBLANES, ntok), F32),
        jax.ShapeDtypeStruct((nall, wbuf, KV_W), F32),
        jax.ShapeDtypeStruct((nall, wbuf, KV_W), F32),
        jax.ShapeDtypeStruct((ntok, CONV_CH), F32),
        jax.ShapeDtypeStruct((ntok // rows, N_EXPERTS, 1), jnp.int32),
    ]
    return pl.pallas_call(
        functools.partial(_sample_body, nseq=nseq, dseq=dseq),
        grid_spec=pltpu.PrefetchScalarGridSpec(
            num_scalar_prefetch=1, grid=(ntok // rows,),
            in_specs=in_specs, out_specs=out_specs),
        out_shape=out_shape,
        compiler_params=pltpu.CompilerParams(
            dimension_semantics=("arbitrary",), vmem_limit_bytes=VMEM_LIMIT),
        name="sample_mixer",
    )(sinks, x, mod, rope, cache_k, cache_v, state, *consts)


SORT_TOKENS = 512
RUN_ALIGN = SUBLANES
EXPERT_TILE = 1024
HALF_D = D_MODEL // 2
LOCAL_ROWS = SORT_TOKENS * TOP_K + N_EXPERTS * RUN_ALIGN
LOCAL_CHUNK = 256
assert LOCAL_ROWS % LOCAL_CHUNK == 0
_HI_MASK = 0xFFFF0000


def _pack_halves(x):
    lo = pltpu.bitcast(x[:, :HALF_D], jnp.uint32)
    hi = pltpu.bitcast(x[:, HALF_D:], jnp.uint32)
    return hi | (lo >> 16)


def _unpack_halves(w):
    lo = pltpu.bitcast(w << 16, F32).astype(BF16)
    hi = pltpu.bitcast(w & jnp.uint32(_HI_MASK), F32).astype(BF16)
    return lo, hi


def _local_positions(topi, lstart_col, tri):
    ntok = topi.shape[1]
    eid = lax.broadcasted_iota(jnp.int32, (N_EXPERTS, ntok), 0)
    hits = [eid == topi[k:k + 1, :] for k in range(TOP_K)]
    chosen = jnp.zeros((N_EXPERTS, ntok), F32)
    for h in hits:
        chosen = jnp.where(h, 1.0, chosen)
    base = _dot(chosen.astype(BF16), tri) + lstart_col
    return [jnp.sum(jnp.where(h, base, 0.0), axis=0, keepdims=True) for h in hits]


def _run_copy(loc, glob, sem, lofs, gofs, n, to_global):
    lo = loc.at[pl.ds(pl.multiple_of(lofs, RUN_ALIGN), n)]
    gl = glob.at[pl.ds(pl.multiple_of(gofs, RUN_ALIGN), n)]
    return pltpu.make_async_copy(lo, gl, sem) if to_global else pltpu.make_async_copy(gl, lo, sem)


def _start_runs(meta, blk, loc, glob, sem, to_global):
    lstart_ref, goff_ref, cpad_ref = meta
    for e in range(N_EXPERTS):
        n = pl.multiple_of(cpad_ref[blk * N_EXPERTS + e], RUN_ALIGN)
        _run_copy(loc, glob, sem, lstart_ref[blk * N_EXPERTS + e], goff_ref[blk * N_EXPERTS + e], n,
                  to_global).start(priority=e % 2)


def _wait_runs(nrows, loc, glob, sem, to_global):
    _run_copy(loc, glob, sem, 0, 0, pl.multiple_of(nrows, RUN_ALIGN), to_global).wait()


def _sort_body(lstart_ref, goff_ref, cpad_ref, ltot_ref, fstart_ref, flen_ref, nact_ref,
               hp_ref, hs_ref, topi_ref, lcol_ref, tri_ref, xs_hbm, pos_ref, buf, zbuf, sems, fsem, *, nblk_p, nblk):
    i = pl.program_id(0)
    slot = i % 2
    meta = (lstart_ref, goff_ref, cpad_ref)

    @pl.when(i == 0)
    def _():
        zbuf[...] = jnp.zeros(zbuf.shape, jnp.uint32)

        def fill(e, tot):
            n = pl.multiple_of(flen_ref[e], RUN_ALIGN)

            @pl.when(n > 0)
            def _():
                _run_copy(zbuf, xs_hbm, fsem, 0, fstart_ref[e], n, True).start()
            return tot + n
        total = lax.fori_loop(0, N_EXPERTS, fill, 0)

        @pl.when(total > 0)
        def _():
            _wait_runs(total, zbuf, xs_hbm, fsem, True)

        def fill_tile(t, carry):
            _run_copy(zbuf, xs_hbm, fsem, 0, t * EXPERT_TILE, EXPERT_TILE, True).start()
            return carry
        lax.fori_loop(nact_ref[0], xs_hbm.shape[0] // EXPERT_TILE, fill_tile, 0)

    def run(h_ref):
        pos = [p.astype(jnp.int32) for p in _local_positions(topi_ref[...], lcol_ref[...], tri_ref[...])]
        h = h_ref[...]
        for r0 in range(0, LOCAL_ROWS, LOCAL_CHUNK):
            rowid = lax.broadcasted_iota(jnp.int32, (LOCAL_CHUNK, SORT_TOKENS), 0) + r0
            onehot = jnp.zeros((LOCAL_CHUNK, SORT_TOKENS), F32)
            for p in pos:
                onehot = jnp.where(rowid == p, 1.0, onehot)
            buf[slot, r0:r0 + LOCAL_CHUNK, :] = _pack_halves(_dot(onehot.astype(BF16), h))
        krow = lax.broadcasted_iota(jnp.int32, (SUBLANES, SORT_TOKENS), 0)
        pos_rows = jnp.zeros((SUBLANES, SORT_TOKENS), jnp.int32)
        for k in range(TOP_K):
            pos_rows = jnp.where(krow == k, pos[k], pos_rows)
        pos_ref[...] = pos_rows

    @pl.when(i < nblk_p)
    def _():
        run(hp_ref)

    @pl.when(i >= nblk_p)
    def _():
        run(hs_ref)

    _start_runs(meta, i, buf.at[slot], xs_hbm, sems.at[slot], True)

    @pl.when(i > 0)
    def _():
        _wait_runs(ltot_ref[jnp.maximum(i - 1, 0)], buf.at[1 - slot], xs_hbm, sems.at[1 - slot], True)

    @pl.when(i == nblk - 1)
    def _():
        _wait_runs(ltot_ref[i], buf.at[slot], xs_hbm, sems.at[slot], True)

        def wait_tile(t, carry):
            _run_copy(zbuf, xs_hbm, fsem, 0, t * EXPERT_TILE, EXPERT_TILE, True).wait()
            return carry
        lax.fori_loop(nact_ref[0], xs_hbm.shape[0] // EXPERT_TILE, wait_tile, 0)


def _sort_tokens(plan, h_p, h_s, topi_all, tri, n_slots):
    nblk_p = h_p.shape[0] // SORT_TOKENS
    nblk = nblk_p + h_s.shape[0] // SORT_TOKENS
    return pl.pallas_call(
        functools.partial(_sort_body, nblk_p=nblk_p, nblk=nblk),
        grid_spec=pltpu.PrefetchScalarGridSpec(
            num_scalar_prefetch=7, grid=(nblk,),
            in_specs=[pl.BlockSpec((SORT_TOKENS, D_MODEL), lambda i, *_: (jnp.minimum(i, nblk_p - 1), 0)),
                      pl.BlockSpec((SORT_TOKENS, D_MODEL), lambda i, *_: (jnp.maximum(i - nblk_p, 0), 0)),
                      pl.BlockSpec((SUBLANES, SORT_TOKENS), lambda i, *_: (0, i)),
                      pl.BlockSpec((None, N_EXPERTS, 1), lambda i, *_: (i, 0, 0)),
                      _const_spec(tri.shape)],
            out_specs=[pl.BlockSpec(memory_space=pl.ANY),
                       pl.BlockSpec((SUBLANES, SORT_TOKENS), lambda i, *_: (0, i))],
            scratch_shapes=[pltpu.VMEM((2, LOCAL_ROWS, HALF_D), jnp.uint32),
                            pltpu.VMEM((EXPERT_TILE, HALF_D), jnp.uint32),
                            pltpu.SemaphoreType.DMA((2,)),
                            pltpu.SemaphoreType.DMA]),
        out_shape=[jax.ShapeDtypeStruct((n_slots, HALF_D), jnp.uint32),
                   jax.ShapeDtypeStruct(topi_all.shape, jnp.int32)],
        compiler_params=pltpu.CompilerParams(
            dimension_semantics=("arbitrary",), vmem_limit_bytes=VMEM_LIMIT),
        name="moe_sort",
    )(plan["lstart"], plan["goff"], plan["cpad"], plan["ltot"], plan["fill_start"], plan["fill_len"], plan["nact"],
      h_p, h_s, topi_all, plan["lstart_col"], tri)


def _experts_body(tile_exp_ref, nact_ref, rows_ref, xs_ref, wg_hbm, bg_ref, wu_hbm, bu_ref, wd_hbm, bd_ref,
                  ys_ref, w_f32, w_bf, sems):
    t = pl.program_id(0)
    nact = nact_ref[0]
    w_hbm = (wg_hbm, wu_hbm, wd_hbm)

    def weight_copies(e, slot):
        return [pltpu.make_async_copy(w.at[e], w_f32.at[slot, j], sems.at[slot, j]) for j, w in enumerate(w_hbm)]

    @pl.when(t < nact)
    def _():
        e = tile_exp_ref[t]
        slot = e % 2

        @pl.when(t == 0)
        def _():
            for cp in weight_copies(e, slot):
                cp.start()

        quarter = EXPERT_TILE // 4
        quarters = (rows_ref[t] + quarter - 1) // quarter
        first = (t == 0) | (e != tile_exp_ref[jnp.maximum(t - 1, 0)])
        cast_inline = first & (quarters == 4)

        def cast_weights():
            for j in range(3):
                w_bf[j] = w_f32[slot, j].astype(BF16)

        @pl.when(first)
        def _():
            @pl.when(e + 1 < N_EXPERTS)
            def _():
                for cp in weight_copies(e + 1, 1 - slot):
                    cp.start()
            for cp in weight_copies(e, slot):
                cp.wait()

        @pl.when(first & (quarters < 4))
        def _():
            cast_weights()

        def swiglu(rows, with_cast=False):
            if with_cast:
                cast_weights()
            x = jnp.concatenate(_unpack_halves(xs_ref[0:rows, :]), axis=1)
            g = _dot(x, w_bf[0]) + bg_ref[0]
            u = _dot(x, w_bf[1]) + bu_ref[0]
            g = jnp.minimum(g, SWIGLU_LIMIT)
            u = jnp.clip(u, -SWIGLU_LIMIT, SWIGLU_LIMIT)
            a = g * _sigmoid(SWIGLU_ALPHA * g) * (u + 1.0)
            out = _dot(a.astype(BF16), w_bf[2]) + bd_ref[0]
            ys_ref[0:rows, :] = _pack_halves(out.astype(BF16).astype(F32))

        for q in range(1, 4):
            @pl.when(quarters == q)
            def _(q=q):
                swiglu(q * quarter)
                ys_ref[q * quarter:, :] = jnp.zeros((EXPERT_TILE - q * quarter, HALF_D), jnp.uint32)

        @pl.when((quarters == 4) & jnp.logical_not(cast_inline))
        def _():
            swiglu(EXPERT_TILE)

        @pl.when(cast_inline)
        def _():
            swiglu(EXPERT_TILE, with_cast=True)


def _experts(tile_exp, nact, tile_rows, xs, wg, bg, wu, bu, wd, bd):
    n_tiles = tile_exp.shape[0]

    def active(t, na):
        return jnp.minimum(t, na[0] - 1)
    wspec = pl.BlockSpec(memory_space=pl.ANY)
    bspec = pl.BlockSpec((1, 1, D_MODEL), lambda t, te, na, tr: (te[active(t, na)], 0, 0))
    xspec = pl.BlockSpec((EXPERT_TILE, HALF_D), lambda t, te, na, tr: (active(t, na), 0))
    return pl.pallas_call(
        _experts_body,
        grid_spec=pltpu.PrefetchScalarGridSpec(
            num_scalar_prefetch=3, grid=(n_tiles,),
            in_specs=[xspec, wspec, bspec, wspec, bspec, wspec, bspec],
            out_specs=xspec,
            scratch_shapes=[pltpu.VMEM((2, 3, D_MODEL, D_MODEL), F32),
                            pltpu.VMEM((3, D_MODEL, D_MODEL), BF16),
                            pltpu.SemaphoreType.DMA((2, 3))]),
        out_shape=jax.ShapeDtypeStruct(xs.shape, jnp.uint32),
        input_output_aliases={3: 0},
        compiler_params=pltpu.CompilerParams(
            dimension_semantics=("arbitrary",), vmem_limit_bytes=VMEM_LIMIT),
        name="experts",
    )(tile_exp, nact, tile_rows, xs, wg, bg, wu, bu, wd, bd)


def _unsort_body(lstart_ref, goff_ref, cpad_ref, ltot_ref,
                 ys_hbm, x1_ref, g2_ref, pos_ref, gate_ref, y_ref, buf, sems, *, blk0, nblk):
    i = pl.program_id(0)
    b = blk0 + i
    slot = i % 2
    meta = (lstart_ref, goff_ref, cpad_ref)

    @pl.when(i == 0)
    def _():
        buf[...] = jnp.zeros(buf.shape, jnp.uint32)
        _start_runs(meta, b, buf.at[0], ys_hbm, sems.at[0], False)

    @pl.when(i + 1 < nblk)
    def _():
        _start_runs(meta, b + 1, buf.at[1 - slot], ys_hbm, sems.at[1 - slot], False)

    _wait_runs(ltot_ref[b], buf.at[slot], ys_hbm, sems.at[slot], False)
    pos_rows = pos_ref[...]
    gate_rows = gate_ref[...]
    tn = (((0,), (0,)), ((), ()))
    acc_lo = acc_hi = None
    for c0 in range(0, LOCAL_ROWS, LOCAL_CHUNK):
        rowid = lax.broadcasted_iota(jnp.int32, (LOCAL_CHUNK, SORT_TOKENS), 0) + c0
        weights = jnp.zeros((LOCAL_CHUNK, SORT_TOKENS), F32)
        for k in range(TOP_K):
            weights = jnp.where(rowid == pos_rows[k:k + 1, :], gate_rows[k:k + 1, :], weights)
        weights = weights.astype(BF16)
        y_lo, y_hi = _unpack_halves(buf[slot, c0:c0 + LOCAL_CHUNK, :])
        d_lo = lax.dot_general(weights, y_lo, tn, preferred_element_type=F32)
        d_hi = lax.dot_general(weights, y_hi, tn, preferred_element_type=F32)
        acc_lo = d_lo if acc_lo is None else acc_lo + d_lo
        acc_hi = d_hi if acc_hi is None else acc_hi + d_hi
    g2 = g2_ref[...]
    if g2.ndim == 3:
        g2 = _expand_rows(g2, SORT_TOKENS // g2.shape[0])
    y_ref[:, :HALF_D] = x1_ref[:, :HALF_D] + g2[:, :HALF_D] * acc_lo
    y_ref[:, HALF_D:] = x1_ref[:, HALF_D:] + g2[:, HALF_D:] * acc_hi


def _unsort_tokens(plan, ys, x1, g2, g2_spec, pos_cols, gate_cols, blk0):
    nblk = x1.shape[0] // SORT_TOKENS
    return pl.pallas_call(
        functools.partial(_unsort_body, blk0=blk0, nblk=nblk),
        grid_spec=pltpu.PrefetchScalarGridSpec(
            num_scalar_prefetch=4, grid=(nblk,),
            in_specs=[pl.BlockSpec(memory_space=pl.ANY),
                      pl.BlockSpec((SORT_TOKENS, D_MODEL), lambda i, *_: (i, 0)),
                      g2_spec,
                      pl.BlockSpec((SUBLANES, SORT_TOKENS), lambda i, *_: (0, blk0 + i)),
                      pl.BlockSpec((SUBLANES, SORT_TOKENS), lambda i, *_: (0, blk0 + i))],
            out_specs=pl.BlockSpec((SORT_TOKENS, D_MODEL), lambda i, *_: (i, 0)),
            scratch_shapes=[pltpu.VMEM((2, LOCAL_ROWS, HALF_D), jnp.uint32),
                            pltpu.SemaphoreType.DMA((2,))]),
        out_shape=jax.ShapeDtypeStruct(x1.shape, F32),
        compiler_params=pltpu.CompilerParams(
            dimension_semantics=("arbitrary",), vmem_limit_bytes=VMEM_LIMIT),
        name="moe_unsort",
    )(plan["lstart"], plan["goff"], plan["cpad"], plan["ltot"],
      ys, x1, g2, pos_cols, gate_cols)


def _rope_table(pos):
    half = ROT_DIM // 2
    inv_freq = ROPE_THETA ** (-jnp.arange(0, ROT_DIM, 2, dtype=F32) / ROT_DIM)
    ang = pos.astype(F32)[:, None] * inv_freq[None, :]
    cos, sin = jnp.cos(ang), jnp.sin(ang)
    n = pos.shape[0]
    rest = HEAD_DIM - ROT_DIM
    c = jnp.concatenate([cos, cos, jnp.ones((n, rest), F32)], axis=1)
    s_next = jnp.concatenate([-sin, jnp.zeros((n, half + rest), F32)], axis=1)
    s_prev = jnp.concatenate([jnp.zeros((n, half), F32), sin, jnp.zeros((n, rest), F32)], axis=1)
    rep = LANES // HEAD_DIM
    return jnp.concatenate([jnp.tile(c, (1, rep)), jnp.tile(s_next, (1, rep)), jnp.tile(s_prev, (1, rep))], axis=1)


def _num_expert_tiles(ntok):
    nblk = ntok // SORT_TOKENS
    worst = ntok * TOP_K + nblk * N_EXPERTS * RUN_ALIGN + N_EXPERTS * (EXPERT_TILE - 1)
    return -(-worst // EXPERT_TILE)


def _routing_plan(cnt):
    nblk = cnt.shape[0]
    n_tiles = _num_expert_tiles(nblk * SORT_TOKENS)
    cpad = jnp.maximum(-(-cnt // RUN_ALIGN) * RUN_ALIGN, RUN_ALIGN)
    lstart = jnp.cumsum(cpad, axis=1) - cpad
    tot = cpad.sum(axis=0)
    region = -(-tot // EXPERT_TILE) * EXPERT_TILE
    region_end = jnp.cumsum(region)
    gstart = region_end - region
    goff = gstart[None, :] + jnp.cumsum(cpad, axis=0) - cpad
    tile_start = jnp.arange(n_tiles, dtype=jnp.int32) * EXPERT_TILE
    tile_exp = jnp.minimum((tile_start[:, None] >= region_end[None, :]).astype(jnp.int32).sum(axis=1), N_EXPERTS - 1)
    onehot = (tile_exp[:, None] == jnp.arange(N_EXPERTS, dtype=jnp.int32)[None, :]).astype(jnp.int32)
    runs_end = (onehot * (gstart + tot)[None, :]).sum(axis=1)
    tile_rows = jnp.clip(runs_end - tile_start, 0, EXPERT_TILE)
    i32 = lambda a: a.astype(jnp.int32)
    return dict(
        tile_rows=i32(tile_rows),
        lstart=i32(lstart).reshape(-1), goff=i32(goff).reshape(-1), cpad=i32(cpad).reshape(-1),
        ltot=i32(cpad.sum(axis=1)), fill_start=i32(gstart + tot), fill_len=i32(region - tot),
        lstart_col=lstart.astype(F32).reshape(nblk, N_EXPERTS, 1),
        tile_exp=i32(tile_exp), nact=i32(region_end[-1] // EXPERT_TILE).reshape(1))


def kernel(x_prompt, x_sample, cache_k_win, cache_v_win, state_conv, c_prompt, c_sample, w_ada, b_ada, norm1_g, norm2_g, w_in, q_norm_g, k_norm_g, attn_sinks, w_attn_out, conv_w, conv_b, w_conv_out, w_o, w_router, b_router, w_gate, b_gate, w_up, b_up, w_down, b_down):
    nb, seq, _ = x_prompt.shape
    ns, dseq, _ = x_sample.shape
    wbuf = cache_k_win.shape[2]
    assert w_ada.shape[0] == 1, "single-layer step"
    l = 0

    mod = _adaln(jnp.concatenate([c_prompt, c_sample], axis=0), w_ada[l], b_ada[l])
    mod_p = mod[:nb].reshape(nb, 6, D_MODEL)
    mod_s = mod[nb:].reshape(ns, 6, D_MODEL).transpose(1, 0, 2).reshape(6, ns, 1, D_MODEL)

    bd = jnp.asarray(np.kron(np.eye(N_HEADS), np.ones((HEAD_DIM, HEAD_DIM))), BF16)
    weights = (norm1_g[l].reshape(1, D_MODEL), norm2_g[l].reshape(1, D_MODEL), w_in[l].astype(BF16), bd,
               jnp.tile(q_norm_g[l], N_HEADS).reshape(1, Q_W), jnp.tile(k_norm_g[l], N_KV).reshape(1, KV_W),
               w_attn_out[l].astype(BF16), conv_w[l], conv_b[l].reshape(1, CONV_CH), w_conv_out[l].astype(BF16),
               w_o[l].astype(BF16), w_router[l].T.astype(BF16), b_router[l].reshape(N_EXPERTS, 1))
    sinks = attn_sinks[l]

    tb = MIXER_TOKENS
    assert tb == SORT_TOKENS and SORT_TOKENS % (SAMPLE_STEP_SEQS * dseq) == 0
    (x1_p, h2_p, topi_p, topg_p, kwin_p, vwin_p, zwin_p, cnt_p) = _prompt_mixer(
        x_prompt, mod_p, _rope_table(jnp.arange(seq)), sinks, weights, tb)
    pos_s = jnp.tile(PAST_LEN + jnp.arange(dseq), ns)
    state_rows = state_conv[l].transpose(1, 0, 2).reshape(CONV_K - 1, ns, 1, CONV_CH)
    (x1_s, h2_s, topi_s, topg_s, knew, vnew, z_s, cnt_s) = _sample_mixer(
        x_sample.reshape(ns * dseq, D_MODEL), mod_s, _rope_table(pos_s),
        cache_k_win[l].reshape(ns, wbuf, KV_W), cache_v_win[l].reshape(ns, wbuf, KV_W),
        state_rows, sinks, weights, SAMPLE_STEP_SEQS)

    ntok_p = nb * seq
    ntok = ntok_p + ns * dseq
    topi_all = jnp.concatenate([topi_p.transpose(1, 0, 2).reshape(SUBLANES, ntok_p), topi_s], axis=1)
    topg_all = jnp.concatenate([topg_p.transpose(1, 0, 2).reshape(SUBLANES, ntok_p), topg_s], axis=1)
    cnt_s = cnt_s.reshape(-1, SORT_TOKENS // (SAMPLE_STEP_SEQS * dseq), N_EXPERTS).sum(axis=1)
    plan = _routing_plan(jnp.concatenate([cnt_p.reshape(-1, N_EXPERTS), cnt_s], axis=0))
    tri = jnp.asarray(np.triu(np.ones((SORT_TOKENS, SORT_TOKENS)), k=1), BF16)
    xs, pos_all = _sort_tokens(plan, h2_p.reshape(ntok_p, D_MODEL), h2_s, topi_all, tri,
                               _num_expert_tiles(ntok) * EXPERT_TILE)
    pos_cols, gate_cols = pos_all, topg_all
    ys = _experts(plan["tile_exp"], plan["nact"], plan["tile_rows"], xs,
                  w_gate[l], b_gate[l].reshape(N_EXPERTS, 1, D_MODEL),
                  w_up[l], b_up[l].reshape(N_EXPERTS, 1, D_MODEL),
                  w_down[l], b_down[l].reshape(N_EXPERTS, 1, D_MODEL))
    blocks_per_seq = seq // SORT_TOKENS
    y_p = _unsort_tokens(plan, ys, x1_p.reshape(ntok_p, D_MODEL), mod_p[:, 5:6, :],
                         pl.BlockSpec((None, 1, D_MODEL), lambda i, *_: (i // blocks_per_seq, 0, 0)),
                         pos_cols, gate_cols, 0)
    y_s = _unsort_tokens(plan, ys, x1_s, mod_s[5],
                         pl.BlockSpec((SORT_TOKENS // dseq, 1, D_MODEL), lambda i, *_: (i, 0, 0)),
                         pos_cols, gate_cols, ntok_p // SORT_TOKENS)

    n_kv_shape = (N_KV, HEAD_DIM)
    k_win_p = kwin_p.reshape(1, nb, WINDOW, *n_kv_shape)
    v_win_p = vwin_p.reshape(1, nb, WINDOW, *n_kv_shape)
    conv_p = zwin_p[:, SUBLANES - (CONV_K - 1):, :][None]
    k_win_s = knew.reshape(1, ns, wbuf, *n_kv_shape)
    v_win_s = vnew.reshape(1, ns, wbuf, *n_kv_shape)
    conv_s = z_s.reshape(ns, dseq, CONV_CH)[:, dseq - (CONV_K - 1):, :][None]
    return (y_p.reshape(nb, seq, D_MODEL), y_s.reshape(ns, dseq, D_MODEL),
            k_win_p, v_win_p, conv_p, k_win_s, v_win_s, conv_s)
```
